```python
import jax, jax.numpy as jnp
from jax import lax
import numpy as np


D_MODEL = 1024
BATCH = 4
SEQ = 4096
DEPTH = 2
DEC_BATCH = 128
DEC_SEQ = 1
PAST_LEN = 16384
PAGE_SIZE = 128

A_HEADS = 4
A_DK = 128
A_DV = 128
A_KEY = A_HEADS * A_DK
A_WIDTH = A_HEADS * A_DV
A_CHUNK = 64
B_Q_HEADS = 8
B_KV_HEADS = 2
B_HEAD_DIM = 64
B_GROUP = B_Q_HEADS // B_KV_HEADS
B_WIDTH = B_Q_HEADS * B_HEAD_DIM
B_KV_WIDTH = B_KV_HEADS * B_HEAD_DIM
WINDOW = 128
ROPE_THETA = 10000.0
ATTN_SCALE = B_HEAD_DIM ** -0.5
N_EXPERTS = 64
TOP_K = 8
N_GROUPS = 8
TOPK_GROUPS = 4
D_EXPERT = D_MODEL // 4
D_SHARED = D_EXPERT
ROUTED_SCALE = 2.5
MOE_BLOCK = 128
DN_ALPHA = (2 * DEPTH) ** 0.25
DN_BETA = (8 * DEPTH) ** -0.25
LN_EPS = 1e-5
RMS_EPS = 1e-6
NEG_BIG = -1e30
TINY = 1.1754944e-38
OFF_AF = A_KEY
OFF_AI = 2 * A_KEY
OFF_AG = OFF_AI + A_WIDTH
OFF_BQ = OFF_AG + A_WIDTH
OFF_BK = OFF_BQ + B_WIDTH
OFF_BV = OFF_BK + B_KV_WIDTH
OFF_GA = OFF_BV + B_KV_WIDTH
OFF_GB = OFF_GA + D_MODEL
IN_COLS = OFF_GB + D_MODEL

kernel_name = 'hgrn2_swa_sink_moe_deepnorm_step'

F32 = jnp.float32


def layer_norm(x, g, b):
    xf = x.astype(F32)
    mu = jnp.mean(xf, -1, keepdims=True)
    var = jnp.mean(jnp.square(xf - mu), -1, keepdims=True)
    return ((xf - mu) * lax.rsqrt(var + LN_EPS) * g.astype(F32) + b.astype(F32)).astype(x.dtype)


def rms_norm(x, g):
    xf = x.astype(F32)
    return xf * lax.rsqrt(jnp.mean(xf * xf, -1, keepdims=True) + RMS_EPS) * g.astype(F32)


def rope(x, pos):
    half = x.shape[-1] // 2
    inv = ROPE_THETA ** (-jnp.arange(half, dtype=F32) / half)
    ang = pos.astype(F32)[:, None] * inv[None, :]
    cos = jnp.cos(ang)[None, :, None, :]
    sin = jnp.sin(ang)[None, :, None, :]
    x1 = x[..., :half].astype(F32)
    x2 = x[..., half:].astype(F32)
    return jnp.concatenate([x1 * cos - x2 * sin, x2 * cos + x1 * sin], axis=-1).astype(x.dtype)


def hgrn2_recurrence(q, k, v, logf, s0):
    bsz, t, h, _ = q.shape
    c = min(A_CHUNK, t)
    n = -(-t // c)
    pad = n * c - t

    def blocks(a):
        a = jnp.pad(a.astype(F32), ((0, 0), (0, pad), (0, 0), (0, 0)))
        return jnp.moveaxis(a.reshape(bsz, n, c, h, a.shape[-1]), 1, 0)

    causal = jnp.tril(jnp.ones((c, c), dtype=bool))

    def step(S, blk):
        qc, kc, vc, gc = blk
        G = jnp.cumsum(gc, axis=1)
        o_inter = jnp.einsum('bthk,bhkv->bthv', qc * jnp.exp(G), S)
        diff = G[:, :, None] - G[:, None, :]
        decay = jnp.exp(jnp.where(causal[None, :, :, None, None], diff, NEG_BIG))
        att = jnp.einsum('bthk,bshk,btshk->bhts', qc, kc, decay)
        o_intra = jnp.einsum('bhts,bshv->bthv', att, vc)
        G_end = G[:, -1]
        S = jnp.exp(G_end)[..., None] * S + jnp.einsum('bshk,bshv->bhkv', kc * jnp.exp(G_end[:, None] - G), vc)
        return S, o_inter + o_intra

    S, o = lax.scan(step, s0.astype(F32), (blocks(q), blocks(k), blocks(v), blocks(logf)))
    o = jnp.moveaxis(o, 0, 1).reshape(bsz, n * c, h, v.shape[-1])[:, :t]
    return o, S


def sink_attention(q, k, v, mask, sinks):
    s = jnp.einsum('...qhgd,...khd->...hgqk', q.astype(F32), k.astype(F32)) * ATTN_SCALE
    s = jnp.where(mask, s, NEG_BIG)
    sk = sinks.astype(F32)[..., None, None]
    m = jnp.maximum(jnp.max(s, -1, keepdims=True), sk)
    p = jnp.exp(s - m)
    denom = jnp.sum(p, -1, keepdims=True) + jnp.exp(sk - m)
    return jnp.einsum('...hgqk,...khd->...qhgd', p / denom, v.astype(F32))


def band_mask(n_q, n_k):
    delta = WINDOW + jnp.arange(n_q)[:, None] - jnp.arange(n_k)[None, :]
    return (delta >= 0) & (delta <= WINDOW)


def swa_prompt(q, k, v, sinks):
    bsz, t = q.shape[0], q.shape[1]
    nb = t // WINDOW
    qb = q.reshape(bsz, nb, WINDOW, B_KV_HEADS, B_GROUP, B_HEAD_DIM)

    def with_prev(a):
        a = a.reshape(bsz, nb, WINDOW, B_KV_HEADS, B_HEAD_DIM)
        prev = jnp.pad(a[:, :-1], ((0, 0), (1, 0), (0, 0), (0, 0), (0, 0)))
        return jnp.concatenate([prev, a], axis=2)

    kj = jnp.arange(2 * WINDOW)
    has_prev = (jnp.arange(nb) > 0)[:, None, None] | (kj >= WINDOW)[None, None, :]
    mask = (band_mask(WINDOW, 2 * WINDOW)[None] & has_prev)[:, None, None]
    o = sink_attention(qb, with_prev(k), with_prev(v), mask, sinks)
    return o.reshape(bsz, t, B_WIDTH)


def swa_sample(q, keys, vals, sinks):
    bsz, tq = q.shape[0], q.shape[1]
    qb = q.reshape(bsz, tq, B_KV_HEADS, B_GROUP, B_HEAD_DIM)
    o = sink_attention(qb, keys, vals, band_mask(tq, keys.shape[1]), sinks)
    return o.reshape(bsz, tq, B_WIDTH)


def log_forget(af, lower):
    ls = jax.nn.log_sigmoid(af.astype(F32))
    mixed = jnp.logaddexp(jnp.log(jnp.maximum(lower, TINY)), jnp.log1p(-lower) + ls)
    return jnp.where(lower > 0, mixed, ls)


def token_mixer(x, pos, s0, k_buf, v_buf, lower, w_in, hgrn_norm_g, sinks, w_branch_a, w_branch_b, w_out):
    bsz, t, _ = x.shape
    proj = x @ w_in
    aq, af, ai, ag, bq, bk, bv, ga, gb = jnp.split(
        proj, [OFF_AF, OFF_AI, OFF_AG, OFF_BQ, OFF_BK, OFF_BV, OFF_GA, OFF_GB], axis=-1)
    logf = log_forget(af, lower)
    q_a = jax.nn.silu(aq.astype(F32)).reshape(bsz, t, A_HEADS, A_DK)
    k_a = (-jnp.expm1(logf)).reshape(bsz, t, A_HEADS, A_DK)
    v_a = ai.reshape(bsz, t, A_HEADS, A_DV)
    o_a, s_new = hgrn2_recurrence(q_a, k_a, v_a, logf.reshape(bsz, t, A_HEADS, A_DK), s0)
    o_a = rms_norm(o_a, hgrn_norm_g) * jax.nn.silu(ag.astype(F32)).reshape(bsz, t, A_HEADS, A_DV)
    o_a = o_a.reshape(bsz, t, A_WIDTH).astype(x.dtype)
    q_b = rope(bq.reshape(bsz, t, B_Q_HEADS, B_HEAD_DIM), pos)
    k_b = rope(bk.reshape(bsz, t, B_KV_HEADS, B_HEAD_DIM), pos)
    v_b = bv.reshape(bsz, t, B_KV_HEADS, B_HEAD_DIM)
    sk = sinks.reshape(B_KV_HEADS, B_GROUP)
    if k_buf is None:
        o_b = swa_prompt(q_b, k_b, v_b, sk)
        k_new, v_new = k_b[:, -WINDOW:], v_b[:, -WINDOW:]
    else:
        keys = jnp.concatenate([k_buf.astype(k_b.dtype), k_b], axis=1)
        vals = jnp.concatenate([v_buf.astype(v_b.dtype), v_b], axis=1)
        o_b = swa_sample(q_b, keys, vals, sk)
        k_new, v_new = keys[:, -WINDOW:], vals[:, -WINDOW:]
    o_b = o_b.astype(x.dtype)
    merged = jax.nn.sigmoid(ga) * (o_a @ w_branch_a) + jax.nn.sigmoid(gb) * (o_b @ w_branch_b)
    return merged @ w_out, s_new.astype(x.dtype), k_new, v_new


def route(h, w_router, router_bias):
    n = h.shape[0]
    scores = jax.nn.sigmoid(h.astype(F32) @ w_router.astype(F32))
    sel = scores + router_bias.astype(F32)
    grp = sel.reshape(n, N_GROUPS, N_EXPERTS // N_GROUPS)
    gscore = jnp.sum(lax.top_k(grp, 2)[0], axis=-1)
    _, gidx = lax.top_k(gscore, TOPK_GROUPS)
    gmask = jnp.zeros((n, N_GROUPS), dtype=bool).at[jnp.arange(n)[:, None], gidx].set(True)
    emask = jnp.repeat(gmask, N_EXPERTS // N_GROUPS, axis=-1)
    _, eidx = lax.top_k(jnp.where(emask, sel, NEG_BIG), TOP_K)
    w = jnp.take_along_axis(scores, eidx, axis=-1)
    w = w / jnp.sum(w, -1, keepdims=True) * ROUTED_SCALE
    return eidx, w


def routed_experts(h, eidx, ew, w_gate, w_up, w_down):
    n = h.shape[0]
    a = n * TOP_K
    flat_e = eidx.reshape(-1)
    flat_tok = jnp.repeat(jnp.arange(n, dtype=jnp.int32), TOP_K)
    flat_w = ew.reshape(-1)
    order = jnp.argsort(flat_e)
    sorted_e = flat_e[order]
    counts = jnp.bincount(flat_e, length=N_EXPERTS)
    starts = jnp.cumsum(counts) - counts
    padded = (counts + MOE_BLOCK - 1) // MOE_BLOCK * MOE_BLOCK
    pad_ends = jnp.cumsum(padded)
    pad_starts = pad_ends - padded
    dest = pad_starts[sorted_e] + jnp.arange(a) - starts[sorted_e]
    n_blocks = -(-a // MOE_BLOCK) + N_EXPERTS
    rows = n_blocks * MOE_BLOCK
    row_tok = jnp.zeros((rows,), jnp.int32).at[dest].set(flat_tok[order])
    row_w = jnp.zeros((rows,), F32).at[dest].set(flat_w[order])
    block_e = jnp.minimum(jnp.searchsorted(pad_ends, jnp.arange(n_blocks) * MOE_BLOCK, side='right'), N_EXPERTS - 1)

    def step(out, blk):
        tok, wgt, e = blk
        xb = h[tok]
        yb = (jax.nn.silu(xb @ w_gate[e]) * (xb @ w_up[e])) @ w_down[e]
        return out.at[tok].add((yb * wgt[:, None]).astype(out.dtype)), None

    out, _ = lax.scan(step, jnp.zeros_like(h),
                      (row_tok.reshape(n_blocks, MOE_BLOCK), row_w.reshape(n_blocks, MOE_BLOCK), block_e))
    return out


def moe(h, w_router, router_bias, w_exp_gate, w_exp_up, w_exp_down, w_sh_gate, w_sh_up, w_sh_down):
    bsz, t, d = h.shape
    hf = h.reshape(bsz * t, d)
    eidx, ew = route(hf, w_router, router_bias)
    routed = routed_experts(hf, eidx, ew, w_exp_gate, w_exp_up, w_exp_down)
    shared = (jax.nn.silu(hf @ w_sh_gate) * (hf @ w_sh_up)) @ w_sh_down
    return (routed + shared).reshape(bsz, t, d)


def setup_inputs(seed: int = 0) -> dict:
    key = jax.random.key(seed)
    ks = jax.random.split(key, 32)

    def nrm(k, shape, scale):
        return jax.random.normal(k, shape, F32) * scale

    col_scale = jnp.concatenate([
        jnp.ones((OFF_AI,), F32), jnp.full((A_WIDTH,), DN_BETA, F32),
        jnp.ones((OFF_BV - OFF_AG,), F32), jnp.full((B_KV_WIDTH,), DN_BETA, F32),
        jnp.ones((IN_COLS - OFF_GA,), F32)])
    return {
        'x_prompt': nrm(ks[0], (BATCH, SEQ, D_MODEL), 1.0),
        'x_sample': nrm(ks[1], (DEC_BATCH, DEC_SEQ, D_MODEL), 1.0),
        'cache_k': nrm(ks[2], (DEPTH, DEC_BATCH, WINDOW, B_KV_HEADS, B_HEAD_DIM), 1.0),
        'cache_v': nrm(ks[3], (DEPTH, DEC_BATCH, WINDOW, B_KV_HEADS, B_HEAD_DIM), 0.5),
        'state_hgrn': nrm(ks[4], (DEPTH, DEC_BATCH, A_HEADS, A_DK, A_DV), 0.5),
        'w_in': nrm(ks[5], (DEPTH, D_MODEL, IN_COLS), D_MODEL ** -0.5) * col_scale,
        'hgrn_lower_bounds': nrm(ks[6], (DEPTH, A_KEY), 0.1),
        'hgrn_norm_g': 1.0 + nrm(ks[7], (DEPTH, A_DV), 0.02),
        'attn_sinks': nrm(ks[8], (DEPTH, B_Q_HEADS), 0.5),
        'w_branch_a': nrm(ks[9], (DEPTH, A_WIDTH, D_MODEL), A_WIDTH ** -0.5 * DN_BETA),
        'w_branch_b': nrm(ks[10], (DEPTH, B_WIDTH, D_MODEL), B_WIDTH ** -0.5 * DN_BETA),
        'w_out': nrm(ks[11], (DEPTH, D_MODEL, D_MODEL), D_MODEL ** -0.5 * DN_BETA),
        'ln1_g': 1.0 + nrm(ks[12], (DEPTH, D_MODEL), 0.02),
        'ln1_b': nrm(ks[13], (DEPTH, D_MODEL), 0.02),
        'w_router': nrm(ks[14], (DEPTH, D_MODEL, N_EXPERTS), D_MODEL ** -0.5),
        'router_bias': nrm(ks[15], (DEPTH, N_EXPERTS), 0.01),
        'w_exp_gate': nrm(ks[16], (DEPTH, N_EXPERTS, D_MODEL, D_EXPERT), D_MODEL ** -0.5),
        'w_exp_up': nrm(ks[17], (DEPTH, N_EXPERTS, D_MODEL, D_EXPERT), D_MODEL ** -0.5),
        'w_exp_down': nrm(ks[18], (DEPTH, N_EXPERTS, D_EXPERT, D_MODEL), D_EXPERT ** -0.5 * DN_BETA),
        'w_sh_gate': nrm(ks[19], (DEPTH, D_MODEL, D_SHARED), D_MODEL ** -0.5),
        'w_sh_up': nrm(ks[20], (DEPTH, D_MODEL, D_SHARED), D_MODEL ** -0.5),
        'w_sh_down': nrm(ks[21], (DEPTH, D_SHARED, D_MODEL), D_SHARED ** -0.5 * DN_BETA),
        'ln2_g': 1.0 + nrm(ks[22], (DEPTH, D_MODEL), 0.02),
        'ln2_b': nrm(ks[23], (DEPTH, D_MODEL), 0.02),
    }


def reference(x_prompt, x_sample, cache_k, cache_v, state_hgrn, w_in, hgrn_lower_bounds, hgrn_norm_g,
              attn_sinks, w_branch_a, w_branch_b, w_out, ln1_g, ln1_b, w_router, router_bias,
              w_exp_gate, w_exp_up, w_exp_down, w_sh_gate, w_sh_up, w_sh_down, ln2_g, ln2_b):
    lb_prob = jax.nn.softmax(hgrn_lower_bounds.astype(F32), axis=0)
    lower = jnp.cumsum(lb_prob, axis=0) - lb_prob[0]
    pos_p = jnp.arange(x_prompt.shape[1])
    pos_s = PAST_LEN + jnp.arange(x_sample.shape[1])
    s0_p = jnp.zeros((x_prompt.shape[0], A_HEADS, A_DK, A_DV), F32)
    xp, xs = x_prompt, x_sample
    pk, pv, ps, sk, sv, ss = [], [], [], [], [], []
    for l in range(DEPTH):
        mix_p, st_p, k_p, v_p = token_mixer(xp, pos_p, s0_p, None, None, lower[l], w_in[l], hgrn_norm_g[l],
                                            attn_sinks[l], w_branch_a[l], w_branch_b[l], w_out[l])
        mix_s, st_s, k_s, v_s = token_mixer(xs, pos_s, state_hgrn[l], cache_k[l], cache_v[l], lower[l], w_in[l],
                                            hgrn_norm_g[l], attn_sinks[l], w_branch_a[l], w_branch_b[l], w_out[l])
        xp = layer_norm(DN_ALPHA * xp + mix_p, ln1_g[l], ln1_b[l])
        xs = layer_norm(DN_ALPHA * xs + mix_s, ln1_g[l], ln1_b[l])
        xp = layer_norm(DN_ALPHA * xp + moe(xp, w_router[l], router_bias[l], w_exp_gate[l], w_exp_up[l], w_exp_down[l],
                                            w_sh_gate[l], w_sh_up[l], w_sh_down[l]), ln2_g[l], ln2_b[l])
        xs = layer_norm(DN_ALPHA * xs + moe(xs, w_router[l], router_bias[l], w_exp_gate[l], w_exp_up[l], w_exp_down[l],
                                            w_sh_gate[l], w_sh_up[l], w_sh_down[l]), ln2_g[l], ln2_b[l])
        pk.append(k_p); pv.append(v_p); ps.append(st_p)
        sk.append(k_s); sv.append(v_s); ss.append(st_s)
    return (xp, xs, jnp.stack(pk), jnp.stack(pv), jnp.stack(ps), jnp.stack(sk), jnp.stack(sv), jnp.stack(ss))
```

```python
import functools

import numpy as np
import jax
import jax.numpy as jnp
from jax import lax
from jax.experimental import pallas as pl
from jax.experimental.pallas import tpu as pltpu

F32 = jnp.float32
BF16 = jnp.bfloat16
I32 = jnp.int32

D_MODEL = 1024
DEPTH = 2
PAST_LEN = 16384
A_HEADS = 4
A_DK = 128
A_DV = 128
A_KEY = A_HEADS * A_DK
A_WIDTH = A_HEADS * A_DV
B_Q_HEADS = 8
B_KV_HEADS = 2
B_HEAD_DIM = 64
B_GROUP = B_Q_HEADS // B_KV_HEADS
B_WIDTH = B_Q_HEADS * B_HEAD_DIM
B_KV_WIDTH = B_KV_HEADS * B_HEAD_DIM
WINDOW = 128
ROPE_THETA = 10000.0
ATTN_SCALE = B_HEAD_DIM ** -0.5
N_EXPERTS = 64
TOP_K = 8
N_GROUPS = 8
GROUP_SIZE = N_EXPERTS // N_GROUPS
TOPK_GROUPS = 4
D_EXPERT = D_MODEL // 4
D_SHARED = D_EXPERT
ROUTED_SCALE = 2.5
DN_ALPHA = (2 * DEPTH) ** 0.25
LN_EPS = 1e-5
RMS_EPS = 1e-6
NEG_BIG = -1e30
TINY = 1.1754944e-38
OFF_AF = A_KEY
OFF_AI = 2 * A_KEY
OFF_AG = OFF_AI + A_WIDTH
OFF_BQ = OFF_AG + A_WIDTH
OFF_BK = OFF_BQ + B_WIDTH
OFF_BV = OFF_BK + B_KV_WIDTH
OFF_GA = OFF_BV + B_KV_WIDTH
OFF_GB = OFF_GA + D_MODEL
IN_COLS = OFF_GB + D_MODEL

LANES = 128
HGRN_CHUNK = 128
HGRN_LEVELS = 7
ROW_TILE = 512
MOE_ROW_TILE = 384
EXPERT_ROW_TILE = 256
VMEM_LIMIT = 56 * 1024 * 1024


def _cparams(sem, vmem=VMEM_LIMIT):
    return pltpu.CompilerParams(dimension_semantics=sem, vmem_limit_bytes=vmem)


def _dot(a, b):
    return jnp.dot(a, b, preferred_element_type=F32)


def _dot_nt(a, b):
    return lax.dot_general(a, b, (((1,), (1,)), ((), ())), preferred_element_type=F32)


def _dot_tn(a, b):
    return lax.dot_general(a, b, (((0,), (0,)), ((), ())), preferred_element_type=F32)


def _sigmoid(x):
    return 1.0 / (1.0 + jnp.exp(-x))


def _silu(x):
    return x * _sigmoid(x)


def _split3(x):
    hi = x.astype(BF16)
    r1 = x - hi.astype(F32)
    mid = r1.astype(BF16)
    lo = (r1 - mid.astype(F32)).astype(BF16)
    return hi, mid, lo


def _layer_norm(y, g, b):
    mu = jnp.mean(y, axis=-1, keepdims=True)
    d = y - mu
    var = jnp.mean(d * d, axis=-1, keepdims=True)
    return d * lax.rsqrt(var + LN_EPS) * g + b


def _log_forget(af, lower):
    ls = jnp.minimum(af, 0.0) - jnp.log1p(jnp.exp(-jnp.abs(af)))
    a = jnp.log(jnp.maximum(lower, TINY))
    b = jnp.log1p(-lower) + ls
    mixed = jnp.maximum(a, b) + jnp.log1p(jnp.exp(-jnp.abs(a - b)))
    return jnp.where(lower > 0.0, mixed, ls)


def _proj_kernel(x_ref, w_ref, cos_ref, sin_ref, low_ref,
                 qa_ref, lf_ref, va_ref, ga_ref, qb_ref, kb_ref, vb_ref, sga_ref, sgb_ref):
    xb = x_ref[...].astype(BF16)

    def mm(c0, n):
        return _dot(xb, w_ref[:, c0:c0 + n])

    qa_ref[...] = _silu(mm(0, A_KEY)).astype(BF16)
    lf_ref[...] = _log_forget(mm(OFF_AF, A_KEY), low_ref[...])
    va_ref[...] = mm(OFF_AI, A_WIDTH).astype(BF16)
    ga_ref[...] = _silu(mm(OFF_AG, A_WIDTH)).astype(BF16)

    cos = cos_ref[...]
    sin = sin_ref[...]
    lane = lax.broadcasted_iota(I32, cos.shape, 1)
    first_half = (lane & (B_HEAD_DIM // 2)) == 0

    def rope(blk):
        partner = jnp.where(first_half, pltpu.roll(blk, LANES - B_HEAD_DIM // 2, 1),
                            pltpu.roll(blk, B_HEAD_DIM // 2, 1))
        return blk * cos + partner * sin

    bq = mm(OFF_BQ, B_WIDTH)
    for j in range(B_WIDTH // LANES):
        sl = slice(j * LANES, (j + 1) * LANES)
        qb_ref[:, sl] = (rope(bq[:, sl]) * ATTN_SCALE).astype(BF16)
    kb_ref[...] = rope(mm(OFF_BK, B_KV_WIDTH))
    vb_ref[...] = mm(OFF_BV, B_KV_WIDTH)
    sga_ref[...] = _sigmoid(mm(OFF_GA, D_MODEL)).astype(BF16)
    sgb_ref[...] = _sigmoid(mm(OFF_GB, D_MODEL)).astype(BF16)


def _proj(x, w_bf, cos_t, sin_t, lower, n_rows, row_off_blocks, tm):
    nt = n_rows // tm
    tab_blocks = cos_t.shape[0] // tm
    row = lambda w: pl.BlockSpec((tm, w), lambda i: (i, 0))
    outs = [(A_KEY, BF16), (A_KEY, F32), (A_WIDTH, BF16), (A_WIDTH, BF16), (B_WIDTH, BF16),
            (B_KV_WIDTH, F32), (B_KV_WIDTH, F32), (D_MODEL, BF16), (D_MODEL, BF16)]
    return pl.pallas_call(
        _proj_kernel,
        grid=(nt,),
        in_specs=[pl.BlockSpec((tm, D_MODEL), lambda i: (i + row_off_blocks, 0)),
                  pl.BlockSpec((D_MODEL, IN_COLS), lambda i: (0, 0)),
                  pl.BlockSpec((tm, LANES), lambda i: (i % tab_blocks, 0)),
                  pl.BlockSpec((tm, LANES), lambda i: (i % tab_blocks, 0)),
                  pl.BlockSpec((1, A_KEY), lambda i: (0, 0))],
        out_specs=[row(w) for w, _ in outs],
        out_shape=[jax.ShapeDtypeStruct((n_rows, w), dt) for w, dt in outs],
        compiler_params=_cparams(("arbitrary",)),
        name="proj",
    )(x, w_bf, cos_t, sin_t, lower)


def _rope_tables(pos):
    half = B_HEAD_DIM // 2
    inv = ROPE_THETA ** (-jnp.arange(half, dtype=F32) / half)
    ang = pos.astype(F32)[:, None] * inv[None, :]
    cos = jnp.cos(ang)
    sin = jnp.sin(ang)
    reps = LANES // B_HEAD_DIM
    cos_t = jnp.tile(jnp.concatenate([cos, cos], axis=1), (1, reps))
    sin_t = jnp.tile(jnp.concatenate([-sin, sin], axis=1), (1, reps))
    return cos_t, sin_t


def _hgrn_constants():
    c = HGRN_CHUNK
    r = np.arange(c)
    tri = (r[None, :] <= r[:, None]).astype(np.float32)
    sel = np.zeros((HGRN_LEVELS, c, c), np.float32)
    upper = np.zeros((HGRN_LEVELS, c, LANES), np.float32)
    pair = np.zeros((HGRN_LEVELS + 1, c, c), np.float32)
    for l in range(HGRN_LEVELS):
        b = c >> (l + 1)
        ref_row = (r // (2 * b)) * (2 * b) + b - 1
        sel[l, r, ref_row] = 1.0
        up = (r % (2 * b)) >= b
        upper[l] = up[:, None]
        same = (r[:, None] // (2 * b)) == (r[None, :] // (2 * b))
        pair[l] = (up[:, None] & ~up[None, :] & same)
    pair[HGRN_LEVELS] = np.eye(c)
    return (jnp.asarray(tri, BF16), jnp.asarray(sel.reshape(HGRN_LEVELS * c, c), BF16),
            jnp.asarray(upper), jnp.asarray(pair))


def _hgrn_kernel(qa_ref, lf_ref, va_ref, ga_ref, g_ref, tri_ref, sel_ref, up_ref, pair_ref,
                 oa_ref, st_ref, s_scr):
    c = HGRN_CHUNK
    step = pl.program_id(1)

    @pl.when(step == 0)
    def _():
        s_scr[...] = jnp.zeros_like(s_scr)

    tri = tri_ref[...]
    sel = sel_ref[...]
    for h in range(A_HEADS):
        sl = slice(h * A_DK, (h + 1) * A_DK)
        lf = lf_ref[:, sl]
        hi, mid, lo = _split3(lf)
        gcum = _dot(tri, hi) + _dot(tri, mid) + _dot(tri, lo)
        ghi, gmid, glo = _split3(gcum)
        qb = qa_ref[:, sl]
        qf = qb.astype(F32)
        kf = 1.0 - jnp.exp(lf)
        vb = va_ref[:, sl]
        att = _dot_nt(qb, kf.astype(BF16)) * pair_ref[HGRN_LEVELS]
        for l in range(HGRN_LEVELS):
            sl_rows = slice(l * c, (l + 1) * c)
            gref = _dot(sel[sl_rows], ghi) + _dot(sel[sl_rows], gmid) + _dot(sel[sl_rows], glo)
            e = jnp.exp(-jnp.abs(gcum - gref))
            w = (jnp.where(up_ref[l] > 0.5, qf, kf) * e).astype(BF16)
            att = att + _dot_nt(w, w) * pair_ref[l]
        s_t = s_scr[h]
        o = _dot_nt((qf * jnp.exp(gcum)).astype(BF16), s_t.astype(BF16)) + _dot(att.astype(BF16), vb)
        ms = jnp.mean(o * o, axis=-1, keepdims=True)
        on = o * lax.rsqrt(ms + RMS_EPS) * g_ref[...] * ga_ref[:, sl].astype(F32)
        oa_ref[:, sl] = on.astype(BF16)
        gend = gcum[c - 1:c, :]
        kend = (kf * jnp.exp(gend - gcum)).astype(BF16)
        s_scr[h] = s_t * jnp.exp(gend) + _dot_tn(vb, kend)

    @pl.when(step == pl.num_programs(1) - 1)
    def _():
        for h in range(A_HEADS):
            st_ref[0, h] = s_scr[h].T


def _hgrn_prompt(qa, lf, va, ga, norm_g, bsz, t):
    c = HGRN_CHUNK
    nc = t // c
    tri, sel, upper, pair = _hgrn_constants()
    blk = lambda: pl.BlockSpec((c, A_KEY), lambda b, i: (b * nc + i, 0))
    const = lambda a: pl.BlockSpec(a.shape, lambda b, i: (0,) * a.ndim)
    return pl.pallas_call(
        _hgrn_kernel,
        grid=(bsz, nc),
        in_specs=[blk(), blk(), blk(), blk(), pl.BlockSpec((1, A_DV), lambda b, i: (0, 0)),
                  const(tri), const(sel), const(upper), const(pair)],
        out_specs=[blk(), pl.BlockSpec((1, A_HEADS, A_DK, A_DV), lambda b, i: (b, 0, 0, 0))],
        out_shape=[jax.ShapeDtypeStruct((bsz * t, A_WIDTH), BF16),
                   jax.ShapeDtypeStruct((bsz, A_HEADS, A_DK, A_DV), F32)],
        scratch_shapes=[pltpu.VMEM((A_HEADS, A_DV, A_DK), F32)],
        compiler_params=_cparams(("arbitrary", "arbitrary")),
        name="hgrn_prompt",
    )(qa, lf, va, ga, norm_g, tri, sel, upper, pair)


def _swa_kernel(sink_ref, q_ref, k_ref, v_ref, o_ref, kprev, vprev):
    w = WINDOW
    i = pl.program_id(1)

    @pl.when(i == 0)
    def _():
        kprev[...] = jnp.zeros_like(kprev)
        vprev[...] = jnp.zeros_like(vprev)

    kc = k_ref[...]
    vc = v_ref[...]
    kk = jnp.concatenate([kprev[...], kc], axis=0)
    vv = jnp.concatenate([vprev[...], vc], axis=0)
    kr = pltpu.roll(kk, B_HEAD_DIM, 1)
    vr = pltpu.roll(vv, B_HEAD_DIM, 1)
    lo2 = lax.broadcasted_iota(I32, kk.shape, 1) < B_HEAD_DIM
    zero = jnp.zeros_like(kk)
    k_lo = [jnp.where(lo2, kk, zero).astype(BF16), jnp.where(lo2, kr, zero).astype(BF16)]
    k_hi = [jnp.where(lo2, zero, kr).astype(BF16), jnp.where(lo2, zero, kk).astype(BF16)]
    v_dup = [jnp.where(lo2, vv, vr).astype(BF16), jnp.where(lo2, vr, vv).astype(BF16)]

    qi = lax.broadcasted_iota(I32, (w, 2 * w), 0)
    kj = lax.broadcasted_iota(I32, (w, 2 * w), 1)
    valid = (kj >= qi) & (kj <= qi + w) & ((kj >= w) | (i > 0))
    lo1 = lax.broadcasted_iota(I32, (w, LANES), 1) < B_HEAD_DIM

    for j in range(B_WIDTH // LANES):
        g = (2 * j) // B_GROUP
        qblk = q_ref[:, j * LANES:(j + 1) * LANES]
        res = []
        for half, kmat in enumerate((k_lo[g], k_hi[g])):
            sk = sink_ref[2 * j + half]
            s = jnp.where(valid, _dot_nt(qblk, kmat), NEG_BIG)
            m = jnp.maximum(jnp.max(s, axis=-1, keepdims=True), sk)
            p = jnp.exp(s - m)
            denom = jnp.sum(p, axis=-1, keepdims=True) + jnp.exp(sk - m)
            res.append(_dot(p.astype(BF16), v_dup[g]) / denom)
        o_ref[:, j * LANES:(j + 1) * LANES] = jnp.where(lo1, res[0], res[1]).astype(BF16)

    kprev[...] = kc
    vprev[...] = vc


def _swa_prompt(sinks, qb, kb, vb, bsz, t):
    w = WINDOW
    nb = t // w
    return pl.pallas_call(
        _swa_kernel,
        grid_spec=pltpu.PrefetchScalarGridSpec(
            num_scalar_prefetch=1,
            grid=(bsz, nb),
            in_specs=[pl.BlockSpec((w, B_WIDTH), lambda b, i, s: (b * nb + i, 0)),
                      pl.BlockSpec((w, B_KV_WIDTH), lambda b, i, s: (b * nb + i, 0)),
                      pl.BlockSpec((w, B_KV_WIDTH), lambda b, i, s: (b * nb + i, 0))],
            out_specs=pl.BlockSpec((w, B_WIDTH), lambda b, i, s: (b * nb + i, 0)),
            scratch_shapes=[pltpu.VMEM((w, B_KV_WIDTH), F32), pltpu.VMEM((w, B_KV_WIDTH), F32)]),
        out_shape=jax.ShapeDtypeStruct((bsz * t, B_WIDTH), BF16),
        compiler_params=_cparams(("arbitrary", "arbitrary")),
        name="swa_prompt",
    )(sinks, qb, kb, vb)


SAMPLE_BLOCK = 8


def _sample_kernel(sink_ref, st_ref, lft_ref, qat_ref, va_ref, ga_ref, g_ref, q3_ref, kn_ref, vn_ref,
                   ck_ref, cv_ref, st_out, oa_ref, ob_ref, ck_out, cv_out, o_scr):
    w = WINDOW
    row = lax.broadcasted_iota(I32, (w, B_KV_WIDTH), 0)
    for i in range(SAMPLE_BLOCK):
        for h in range(A_HEADS):
            sl = slice(h * A_DV, (h + 1) * A_DV)
            fcol = jnp.exp(lft_ref[h, 0][:, i:i + 1])
            qcol = qat_ref[h, 0][:, i:i + 1]
            vrow = va_ref[i:i + 1, sl].astype(F32)
            s_new = st_ref[i, h] * fcol + (1.0 - fcol) * vrow
            st_out[i, h] = s_new
            o_scr[i:i + 1, sl] = jnp.sum(s_new * qcol, axis=0, keepdims=True)
        kc = ck_ref[i]
        vc = cv_ref[i]
        kn = kn_ref[i:i + 1, :]
        vn = vn_ref[i:i + 1, :]
        q3 = q3_ref[i]
        s = _dot_nt(q3.astype(BF16), kc.astype(BF16))
        s_new_key = jnp.sum(q3 * kn, axis=-1, keepdims=True)
        sk = sink_ref[...][:, 0:1]
        m = jnp.maximum(jnp.maximum(jnp.max(s, axis=-1, keepdims=True), s_new_key), sk)
        p = jnp.exp(s - m)
        pn = jnp.exp(s_new_key - m)
        denom = jnp.sum(p, axis=-1, keepdims=True) + pn + jnp.exp(sk - m)
        ob_ref[i] = (_dot(p.astype(BF16), vc.astype(BF16)) + pn * vn) / denom
        ck_out[i] = jnp.where(row == w - 1, kn, pltpu.roll(kc, w - 1, 0))
        cv_out[i] = jnp.where(row == w - 1, vn, pltpu.roll(vc, w - 1, 0))
    for h in range(A_HEADS):
        sl = slice(h * A_DV, (h + 1) * A_DV)
        o = o_scr[:, sl]
        ms = jnp.mean(o * o, axis=-1, keepdims=True)
        oa_ref[:, sl] = (o * lax.rsqrt(ms + RMS_EPS) * g_ref[...] * ga_ref[:, sl].astype(F32)).astype(BF16)


def _sample_step(sinks8, state, lft, qat, va, ga, norm_g, q3, kn, vn, ck, cv):
    nb = state.shape[0]
    sb = SAMPLE_BLOCK
    steps = nb // sb
    w = WINDOW
    b4 = lambda: pl.BlockSpec((sb, A_HEADS, A_DK, A_DV), lambda i: (i, 0, 0, 0))
    t4 = lambda: pl.BlockSpec((A_HEADS, 1, A_DK, sb), lambda i: (0, i, 0, 0))
    r2 = lambda wd: pl.BlockSpec((sb, wd), lambda i: (i, 0))
    c3 = lambda: pl.BlockSpec((sb, w, B_KV_WIDTH), lambda i: (i, 0, 0))
    return pl.pallas_call(
        _sample_kernel,
        grid=(steps,),
        in_specs=[pl.BlockSpec((B_Q_HEADS, LANES), lambda i: (0, 0)),
                  b4(), t4(), t4(), r2(A_WIDTH), r2(A_WIDTH), pl.BlockSpec((1, A_DV), lambda i: (0, 0)),
                  pl.BlockSpec((sb, B_Q_HEADS, LANES), lambda i: (i, 0, 0)), r2(B_KV_WIDTH), r2(B_KV_WIDTH),
                  c3(), c3()],
        out_specs=[b4(), r2(A_WIDTH), pl.BlockSpec((sb, B_Q_HEADS, LANES), lambda i: (i, 0, 0)), c3(), c3()],
        out_shape=[jax.ShapeDtypeStruct(state.shape, F32),
                   jax.ShapeDtypeStruct((nb, A_WIDTH), BF16),
                   jax.ShapeDtypeStruct((nb, B_Q_HEADS, LANES), F32),
                   jax.ShapeDtypeStruct((nb, w, B_KV_WIDTH), F32),
                   jax.ShapeDtypeStruct((nb, w, B_KV_WIDTH), F32)],
        scratch_shapes=[pltpu.VMEM((sb, A_WIDTH), F32)],
        compiler_params=_cparams(("arbitrary",)),
        name="sample_step",
    )(sinks8, state, lft, qat, va, ga, norm_g, q3, kn, vn, ck, cv)


def _merge_kernel(x_ref, oa_ref, ob_ref, sga_ref, sgb_ref, wa_ref, wb_ref, wo_ref, g_ref, b_ref, h_ref):
    merged = (sga_ref[...].astype(F32) * _dot(oa_ref[...], wa_ref[...])
              + sgb_ref[...].astype(F32) * _dot(ob_ref[...], wb_ref[...]))
    mix = _dot(merged.astype(BF16), wo_ref[...])
    h_ref[...] = _layer_norm(DN_ALPHA * x_ref[...] + mix, g_ref[...], b_ref[...])


def _merge(x, oa, ob, sga, sgb, wa, wb, wo, g, b, n_rows, x_off_blocks, out_rows, out_off_blocks, tm, h_prev=None):
    nt = n_rows // tm
    row = lambda wd: pl.BlockSpec((tm, wd), lambda i: (i, 0))
    const = lambda a: pl.BlockSpec(a.shape, lambda i: (0, 0))
    args = [x, oa, ob, sga, sgb, wa, wb, wo, g, b]
    in_specs = [pl.BlockSpec((tm, D_MODEL), lambda i: (i + x_off_blocks, 0)),
                row(A_WIDTH), row(B_WIDTH), row(D_MODEL), row(D_MODEL),
                const(wa), const(wb), const(wo), const(g), const(b)]
    kern = _merge_kernel
    aliases = {}
    if h_prev is not None:
        args.append(h_prev)
        in_specs.append(pl.BlockSpec(memory_space=pl.ANY))
        aliases = {len(args) - 1: 0}
        kern = lambda *refs: _merge_kernel(*refs[:10], refs[11])
    return pl.pallas_call(
        kern,
        grid=(nt,),
        in_specs=in_specs,
        out_specs=pl.BlockSpec((tm, D_MODEL), lambda i: (i + out_off_blocks, 0)),
        out_shape=jax.ShapeDtypeStruct((out_rows, D_MODEL), F32),
        input_output_aliases=aliases,
        compiler_params=_cparams(("arbitrary",)),
        name="merge",
    )(*args)


def _mixer_layer(l, xp, xs, xs_off, n_prompt, bsz, t, n_sample, prm):
    tm = ROW_TILE
    p = _proj(xp, prm["w_in"][l], prm["cos_p"], prm["sin_p"], prm["lower"][l], n_prompt, 0, tm)
    qa, lf, va, ga, qb, kb, vb, sga, sgb = p
    oa, st_p = _hgrn_prompt(qa, lf, va, ga, prm["norm_g"][l], bsz, t)
    ob = _swa_prompt(prm["sinks"][l], qb, kb, vb, bsz, t)
    n_all = n_prompt + n_sample
    h_all = _merge(xp, oa, ob, sga, sgb, prm["wa"][l], prm["wb"][l], prm["wo"][l], prm["ln1_g"][l],
                   prm["ln1_b"][l], n_prompt, 0, n_all, 0, tm, h_prev=jnp.zeros((n_all, D_MODEL), F32))
    kp = kb.reshape(bsz, t, B_KV_HEADS, B_HEAD_DIM)[:, -WINDOW:]
    vp = vb.reshape(bsz, t, B_KV_HEADS, B_HEAD_DIM)[:, -WINDOW:]

    ts = n_sample
    ps = _proj(xs, prm["w_in"][l], prm["cos_s"], prm["sin_s"], prm["lower"][l], ts, xs_off, ts)
    qa_s, lf_s, va_s, ga_s, qb_s, kb_s, vb_s, sga_s, sgb_s = ps
    sb = SAMPLE_BLOCK
    to_t = lambda a: a.reshape(ts // sb, sb, A_HEADS, A_DK).transpose(2, 0, 3, 1)
    qh = qb_s.astype(F32).reshape(ts, B_Q_HEADS, B_HEAD_DIM)
    z = jnp.zeros_like(qh[:, :B_GROUP])
    q3 = jnp.concatenate([jnp.concatenate([qh[:, :B_GROUP], z], axis=-1),
                          jnp.concatenate([z, qh[:, B_GROUP:]], axis=-1)], axis=1)
    st_s, oa_s, ob3, ck_s, cv_s = _sample_step(
        prm["sinks8"][l], prm["state"][l], to_t(lf_s), to_t(qa_s.astype(F32)), va_s, ga_s, prm["norm_g"][l],
        q3, kb_s, vb_s, prm["cache_k"][l], prm["cache_v"][l])
    ob_s = jnp.concatenate([ob3[:, :B_GROUP, :B_HEAD_DIM], ob3[:, B_GROUP:, B_HEAD_DIM:]], axis=1)
    ob_s = ob_s.reshape(ts, B_WIDTH).astype(BF16)
    h_all = _merge(xs, oa_s, ob_s, sga_s, sgb_s, prm["wa"][l], prm["wb"][l], prm["wo"][l], prm["ln1_g"][l],
                   prm["ln1_b"][l], ts, xs_off, n_all, n_prompt // ts, ts, h_prev=h_all)
    return h_all, (kp, vp, st_p, ck_s, cv_s, st_s)


def _router_kernel(h_ref, wr_ref, bias_ref, e_ref, w_ref, mask_ref, cnt_ref):
    tm = h_ref.shape[0]
    gsz = GROUP_SIZE

    @pl.when(pl.program_id(0) == 0)
    def _():
        cnt_ref[...] = jnp.zeros_like(cnt_ref)

    logits = lax.dot_general(wr_ref[...], h_ref[...], (((1,), (1,)), ((), ())),
                             precision=lax.Precision.HIGHEST, preferred_element_type=F32)
    scores = _sigmoid(logits)
    sel = scores + bias_ref[...][:, 0:1]
    rowi = lax.broadcasted_iota(I32, (gsz, tm), 0)
    neg_inf = -jnp.inf
    blocks = [sel[g * gsz:(g + 1) * gsz] for g in range(N_GROUPS)]
    sblocks = [scores[g * gsz:(g + 1) * gsz] for g in range(N_GROUPS)]

    gscore = []
    for blk in blocks:
        m1 = jnp.max(blk, axis=0, keepdims=True)
        i1 = jnp.min(jnp.where(blk == m1, rowi, gsz), axis=0, keepdims=True)
        m2 = jnp.max(jnp.where(rowi == i1, neg_inf, blk), axis=0, keepdims=True)
        gscore.append(m1 + m2)
    work = []
    for g in range(N_GROUPS):
        ahead = jnp.zeros((1, tm), I32)
        for g2 in range(N_GROUPS):
            if g2 != g:
                beats = (gscore[g2] > gscore[g]) | ((gscore[g2] == gscore[g]) & (g2 < g))
                ahead = ahead + beats.astype(I32)
        work.append(jnp.where(ahead < TOPK_GROUPS, blocks[g], NEG_BIG))

    chosen = [jnp.zeros((gsz, tm), F32) for _ in range(N_GROUPS)]
    es, ws = [], []
    for _ in range(TOP_K):
        m = work[0]
        for g in range(1, N_GROUPS):
            m = jnp.maximum(m, work[g])
        m = jnp.max(m, axis=0, keepdims=True)
        cand = jnp.where(work[0] == m, rowi, N_EXPERTS)
        for g in range(1, N_GROUPS):
            cand = jnp.minimum(cand, jnp.where(work[g] == m, rowi + g * gsz, N_EXPERTS))
        idx = jnp.min(cand, axis=0, keepdims=True)
        wj = jnp.zeros((1, tm), F32)
        for g in range(N_GROUPS):
            hit = (rowi + g * gsz) == idx
            wj = wj + jnp.sum(jnp.where(hit, sblocks[g], 0.0), axis=0, keepdims=True)
            chosen[g] = jnp.where(hit, 1.0, chosen[g])
            work[g] = jnp.where(hit, neg_inf, work[g])
        es.append(idx)
        ws.append(wj)
    wsum = ws[0]
    for j in range(1, TOP_K):
        wsum = wsum + ws[j]
    for j in range(TOP_K):
        e_ref[j:j + 1, :] = es[j]
        w_ref[j:j + 1, :] = ws[j] / wsum * ROUTED_SCALE
    for g in range(N_GROUPS):
        rows = slice(g * gsz, (g + 1) * gsz)
        mask_ref[rows, :] = chosen[g]
        part = chosen[g][:, 0:LANES]
        for c in range(1, tm // LANES):
            part = part + chosen[g][:, c * LANES:(c + 1) * LANES]
        cnt_ref[rows, :] = cnt_ref[rows, :] + part


def _router(h_all, wr_t, bias_b, tm):
    n = h_all.shape[0]
    col = lambda r: pl.BlockSpec((r, tm), lambda i: (0, i))
    return pl.pallas_call(
        _router_kernel,
        grid=(n // tm,),
        in_specs=[pl.BlockSpec((tm, D_MODEL), lambda i: (i, 0)),
                  pl.BlockSpec((N_EXPERTS, D_MODEL), lambda i: (0, 0)),
                  pl.BlockSpec((N_EXPERTS, LANES), lambda i: (0, 0))],
        out_specs=[col(TOP_K), col(TOP_K), col(N_EXPERTS), pl.BlockSpec((N_EXPERTS, LANES), lambda i: (0, 0))],
        out_shape=[jax.ShapeDtypeStruct((TOP_K, n), I32), jax.ShapeDtypeStruct((TOP_K, n), F32),
                   jax.ShapeDtypeStruct((N_EXPERTS, n), F32), jax.ShapeDtypeStruct((N_EXPERTS, LANES), F32)],
        compiler_params=_cparams(("arbitrary",)),
        name="router",
    )(h_all, wr_t, bias_b)


def _rank_kernel(mask_ref, e_ref, offs_ref, triu_ref, dest_ref, carry):
    tm = mask_ref.shape[1]

    @pl.when(pl.program_id(0) == 0)
    def _():
        carry[...] = jnp.zeros_like(carry)

    mk = mask_ref[...]
    rank = _dot(mk.astype(BF16), triu_ref[...])
    dest_full = rank + (offs_ref[...][:, 0:1] + carry[...][:, 0:1])
    rowi = lax.broadcasted_iota(I32, (N_EXPERTS, tm), 0)
    for j in range(TOP_K):
        d = jnp.sum(jnp.where(rowi == e_ref[j:j + 1, :], dest_full, 0.0), axis=0, keepdims=True)
        dest_ref[j:j + 1, :] = d.astype(I32)
    carry[...] = carry[...] + jnp.sum(mk, axis=1, keepdims=True)


def _rank(mask_t, e_t, offs_b, tm):
    n = mask_t.shape[1]
    r = np.arange(tm)
    triu = jnp.asarray((r[:, None] < r[None, :]).astype(np.float32), BF16)
    return pl.pallas_call(
        _rank_kernel,
        grid=(n // tm,),
        in_specs=[pl.BlockSpec((N_EXPERTS, tm), lambda i: (0, i)),
                  pl.BlockSpec((TOP_K, tm), lambda i: (0, i)),
                  pl.BlockSpec((N_EXPERTS, LANES), lambda i: (0, 0)),
                  pl.BlockSpec((tm, tm), lambda i: (0, 0))],
        out_specs=pl.BlockSpec((TOP_K, tm), lambda i: (0, i)),
        out_shape=jax.ShapeDtypeStruct((TOP_K, n), I32),
        scratch_shapes=[pltpu.VMEM((N_EXPERTS, LANES), F32)],
        compiler_params=_cparams(("arbitrary",)),
        name="rank",
    )(mask_t, e_t, offs_b, triu)


def _row_copy(src_ref, src_row, dst_ref, dst_row, sem):
    return pltpu.make_async_copy(src_ref.at[pl.ds(src_row, 1)], dst_ref.at[pl.ds(dst_row, 1)], sem)


def _dispatch_kernel(dest_ref, h_ref, xs_ref, sem):
    tm = dest_ref.shape[1]
    base = pl.program_id(0) * tm

    def issue(t, c):
        for j in range(TOP_K):
            _row_copy(h_ref, base + t, xs_ref, dest_ref[j, t], sem).start()
        return c

    lax.fori_loop(0, tm, issue, 0)

    def drain(t, c):
        for j in range(TOP_K):
            _row_copy(h_ref, base, xs_ref, 0, sem).wait()
        return c

    lax.fori_loop(0, tm, drain, 0)


def _dispatch(dest_t, h_all, tm):
    n = h_all.shape[0]
    return pl.pallas_call(
        _dispatch_kernel,
        grid=(n // tm,),
        in_specs=[pl.BlockSpec((TOP_K, tm), lambda i: (0, i), memory_space=pltpu.SMEM),
                  pl.BlockSpec(memory_space=pl.ANY)],
        out_specs=pl.BlockSpec(memory_space=pl.ANY),
        out_shape=jax.ShapeDtypeStruct((n * TOP_K, D_MODEL), F32),
        scratch_shapes=[pltpu.SemaphoreType.DMA(())],
        compiler_params=_cparams(("arbitrary",)),
        name="dispatch",
    )(dest_t, h_all)


def _expert_kernel(tile_ref, exp_ref, valid_ref, first_ref, lo_ref, hi_ref,
                   x_ref, wg_ref, wu_ref, wd_ref, y_ref):
    i = pl.program_id(0)

    @pl.when(valid_ref[i] == 1)
    def _():
        xb = x_ref[...].astype(BF16)
        gate = _dot(xb, wg_ref[0].astype(BF16))
        up = _dot(xb, wu_ref[0].astype(BF16))
        y = _dot((_silu(gate) * up).astype(BF16), wd_ref[0].astype(BF16))
        rowi = lax.broadcasted_iota(I32, y.shape, 0)
        mine = (rowi >= lo_ref[i]) & (rowi < hi_ref[i])

        @pl.when(first_ref[i] == 1)
        def _():
            y_ref[...] = jnp.where(mine, y, 0.0)

        @pl.when(first_ref[i] == 0)
        def _():
            y_ref[...] = jnp.where(mine, y, y_ref[...])


def _group_metadata(counts, n_rows):
    tmo = EXPERT_ROW_TILE
    n_tiles = n_rows // tmo
    ends = jnp.cumsum(counts)
    offs = ends - counts
    first_tile = offs // tmo
    n_t = jnp.where(counts > 0, (ends - 1) // tmo - first_tile + 1, 0)
    cum = jnp.cumsum(n_t)
    base = cum - n_t
    n_items = n_tiles + N_EXPERTS
    idx = jnp.arange(n_items, dtype=I32)
    valid = (idx < cum[-1]).astype(I32)
    idc = jnp.minimum(idx, cum[-1] - 1)
    e = jnp.minimum(jnp.searchsorted(cum, idc, side="right"), N_EXPERTS - 1).astype(I32)
    tile = (first_tile[e] + idc - base[e]).astype(I32)
    first = jnp.concatenate([jnp.ones((1,), I32), (tile[1:] != tile[:-1]).astype(I32)])
    lo = jnp.clip(offs[e] - tile * tmo, 0, tmo).astype(I32)
    hi = jnp.clip(ends[e] - tile * tmo, 0, tmo).astype(I32)
    return tile, e, valid, first, lo, hi


def _experts(meta, xs, wg, wu, wd):
    tmo = EXPERT_ROW_TILE
    n_rows = xs.shape[0]
    n_items = meta[0].shape[0]
    wspec = lambda shp: pl.BlockSpec((1,) + shp, lambda i, tl, ex, *_: (ex[i], 0, 0))
    return pl.pallas_call(
        _expert_kernel,
        grid_spec=pltpu.PrefetchScalarGridSpec(
            num_scalar_prefetch=6,
            grid=(n_items,),
            in_specs=[pl.BlockSpec((tmo, D_MODEL), lambda i, tl, *_: (tl[i], 0)),
                      wspec((D_MODEL, D_EXPERT)), wspec((D_MODEL, D_EXPERT)), wspec((D_EXPERT, D_MODEL))],
            out_specs=pl.BlockSpec((tmo, D_MODEL), lambda i, tl, *_: (tl[i], 0))),
        out_shape=jax.ShapeDtypeStruct((n_rows, D_MODEL), F32),
        compiler_params=_cparams(("arbitrary",)),
        name="experts",
    )(*meta, xs, wg, wu, wd)


def _post_kernel(dest_ref, h_ref, wt_ref, y_ref, wsg_ref, wsu_ref, wsd_ref, g_ref, b_ref, out_ref, buf, sem):
    tm = h_ref.shape[0]

    def issue(t, c):
        for j in range(TOP_K):
            _row_copy(y_ref, dest_ref[j, t], buf.at[j], t, sem).start()
        return c

    lax.fori_loop(0, tm, issue, 0)

    h = h_ref[...]
    hb = h.astype(BF16)
    shared = _dot((_silu(_dot(hb, wsg_ref[...])) * _dot(hb, wsu_ref[...])).astype(BF16), wsd_ref[...])

    def drain(t, c):
        for j in range(TOP_K):
            _row_copy(y_ref, 0, buf.at[j], 0, sem).wait()
        return c

    lax.fori_loop(0, tm, drain, 0)

    wt = wt_ref[...]
    routed = buf[0] * wt[:, 0:1]
    for j in range(1, TOP_K):
        routed = routed + buf[j] * wt[:, j:j + 1]
    out_ref[...] = _layer_norm(DN_ALPHA * h + (routed + shared), g_ref[...], b_ref[...])


def _post(dest_t, h_all, w_tok, y, wsg, wsu, wsd, g, b, tm):
    n = h_all.shape[0]
    const = lambda a: pl.BlockSpec(a.shape, lambda i: (0, 0))
    return pl.pallas_call(
        _post_kernel,
        grid=(n // tm,),
        in_specs=[pl.BlockSpec((TOP_K, tm), lambda i: (0, i), memory_space=pltpu.SMEM),
                  pl.BlockSpec((tm, D_MODEL), lambda i: (i, 0)),
                  pl.BlockSpec((tm, TOP_K), lambda i: (i, 0)),
                  pl.BlockSpec(memory_space=pl.ANY),
                  const(wsg), const(wsu), const(wsd), const(g), const(b)],
        out_specs=pl.BlockSpec((tm, D_MODEL), lambda i: (i, 0)),
        out_shape=jax.ShapeDtypeStruct((n, D_MODEL), F32),
        scratch_shapes=[pltpu.VMEM((TOP_K, tm, D_MODEL), F32), pltpu.SemaphoreType.DMA(())],
        compiler_params=_cparams(("arbitrary",)),
        name="moe_post",
    )(dest_t, h_all, w_tok, y, wsg, wsu, wsd, g, b)


def _moe_layer(l, h_all, prm):
    tm = MOE_ROW_TILE
    e_t, w_t, mask_t, cnt = _router(h_all, prm["wr_t"][l], prm["rbias"][l], tm)
    counts = jnp.sum(cnt, axis=1).astype(I32)
    offs = jnp.cumsum(counts) - counts
    offs_b = jnp.broadcast_to(offs.astype(F32)[:, None], (N_EXPERTS, LANES))
    dest_t = _rank(mask_t, e_t, offs_b, tm)
    xs = _dispatch(dest_t, h_all, tm)
    meta = _group_metadata(counts, xs.shape[0])
    y = _experts(meta, xs, prm["w_exp_gate"][l], prm["w_exp_up"][l], prm["w_exp_down"][l])
    return _post(dest_t, h_all, w_t.T, y, prm["wsg"][l], prm["wsu"][l], prm["wsd"][l],
                 prm["ln2_g"][l], prm["ln2_b"][l], tm)


def kernel(x_prompt, x_sample, cache_k, cache_v, state_hgrn, w_in, hgrn_lower_bounds, hgrn_norm_g, attn_sinks, w_branch_a, w_branch_b, w_out, ln1_g, ln1_b, w_router, router_bias, w_exp_gate, w_exp_up, w_exp_down, w_sh_gate, w_sh_up, w_sh_down, ln2_g, ln2_b):
    bsz, t, d = x_prompt.shape
    n_sample = x_sample.shape[0] * x_sample.shape[1]
    n_prompt = bsz * t
    depth = w_in.shape[0]
    assert d == D_MODEL and x_sample.shape[1] == 1 and n_prompt % ROW_TILE == 0 and t % ROW_TILE == 0
    assert n_prompt % n_sample == 0 and (n_prompt + n_sample) % MOE_ROW_TILE == 0
    assert ((n_prompt + n_sample) * TOP_K) % EXPERT_ROW_TILE == 0 and n_sample % SAMPLE_BLOCK == 0

    lb_prob = jax.nn.softmax(hgrn_lower_bounds.astype(F32), axis=0)
    lower = (jnp.cumsum(lb_prob, axis=0) - lb_prob[0])[:, None, :]
    cos_p, sin_p = _rope_tables(jnp.arange(t))
    cos_s, sin_s = _rope_tables(jnp.full((n_sample,), PAST_LEN))
    row = lambda a: a[:, None, :]
    prm = dict(
        w_in=w_in.astype(BF16), lower=lower, cos_p=cos_p, sin_p=sin_p, cos_s=cos_s, sin_s=sin_s,
        norm_g=row(hgrn_norm_g), sinks=attn_sinks,
        sinks8=jnp.broadcast_to(attn_sinks[:, :, None], (depth, B_Q_HEADS, LANES)),
        wa=w_branch_a.astype(BF16), wb=w_branch_b.astype(BF16), wo=w_out.astype(BF16),
        ln1_g=row(ln1_g), ln1_b=row(ln1_b), ln2_g=row(ln2_g), ln2_b=row(ln2_b),
        wr_t=jnp.swapaxes(w_router, 1, 2),
        rbias=jnp.broadcast_to(router_bias[:, :, None], (depth, N_EXPERTS, LANES)),
        w_exp_gate=w_exp_gate, w_exp_up=w_exp_up, w_exp_down=w_exp_down,
        wsg=w_sh_gate.astype(BF16), wsu=w_sh_up.astype(BF16), wsd=w_sh_down.astype(BF16),
        state=state_hgrn,
        cache_k=cache_k.reshape(depth, n_sample, WINDOW, B_KV_WIDTH),
        cache_v=cache_v.reshape(depth, n_sample, WINDOW, B_KV_WIDTH),
    )

    xp, xs, xs_off = x_prompt.reshape(n_prompt, d), x_sample.reshape(n_sample, d), 0
    per_layer = []
    for l in range(depth):
        h_all, outs = _mixer_layer(l, xp, xs, xs_off, n_prompt, bsz, t, n_sample, prm)
        y_all = _moe_layer(l, h_all, prm)
        per_layer.append(outs)
        xp, xs, xs_off = y_all, y_all, n_prompt // n_sample

    kv_shape = (n_sample, WINDOW, B_KV_HEADS, B_HEAD_DIM)
    stack = lambda k, f=lambda a: a: jnp.stack([f(o[k]) for o in per_layer])
    return (y_all[:n_prompt].reshape(bsz, t, d), y_all[n_prompt:].reshape(n_sample, 1, d),
            stack(0), stack(1), stack(2),
            stack(3, lambda a: a.reshape(kv_shape)), stack(4, lambda a: a.reshape(kv_shape)), stack(5))
```

```python
import functools

import numpy as np
import jax
import jax.numpy as jnp
from jax import lax
from jax.experimental import pallas as pl
from jax.experimental.pallas import tpu as pltpu

F32 = jnp.float32
BF16 = jnp.bfloat16
I32 = jnp.int32

D_MODEL = 1024
DEPTH = 2
PAST_LEN = 16384
A_HEADS = 4
A_DK = 128
A_DV = 128
A_KEY = A_HEADS * A_DK
A_WIDTH = A_HEADS * A_DV
B_Q_HEADS = 8
B_KV_HEADS = 2
B_HEAD_DIM = 64
B_GROUP = B_Q_HEADS // B_KV_HEADS
B_WIDTH = B_Q_HEADS * B_HEAD_DIM
B_KV_WIDTH = B_KV_HEADS * B_HEAD_DIM
WINDOW = 128
ROPE_THETA = 10000.0
ATTN_SCALE = B_HEAD_DIM ** -0.5
N_EXPERTS = 64
TOP_K = 8
N_GROUPS = 8
GROUP_SIZE = N_EXPERTS // N_GROUPS
TOPK_GROUPS = 4
D_EXPERT = D_MODEL // 4
D_SHARED = D_EXPERT
ROUTED_SCALE = 2.5
DN_ALPHA = (2 * DEPTH) ** 0.25
LN_EPS = 1e-5
RMS_EPS = 1e-6
NEG_BIG = -1e30
TINY = 1.1754944e-38
OFF_AF = A_KEY
OFF_AI = 2 * A_KEY
OFF_AG = OFF_AI + A_WIDTH
OFF_BQ = OFF_AG + A_WIDTH
OFF_BK = OFF_BQ + B_WIDTH
OFF_BV = OFF_BK + B_KV_WIDTH
OFF_GA = OFF_BV + B_KV_WIDTH
OFF_GB = OFF_GA + D_MODEL
IN_COLS = OFF_GB + D_MODEL

LANES = 128
HGRN_CHUNK = 128
HGRN_LEVELS = 7
ROW_TILE = 512
MOE_ROW_TILE = 384
EXPERT_ROW_TILE = 256
VMEM_LIMIT = 56 * 1024 * 1024


def _cparams(sem, vmem=VMEM_LIMIT):
    return pltpu.CompilerParams(dimension_semantics=sem, vmem_limit_bytes=vmem)


def _dot(a, b):
    return jnp.dot(a, b, preferred_element_type=F32)


def _dot_nt(a, b):
    return lax.dot_general(a, b, (((1,), (1,)), ((), ())), preferred_element_type=F32)


def _dot_tn(a, b):
    return lax.dot_general(a, b, (((0,), (0,)), ((), ())), preferred_element_type=F32)


def _sigmoid(x):
    return 1.0 / (1.0 + jnp.exp(-x))


def _silu(x):
    return x * _sigmoid(x)


def _split3(x):
    hi = x.astype(BF16)
    r1 = x - hi.astype(F32)
    mid = r1.astype(BF16)
    lo = (r1 - mid.astype(F32)).astype(BF16)
    return hi, mid, lo


def _layer_norm(y, g, b):
    mu = jnp.mean(y, axis=-1, keepdims=True)
    d = y - mu
    var = jnp.mean(d * d, axis=-1, keepdims=True)
    return d * lax.rsqrt(var + LN_EPS) * g + b


def _log_forget(af, lower):
    ls = jnp.minimum(af, 0.0) - jnp.log1p(jnp.exp(-jnp.abs(af)))
    a = jnp.log(jnp.maximum(lower, TINY))
    b = jnp.log1p(-lower) + ls
    mixed = jnp.maximum(a, b) + jnp.log1p(jnp.exp(-jnp.abs(a - b)))
    return jnp.where(lower > 0.0, mixed, ls)


def _proj_kernel(x_ref, w_ref, cos_ref, sin_ref, low_ref,
                 qa_ref, lf_ref, va_ref, ga_ref, qb_ref, kb_ref, vb_ref, sga_ref, sgb_ref):
    xb = x_ref[...].astype(BF16)

    def mm(c0, n):
        return _dot(xb, w_ref[:, c0:c0 + n])

    qa_ref[...] = _silu(mm(0, A_KEY)).astype(BF16)
    lf_ref[...] = _log_forget(mm(OFF_AF, A_KEY), low_ref[...])
    va_ref[...] = mm(OFF_AI, A_WIDTH).astype(BF16)
    ga_ref[...] = _silu(mm(OFF_AG, A_WIDTH)).astype(BF16)

    cos = cos_ref[...]
    sin = sin_ref[...]
    lane = lax.broadcasted_iota(I32, cos.shape, 1)
    first_half = (lane & (B_HEAD_DIM // 2)) == 0

    def rope(blk):
        partner = jnp.where(first_half, pltpu.roll(blk, LANES - B_HEAD_DIM // 2, 1),
                            pltpu.roll(blk, B_HEAD_DIM // 2, 1))
        return blk * cos + partner * sin

    bq = mm(OFF_BQ, B_WIDTH)
    for j in range(B_WIDTH // LANES):
        sl = slice(j * LANES, (j + 1) * LANES)
        qb_ref[:, sl] = (rope(bq[:, sl]) * ATTN_SCALE).astype(BF16)
    kb_ref[...] = rope(mm(OFF_BK, B_KV_WIDTH))
    vb_ref[...] = mm(OFF_BV, B_KV_WIDTH)
    sga_ref[...] = _sigmoid(mm(OFF_GA, D_MODEL)).astype(BF16)
    sgb_ref[...] = _sigmoid(mm(OFF_GB, D_MODEL)).astype(BF16)


def _proj(x, w_bf, cos_t, sin_t, lower, n_rows, row_off_blocks, tm):
    nt = n_rows // tm
    tab_blocks = cos_t.shape[0] // tm
    row = lambda w: pl.BlockSpec((tm, w), lambda i: (i, 0))
    outs = [(A_KEY, BF16), (A_KEY, F32), (A_WIDTH, BF16), (A_WIDTH, BF16), (B_WIDTH, BF16),
            (B_KV_WIDTH, F32), (B_KV_WIDTH, F32), (D_MODEL, BF16), (D_MODEL, BF16)]
    return pl.pallas_call(
        _proj_kernel,
        grid=(nt,),
        in_specs=[pl.BlockSpec((tm, D_MODEL), lambda i: (i + row_off_blocks, 0)),
                  pl.BlockSpec((D_MODEL, IN_COLS), lambda i: (0, 0)),
                  pl.BlockSpec((tm, LANES), lambda i: (i % tab_blocks, 0)),
                  pl.BlockSpec((tm, LANES), lambda i: (i % tab_blocks, 0)),
                  pl.BlockSpec((1, A_KEY), lambda i: (0, 0))],
        out_specs=[row(w) for w, _ in outs],
        out_shape=[jax.ShapeDtypeStruct((n_rows, w), dt) for w, dt in outs],
        compiler_params=_cparams(("arbitrary",)),
        name="proj",
    )(x, w_bf, cos_t, sin_t, lower)


def _rope_tables(pos):
    half = B_HEAD_DIM // 2
    inv = ROPE_THETA ** (-jnp.arange(half, dtype=F32) / half)
    ang = pos.astype(F32)[:, None] * inv[None, :]
    cos = jnp.cos(ang)
    sin = jnp.sin(ang)
    reps = LANES // B_HEAD_DIM
    cos_t = jnp.tile(jnp.concatenate([cos, cos], axis=1), (1, reps))
    sin_t = jnp.tile(jnp.concatenate([-sin, sin], axis=1), (1, reps))
    return cos_t, sin_t


def _hgrn_constants():
    c = HGRN_CHUNK
    r = np.arange(c)
    tri = (r[None, :] <= r[:, None]).astype(np.float32)
    sel = np.zeros((HGRN_LEVELS, c, c), np.float32)
    upper = np.zeros((HGRN_LEVELS, c, LANES), np.float32)
    pair = np.zeros((HGRN_LEVELS + 1, c, c), np.float32)
    for l in range(HGRN_LEVELS):
        b = c >> (l + 1)
        ref_row = (r // (2 * b)) * (2 * b) + b - 1
        sel[l, r, ref_row] = 1.0
        up = (r % (2 * b)) >= b
        upper[l] = up[:, None]
        same = (r[:, None] // (2 * b)) == (r[None, :] // (2 * b))
        pair[l] = (up[:, None] & ~up[None, :] & same)
    pair[HGRN_LEVELS] = np.eye(c)
    return (jnp.asarray(tri, BF16), jnp.asarray(sel.reshape(HGRN_LEVELS * c, c), BF16),
            jnp.asarray(upper), jnp.asarray(pair))


def _hgrn_kernel(qa_ref, lf_ref, va_ref, ga_ref, g_ref, tri_ref, sel_ref, up_ref, pair_ref,
                 oa_ref, st_ref, s_scr):
    c = HGRN_CHUNK
    step = pl.program_id(1)

    @pl.when(step == 0)
    def _():
        s_scr[...] = jnp.zeros_like(s_scr)

    tri = tri_ref[...]
    sel = sel_ref[...]
    for h in range(A_HEADS):
        sl = slice(h * A_DK, (h + 1) * A_DK)
        lf = lf_ref[:, sl]
        hi, mid, lo = _split3(lf)
        gcum = _dot(tri, hi) + _dot(tri, mid) + _dot(tri, lo)
        ghi, gmid, glo = _split3(gcum)
        qb = qa_ref[:, sl]
        qf = qb.astype(F32)
        kf = 1.0 - jnp.exp(lf)
        vb = va_ref[:, sl]
        att = _dot_nt(qb, kf.astype(BF16)) * pair_ref[HGRN_LEVELS]
        for l in range(HGRN_LEVELS):
            sl_rows = slice(l * c, (l + 1) * c)
            gref = _dot(sel[sl_rows], ghi) + _dot(sel[sl_rows], gmid) + _dot(sel[sl_rows], glo)
            e = jnp.exp(-jnp.abs(gcum - gref))
            w = (jnp.where(up_ref[l] > 0.5, qf, kf) * e).astype(BF16)
            att = att + _dot_nt(w, w) * pair_ref[l]
        s_t = s_scr[h]
        o = _dot_nt((qf * jnp.exp(gcum)).astype(BF16), s_t.astype(BF16)) + _dot(att.astype(BF16), vb)
        ms = jnp.mean(o * o, axis=-1, keepdims=True)
        on = o * lax.rsqrt(ms + RMS_EPS) * g_ref[...] * ga_ref[:, sl].astype(F32)
        oa_ref[:, sl] = on.astype(BF16)
        gend = gcum[c - 1:c, :]
        kend = (kf * jnp.exp(gend - gcum)).astype(BF16)
        s_scr[h] = s_t * jnp.exp(gend) + _dot_tn(vb, kend)

    @pl.when(step == pl.num_programs(1) - 1)
    def _():
        for h in range(A_HEADS):
            st_ref[0, h] = s_scr[h].T


def _hgrn_prompt(qa, lf, va, ga, norm_g, bsz, t):
    c = HGRN_CHUNK
    nc = t // c
    tri, sel, upper, pair = _hgrn_constants()
    blk = lambda: pl.BlockSpec((c, A_KEY), lambda b, i: (b * nc + i, 0))
    const = lambda a: pl.BlockSpec(a.shape, lambda b, i: (0,) * a.ndim)
    return pl.pallas_call(
        _hgrn_kernel,
        grid=(bsz, nc),
        in_specs=[blk(), blk(), blk(), blk(), pl.BlockSpec((1, A_DV), lambda b, i: (0, 0)),
                  const(tri), const(sel), const(upper), const(pair)],
        out_specs=[blk(), pl.BlockSpec((1, A_HEADS, A_DK, A_DV), lambda b, i: (b, 0, 0, 0))],
        out_shape=[jax.ShapeDtypeStruct((bsz * t, A_WIDTH), BF16),
                   jax.ShapeDtypeStruct((bsz, A_HEADS, A_DK, A_DV), F32)],
        scratch_shapes=[pltpu.VMEM((A_HEADS, A_DV, A_DK), F32)],
        compiler_params=_cparams(("arbitrary", "arbitrary")),
        name="hgrn_prompt",
    )(qa, lf, va, ga, norm_g, tri, sel, upper, pair)


def _swa_kernel(sink_ref, q_ref, k_ref, v_ref, o_ref, kprev, vprev):
    w = WINDOW
    i = pl.program_id(1)

    @pl.when(i == 0)
    def _():
        kprev[...] = jnp.zeros_like(kprev)
        vprev[...] = jnp.zeros_like(vprev)

    kc = k_ref[...]
    vc = v_ref[...]
    kk = jnp.concatenate([kprev[...], kc], axis=0)
    vv = jnp.concatenate([vprev[...], vc], axis=0)
    kr = pltpu.roll(kk, B_HEAD_DIM, 1)
    vr = pltpu.roll(vv, B_HEAD_DIM, 1)
    lo2 = lax.broadcasted_iota(I32, kk.shape, 1) < B_HEAD_DIM
    zero = jnp.zeros_like(kk)
    k_lo = [jnp.where(lo2, kk, zero).astype(BF16), jnp.where(lo2, kr, zero).astype(BF16)]
    k_hi = [jnp.where(lo2, zero, kr).astype(BF16), jnp.where(lo2, zero, kk).astype(BF16)]
    v_dup = [jnp.where(lo2, vv, vr).astype(BF16), jnp.where(lo2, vr, vv).astype(BF16)]

    qi = lax.broadcasted_iota(I32, (w, 2 * w), 0)
    kj = lax.broadcasted_iota(I32, (w, 2 * w), 1)
    valid = (kj >= qi) & (kj <= qi + w) & ((kj >= w) | (i > 0))
    lo1 = lax.broadcasted_iota(I32, (w, LANES), 1) < B_HEAD_DIM

    for j in range(B_WIDTH // LANES):
        g = (2 * j) // B_GROUP
        qblk = q_ref[:, j * LANES:(j + 1) * LANES]
        res = []
        for half, kmat in enumerate((k_lo[g], k_hi[g])):
            sk = sink_ref[2 * j + half]
            s = jnp.where(valid, _dot_nt(qblk, kmat), NEG_BIG)
            m = jnp.maximum(jnp.max(s, axis=-1, keepdims=True), sk)
            p = jnp.exp(s - m)
            denom = jnp.sum(p, axis=-1, keepdims=True) + jnp.exp(sk - m)
            res.append(_dot(p.astype(BF16), v_dup[g]) / denom)
        o_ref[:, j * LANES:(j + 1) * LANES] = jnp.where(lo1, res[0], res[1]).astype(BF16)

    kprev[...] = kc
    vprev[...] = vc


def _swa_prompt(sinks, qb, kb, vb, bsz, t):
    w = WINDOW
    nb = t // w
    return pl.pallas_call(
        _swa_kernel,
        grid_spec=pltpu.PrefetchScalarGridSpec(
            num_scalar_prefetch=1,
            grid=(bsz, nb),
            in_specs=[pl.BlockSpec((w, B_WIDTH), lambda b, i, s: (b * nb + i, 0)),
                      pl.BlockSpec((w, B_KV_WIDTH), lambda b, i, s: (b * nb + i, 0)),
                      pl.BlockSpec((w, B_KV_WIDTH), lambda b, i, s: (b * nb + i, 0))],
            out_specs=pl.BlockSpec((w, B_WIDTH), lambda b, i, s: (b * nb + i, 0)),
            scratch_shapes=[pltpu.VMEM((w, B_KV_WIDTH), F32), pltpu.VMEM((w, B_KV_WIDTH), F32)]),
        out_shape=jax.ShapeDtypeStruct((bsz * t, B_WIDTH), BF16),
        compiler_params=_cparams(("arbitrary", "arbitrary")),
        name="swa_prompt",
    )(sinks, qb, kb, vb)


SAMPLE_BLOCK = 8


def _sample_kernel(sink_ref, st_ref, lft_ref, qat_ref, va_ref, ga_ref, g_ref, q3_ref, kn_ref, vn_ref,
                   ck_ref, cv_ref, st_out, oa_ref, ob_ref, ck_out, cv_out, o_scr):
    w = WINDOW
    row = lax.broadcasted_iota(I32, (w, B_KV_WIDTH), 0)
    for i in range(SAMPLE_BLOCK):
        for h in range(A_HEADS):
            sl = slice(h * A_DV, (h + 1) * A_DV)
            fcol = jnp.exp(lft_ref[h, 0][:, i:i + 1])
            qcol = qat_ref[h, 0][:, i:i + 1]
            vrow = va_ref[i:i + 1, sl].astype(F32)
            s_new = st_ref[i, h] * fcol + (1.0 - fcol) * vrow
            st_out[i, h] = s_new
            o_scr[i:i + 1, sl] = jnp.sum(s_new * qcol, axis=0, keepdims=True)
        kc = ck_ref[i]
        vc = cv_ref[i]
        kn = kn_ref[i:i + 1, :]
        vn = vn_ref[i:i + 1, :]
        q3 = q3_ref[i]
        s = _dot_nt(q3.astype(BF16), kc.astype(BF16))
        s_new_key = jnp.sum(q3 * kn, axis=-1, keepdims=True)
        sk = sink_ref[...][:, 0:1]
        m = jnp.maximum(jnp.maximum(jnp.max(s, axis=-1, keepdims=True), s_new_key), sk)
        p = jnp.exp(s - m)
        pn = jnp.exp(s_new_key - m)
        denom = jnp.sum(p, axis=-1, keepdims=True) + pn + jnp.exp(sk - m)
        ob_ref[i] = (_dot(p.astype(BF16), vc.astype(BF16)) + pn * vn) / denom
        ck_out[i] = jnp.where(row == w - 1, kn, pltpu.roll(kc, w - 1, 0))
        cv_out[i] = jnp.where(row == w - 1, vn, pltpu.roll(vc, w - 1, 0))
    for h in range(A_HEADS):
        sl = slice(h * A_DV, (h + 1) * A_DV)
        o = o_scr[:, sl]
        ms = jnp.mean(o * o, axis=-1, keepdims=True)
        oa_ref[:, sl] = (o * lax.rsqrt(ms + RMS_EPS) * g_ref[...] * ga_ref[:, sl].astype(F32)).astype(BF16)


def _sample_step(sinks8, state, lft, qat, va, ga, norm_g, q3, kn, vn, ck, cv):
    nb = state.shape[0]
    sb = SAMPLE_BLOCK
    steps = nb // sb
    w = WINDOW
    b4 = lambda: pl.BlockSpec((sb, A_HEADS, A_DK, A_DV), lambda i: (i, 0, 0, 0))
    t4 = lambda: pl.BlockSpec((A_HEADS, 1, A_DK, sb), lambda i: (0, i, 0, 0))
    r2 = lambda wd: pl.BlockSpec((sb, wd), lambda i: (i, 0))
    c3 = lambda: pl.BlockSpec((sb, w, B_KV_WIDTH), lambda i: (i, 0, 0))
    return pl.pallas_call(
        _sample_kernel,
        grid=(steps,),
        in_specs=[pl.BlockSpec((B_Q_HEADS, LANES), lambda i: (0, 0)),
                  b4(), t4(), t4(), r2(A_WIDTH), r2(A_WIDTH), pl.BlockSpec((1, A_DV), lambda i: (0, 0)),
                  pl.BlockSpec((sb, B_Q_HEADS, LANES), lambda i: (i, 0, 0)), r2(B_KV_WIDTH), r2(B_KV_WIDTH),
                  c3(), c3()],
        out_specs=[b4(), r2(A_WIDTH), pl.BlockSpec((sb, B_Q_HEADS, LANES), lambda i: (i, 0, 0)), c3(), c3()],
        out_shape=[jax.ShapeDtypeStruct(state.shape, F32),
                   jax.ShapeDtypeStruct((nb, A_WIDTH), BF16),
                   jax.ShapeDtypeStruct((nb, B_Q_HEADS, LANES), F32),
                   jax.ShapeDtypeStruct((nb, w, B_KV_WIDTH), F32),
                   jax.ShapeDtypeStruct((nb, w, B_KV_WIDTH), F32)],
        scratch_shapes=[pltpu.VMEM((sb, A_WIDTH), F32)],
        compiler_params=_cparams(("arbitrary",)),
        name="sample_step",
    )(sinks8, state, lft, qat, va, ga, norm_g, q3, kn, vn, ck, cv)


def _merge_kernel(x_ref, oa_ref, ob_ref, sga_ref, sgb_ref, wa_ref, wb_ref, wo_ref, g_ref, b_ref, h_ref):
    merged = (sga_ref[...].astype(F32) * _dot(oa_ref[...], wa_ref[...])
              + sgb_ref[...].astype(F32) * _dot(ob_ref[...], wb_ref[...]))
    mix = _dot(merged.astype(BF16), wo_ref[...])
    h_ref[...] = _layer_norm(DN_ALPHA * x_ref[...] + mix, g_ref[...], b_ref[...])


def _merge(x, oa, ob, sga, sgb, wa, wb, wo, g, b, n_rows, x_off_blocks, out_rows, out_off_blocks, tm, h_prev=None):
    nt = n_rows // tm
    row = lambda wd: pl.BlockSpec((tm, wd), lambda i: (i, 0))
    const = lambda a: pl.BlockSpec(a.shape, lambda i: (0, 0))
    args = [x, oa, ob, sga, sgb, wa, wb, wo, g, b]
    in_specs = [pl.BlockSpec((tm, D_MODEL), lambda i: (i + x_off_blocks, 0)),
                row(A_WIDTH), row(B_WIDTH), row(D_MODEL), row(D_MODEL),
                const(wa), const(wb), const(wo), const(g), const(b)]
    kern = _merge_kernel
    aliases = {}
    if h_prev is not None:
        args.append(h_prev)
        in_specs.append(pl.BlockSpec(memory_space=pl.ANY))
        aliases = {len(args) - 1: 0}
        kern = lambda *refs: _merge_kernel(*refs[:10], refs[11])
    return pl.pallas_call(
        kern,
        grid=(nt,),
        in_specs=in_specs,
        out_specs=pl.BlockSpec((tm, D_MODEL), lambda i: (i + out_off_blocks, 0)),
        out_shape=jax.ShapeDtypeStruct((out_rows, D_MODEL), F32),
        input_output_aliases=aliases,
        compiler_params=_cparams(("arbitrary",)),
        name="merge",
    )(*args)


def _mixer_layer(l, xp, xs, xs_off, n_prompt, bsz, t, n_sample, prm):
    tm = ROW_TILE
    p = _proj(xp, prm["w_in"][l], prm["cos_p"], prm["sin_p"], prm["lower"][l], n_prompt, 0, tm)
    qa, lf, va, ga, qb, kb, vb, sga, sgb = p
    oa, st_p = _hgrn_prompt(qa, lf, va, ga, prm["norm_g"][l], bsz, t)
    ob = _swa_prompt(prm["sinks"][l], qb, kb, vb, bsz, t)
    n_all = n_prompt + n_sample
    h_all = _merge(xp, oa, ob, sga, sgb, prm["wa"][l], prm["wb"][l], prm["wo"][l], prm["ln1_g"][l],
                   prm["ln1_b"][l], n_prompt, 0, n_all, 0, tm, h_prev=jnp.zeros((n_all, D_MODEL), F32))
    kp = kb.reshape(bsz, t, B_KV_HEADS, B_HEAD_DIM)[:, -WINDOW:]
    vp = vb.reshape(bsz, t, B_KV_HEADS, B_HEAD_DIM)[:, -WINDOW:]

    ts = n_sample
    ps = _proj(xs, prm["w_in"][l], prm["cos_s"], prm["sin_s"], prm["lower"][l], ts, xs_off, ts)
    qa_s, lf_s, va_s, ga_s, qb_s, kb_s, vb_s, sga_s, sgb_s = ps
    sb = SAMPLE_BLOCK
    to_t = lambda a: a.reshape(ts // sb, sb, A_HEADS, A_DK).transpose(2, 0, 3, 1)
    qh = qb_s.astype(F32).reshape(ts, B_Q_HEADS, B_HEAD_DIM)
    z = jnp.zeros_like(qh[:, :B_GROUP])
    q3 = jnp.concatenate([jnp.concatenate([qh[:, :B_GROUP], z], axis=-1),
                          jnp.concatenate([z, qh[:, B_GROUP:]], axis=-1)], axis=1)
    st_s, oa_s, ob3, ck_s, cv_s = _sample_step(
        prm["sinks8"][l], prm["state"][l], to_t(lf_s), to_t(qa_s.astype(F32)), va_s, ga_s, prm["norm_g"][l],
        q3, kb_s, vb_s, prm["cache_k"][l], prm["cache_v"][l])
    ob_s = jnp.concatenate([ob3[:, :B_GROUP, :B_HEAD_DIM], ob3[:, B_GROUP:, B_HEAD_DIM:]], axis=1)
    ob_s = ob_s.reshape(ts, B_WIDTH).astype(BF16)
    h_all = _merge(xs, oa_s, ob_s, sga_s, sgb_s, prm["wa"][l], prm["wb"][l], prm["wo"][l], prm["ln1_g"][l],
                   prm["ln1_b"][l], ts, xs_off, n_all, n_prompt // ts, ts, h_prev=h_all)
    return h_all, (kp, vp, st_p, ck_s, cv_s, st_s)


def _router_kernel(h_ref, wr_ref, bias_ref, e_ref, w_ref, mask_ref, cnt_ref):
    tm = h_ref.shape[0]
    gsz = GROUP_SIZE

    @pl.when(pl.program_id(0) == 0)
    def _():
        cnt_ref[...] = jnp.zeros_like(cnt_ref)

    logits = lax.dot_general(wr_ref[...], h_ref[...], (((1,), (1,)), ((), ())),
                             precision=lax.Precision.HIGHEST, preferred_element_type=F32)
    scores = _sigmoid(logits)
    sel = scores + bias_ref[...][:, 0:1]
    rowi = lax.broadcasted_iota(I32, (gsz, tm), 0)
    neg_inf = -jnp.inf
    blocks = [sel[g * gsz:(g + 1) * gsz] for g in range(N_GROUPS)]
    sblocks = [scores[g * gsz:(g + 1) * gsz] for g in range(N_GROUPS)]

    gscore = []
    for blk in blocks:
        m1 = jnp.max(blk, axis=0, keepdims=True)
        i1 = jnp.min(jnp.where(blk == m1, rowi, gsz), axis=0, keepdims=True)
        m2 = jnp.max(jnp.where(rowi == i1, neg_inf, blk), axis=0, keepdims=True)
        gscore.append(m1 + m2)
    work = []
    for g in range(N_GROUPS):
        ahead = jnp.zeros((1, tm), I32)
        for g2 in range(N_GROUPS):
            if g2 != g:
                beats = (gscore[g2] > gscore[g]) | ((gscore[g2] == gscore[g]) & (g2 < g))
                ahead = ahead + beats.astype(I32)
        work.append(jnp.where(ahead < TOPK_GROUPS, blocks[g], NEG_BIG))

    chosen = [jnp.zeros((gsz, tm), F32) for _ in range(N_GROUPS)]
    es, ws = [], []
    for _ in range(TOP_K):
        m = work[0]
        for g in range(1, N_GROUPS):
            m = jnp.maximum(m, work[g])
        m = jnp.max(m, axis=0, keepdims=True)
        cand = jnp.where(work[0] == m, rowi, N_EXPERTS)
        for g in range(1, N_GROUPS):
            cand = jnp.minimum(cand, jnp.where(work[g] == m, rowi + g * gsz, N_EXPERTS))
        idx = jnp.min(cand, axis=0, keepdims=True)
        wj = jnp.zeros((1, tm), F32)
        for g in range(N_GROUPS):
            hit = (rowi + g * gsz) == idx
            wj = wj + jnp.sum(jnp.where(hit, sblocks[g], 0.0), axis=0, keepdims=True)
            chosen[g] = jnp.where(hit, 1.0, chosen[g])
            work[g] = jnp.where(hit, neg_inf, work[g])
        es.append(idx)
        ws.append(wj)
    wsum = ws[0]
    for j in range(1, TOP_K):
        wsum = wsum + ws[j]
    for j in range(TOP_K):
        e_ref[j:j + 1, :] = es[j]
        w_ref[j:j + 1, :] = ws[j] / wsum * ROUTED_SCALE
    for g in range(N_GROUPS):
        rows = slice(g * gsz, (g + 1) * gsz)
        mask_ref[rows, :] = chosen[g]
        part = chosen[g][:, 0:LANES]
        for c in range(1, tm // LANES):
            part = part + chosen[g][:, c * LANES:(c + 1) * LANES]
        cnt_ref[rows, :] = cnt_ref[rows, :] + part


def _router(h_all, wr_t, bias_b, tm):
    n = h_all.shape[0]
    col = lambda r: pl.BlockSpec((r, tm), lambda i: (0, i))
    return pl.pallas_call(
        _router_kernel,
        grid=(n // tm,),
        in_specs=[pl.BlockSpec((tm, D_MODEL), lambda i: (i, 0)),
                  pl.BlockSpec((N_EXPERTS, D_MODEL), lambda i: (0, 0)),
                  pl.BlockSpec((N_EXPERTS, LANES), lambda i: (0, 0))],
        out_specs=[col(TOP_K), col(TOP_K), col(N_EXPERTS), pl.BlockSpec((N_EXPERTS, LANES), lambda i: (0, 0))],
        out_shape=[jax.ShapeDtypeStruct((TOP_K, n), I32), jax.ShapeDtypeStruct((TOP_K, n), F32),
                   jax.ShapeDtypeStruct((N_EXPERTS, n), F32), jax.ShapeDtypeStruct((N_EXPERTS, LANES), F32)],
        compiler_params=_cparams(("arbitrary",)),
        name="router",
    )(h_all, wr_t, bias_b)


def _rank_kernel(mask_ref, e_ref, offs_ref, triu_ref, dest_ref, carry):
    tm = mask_ref.shape[1]

    @pl.when(pl.program_id(0) == 0)
    def _():
        carry[...] = jnp.zeros_like(carry)

    mk = mask_ref[...]
    rank = _dot(mk.astype(BF16), triu_ref[...])
    dest_full = rank + (offs_ref[...][:, 0:1] + carry[...][:, 0:1])
    rowi = lax.broadcasted_iota(I32, (N_EXPERTS, tm), 0)
    for j in range(TOP_K):
        d = jnp.sum(jnp.where(rowi == e_ref[j:j + 1, :], dest_full, 0.0), axis=0, keepdims=True)
        dest_ref[j:j + 1, :] = d.astype(I32)
    carry[...] = carry[...] + jnp.sum(mk, axis=1, keepdims=True)


def _rank(mask_t, e_t, offs_b, tm):
    n = mask_t.shape[1]
    r = np.arange(tm)
    triu = jnp.asarray((r[:, None] < r[None, :]).astype(np.float32), BF16)
    return pl.pallas_call(
        _rank_kernel,
        grid=(n // tm,),
        in_specs=[pl.BlockSpec((N_EXPERTS, tm), lambda i: (0, i)),
                  pl.BlockSpec((TOP_K, tm), lambda i: (0, i)),
                  pl.BlockSpec((N_EXPERTS, LANES), lambda i: (0, 0)),
                  pl.BlockSpec((tm, tm), lambda i: (0, 0))],
        out_specs=pl.BlockSpec((TOP_K, tm), lambda i: (0, i)),
        out_shape=jax.ShapeDtypeStruct((TOP_K, n), I32),
        scratch_shapes=[pltpu.VMEM((N_EXPERTS, LANES), F32)],
        compiler_params=_cparams(("arbitrary",)),
        name="rank",
    )(mask_t, e_t, offs_b, triu)


INV_COLS = 512
TOKEN_RADIX = 128


def _inverse_kernel(dest_ref, inv_ref):
    tm = dest_ref.shape[1]
    nq = inv_ref.shape[0]

    @pl.when(pl.program_id(0) == 0)
    def _():
        inv_ref[...] = jnp.zeros_like(inv_ref)

    tok = pl.program_id(0) * tm + lax.broadcasted_iota(I32, (1, tm), 1)
    t_hi = jnp.right_shift(tok, TOKEN_RADIX.bit_length() - 1).astype(F32)
    t_lo = jnp.bitwise_and(tok, TOKEN_RADIX - 1).astype(F32)
    qi = lax.broadcasted_iota(I32, (nq, tm), 0)
    si = lax.broadcasted_iota(I32, (INV_COLS, tm), 0)
    acc_hi = jnp.zeros(inv_ref.shape, F32)
    acc_lo = jnp.zeros(inv_ref.shape, F32)
    for j in range(TOP_K):
        d = dest_ref[j:j + 1, :]
        at_q = qi == jnp.right_shift(d, INV_COLS.bit_length() - 1)
        col = jnp.where(si == jnp.bitwise_and(d, INV_COLS - 1), 1.0, 0.0).astype(BF16)
        acc_hi = acc_hi + _dot_nt(jnp.where(at_q, t_hi, 0.0).astype(BF16), col)
        acc_lo = acc_lo + _dot_nt(jnp.where(at_q, t_lo, 0.0).astype(BF16), col)
    inv_ref[...] = inv_ref[...] + (acc_hi * float(TOKEN_RADIX) + acc_lo)


def _inverse_map(dest_t, tm):
    n = dest_t.shape[1]
    n_rows = n * TOP_K
    nq = -(-(n_rows // INV_COLS) // 8) * 8
    inv = pl.pallas_call(
        _inverse_kernel,
        grid=(n // tm,),
        in_specs=[pl.BlockSpec((TOP_K, tm), lambda i: (0, i))],
        out_specs=pl.BlockSpec((nq, INV_COLS), lambda i: (0, 0)),
        out_shape=jax.ShapeDtypeStruct((nq, INV_COLS), F32),
        compiler_params=_cparams(("arbitrary",)),
        name="inverse_map",
    )(dest_t)
    return inv.reshape(-1)[:n_rows].astype(I32).reshape(n_rows // EXPERT_ROW_TILE, 1, EXPERT_ROW_TILE)


def _row_copy(src_ref, src_row, dst_ref, dst_row, sem):
    return pltpu.make_async_copy(src_ref.at[pl.ds(src_row, 1)], dst_ref.at[pl.ds(dst_row, 1)], sem)


def _expert_kernel(tile_ref, exp_ref, valid_ref, first_ref, lo_ref, hi_ref,
                   inv_ref, h_ref, wg_ref, wu_ref, wd_ref, y_ref, xbuf, sem):
    i = pl.program_id(0)
    rows = xbuf.shape[0]

    @pl.when((valid_ref[i] == 1) & (first_ref[i] == 1))
    def _():
        def issue(r, c):
            _row_copy(h_ref, inv_ref[0, 0, r], xbuf, r, sem).start()
            return c

        lax.fori_loop(0, rows, issue, 0)

        def drain(r, c):
            _row_copy(h_ref, 0, xbuf, 0, sem).wait()
            return c

        lax.fori_loop(0, rows, drain, 0)

    @pl.when(valid_ref[i] == 1)
    def _():
        xb = xbuf[...].astype(BF16)
        gate = _dot(xb, wg_ref[0].astype(BF16))
        up = _dot(xb, wu_ref[0].astype(BF16))
        y = _dot((_silu(gate) * up).astype(BF16), wd_ref[0].astype(BF16))
        rowi = lax.broadcasted_iota(I32, y.shape, 0)
        mine = (rowi >= lo_ref[i]) & (rowi < hi_ref[i])

        @pl.when(first_ref[i] == 1)
        def _():
            y_ref[...] = jnp.where(mine, y, 0.0)

        @pl.when(first_ref[i] == 0)
        def _():
            y_ref[...] = jnp.where(mine, y, y_ref[...])


def _group_metadata(counts, n_rows):
    tmo = EXPERT_ROW_TILE
    n_tiles = n_rows // tmo
    ends = jnp.cumsum(counts)
    offs = ends - counts
    first_tile = offs // tmo
    n_t = jnp.where(counts > 0, (ends - 1) // tmo - first_tile + 1, 0)
    cum = jnp.cumsum(n_t)
    base = cum - n_t
    n_items = n_tiles + N_EXPERTS
    idx = jnp.arange(n_items, dtype=I32)
    valid = (idx < cum[-1]).astype(I32)
    idc = jnp.minimum(idx, cum[-1] - 1)
    e = jnp.minimum(jnp.sum((cum[None, :] <= idc[:, None]).astype(I32), axis=1), N_EXPERTS - 1)
    tile = (first_tile[e] + idc - base[e]).astype(I32)
    first = jnp.concatenate([jnp.ones((1,), I32), (tile[1:] != tile[:-1]).astype(I32)])
    lo = jnp.clip(offs[e] - tile * tmo, 0, tmo).astype(I32)
    hi = jnp.clip(ends[e] - tile * tmo, 0, tmo).astype(I32)
    return tile, e, valid, first, lo, hi


def _experts(meta, inv3, h_all, wg, wu, wd):
    tmo = EXPERT_ROW_TILE
    n_rows = inv3.shape[0] * tmo
    n_items = meta[0].shape[0]
    wspec = lambda shp: pl.BlockSpec((1,) + shp, lambda i, tl, ex, *_: (ex[i], 0, 0))
    return pl.pallas_call(
        _expert_kernel,
        grid_spec=pltpu.PrefetchScalarGridSpec(
            num_scalar_prefetch=6,
            grid=(n_items,),
            in_specs=[pl.BlockSpec((1, 1, tmo), lambda i, tl, *_: (tl[i], 0, 0), memory_space=pltpu.SMEM),
                      pl.BlockSpec(memory_space=pl.ANY),
                      wspec((D_MODEL, D_EXPERT)), wspec((D_MODEL, D_EXPERT)), wspec((D_EXPERT, D_MODEL))],
            out_specs=pl.BlockSpec((tmo, D_MODEL), lambda i, tl, *_: (tl[i], 0)),
            scratch_shapes=[pltpu.VMEM((tmo, D_MODEL), F32), pltpu.SemaphoreType.DMA(())]),
        out_shape=jax.ShapeDtypeStruct((n_rows, D_MODEL), F32),
        compiler_params=_cparams(("arbitrary",)),
        name="experts",
    )(*meta, inv3, h_all, wg, wu, wd)


def _post_kernel(dest_ref, h_ref, wt_ref, y_ref, wsg_ref, wsu_ref, wsd_ref, g_ref, b_ref, out_ref, buf, sem):
    tm = h_ref.shape[0]

    def issue(t, c):
        for j in range(TOP_K):
            _row_copy(y_ref, dest_ref[j, t], buf.at[j], t, sem).start()
        return c

    lax.fori_loop(0, tm, issue, 0)

    h = h_ref[...]
    hb = h.astype(BF16)
    shared = _dot((_silu(_dot(hb, wsg_ref[...])) * _dot(hb, wsu_ref[...])).astype(BF16), wsd_ref[...])

    def drain(t, c):
        for j in range(TOP_K):
            _row_copy(y_ref, 0, buf.at[j], 0, sem).wait()
        return c

    lax.fori_loop(0, tm, drain, 0)

    wt = wt_ref[...]
    routed = buf[0] * wt[:, 0:1]
    for j in range(1, TOP_K):
        routed = routed + buf[j] * wt[:, j:j + 1]
    out_ref[...] = _layer_norm(DN_ALPHA * h + (routed + shared), g_ref[...], b_ref[...])


def _post(dest_t, h_all, w_tok, y, wsg, wsu, wsd, g, b, tm):
    n = h_all.shape[0]
    const = lambda a: pl.BlockSpec(a.shape, lambda i: (0, 0))
    return pl.pallas_call(
        _post_kernel,
        grid=(n // tm,),
        in_specs=[pl.BlockSpec((TOP_K, tm), lambda i: (0, i), memory_space=pltpu.SMEM),
                  pl.BlockSpec((tm, D_MODEL), lambda i: (i, 0)),
                  pl.BlockSpec((tm, TOP_K), lambda i: (i, 0)),
                  pl.BlockSpec(memory_space=pl.ANY),
                  const(wsg), const(wsu), const(wsd), const(g), const(b)],
        out_specs=pl.BlockSpec((tm, D_MODEL), lambda i: (i, 0)),
        out_shape=jax.ShapeDtypeStruct((n, D_MODEL), F32),
        scratch_shapes=[pltpu.VMEM((TOP_K, tm, D_MODEL), F32), pltpu.SemaphoreType.DMA(())],
        compiler_params=_cparams(("arbitrary",)),
        name="moe_post",
    )(dest_t, h_all, w_tok, y, wsg, wsu, wsd, g, b)


def _moe_layer(l, h_all, prm):
    tm = MOE_ROW_TILE
    e_t, w_t, mask_t, cnt = _router(h_all, prm["wr_t"][l], prm["rbias"][l], tm)
    counts = jnp.sum(cnt, axis=1).astype(I32)
    offs = jnp.cumsum(counts) - counts
    offs_b = jnp.broadcast_to(offs.astype(F32)[:, None], (N_EXPERTS, LANES))
    dest_t = _rank(mask_t, e_t, offs_b, tm)
    inv3 = _inverse_map(dest_t, tm)
    meta = _group_metadata(counts, h_all.shape[0] * TOP_K)
    y = _experts(meta, inv3, h_all, prm["w_exp_gate"][l], prm["w_exp_up"][l], prm["w_exp_down"][l])
    return _post(dest_t, h_all, w_t.T, y, prm["wsg"][l], prm["wsu"][l], prm["wsd"][l],
                 prm["ln2_g"][l], prm["ln2_b"][l], tm)


def kernel(x_prompt, x_sample, cache_k, cache_v, state_hgrn, w_in, hgrn_lower_bounds, hgrn_norm_g, attn_sinks, w_branch_a, w_branch_b, w_out, ln1_g, ln1_b, w_router, router_bias, w_exp_gate, w_exp_up, w_exp_down, w_sh_gate, w_sh_up, w_sh_down, ln2_g, ln2_b):
    bsz, t, d = x_prompt.shape
    n_sample = x_sample.shape[0] * x_sample.shape[1]
    n_prompt = bsz * t
    depth = w_in.shape[0]
    assert d == D_MODEL and x_sample.shape[1] == 1 and n_prompt % ROW_TILE == 0 and t % ROW_TILE == 0
    assert n_prompt % n_sample == 0 and (n_prompt + n_sample) % MOE_ROW_TILE == 0
    assert ((n_prompt + n_sample) * TOP_K) % EXPERT_ROW_TILE == 0 and n_sample % SAMPLE_BLOCK == 0

    lb_prob = jax.nn.softmax(hgrn_lower_bounds.astype(F32), axis=0)
    lower = (jnp.cumsum(lb_prob, axis=0) - lb_prob[0])[:, None, :]
    cos_p, sin_p = _rope_tables(jnp.arange(t))
    cos_s, sin_s = _rope_tables(jnp.full((n_sample,), PAST_LEN))
    row = lambda a: a[:, None, :]
    prm = dict(
        w_in=w_in.astype(BF16), lower=lower, cos_p=cos_p, sin_p=sin_p, cos_s=cos_s, sin_s=sin_s,
        norm_g=row(hgrn_norm_g), sinks=attn_sinks,
        sinks8=jnp.broadcast_to(attn_sinks[:, :, None], (depth, B_Q_HEADS, LANES)),
        wa=w_branch_a.astype(BF16), wb=w_branch_b.astype(BF16), wo=w_out.astype(BF16),
        ln1_g=row(ln1_g), ln1_b=row(ln1_b), ln2_g=row(ln2_g), ln2_b=row(ln2_b),
        wr_t=jnp.swapaxes(w_router, 1, 2),
        rbias=jnp.broadcast_to(router_bias[:, :, None], (depth, N_EXPERTS, LANES)),
        w_exp_gate=w_exp_gate, w_exp_up=w_exp_up, w_exp_down=w_exp_down,
        wsg=w_sh_gate.astype(BF16), wsu=w_sh_up.astype(BF16), wsd=w_sh_down.astype(BF16),
        state=state_hgrn,
        cache_k=cache_k.reshape(depth, n_sample, WINDOW, B_KV_WIDTH),
        cache_v=cache_v.reshape(depth, n_sample, WINDOW, B_KV_WIDTH),
    )

    xp, xs, xs_off = x_prompt.reshape(n_prompt, d), x_sample.reshape(n_sample, d), 0
    per_layer = []
    for l in range(depth):
        h_all, outs = _mixer_layer(l, xp, xs, xs_off, n_prompt, bsz, t, n_sample, prm)
        y_all = _moe_layer(l, h_all, prm)
        per_layer.append(outs)
        xp, xs, xs_off = y_all, y_all, n_prompt // n_sample

    kv_shape = (n_sample, WINDOW, B_KV_HEADS, B_HEAD_DIM)
    stack = lambda k, f=lambda a: a: jnp.stack([f(o[k]) for o in per_layer])
    return (y_all[:n_prompt].reshape(bsz, t, d), y_all[n_prompt:].reshape(n_sample, 1, d),
            stack(0), stack(1), stack(2),
            stack(3, lambda a: a.reshape(kv_shape)), stack(4, lambda a: a.reshape(kv_shape)), stack(5))
```

```python
import functools

import numpy as np
import jax
import jax.numpy as jnp
from jax import lax
from jax.experimental import pallas as pl
from jax.experimental.pallas import tpu as pltpu

F32 = jnp.float32
BF16 = jnp.bfloat16
I32 = jnp.int32

D_MODEL = 1024
DEPTH = 2
PAST_LEN = 16384
A_HEADS = 4
A_DK = 128
A_DV = 128
A_KEY = A_HEADS * A_DK
A_WIDTH = A_HEADS * A_DV
B_Q_HEADS = 8
B_KV_HEADS = 2
B_HEAD_DIM = 64
B_GROUP = B_Q_HEADS // B_KV_HEADS
B_WIDTH = B_Q_HEADS * B_HEAD_DIM
B_KV_WIDTH = B_KV_HEADS * B_HEAD_DIM
WINDOW = 128
ROPE_THETA = 10000.0
ATTN_SCALE = B_HEAD_DIM ** -0.5
N_EXPERTS = 64
TOP_K = 8
N_GROUPS = 8
GROUP_SIZE = N_EXPERTS // N_GROUPS
TOPK_GROUPS = 4
D_EXPERT = D_MODEL // 4
D_SHARED = D_EXPERT
ROUTED_SCALE = 2.5
DN_ALPHA = (2 * DEPTH) ** 0.25
LN_EPS = 1e-5
RMS_EPS = 1e-6
NEG_BIG = -1e30
TINY = 1.1754944e-38
OFF_AF = A_KEY
OFF_AI = 2 * A_KEY
OFF_AG = OFF_AI + A_WIDTH
OFF_BQ = OFF_AG + A_WIDTH
OFF_BK = OFF_BQ + B_WIDTH
OFF_BV = OFF_BK + B_KV_WIDTH
OFF_GA = OFF_BV + B_KV_WIDTH
OFF_GB = OFF_GA + D_MODEL
IN_COLS = OFF_GB + D_MODEL

LANES = 128
HGRN_CHUNK = 128
HGRN_LEVELS = 7
ROW_TILE = 512
MOE_ROW_TILE = 384
EXPERT_ROW_TILE = 256
VMEM_LIMIT = 56 * 1024 * 1024


def _cparams(sem, vmem=VMEM_LIMIT, checked_dma=True):
    return pltpu.CompilerParams(dimension_semantics=sem, vmem_limit_bytes=vmem,
                                disable_bounds_checks=not checked_dma)


def _dot(a, b):
    return jnp.dot(a, b, preferred_element_type=F32)


def _dot_nt(a, b):
    return lax.dot_general(a, b, (((1,), (1,)), ((), ())), preferred_element_type=F32)


def _dot_tn(a, b):
    return lax.dot_general(a, b, (((0,), (0,)), ((), ())), preferred_element_type=F32)


def _sigmoid(x):
    return 1.0 / (1.0 + jnp.exp(-x))


def _silu(x):
    return x * _sigmoid(x)


def _split3(x):
    hi = x.astype(BF16)
    r1 = x - hi.astype(F32)
    mid = r1.astype(BF16)
    lo = (r1 - mid.astype(F32)).astype(BF16)
    return hi, mid, lo


def _layer_norm(y, g, b):
    mu = jnp.mean(y, axis=-1, keepdims=True)
    d = y - mu
    var = jnp.mean(d * d, axis=-1, keepdims=True)
    return d * lax.rsqrt(var + LN_EPS) * g + b


def _log_forget(af, lower):
    ls = jnp.minimum(af, 0.0) - jnp.log1p(jnp.exp(-jnp.abs(af)))
    a = jnp.log(jnp.maximum(lower, TINY))
    b = jnp.log1p(-lower) + ls
    mixed = jnp.maximum(a, b) + jnp.log1p(jnp.exp(-jnp.abs(a - b)))
    return jnp.where(lower > 0.0, mixed, ls)


def _proj_kernel(x_ref, w_ref, cos_ref, sin_ref, low_ref,
                 qa_ref, lf_ref, va_ref, ga_ref, qb_ref, kb_ref, vb_ref, sga_ref, sgb_ref):
    xb = x_ref[...].astype(BF16)

    def mm(c0, n):
        return _dot(xb, w_ref[:, c0:c0 + n])

    qa_ref[...] = _silu(mm(0, A_KEY)).astype(BF16)
    lf_ref[...] = _log_forget(mm(OFF_AF, A_KEY), low_ref[...])
    va_ref[...] = mm(OFF_AI, A_WIDTH).astype(BF16)
    ga_ref[...] = _silu(mm(OFF_AG, A_WIDTH)).astype(BF16)

    cos = cos_ref[...]
    sin = sin_ref[...]
    lane = lax.broadcasted_iota(I32, cos.shape, 1)
    first_half = (lane & (B_HEAD_DIM // 2)) == 0

    def rope(blk):
        partner = jnp.where(first_half, pltpu.roll(blk, LANES - B_HEAD_DIM // 2, 1),
                            pltpu.roll(blk, B_HEAD_DIM // 2, 1))
        return blk * cos + partner * sin

    bq = mm(OFF_BQ, B_WIDTH)
    for j in range(B_WIDTH // LANES):
        sl = slice(j * LANES, (j + 1) * LANES)
        qb_ref[:, sl] = (rope(bq[:, sl]) * ATTN_SCALE).astype(BF16)
    kb_ref[...] = rope(mm(OFF_BK, B_KV_WIDTH))
    vb_ref[...] = mm(OFF_BV, B_KV_WIDTH)
    sga_ref[...] = _sigmoid(mm(OFF_GA, D_MODEL)).astype(BF16)
    sgb_ref[...] = _sigmoid(mm(OFF_GB, D_MODEL)).astype(BF16)


def _proj(x, w_bf, cos_t, sin_t, lower, n_rows, row_off_blocks, tm):
    nt = n_rows // tm
    tab_blocks = cos_t.shape[0] // tm
    row = lambda w: pl.BlockSpec((tm, w), lambda i: (i, 0))
    outs = [(A_KEY, BF16), (A_KEY, F32), (A_WIDTH, BF16), (A_WIDTH, BF16), (B_WIDTH, BF16),
            (B_KV_WIDTH, F32), (B_KV_WIDTH, F32), (D_MODEL, BF16), (D_MODEL, BF16)]
    return pl.pallas_call(
        _proj_kernel,
        grid=(nt,),
        in_specs=[pl.BlockSpec((tm, D_MODEL), lambda i: (i + row_off_blocks, 0)),
                  pl.BlockSpec((D_MODEL, IN_COLS), lambda i: (0, 0)),
                  pl.BlockSpec((tm, LANES), lambda i: (i % tab_blocks, 0)),
                  pl.BlockSpec((tm, LANES), lambda i: (i % tab_blocks, 0)),
                  pl.BlockSpec((1, A_KEY), lambda i: (0, 0))],
        out_specs=[row(w) for w, _ in outs],
        out_shape=[jax.ShapeDtypeStruct((n_rows, w), dt) for w, dt in outs],
        compiler_params=_cparams(("arbitrary",)),
        name="proj",
    )(x, w_bf, cos_t, sin_t, lower)


def _rope_tables(pos):
    half = B_HEAD_DIM // 2
    inv = ROPE_THETA ** (-jnp.arange(half, dtype=F32) / half)
    ang = pos.astype(F32)[:, None] * inv[None, :]
    cos = jnp.cos(ang)
    sin = jnp.sin(ang)
    reps = LANES // B_HEAD_DIM
    cos_t = jnp.tile(jnp.concatenate([cos, cos], axis=1), (1, reps))
    sin_t = jnp.tile(jnp.concatenate([-sin, sin], axis=1), (1, reps))
    return cos_t, sin_t


def _hgrn_constants():
    c = HGRN_CHUNK
    r = np.arange(c)
    tri = (r[None, :] <= r[:, None]).astype(np.float32)
    sel = np.zeros((HGRN_LEVELS, c, c), np.float32)
    upper = np.zeros((HGRN_LEVELS, c, LANES), np.float32)
    pair = np.zeros((HGRN_LEVELS + 1, c, c), np.float32)
    for l in range(HGRN_LEVELS):
        b = c >> (l + 1)
        ref_row = (r // (2 * b)) * (2 * b) + b - 1
        sel[l, r, ref_row] = 1.0
        up = (r % (2 * b)) >= b
        upper[l] = up[:, None]
        same = (r[:, None] // (2 * b)) == (r[None, :] // (2 * b))
        pair[l] = (up[:, None] & ~up[None, :] & same)
    pair[HGRN_LEVELS] = np.eye(c)
    return (jnp.asarray(tri, BF16), jnp.asarray(sel.reshape(HGRN_LEVELS * c, c), BF16),
            jnp.asarray(upper), jnp.asarray(pair))


def _hgrn_kernel(qa_ref, lf_ref, va_ref, ga_ref, g_ref, tri_ref, sel_ref, up_ref, pair_ref,
                 oa_ref, st_ref, s_scr):
    c = HGRN_CHUNK
    step = pl.program_id(1)

    @pl.when(step == 0)
    def _():
        s_scr[...] = jnp.zeros_like(s_scr)

    tri = tri_ref[...]
    sel = sel_ref[...]
    for h in range(A_HEADS):
        sl = slice(h * A_DK, (h + 1) * A_DK)
        lf = lf_ref[:, sl]
        hi, mid, lo = _split3(lf)
        gcum = _dot(tri, hi) + _dot(tri, mid) + _dot(tri, lo)
        ghi, gmid, glo = _split3(gcum)
        qb = qa_ref[:, sl]
        qf = qb.astype(F32)
        kf = 1.0 - jnp.exp(lf)
        vb = va_ref[:, sl]
        att = _dot_nt(qb, kf.astype(BF16)) * pair_ref[HGRN_LEVELS]
        for l in range(HGRN_LEVELS):
            sl_rows = slice(l * c, (l + 1) * c)
            gref = _dot(sel[sl_rows], ghi) + _dot(sel[sl_rows], gmid) + _dot(sel[sl_rows], glo)
            e = jnp.exp(-jnp.abs(gcum - gref))
            w = (jnp.where(up_ref[l] > 0.5, qf, kf) * e).astype(BF16)
            att = att + _dot_nt(w, w) * pair_ref[l]
        s_t = s_scr[h]
        o = _dot_nt((qf * jnp.exp(gcum)).astype(BF16), s_t.astype(BF16)) + _dot(att.astype(BF16), vb)
        ms = jnp.mean(o * o, axis=-1, keepdims=True)
        on = o * lax.rsqrt(ms + RMS_EPS) * g_ref[...] * ga_ref[:, sl].astype(F32)
        oa_ref[:, sl] = on.astype(BF16)
        gend = gcum[c - 1:c, :]
        kend = (kf * jnp.exp(gend - gcum)).astype(BF16)
        s_scr[h] = s_t * jnp.exp(gend) + _dot_tn(vb, kend)

    @pl.when(step == pl.num_programs(1) - 1)
    def _():
        for h in range(A_HEADS):
            st_ref[0, h] = s_scr[h].T


def _hgrn_prompt(qa, lf, va, ga, norm_g, bsz, t):
    c = HGRN_CHUNK
    nc = t // c
    tri, sel, upper, pair = _hgrn_constants()
    blk = lambda: pl.BlockSpec((c, A_KEY), lambda b, i: (b * nc + i, 0))
    const = lambda a: pl.BlockSpec(a.shape, lambda b, i: (0,) * a.ndim)
    return pl.pallas_call(
        _hgrn_kernel,
        grid=(bsz, nc),
        in_specs=[blk(), blk(), blk(), blk(), pl.BlockSpec((1, A_DV), lambda b, i: (0, 0)),
                  const(tri), const(sel), const(upper), const(pair)],
        out_specs=[blk(), pl.BlockSpec((1, A_HEADS, A_DK, A_DV), lambda b, i: (b, 0, 0, 0))],
        out_shape=[jax.ShapeDtypeStruct((bsz * t, A_WIDTH), BF16),
                   jax.ShapeDtypeStruct((bsz, A_HEADS, A_DK, A_DV), F32)],
        scratch_shapes=[pltpu.VMEM((A_HEADS, A_DV, A_DK), F32)],
        compiler_params=_cparams(("arbitrary", "arbitrary")),
        name="hgrn_prompt",
    )(qa, lf, va, ga, norm_g, tri, sel, upper, pair)


def _swa_kernel(sink_ref, q_ref, k_ref, v_ref, o_ref, kprev, vprev):
    w = WINDOW
    i = pl.program_id(1)

    @pl.when(i == 0)
    def _():
        kprev[...] = jnp.zeros_like(kprev)
        vprev[...] = jnp.zeros_like(vprev)

    kc = k_ref[...]
    vc = v_ref[...]
    kk = jnp.concatenate([kprev[...], kc], axis=0)
    vv = jnp.concatenate([vprev[...], vc], axis=0)
    kr = pltpu.roll(kk, B_HEAD_DIM, 1)
    vr = pltpu.roll(vv, B_HEAD_DIM, 1)
    lo2 = lax.broadcasted_iota(I32, kk.shape, 1) < B_HEAD_DIM
    zero = jnp.zeros_like(kk)
    k_lo = [jnp.where(lo2, kk, zero).astype(BF16), jnp.where(lo2, kr, zero).astype(BF16)]
    k_hi = [jnp.where(lo2, zero, kr).astype(BF16), jnp.where(lo2, zero, kk).astype(BF16)]
    v_dup = [jnp.where(lo2, vv, vr).astype(BF16), jnp.where(lo2, vr, vv).astype(BF16)]

    qi = lax.broadcasted_iota(I32, (w, 2 * w), 0)
    kj = lax.broadcasted_iota(I32, (w, 2 * w), 1)
    valid = (kj >= qi) & (kj <= qi + w) & ((kj >= w) | (i > 0))
    lo1 = lax.broadcasted_iota(I32, (w, LANES), 1) < B_HEAD_DIM

    for j in range(B_WIDTH // LANES):
        g = (2 * j) // B_GROUP
        qblk = q_ref[:, j * LANES:(j + 1) * LANES]
        res = []
        for half, kmat in enumerate((k_lo[g], k_hi[g])):
            sk = sink_ref[2 * j + half]
            s = jnp.where(valid, _dot_nt(qblk, kmat), NEG_BIG)
            m = jnp.maximum(jnp.max(s, axis=-1, keepdims=True), sk)
            p = jnp.exp(s - m)
            denom = jnp.sum(p, axis=-1, keepdims=True) + jnp.exp(sk - m)
            res.append(_dot(p.astype(BF16), v_dup[g]) / denom)
        o_ref[:, j * LANES:(j + 1) * LANES] = jnp.where(lo1, res[0], res[1]).astype(BF16)

    kprev[...] = kc
    vprev[...] = vc


def _swa_prompt(sinks, qb, kb, vb, bsz, t):
    w = WINDOW
    nb = t // w
    return pl.pallas_call(
        _swa_kernel,
        grid_spec=pltpu.PrefetchScalarGridSpec(
            num_scalar_prefetch=1,
            grid=(bsz, nb),
            in_specs=[pl.BlockSpec((w, B_WIDTH), lambda b, i, s: (b * nb + i, 0)),
                      pl.BlockSpec((w, B_KV_WIDTH), lambda b, i, s: (b * nb + i, 0)),
                      pl.BlockSpec((w, B_KV_WIDTH), lambda b, i, s: (b * nb + i, 0))],
            out_specs=pl.BlockSpec((w, B_WIDTH), lambda b, i, s: (b * nb + i, 0)),
            scratch_shapes=[pltpu.VMEM((w, B_KV_WIDTH), F32), pltpu.VMEM((w, B_KV_WIDTH), F32)]),
        out_shape=jax.ShapeDtypeStruct((bsz * t, B_WIDTH), BF16),
        compiler_params=_cparams(("arbitrary", "arbitrary")),
        name="swa_prompt",
    )(sinks, qb, kb, vb)


SAMPLE_BLOCK = 8


def _sample_kernel(sink_ref, st_ref, lft_ref, qat_ref, va_ref, ga_ref, g_ref, q3_ref, kn_ref, vn_ref,
                   ck_ref, cv_ref, st_out, oa_ref, ob_ref, ck_out, cv_out, o_scr):
    w = WINDOW
    row = lax.broadcasted_iota(I32, (w, B_KV_WIDTH), 0)
    for i in range(SAMPLE_BLOCK):
        for h in range(A_HEADS):
            sl = slice(h * A_DV, (h + 1) * A_DV)
            fcol = jnp.exp(lft_ref[h, 0][:, i:i + 1])
            qcol = qat_ref[h, 0][:, i:i + 1]
            vrow = va_ref[i:i + 1, sl].astype(F32)
            s_new = st_ref[i, h] * fcol + (1.0 - fcol) * vrow
            st_out[i, h] = s_new
            o_scr[i:i + 1, sl] = jnp.sum(s_new * qcol, axis=0, keepdims=True)
        kc = ck_ref[i]
        vc = cv_ref[i]
        kn = kn_ref[i:i + 1, :]
        vn = vn_ref[i:i + 1, :]
        q3 = q3_ref[i]
        s = _dot_nt(q3.astype(BF16), kc.astype(BF16))
        s_new_key = jnp.sum(q3 * kn, axis=-1, keepdims=True)
        sk = sink_ref[...][:, 0:1]
        m = jnp.maximum(jnp.maximum(jnp.max(s, axis=-1, keepdims=True), s_new_key), sk)
        p = jnp.exp(s - m)
        pn = jnp.exp(s_new_key - m)
        denom = jnp.sum(p, axis=-1, keepdims=True) + pn + jnp.exp(sk - m)
        ob_ref[i] = (_dot(p.astype(BF16), vc.astype(BF16)) + pn * vn) / denom
        ck_out[i] = jnp.where(row == w - 1, kn, pltpu.roll(kc, w - 1, 0))
        cv_out[i] = jnp.where(row == w - 1, vn, pltpu.roll(vc, w - 1, 0))
    for h in range(A_HEADS):
        sl = slice(h * A_DV, (h + 1) * A_DV)
        o = o_scr[:, sl]
        ms = jnp.mean(o * o, axis=-1, keepdims=True)
        oa_ref[:, sl] = (o * lax.rsqrt(ms + RMS_EPS) * g_ref[...] * ga_ref[:, sl].astype(F32)).astype(BF16)


def _sample_step(l, sinks8, state, lft, qat, va, ga, norm_g, q3, kn, vn, ck, cv):
    nb = state.shape[1]
    sb = SAMPLE_BLOCK
    steps = nb // sb
    w = WINDOW
    b4 = lambda: pl.BlockSpec((sb, A_HEADS, A_DK, A_DV), lambda i: (i, 0, 0, 0))
    b4_in = pl.BlockSpec((None, sb, A_HEADS, A_DK, A_DV), lambda i: (l, i, 0, 0, 0))
    c3_in = lambda: pl.BlockSpec((None, sb, w, B_KV_WIDTH), lambda i: (l, i, 0, 0))
    t4 = lambda: pl.BlockSpec((A_HEADS, 1, A_DK, sb), lambda i: (0, i, 0, 0))
    r2 = lambda wd: pl.BlockSpec((sb, wd), lambda i: (i, 0))
    c3 = lambda: pl.BlockSpec((sb, w, B_KV_WIDTH), lambda i: (i, 0, 0))
    return pl.pallas_call(
        _sample_kernel,
        grid=(steps,),
        in_specs=[pl.BlockSpec((B_Q_HEADS, LANES), lambda i: (0, 0)),
                  b4_in, t4(), t4(), r2(A_WIDTH), r2(A_WIDTH), pl.BlockSpec((1, A_DV), lambda i: (0, 0)),
                  pl.BlockSpec((sb, B_Q_HEADS, LANES), lambda i: (i, 0, 0)), r2(B_KV_WIDTH), r2(B_KV_WIDTH),
                  c3_in(), c3_in()],
        out_specs=[b4(), r2(A_WIDTH), pl.BlockSpec((sb, B_Q_HEADS, LANES), lambda i: (i, 0, 0)), c3(), c3()],
        out_shape=[jax.ShapeDtypeStruct(state.shape[1:], F32),
                   jax.ShapeDtypeStruct((nb, A_WIDTH), BF16),
                   jax.ShapeDtypeStruct((nb, B_Q_HEADS, LANES), F32),
                   jax.ShapeDtypeStruct((nb, w, B_KV_WIDTH), F32),
                   jax.ShapeDtypeStruct((nb, w, B_KV_WIDTH), F32)],
        scratch_shapes=[pltpu.VMEM((sb, A_WIDTH), F32)],
        compiler_params=_cparams(("arbitrary",)),
        name="sample_step",
    )(sinks8, state, lft, qat, va, ga, norm_g, q3, kn, vn, ck, cv)


def _merge_kernel(x_ref, oa_ref, ob_ref, sga_ref, sgb_ref, wa_ref, wb_ref, wo_ref, g_ref, b_ref, h_ref):
    merged = (sga_ref[...].astype(F32) * _dot(oa_ref[...], wa_ref[...])
              + sgb_ref[...].astype(F32) * _dot(ob_ref[...], wb_ref[...]))
    mix = _dot(merged.astype(BF16), wo_ref[...])
    h_ref[...] = _layer_norm(DN_ALPHA * x_ref[...] + mix, g_ref[...], b_ref[...])


def _merge(x, oa, ob, sga, sgb, wa, wb, wo, g, b, n_rows, x_off_blocks, out_rows, out_off_blocks, tm, h_prev=None):
    nt = n_rows // tm
    row = lambda wd: pl.BlockSpec((tm, wd), lambda i: (i, 0))
    const = lambda a: pl.BlockSpec(a.shape, lambda i: (0, 0))
    args = [x, oa, ob, sga, sgb, wa, wb, wo, g, b]
    in_specs = [pl.BlockSpec((tm, D_MODEL), lambda i: (i + x_off_blocks, 0)),
                row(A_WIDTH), row(B_WIDTH), row(D_MODEL), row(D_MODEL),
                const(wa), const(wb), const(wo), const(g), const(b)]
    kern = _merge_kernel
    aliases = {}
    if h_prev is not None:
        args.append(h_prev)
        in_specs.append(pl.BlockSpec(memory_space=pl.ANY))
        aliases = {len(args) - 1: 0}
        kern = lambda *refs: _merge_kernel(*refs[:10], refs[11])
    return pl.pallas_call(
        kern,
        grid=(nt,),
        in_specs=in_specs,
        out_specs=pl.BlockSpec((tm, D_MODEL), lambda i: (i + out_off_blocks, 0)),
        out_shape=jax.ShapeDtypeStruct((out_rows, D_MODEL), F32),
        input_output_aliases=aliases,
        compiler_params=_cparams(("arbitrary",)),
        name="merge",
    )(*args)


def _mixer_layer(l, xp, xs, xs_off, n_prompt, bsz, t, n_sample, prm):
    tm = ROW_TILE
    p = _proj(xp, prm["w_in"][l], prm["cos_p"], prm["sin_p"], prm["lower"][l], n_prompt, 0, tm)
    qa, lf, va, ga, qb, kb, vb, sga, sgb = p
    oa, st_p = _hgrn_prompt(qa, lf, va, ga, prm["norm_g"][l], bsz, t)
    ob = _swa_prompt(prm["sinks"][l], qb, kb, vb, bsz, t)
    n_all = n_prompt + n_sample
    h_all = _merge(xp, oa, ob, sga, sgb, prm["wa"][l], prm["wb"][l], prm["wo"][l], prm["ln1_g"][l],
                   prm["ln1_b"][l], n_prompt, 0, n_all, 0, tm, h_prev=jnp.zeros((n_all, D_MODEL), F32))
    kp = kb.reshape(bsz, t, B_KV_HEADS, B_HEAD_DIM)[:, -WINDOW:]
    vp = vb.reshape(bsz, t, B_KV_HEADS, B_HEAD_DIM)[:, -WINDOW:]

    ts = n_sample
    ps = _proj(xs, prm["w_in"][l], prm["cos_s"], prm["sin_s"], prm["lower"][l], ts, xs_off, ts)
    qa_s, lf_s, va_s, ga_s, qb_s, kb_s, vb_s, sga_s, sgb_s = ps
    sb = SAMPLE_BLOCK
    to_t = lambda a: a.reshape(ts // sb, sb, A_HEADS, A_DK).transpose(2, 0, 3, 1)
    qh = qb_s.astype(F32).reshape(ts, B_Q_HEADS, B_HEAD_DIM)
    z = jnp.zeros_like(qh[:, :B_GROUP])
    q3 = jnp.concatenate([jnp.concatenate([qh[:, :B_GROUP], z], axis=-1),
                          jnp.concatenate([z, qh[:, B_GROUP:]], axis=-1)], axis=1)
    st_s, oa_s, ob3, ck_s, cv_s = _sample_step(
        l, prm["sinks8"][l], prm["state"], to_t(lf_s), to_t(qa_s.astype(F32)), va_s, ga_s, prm["norm_g"][l],
        q3, kb_s, vb_s, prm["cache_k"], prm["cache_v"])
    ob_s = jnp.concatenate([ob3[:, :B_GROUP, :B_HEAD_DIM], ob3[:, B_GROUP:, B_HEAD_DIM:]], axis=1)
    ob_s = ob_s.reshape(ts, B_WIDTH).astype(BF16)
    h_all = _merge(xs, oa_s, ob_s, sga_s, sgb_s, prm["wa"][l], prm["wb"][l], prm["wo"][l], prm["ln1_g"][l],
                   prm["ln1_b"][l], ts, xs_off, n_all, n_prompt // ts, ts, h_prev=h_all)
    return h_all, (kp, vp, st_p, ck_s, cv_s, st_s)


def _router_kernel(h_ref, wr_ref, bias_ref, e_ref, w_ref, mask_ref, cnt_ref):
    tm = h_ref.shape[0]
    gsz = GROUP_SIZE

    @pl.when(pl.program_id(0) == 0)
    def _():
        cnt_ref[...] = jnp.zeros_like(cnt_ref)

    logits = lax.dot_general(wr_ref[...], h_ref[...], (((1,), (1,)), ((), ())),
                             precision=lax.Precision.HIGHEST, preferred_element_type=F32)
    scores = _sigmoid(logits)
    sel = scores + bias_ref[...][:, 0:1]
    rowi = lax.broadcasted_iota(I32, (gsz, tm), 0)
    neg_inf = -jnp.inf
    blocks = [sel[g * gsz:(g + 1) * gsz] for g in range(N_GROUPS)]
    sblocks = [scores[g * gsz:(g + 1) * gsz] for g in range(N_GROUPS)]

    gscore = []
    for blk in blocks:
        m1 = jnp.max(blk, axis=0, keepdims=True)
        i1 = jnp.min(jnp.where(blk == m1, rowi, gsz), axis=0, keepdims=True)
        m2 = jnp.max(jnp.where(rowi == i1, neg_inf, blk), axis=0, keepdims=True)
        gscore.append(m1 + m2)
    work = []
    for g in range(N_GROUPS):
        ahead = jnp.zeros((1, tm), I32)
        for g2 in range(N_GROUPS):
            if g2 != g:
                beats = (gscore[g2] > gscore[g]) | ((gscore[g2] == gscore[g]) & (g2 < g))
                ahead = ahead + beats.astype(I32)
        work.append(jnp.where(ahead < TOPK_GROUPS, blocks[g], NEG_BIG))

    chosen = [jnp.zeros((gsz, tm), F32) for _ in range(N_GROUPS)]
    es, ws = [], []
    for _ in range(TOP_K):
        m = work[0]
        for g in range(1, N_GROUPS):
            m = jnp.maximum(m, work[g])
        m = jnp.max(m, axis=0, keepdims=True)
        cand = jnp.where(work[0] == m, rowi, N_EXPERTS)
        for g in range(1, N_GROUPS):
            cand = jnp.minimum(cand, jnp.where(work[g] == m, rowi + g * gsz, N_EXPERTS))
        idx = jnp.min(cand, axis=0, keepdims=True)
        wj = jnp.zeros((1, tm), F32)
        for g in range(N_GROUPS):
            hit = (rowi + g * gsz) == idx
            wj = wj + jnp.sum(jnp.where(hit, sblocks[g], 0.0), axis=0, keepdims=True)
            chosen[g] = jnp.where(hit, 1.0, chosen[g])
            work[g] = jnp.where(hit, neg_inf, work[g])
        es.append(idx)
        ws.append(wj)
    wsum = ws[0]
    for j in range(1, TOP_K):
        wsum = wsum + ws[j]
    for j in range(TOP_K):
        e_ref[j:j + 1, :] = es[j]
        w_ref[j:j + 1, :] = ws[j] / wsum * ROUTED_SCALE
    for g in range(N_GROUPS):
        rows = slice(g * gsz, (g + 1) * gsz)
        mask_ref[rows, :] = chosen[g]
        part = chosen[g][:, 0:LANES]
        for c in range(1, tm // LANES):
            part = part + chosen[g][:, c * LANES:(c + 1) * LANES]
        cnt_ref[rows, :] = cnt_ref[rows, :] + part


def _router(h_all, wr_t, bias_b, tm):
    n = h_all.shape[0]
    col = lambda r: pl.BlockSpec((r, tm), lambda i: (0, i))
    return pl.pallas_call(
        _router_kernel,
        grid=(n // tm,),
        in_specs=[pl.BlockSpec((tm, D_MODEL), lambda i: (i, 0)),
                  pl.BlockSpec((N_EXPERTS, D_MODEL), lambda i: (0, 0)),
                  pl.BlockSpec((N_EXPERTS, LANES), lambda i: (0, 0))],
        out_specs=[col(TOP_K), col(TOP_K), col(N_EXPERTS), pl.BlockSpec((N_EXPERTS, LANES), lambda i: (0, 0))],
        out_shape=[jax.ShapeDtypeStruct((TOP_K, n), I32), jax.ShapeDtypeStruct((TOP_K, n), F32),
                   jax.ShapeDtypeStruct((N_EXPERTS, n), F32), jax.ShapeDtypeStruct((N_EXPERTS, LANES), F32)],
        compiler_params=_cparams(("arbitrary",)),
        name="router",
    )(h_all, wr_t, bias_b)


def _rank_kernel(mask_ref, e_ref, offs_ref, triu_ref, dest_ref, carry):
    tm = mask_ref.shape[1]

    @pl.when(pl.program_id(0) == 0)
    def _():
        carry[...] = jnp.zeros_like(carry)

    mk = mask_ref[...]
    rank = _dot(mk.astype(BF16), triu_ref[...])
    dest_full = rank + (offs_ref[...][:, 0:1] + carry[...][:, 0:1])
    rowi = lax.broadcasted_iota(I32, (N_EXPERTS, tm), 0)
    for j in range(TOP_K):
        d = jnp.sum(jnp.where(rowi == e_ref[j:j + 1, :], dest_full, 0.0), axis=0, keepdims=True)
        dest_ref[j:j + 1, :] = d.astype(I32)
    carry[...] = carry[...] + jnp.sum(mk, axis=1, keepdims=True)


def _rank(mask_t, e_t, offs_b, tm):
    n = mask_t.shape[1]
    r = np.arange(tm)
    triu = jnp.asarray((r[:, None] < r[None, :]).astype(np.float32), BF16)
    return pl.pallas_call(
        _rank_kernel,
        grid=(n // tm,),
        in_specs=[pl.BlockSpec((N_EXPERTS, tm), lambda i: (0, i)),
                  pl.BlockSpec((TOP_K, tm), lambda i: (0, i)),
                  pl.BlockSpec((N_EXPERTS, LANES), lambda i: (0, 0)),
                  pl.BlockSpec((tm, tm), lambda i: (0, 0))],
        out_specs=pl.BlockSpec((TOP_K, tm), lambda i: (0, i)),
        out_shape=jax.ShapeDtypeStruct((TOP_K, n), I32),
        scratch_shapes=[pltpu.VMEM((N_EXPERTS, LANES), F32)],
        compiler_params=_cparams(("arbitrary",)),
        name="rank",
    )(mask_t, e_t, offs_b, triu)


INV_COLS = 512
TOKEN_RADIX = 128


def _inverse_kernel(dest_ref, inv_ref):
    tm = dest_ref.shape[1]
    nq = inv_ref.shape[0]

    @pl.when(pl.program_id(0) == 0)
    def _():
        inv_ref[...] = jnp.zeros_like(inv_ref)

    tok = pl.program_id(0) * tm + lax.broadcasted_iota(I32, (1, tm), 1)
    t_hi = jnp.right_shift(tok, TOKEN_RADIX.bit_length() - 1).astype(F32)
    t_lo = jnp.bitwise_and(tok, TOKEN_RADIX - 1).astype(F32)
    qi = lax.broadcasted_iota(I32, (nq, tm), 0)
    si = lax.broadcasted_iota(I32, (INV_COLS, tm), 0)
    acc_hi = jnp.zeros(inv_ref.shape, F32)
    acc_lo = jnp.zeros(inv_ref.shape, F32)
    for j in range(TOP_K):
        d = dest_ref[j:j + 1, :]
        at_q = qi == jnp.right_shift(d, INV_COLS.bit_length() - 1)
        col = jnp.where(si == jnp.bitwise_and(d, INV_COLS - 1), 1.0, 0.0).astype(BF16)
        acc_hi = acc_hi + _dot_nt(jnp.where(at_q, t_hi, 0.0).astype(BF16), col)
        acc_lo = acc_lo + _dot_nt(jnp.where(at_q, t_lo, 0.0).astype(BF16), col)
    inv_ref[...] = inv_ref[...] + (acc_hi * float(TOKEN_RADIX) + acc_lo)


def _inverse_map(dest_t, tm):
    n = dest_t.shape[1]
    n_rows = n * TOP_K
    nq = -(-(n_rows // INV_COLS) // 8) * 8
    inv = pl.pallas_call(
        _inverse_kernel,
        grid=(n // tm,),
        in_specs=[pl.BlockSpec((TOP_K, tm), lambda i: (0, i))],
        out_specs=pl.BlockSpec((nq, INV_COLS), lambda i: (0, 0)),
        out_shape=jax.ShapeDtypeStruct((nq, INV_COLS), F32),
        compiler_params=_cparams(("arbitrary",)),
        name="inverse_map",
    )(dest_t)
    return inv.reshape(-1)[:n_rows].astype(I32).reshape(n_rows // EXPERT_ROW_TILE, 1, EXPERT_ROW_TILE)


SUBLANES = 8


def _tile_row_copy(src_ref, src_row, dst_ref, dst_group, dst_sub, sem):
    return pltpu.make_async_copy(
        src_ref.at[jnp.right_shift(src_row, 3), pl.ds(jnp.bitwise_and(src_row, SUBLANES - 1), 1)],
        dst_ref.at[dst_group, pl.ds(dst_sub, 1)], sem)


def _expert_kernel(n_tiles, tile_ref, exp_ref, valid_ref, first_ref, newexp_ref, lo_ref, hi_ref,
                   inv_ref, invn_ref, h_ref, wg_ref, wu_ref, wd_ref, y_ref, xbuf, wgb, wub, wdb, sem):
    i = pl.program_id(0)
    rows = xbuf.shape[1] * SUBLANES
    tile = tile_ref[i]
    slot = tile % 2
    valid = valid_ref[i] == 1
    first = first_ref[i] == 1

    def gather(idx_ref, s):
        def issue(k, c):
            for u in range(SUBLANES):
                _tile_row_copy(h_ref, idx_ref[0, 0, k * SUBLANES + u], xbuf.at[s], k, u, sem.at[s]).start()
            return c

        lax.fori_loop(0, rows // SUBLANES, issue, 0)

    @pl.when(valid & first)
    def _():
        @pl.when(i == 0)
        def _():
            gather(inv_ref, 0)

        @pl.when(tile + 1 < n_tiles)
        def _():
            gather(invn_ref, 1 - slot)

        pltpu.make_async_copy(h_ref.at[pl.ds(0, rows // SUBLANES)], xbuf.at[slot], sem.at[slot]).wait()

    @pl.when(valid & (newexp_ref[i] == 1))
    def _():
        wgb[...] = wg_ref[0].astype(BF16)
        wub[...] = wu_ref[0].astype(BF16)
        wdb[...] = wd_ref[0].astype(BF16)

    @pl.when(valid)
    def _():
        xb = xbuf[slot].reshape(rows, D_MODEL).astype(BF16)
        gate = _dot(xb, wgb[...])
        up = _dot(xb, wub[...])
        y = _dot((_silu(gate) * up).astype(BF16), wdb[...])
        rowi = lax.broadcasted_iota(I32, y.shape, 0)
        mine = (rowi >= lo_ref[i]) & (rowi < hi_ref[i])

        @pl.when(first)
        def _():
            y_ref[...] = jnp.where(mine, y, 0.0)

        @pl.when(jnp.logical_not(first))
        def _():
            y_ref[...] = jnp.where(mine, y, y_ref[...])


def _group_metadata(counts, n_rows):
    tmo = EXPERT_ROW_TILE
    n_tiles = n_rows // tmo
    ends = jnp.cumsum(counts)
    offs = ends - counts
    first_tile = offs // tmo
    n_t = jnp.where(counts > 0, (ends - 1) // tmo - first_tile + 1, 0)
    cum = jnp.cumsum(n_t)
    base = cum - n_t
    n_items = n_tiles + N_EXPERTS
    idx = jnp.arange(n_items, dtype=I32)
    valid = (idx < cum[-1]).astype(I32)
    idc = jnp.minimum(idx, cum[-1] - 1)
    e = jnp.minimum(jnp.sum((cum[None, :] <= idc[:, None]).astype(I32), axis=1), N_EXPERTS - 1)
    tile = (first_tile[e] + idc - base[e]).astype(I32)
    one = jnp.ones((1,), I32)
    first = jnp.concatenate([one, (tile[1:] != tile[:-1]).astype(I32)])
    new_expert = jnp.concatenate([one, (e[1:] != e[:-1]).astype(I32)])
    lo = jnp.clip(offs[e] - tile * tmo, 0, tmo).astype(I32)
    hi = jnp.clip(ends[e] - tile * tmo, 0, tmo).astype(I32)
    return tile, e, valid, first, new_expert, lo, hi


def _experts(l, meta, inv3, h_all, wg, wu, wd):
    tmo = EXPERT_ROW_TILE
    n_rows = inv3.shape[0] * tmo
    n_items = meta[0].shape[0]
    wspec = lambda shp: pl.BlockSpec((None, 1) + shp, lambda i, tl, ex, *_: (l, ex[i], 0, 0))
    n_tiles = inv3.shape[0]
    idx_spec = lambda nxt: pl.BlockSpec(
        (1, 1, tmo), lambda i, tl, *_: (jnp.minimum(tl[i] + nxt, n_tiles - 1), 0, 0), memory_space=pltpu.SMEM)
    return pl.pallas_call(
        functools.partial(_expert_kernel, n_tiles),
        grid_spec=pltpu.PrefetchScalarGridSpec(
            num_scalar_prefetch=7,
            grid=(n_items,),
            in_specs=[idx_spec(0), idx_spec(1), pl.BlockSpec(memory_space=pl.ANY),
                      wspec((D_MODEL, D_EXPERT)), wspec((D_MODEL, D_EXPERT)), wspec((D_EXPERT, D_MODEL))],
            out_specs=pl.BlockSpec((tmo, D_MODEL), lambda i, tl, *_: (tl[i], 0)),
            scratch_shapes=[pltpu.VMEM((2, tmo // SUBLANES, SUBLANES, D_MODEL), F32),
                            pltpu.VMEM((D_MODEL, D_EXPERT), BF16), pltpu.VMEM((D_MODEL, D_EXPERT), BF16),
                            pltpu.VMEM((D_EXPERT, D_MODEL), BF16), pltpu.SemaphoreType.DMA((2,))]),
        out_shape=jax.ShapeDtypeStruct((n_rows, D_MODEL), F32),
        compiler_params=_cparams(("arbitrary",), checked_dma=False),
        name="experts",
    )(*meta, inv3, inv3, h_all.reshape(-1, SUBLANES, D_MODEL), wg, wu, wd)


def _post_kernel(dest_ref, h_ref, wt_ref, y_ref, wsg_ref, wsu_ref, wsd_ref, g_ref, b_ref, out_ref, buf, sem):
    tm = h_ref.shape[0]

    def issue(k, c):
        for u in range(SUBLANES):
            for j in range(TOP_K):
                _tile_row_copy(y_ref, dest_ref[j, k * SUBLANES + u], buf.at[j], k, u, sem).start()
        return c

    lax.fori_loop(0, tm // SUBLANES, issue, 0)

    h = h_ref[...]
    hb = h.astype(BF16)
    shared = _dot((_silu(_dot(hb, wsg_ref[...])) * _dot(hb, wsu_ref[...])).astype(BF16), wsd_ref[...])

    for j in range(TOP_K):
        pltpu.make_async_copy(y_ref.at[pl.ds(0, tm // SUBLANES)], buf.at[j], sem).wait()

    wt = wt_ref[...]
    routed = buf[0].reshape(tm, D_MODEL) * wt[:, 0:1]
    for j in range(1, TOP_K):
        routed = routed + buf[j].reshape(tm, D_MODEL) * wt[:, j:j + 1]
    out_ref[...] = _layer_norm(DN_ALPHA * h + (routed + shared), g_ref[...], b_ref[...])


def _post(dest_t, h_all, w_tok, y, wsg, wsu, wsd, g, b, tm):
    n = h_all.shape[0]
    const = lambda a: pl.BlockSpec(a.shape, lambda i: (0, 0))
    return pl.pallas_call(
        _post_kernel,
        grid=(n // tm,),
        in_specs=[pl.BlockSpec((TOP_K, tm), lambda i: (0, i), memory_space=pltpu.SMEM),
                  pl.BlockSpec((tm, D_MODEL), lambda i: (i, 0)),
                  pl.BlockSpec((tm, TOP_K), lambda i: (i, 0)),
                  pl.BlockSpec(memory_space=pl.ANY),
                  const(wsg), const(wsu), const(wsd), const(g), const(b)],
        out_specs=pl.BlockSpec((tm, D_MODEL), lambda i: (i, 0)),
        out_shape=jax.ShapeDtypeStruct((n, D_MODEL), F32),
        scratch_shapes=[pltpu.VMEM((TOP_K, tm // SUBLANES, SUBLANES, D_MODEL), F32),
                        pltpu.SemaphoreType.DMA(())],
        compiler_params=_cparams(("arbitrary",), checked_dma=False),
        name="moe_post",
    )(dest_t, h_all, w_tok, y.reshape(-1, SUBLANES, D_MODEL), wsg, wsu, wsd, g, b)


def _moe_layer(l, h_all, prm):
    tm = MOE_ROW_TILE
    e_t, w_t, mask_t, cnt = _router(h_all, prm["wr_t"][l], prm["rbias"][l], tm)
    counts = jnp.sum(cnt, axis=1).astype(I32)
    offs = jnp.cumsum(counts) - counts
    offs_b = jnp.broadcast_to(offs.astype(F32)[:, None], (N_EXPERTS, LANES))
    dest_t = _rank(mask_t, e_t, offs_b, tm)
    inv3 = _inverse_map(dest_t, tm)
    meta = _group_metadata(counts, h_all.shape[0] * TOP_K)
    y = _experts(l, meta, inv3, h_all, prm["w_exp_gate"], prm["w_exp_up"], prm["w_exp_down"])
    return _post(dest_t, h_all, w_t.T, y, prm["wsg"][l], prm["wsu"][l], prm["wsd"][l],
                 prm["ln2_g"][l], prm["ln2_b"][l], tm)


def kernel(x_prompt, x_sample, cache_k, cache_v, state_hgrn, w_in, hgrn_lower_bounds, hgrn_norm_g, attn_sinks, w_branch_a, w_branch_b, w_out, ln1_g, ln1_b, w_router, router_bias, w_exp_gate, w_exp_up, w_exp_down, w_sh_gate, w_sh_up, w_sh_down, ln2_g, ln2_b):
    bsz, t, d = x_prompt.shape
    n_sample = x_sample.shape[0] * x_sample.shape[1]
    n_prompt = bsz * t
    depth = w_in.shape[0]
    assert d == D_MODEL and x_sample.shape[1] == 1 and n_prompt % ROW_TILE == 0 and t % ROW_TILE == 0
    assert n_prompt % n_sample == 0 and (n_prompt + n_sample) % MOE_ROW_TILE == 0
    assert ((n_prompt + n_sample) * TOP_K) % EXPERT_ROW_TILE == 0 and n_sample % SAMPLE_BLOCK == 0

    lb_prob = jax.nn.softmax(hgrn_lower_bounds.astype(F32), axis=0)
    lower = (jnp.cumsum(lb_prob, axis=0) - lb_prob[0])[:, None, :]
    cos_p, sin_p = _rope_tables(jnp.arange(t))
    cos_s, sin_s = _rope_tables(jnp.full((n_sample,), PAST_LEN))
    row = lambda a: a[:, None, :]
    prm = dict(
        w_in=w_in.astype(BF16), lower=lower, cos_p=cos_p, sin_p=sin_p, cos_s=cos_s, sin_s=sin_s,
        norm_g=row(hgrn_norm_g), sinks=attn_sinks,
        sinks8=jnp.broadcast_to(attn_sinks[:, :, None], (depth, B_Q_HEADS, LANES)),
        wa=w_branch_a.astype(BF16), wb=w_branch_b.astype(BF16), wo=w_out.astype(BF16),
        ln1_g=row(ln1_g), ln1_b=row(ln1_b), ln2_g=row(ln2_g), ln2_b=row(ln2_b),
        wr_t=jnp.swapaxes(w_router, 1, 2),
        rbias=jnp.broadcast_to(router_bias[:, :, None], (depth, N_EXPERTS, LANES)),
        w_exp_gate=w_exp_gate, w_exp_up=w_exp_up, w_exp_down=w_exp_down,
        wsg=w_sh_gate.astype(BF16), wsu=w_sh_up.astype(BF16), wsd=w_sh_down.astype(BF16),
        state=state_hgrn,
        cache_k=cache_k.reshape(depth, n_sample, WINDOW, B_KV_WIDTH),
        cache_v=cache_v.reshape(depth, n_sample, WINDOW, B_KV_WIDTH),
    )

    xp, xs, xs_off = x_prompt.reshape(n_prompt, d), x_sample.reshape(n_sample, d), 0
    per_layer = []
    for l in range(depth):
        h_all, outs = _mixer_layer(l, xp, xs, xs_off, n_prompt, bsz, t, n_sample, prm)
        y_all = _moe_layer(l, h_all, prm)
        per_layer.append(outs)
        xp, xs, xs_off = y_all, y_all, n_prompt // n_sample

    kv_shape = (n_sample, WINDOW, B_KV_HEADS, B_HEAD_DIM)
    stack = lambda k, f=lambda a: a: jnp.stack([f(o[k]) for o in per_layer])
    return (y_all[:n_prompt].reshape(bsz, t, d), y_all[n_prompt:].reshape(n_sample, 1, d),
            stack(0), stack(1), stack(2),
            stack(3, lambda a: a.reshape(kv_shape)), stack(4, lambda a: a.reshape(kv_shape)), stack(5))
```

```python
import functools

import numpy as np
import jax
import jax.numpy as jnp
from jax import lax
from jax.experimental import pallas as pl
from jax.experimental.pallas import tpu as pltpu

F32 = jnp.float32
BF16 = jnp.bfloat16
I32 = jnp.int32

D_MODEL = 1024
DEPTH = 2
PAST_LEN = 16384
A_HEADS = 4
A_DK = 128
A_DV = 128
A_KEY = A_HEADS * A_DK
A_WIDTH = A_HEADS * A_DV
B_Q_HEADS = 8
B_KV_HEADS = 2
B_HEAD_DIM = 64
B_GROUP = B_Q_HEADS // B_KV_HEADS
B_WIDTH = B_Q_HEADS * B_HEAD_DIM
B_KV_WIDTH = B_KV_HEADS * B_HEAD_DIM
WINDOW = 128
ROPE_THETA = 10000.0
ATTN_SCALE = B_HEAD_DIM ** -0.5
N_EXPERTS = 64
TOP_K = 8
N_GROUPS = 8
GROUP_SIZE = N_EXPERTS // N_GROUPS
TOPK_GROUPS = 4
D_EXPERT = D_MODEL // 4
D_SHARED = D_EXPERT
ROUTED_SCALE = 2.5
DN_ALPHA = (2 * DEPTH) ** 0.25
LN_EPS = 1e-5
RMS_EPS = 1e-6
NEG_BIG = -1e30
TINY = 1.1754944e-38
OFF_AF = A_KEY
OFF_AI = 2 * A_KEY
OFF_AG = OFF_AI + A_WIDTH
OFF_BQ = OFF_AG + A_WIDTH
OFF_BK = OFF_BQ + B_WIDTH
OFF_BV = OFF_BK + B_KV_WIDTH
OFF_GA = OFF_BV + B_KV_WIDTH
OFF_GB = OFF_GA + D_MODEL
IN_COLS = OFF_GB + D_MODEL

LANES = 128
HGRN_CHUNK = 128
HGRN_LEVELS = 7
ROW_TILE = 512
MOE_ROW_TILE = 384
EXPERT_ROW_TILE = 256
VMEM_LIMIT = 56 * 1024 * 1024


def _cparams(sem, vmem=VMEM_LIMIT, checked_dma=True):
    return pltpu.CompilerParams(dimension_semantics=sem, vmem_limit_bytes=vmem,
                                disable_bounds_checks=not checked_dma)


def _dot(a, b):
    return jnp.dot(a, b, preferred_element_type=F32)


def _dot_nt(a, b):
    return lax.dot_general(a, b, (((1,), (1,)), ((), ())), preferred_element_type=F32)


def _dot_tn(a, b):
    return lax.dot_general(a, b, (((0,), (0,)), ((), ())), preferred_element_type=F32)


def _sigmoid(x):
    return 1.0 / (1.0 + jnp.exp(-x))


def _silu(x):
    return x * _sigmoid(x)


def _split3(x):
    hi = x.astype(BF16)
    r1 = x - hi.astype(F32)
    mid = r1.astype(BF16)
    lo = (r1 - mid.astype(F32)).astype(BF16)
    return hi, mid, lo


def _layer_norm(y, g, b):
    mu = jnp.mean(y, axis=-1, keepdims=True)
    d = y - mu
    var = jnp.mean(d * d, axis=-1, keepdims=True)
    return d * lax.rsqrt(var + LN_EPS) * g + b


def _log_forget(af, lower):
    ls = jnp.minimum(af, 0.0) - jnp.log1p(jnp.exp(-jnp.abs(af)))
    a = jnp.log(jnp.maximum(lower, TINY))
    b = jnp.log1p(-lower) + ls
    mixed = jnp.maximum(a, b) + jnp.log1p(jnp.exp(-jnp.abs(a - b)))
    return jnp.where(lower > 0.0, mixed, ls)


def _proj_kernel(x_ref, w_ref, cos_ref, sin_ref, low_ref,
                 qa_ref, lf_ref, va_ref, ga_ref, qb_ref, kb_ref, vb_ref, sga_ref, sgb_ref):
    xb = x_ref[...].astype(BF16)

    def mm(c0, n):
        return _dot(xb, w_ref[:, c0:c0 + n])

    qa_ref[...] = _silu(mm(0, A_KEY)).astype(BF16)
    lf_ref[...] = _log_forget(mm(OFF_AF, A_KEY), low_ref[...])
    va_ref[...] = mm(OFF_AI, A_WIDTH).astype(BF16)
    ga_ref[...] = _silu(mm(OFF_AG, A_WIDTH)).astype(BF16)

    cos = cos_ref[...]
    sin = sin_ref[...]
    lane = lax.broadcasted_iota(I32, cos.shape, 1)
    first_half = (lane & (B_HEAD_DIM // 2)) == 0

    def rope(blk):
        partner = jnp.where(first_half, pltpu.roll(blk, LANES - B_HEAD_DIM // 2, 1),
                            pltpu.roll(blk, B_HEAD_DIM // 2, 1))
        return blk * cos + partner * sin

    bq = mm(OFF_BQ, B_WIDTH)
    for j in range(B_WIDTH // LANES):
        sl = slice(j * LANES, (j + 1) * LANES)
        qb_ref[:, sl] = (rope(bq[:, sl]) * ATTN_SCALE).astype(BF16)
    kb_ref[...] = rope(mm(OFF_BK, B_KV_WIDTH))
    vb_ref[...] = mm(OFF_BV, B_KV_WIDTH)
    sga_ref[...] = _sigmoid(mm(OFF_GA, D_MODEL)).astype(BF16)
    sgb_ref[...] = _sigmoid(mm(OFF_GB, D_MODEL)).astype(BF16)


def _proj(x, w_bf, cos_t, sin_t, lower, n_rows, row_off_blocks, tm):
    nt = n_rows // tm
    tab_blocks = cos_t.shape[0] // tm
    row = lambda w: pl.BlockSpec((tm, w), lambda i: (i, 0))
    outs = [(A_KEY, BF16), (A_KEY, F32), (A_WIDTH, BF16), (A_WIDTH, BF16), (B_WIDTH, BF16),
            (B_KV_WIDTH, F32), (B_KV_WIDTH, F32), (D_MODEL, BF16), (D_MODEL, BF16)]
    return pl.pallas_call(
        _proj_kernel,
        grid=(nt,),
        in_specs=[pl.BlockSpec((tm, D_MODEL), lambda i: (i + row_off_blocks, 0)),
                  pl.BlockSpec((D_MODEL, IN_COLS), lambda i: (0, 0)),
                  pl.BlockSpec((tm, LANES), lambda i: (i % tab_blocks, 0)),
                  pl.BlockSpec((tm, LANES), lambda i: (i % tab_blocks, 0)),
                  pl.BlockSpec((1, A_KEY), lambda i: (0, 0))],
        out_specs=[row(w) for w, _ in outs],
        out_shape=[jax.ShapeDtypeStruct((n_rows, w), dt) for w, dt in outs],
        compiler_params=_cparams(("arbitrary",)),
        name="proj",
    )(x, w_bf, cos_t, sin_t, lower)


def _rope_tables(pos):
    half = B_HEAD_DIM // 2
    inv = ROPE_THETA ** (-jnp.arange(half, dtype=F32) / half)
    ang = pos.astype(F32)[:, None] * inv[None, :]
    cos = jnp.cos(ang)
    sin = jnp.sin(ang)
    reps = LANES // B_HEAD_DIM
    cos_t = jnp.tile(jnp.concatenate([cos, cos], axis=1), (1, reps))
    sin_t = jnp.tile(jnp.concatenate([-sin, sin], axis=1), (1, reps))
    return cos_t, sin_t


def _hgrn_constants():
    c = HGRN_CHUNK
    r = np.arange(c)
    tri = (r[None, :] <= r[:, None]).astype(np.float32)
    sel = np.zeros((HGRN_LEVELS, c, c), np.float32)
    upper = np.zeros((HGRN_LEVELS, c, A_KEY), np.float32)
    pair = np.zeros((HGRN_LEVELS + 1, c, c), np.float32)
    for l in range(HGRN_LEVELS):
        b = c >> (l + 1)
        ref_row = (r // (2 * b)) * (2 * b) + b - 1
        sel[l, r, ref_row] = 1.0
        up = (r % (2 * b)) >= b
        upper[l] = up[:, None]
        same = (r[:, None] // (2 * b)) == (r[None, :] // (2 * b))
        pair[l] = (up[:, None] & ~up[None, :] & same)
    pair[HGRN_LEVELS] = np.eye(c)
    return (jnp.asarray(tri, BF16), jnp.asarray(sel.reshape(HGRN_LEVELS * c, c), BF16),
            jnp.asarray(upper), jnp.asarray(pair))


def _hgrn_kernel(qa_ref, lf_ref, va_ref, ga_ref, g_ref, tri_ref, sel_ref, up_ref, pair_ref,
                 oa_ref, st_ref, s_scr):
    c = HGRN_CHUNK
    step = pl.program_id(1)

    @pl.when(step == 0)
    def _():
        s_scr[...] = jnp.zeros_like(s_scr)

    lf = lf_ref[...]
    tri = tri_ref[...]
    sel = sel_ref[...]
    hi, mid, lo = _split3(lf)
    gcum = _dot(tri, hi) + _dot(tri, mid) + _dot(tri, lo)
    ghi, gmid, glo = _split3(gcum)
    gref = _dot(sel, ghi) + _dot(sel, gmid) + _dot(sel, glo)
    qb = qa_ref[...]
    qf = qb.astype(F32)
    kf = 1.0 - jnp.exp(lf)
    kb = kf.astype(BF16)
    vb = va_ref[...]
    ws = []
    for l in range(HGRN_LEVELS):
        e = jnp.exp(-jnp.abs(gcum - gref[l * c:(l + 1) * c]))
        ws.append((jnp.where(up_ref[l] > 0.5, qf, kf) * e).astype(BF16))
    qg = (qf * jnp.exp(gcum)).astype(BF16)
    gend = gcum[c - 1:c, :]
    kend = (kf * jnp.exp(gend - gcum)).astype(BF16)
    decay = jnp.exp(gend)
    gate = g_ref[...]

    for h in range(A_HEADS):
        sl = slice(h * A_DK, (h + 1) * A_DK)
        att = _dot_nt(qb[:, sl], kb[:, sl]) * pair_ref[HGRN_LEVELS]
        for l in range(HGRN_LEVELS):
            w = ws[l][:, sl]
            att = att + _dot_nt(w, w) * pair_ref[l]
        s_t = s_scr[h]
        o = _dot_nt(qg[:, sl], s_t.astype(BF16)) + _dot(att.astype(BF16), vb[:, sl])
        ms = jnp.mean(o * o, axis=-1, keepdims=True)
        on = o * lax.rsqrt(ms + RMS_EPS) * gate * ga_ref[:, sl].astype(F32)
        oa_ref[:, sl] = on.astype(BF16)
        s_scr[h] = s_t * decay[:, sl] + _dot_tn(vb[:, sl], kend[:, sl])

    @pl.when(step == pl.num_programs(1) - 1)
    def _():
        for h in range(A_HEADS):
            st_ref[0, h] = s_scr[h].T


def _hgrn_prompt(qa, lf, va, ga, norm_g, bsz, t):
    c = HGRN_CHUNK
    nc = t // c
    tri, sel, upper, pair = _hgrn_constants()
    blk = lambda: pl.BlockSpec((c, A_KEY), lambda b, i: (b * nc + i, 0))
    const = lambda a: pl.BlockSpec(a.shape, lambda b, i: (0,) * a.ndim)
    return pl.pallas_call(
        _hgrn_kernel,
        grid=(bsz, nc),
        in_specs=[blk(), blk(), blk(), blk(), pl.BlockSpec((1, A_DV), lambda b, i: (0, 0)),
                  const(tri), const(sel), const(upper), const(pair)],
        out_specs=[blk(), pl.BlockSpec((1, A_HEADS, A_DK, A_DV), lambda b, i: (b, 0, 0, 0))],
        out_shape=[jax.ShapeDtypeStruct((bsz * t, A_WIDTH), BF16),
                   jax.ShapeDtypeStruct((bsz, A_HEADS, A_DK, A_DV), F32)],
        scratch_shapes=[pltpu.VMEM((A_HEADS, A_DV, A_DK), F32)],
        compiler_params=_cparams(("arbitrary", "arbitrary")),
        name="hgrn_prompt",
    )(qa, lf, va, ga, norm_g, tri, sel, upper, pair)


def _swa_kernel(sink_ref, q_ref, k_ref, v_ref, o_ref, kprev, vprev):
    w = WINDOW
    i = pl.program_id(1)

    @pl.when(i == 0)
    def _():
        kprev[...] = jnp.zeros_like(kprev)
        vprev[...] = jnp.zeros_like(vprev)

    kc = k_ref[...]
    vc = v_ref[...]
    kk = jnp.concatenate([kprev[...], kc], axis=0)
    vv = jnp.concatenate([vprev[...], vc], axis=0)
    kr = pltpu.roll(kk, B_HEAD_DIM, 1)
    vr = pltpu.roll(vv, B_HEAD_DIM, 1)
    lo2 = lax.broadcasted_iota(I32, kk.shape, 1) < B_HEAD_DIM
    zero = jnp.zeros_like(kk)
    k_lo = [jnp.where(lo2, kk, zero).astype(BF16), jnp.where(lo2, kr, zero).astype(BF16)]
    k_hi = [jnp.where(lo2, zero, kr).astype(BF16), jnp.where(lo2, zero, kk).astype(BF16)]
    v_dup = [jnp.where(lo2, vv, vr).astype(BF16), jnp.where(lo2, vr, vv).astype(BF16)]

    qi = lax.broadcasted_iota(I32, (w, 2 * w), 0)
    kj = lax.broadcasted_iota(I32, (w, 2 * w), 1)
    valid = (kj >= qi) & (kj <= qi + w) & ((kj >= w) | (i > 0))
    lo1 = lax.broadcasted_iota(I32, (w, LANES), 1) < B_HEAD_DIM

    for j in range(B_WIDTH // LANES):
        g = (2 * j) // B_GROUP
        qblk = q_ref[:, j * LANES:(j + 1) * LANES]
        res = []
        for half, kmat in enumerate((k_lo[g], k_hi[g])):
            sk = sink_ref[2 * j + half]
            s = jnp.where(valid, _dot_nt(qblk, kmat), NEG_BIG)
            m = jnp.maximum(jnp.max(s, axis=-1, keepdims=True), sk)
            p = jnp.exp(s - m)
            denom = jnp.sum(p, axis=-1, keepdims=True) + jnp.exp(sk - m)
            res.append(_dot(p.astype(BF16), v_dup[g]) / denom)
        o_ref[:, j * LANES:(j + 1) * LANES] = jnp.where(lo1, res[0], res[1]).astype(BF16)

    kprev[...] = kc
    vprev[...] = vc


def _swa_prompt(sinks, qb, kb, vb, bsz, t):
    w = WINDOW
    nb = t // w
    return pl.pallas_call(
        _swa_kernel,
        grid_spec=pltpu.PrefetchScalarGridSpec(
            num_scalar_prefetch=1,
            grid=(bsz, nb),
            in_specs=[pl.BlockSpec((w, B_WIDTH), lambda b, i, s: (b * nb + i, 0)),
                      pl.BlockSpec((w, B_KV_WIDTH), lambda b, i, s: (b * nb + i, 0)),
                      pl.BlockSpec((w, B_KV_WIDTH), lambda b, i, s: (b * nb + i, 0))],
            out_specs=pl.BlockSpec((w, B_WIDTH), lambda b, i, s: (b * nb + i, 0)),
            scratch_shapes=[pltpu.VMEM((w, B_KV_WIDTH), F32), pltpu.VMEM((w, B_KV_WIDTH), F32)]),
        out_shape=jax.ShapeDtypeStruct((bsz * t, B_WIDTH), BF16),
        compiler_params=_cparams(("arbitrary", "arbitrary")),
        name="swa_prompt",
    )(sinks, qb, kb, vb)


SAMPLE_BLOCK = 8


def _sample_kernel(sink_ref, st_ref, lft_ref, qat_ref, va_ref, ga_ref, g_ref, q3_ref, kn_ref, vn_ref,
                   ck_ref, cv_ref, st_out, oa_ref, ob_ref, ck_out, cv_out, o_scr):
    w = WINDOW
    row = lax.broadcasted_iota(I32, (w, B_KV_WIDTH), 0)
    for i in range(SAMPLE_BLOCK):
        for h in range(A_HEADS):
            sl = slice(h * A_DV, (h + 1) * A_DV)
            fcol = jnp.exp(lft_ref[h, 0][:, i:i + 1])
            qcol = qat_ref[h, 0][:, i:i + 1]
            vrow = va_ref[i:i + 1, sl].astype(F32)
            s_new = st_ref[i, h] * fcol + (1.0 - fcol) * vrow
            st_out[i, h] = s_new
            o_scr[i:i + 1, sl] = jnp.sum(s_new * qcol, axis=0, keepdims=True)
        kc = ck_ref[i]
        vc = cv_ref[i]
        kn = kn_ref[i:i + 1, :]
        vn = vn_ref[i:i + 1, :]
        q3 = q3_ref[i]
        s = _dot_nt(q3.astype(BF16), kc.astype(BF16))
        s_new_key = jnp.sum(q3 * kn, axis=-1, keepdims=True)
        sk = sink_ref[...][:, 0:1]
        m = jnp.maximum(jnp.maximum(jnp.max(s, axis=-1, keepdims=True), s_new_key), sk)
        p = jnp.exp(s - m)
        pn = jnp.exp(s_new_key - m)
        denom = jnp.sum(p, axis=-1, keepdims=True) + pn + jnp.exp(sk - m)
        ob_ref[i] = (_dot(p.astype(BF16), vc.astype(BF16)) + pn * vn) / denom
        ck_out[i] = jnp.where(row == w - 1, kn, pltpu.roll(kc, w - 1, 0))
        cv_out[i] = jnp.where(row == w - 1, vn, pltpu.roll(vc, w - 1, 0))
    for h in range(A_HEADS):
        sl = slice(h * A_DV, (h + 1) * A_DV)
        o = o_scr[:, sl]
        ms = jnp.mean(o * o, axis=-1, keepdims=True)
        oa_ref[:, sl] = (o * lax.rsqrt(ms + RMS_EPS) * g_ref[...] * ga_ref[:, sl].astype(F32)).astype(BF16)


def _sample_step(l, sinks8, state, lft, qat, va, ga, norm_g, q3, kn, vn, ck, cv):
    nb = state.shape[1]
    sb = SAMPLE_BLOCK
    steps = nb // sb
    w = WINDOW
    b4 = lambda: pl.BlockSpec((sb, A_HEADS, A_DK, A_DV), lambda i: (i, 0, 0, 0))
    b4_in = pl.BlockSpec((None, sb, A_HEADS, A_DK, A_DV), lambda i: (l, i, 0, 0, 0))
    c3_in = lambda: pl.BlockSpec((None, sb, w, B_KV_WIDTH), lambda i: (l, i, 0, 0))
    t4 = lambda: pl.BlockSpec((A_HEADS, 1, A_DK, sb), lambda i: (0, i, 0, 0))
    r2 = lambda wd: pl.BlockSpec((sb, wd), lambda i: (i, 0))
    c3 = lambda: pl.BlockSpec((sb, w, B_KV_WIDTH), lambda i: (i, 0, 0))
    return pl.pallas_call(
        _sample_kernel,
        grid=(steps,),
        in_specs=[pl.BlockSpec((B_Q_HEADS, LANES), lambda i: (0, 0)),
                  b4_in, t4(), t4(), r2(A_WIDTH), r2(A_WIDTH), pl.BlockSpec((1, A_DV), lambda i: (0, 0)),
                  pl.BlockSpec((sb, B_Q_HEADS, LANES), lambda i: (i, 0, 0)), r2(B_KV_WIDTH), r2(B_KV_WIDTH),
                  c3_in(), c3_in()],
        out_specs=[b4(), r2(A_WIDTH), pl.BlockSpec((sb, B_Q_HEADS, LANES), lambda i: (i, 0, 0)), c3(), c3()],
        out_shape=[jax.ShapeDtypeStruct(state.shape[1:], F32),
                   jax.ShapeDtypeStruct((nb, A_WIDTH), BF16),
                   jax.ShapeDtypeStruct((nb, B_Q_HEADS, LANES), F32),
                   jax.ShapeDtypeStruct((nb, w, B_KV_WIDTH), F32),
                   jax.ShapeDtypeStruct((nb, w, B_KV_WIDTH), F32)],
        scratch_shapes=[pltpu.VMEM((sb, A_WIDTH), F32)],
        compiler_params=_cparams(("arbitrary",)),
        name="sample_step",
    )(sinks8, state, lft, qat, va, ga, norm_g, q3, kn, vn, ck, cv)


def _merge_kernel(x_ref, oa_ref, ob_ref, sga_ref, sgb_ref, wa_ref, wb_ref, wo_ref, g_ref, b_ref, h_ref):
    merged = (sga_ref[...].astype(F32) * _dot(oa_ref[...], wa_ref[...])
              + sgb_ref[...].astype(F32) * _dot(ob_ref[...], wb_ref[...]))
    mix = _dot(merged.astype(BF16), wo_ref[...])
    h_ref[...] = _layer_norm(DN_ALPHA * x_ref[...] + mix, g_ref[...], b_ref[...])


def _merge(x, oa, ob, sga, sgb, wa, wb, wo, g, b, n_rows, x_off_blocks, out_rows, out_off_blocks, tm, h_prev=None):
    nt = n_rows // tm
    row = lambda wd: pl.BlockSpec((tm, wd), lambda i: (i, 0))
    const = lambda a: pl.BlockSpec(a.shape, lambda i: (0, 0))
    args = [x, oa, ob, sga, sgb, wa, wb, wo, g, b]
    in_specs = [pl.BlockSpec((tm, D_MODEL), lambda i: (i + x_off_blocks, 0)),
                row(A_WIDTH), row(B_WIDTH), row(D_MODEL), row(D_MODEL),
                const(wa), const(wb), const(wo), const(g), const(b)]
    kern = _merge_kernel
    aliases = {}
    if h_prev is not None:
        args.append(h_prev)
        in_specs.append(pl.BlockSpec(memory_space=pl.ANY))
        aliases = {len(args) - 1: 0}
        kern = lambda *refs: _merge_kernel(*refs[:10], refs[11])
    return pl.pallas_call(
        kern,
        grid=(nt,),
        in_specs=in_specs,
        out_specs=pl.BlockSpec((tm, D_MODEL), lambda i: (i + out_off_blocks, 0)),
        out_shape=jax.ShapeDtypeStruct((out_rows, D_MODEL), F32),
        input_output_aliases=aliases,
        compiler_params=_cparams(("arbitrary",)),
        name="merge",
    )(*args)


def _mixer_layer(l, xp, xs, xs_off, n_prompt, bsz, t, n_sample, prm):
    tm = ROW_TILE
    p = _proj(xp, prm["w_in"][l], prm["cos_p"], prm["sin_p"], prm["lower"][l], n_prompt, 0, tm)
    qa, lf, va, ga, qb, kb, vb, sga, sgb = p
    oa, st_p = _hgrn_prompt(qa, lf, va, ga, prm["norm_g"][l], bsz, t)
    ob = _swa_prompt(prm["sinks"][l], qb, kb, vb, bsz, t)
    n_all = n_prompt + n_sample
    h_all = _merge(xp, oa, ob, sga, sgb, prm["wa"][l], prm["wb"][l], prm["wo"][l], prm["ln1_g"][l],
                   prm["ln1_b"][l], n_prompt, 0, n_all, 0, tm, h_prev=jnp.zeros((n_all, D_MODEL), F32))
    kp = kb.reshape(bsz, t, B_KV_HEADS, B_HEAD_DIM)[:, -WINDOW:]
    vp = vb.reshape(bsz, t, B_KV_HEADS, B_HEAD_DIM)[:, -WINDOW:]

    ts = n_sample
    ps = _proj(xs, prm["w_in"][l], prm["cos_s"], prm["sin_s"], prm["lower"][l], ts, xs_off, ts)
    qa_s, lf_s, va_s, ga_s, qb_s, kb_s, vb_s, sga_s, sgb_s = ps
    sb = SAMPLE_BLOCK
    to_t = lambda a: a.reshape(ts // sb, sb, A_HEADS, A_DK).transpose(2, 0, 3, 1)
    qh = qb_s.astype(F32).reshape(ts, B_Q_HEADS, B_HEAD_DIM)
    z = jnp.zeros_like(qh[:, :B_GROUP])
    q3 = jnp.concatenate([jnp.concatenate([qh[:, :B_GROUP], z], axis=-1),
                          jnp.concatenate([z, qh[:, B_GROUP:]], axis=-1)], axis=1)
    st_s, oa_s, ob3, ck_s, cv_s = _sample_step(
        l, prm["sinks8"][l], prm["state"], to_t(lf_s), to_t(qa_s.astype(F32)), va_s, ga_s, prm["norm_g"][l],
        q3, kb_s, vb_s, prm["cache_k"], prm["cache_v"])
    ob_s = jnp.concatenate([ob3[:, :B_GROUP, :B_HEAD_DIM], ob3[:, B_GROUP:, B_HEAD_DIM:]], axis=1)
    ob_s = ob_s.reshape(ts, B_WIDTH).astype(BF16)
    h_all = _merge(xs, oa_s, ob_s, sga_s, sgb_s, prm["wa"][l], prm["wb"][l], prm["wo"][l], prm["ln1_g"][l],
                   prm["ln1_b"][l], ts, xs_off, n_all, n_prompt // ts, ts, h_prev=h_all)
    return h_all, (kp, vp, st_p, ck_s, cv_s, st_s)


def _router_kernel(h_ref, wr_ref, bias_ref, e_ref, w_ref, mask_ref, cnt_ref):
    tm = h_ref.shape[0]
    gsz = GROUP_SIZE

    @pl.when(pl.program_id(0) == 0)
    def _():
        cnt_ref[...] = jnp.zeros_like(cnt_ref)

    logits = lax.dot_general(wr_ref[...], h_ref[...], (((1,), (1,)), ((), ())),
                             precision=lax.Precision.HIGHEST, preferred_element_type=F32)
    scores = _sigmoid(logits)
    sel = scores + bias_ref[...][:, 0:1]
    rowi = lax.broadcasted_iota(I32, (gsz, tm), 0)
    neg_inf = -jnp.inf
    blocks = [sel[g * gsz:(g + 1) * gsz] for g in range(N_GROUPS)]
    sblocks = [scores[g * gsz:(g + 1) * gsz] for g in range(N_GROUPS)]

    gscore = []
    for blk in blocks:
        m1 = jnp.max(blk, axis=0, keepdims=True)
        i1 = jnp.min(jnp.where(blk == m1, rowi, gsz), axis=0, keepdims=True)
        m2 = jnp.max(jnp.where(rowi == i1, neg_inf, blk), axis=0, keepdims=True)
        gscore.append(m1 + m2)
    work = []
    for g in range(N_GROUPS):
        ahead = jnp.zeros((1, tm), I32)
        for g2 in range(N_GROUPS):
            if g2 != g:
                beats = (gscore[g2] > gscore[g]) | ((gscore[g2] == gscore[g]) & (g2 < g))
                ahead = ahead + beats.astype(I32)
        work.append(jnp.where(ahead < TOPK_GROUPS, blocks[g], NEG_BIG))

    chosen = [jnp.zeros((gsz, tm), F32) for _ in range(N_GROUPS)]
    es, ws = [], []
    for _ in range(TOP_K):
        m = work[0]
        for g in range(1, N_GROUPS):
            m = jnp.maximum(m, work[g])
        m = jnp.max(m, axis=0, keepdims=True)
        cand = jnp.where(work[0] == m, rowi, N_EXPERTS)
        for g in range(1, N_GROUPS):
            cand = jnp.minimum(cand, jnp.where(work[g] == m, rowi + g * gsz, N_EXPERTS))
        idx = jnp.min(cand, axis=0, keepdims=True)
        wj = jnp.zeros((1, tm), F32)
        for g in range(N_GROUPS):
            hit = (rowi + g * gsz) == idx
            wj = wj + jnp.sum(jnp.where(hit, sblocks[g], 0.0), axis=0, keepdims=True)
            chosen[g] = jnp.where(hit, 1.0, chosen[g])
            work[g] = jnp.where(hit, neg_inf, work[g])
        es.append(idx)
        ws.append(wj)
    wsum = ws[0]
    for j in range(1, TOP_K):
        wsum = wsum + ws[j]
    for j in range(TOP_K):
        e_ref[j:j + 1, :] = es[j]
        w_ref[j:j + 1, :] = ws[j] / wsum * ROUTED_SCALE
    for g in range(N_GROUPS):
        rows = slice(g * gsz, (g + 1) * gsz)
        mask_ref[rows, :] = chosen[g]
        part = chosen[g][:, 0:LANES]
        for c in range(1, tm // LANES):
            part = part + chosen[g][:, c * LANES:(c + 1) * LANES]
        cnt_ref[rows, :] = cnt_ref[rows, :] + part


def _router(h_all, wr_t, bias_b, tm):
    n = h_all.shape[0]
    col = lambda r: pl.BlockSpec((r, tm), lambda i: (0, i))
    return pl.pallas_call(
        _router_kernel,
        grid=(n // tm,),
        in_specs=[pl.BlockSpec((tm, D_MODEL), lambda i: (i, 0)),
                  pl.BlockSpec((N_EXPERTS, D_MODEL), lambda i: (0, 0)),
                  pl.BlockSpec((N_EXPERTS, LANES), lambda i: (0, 0))],
        out_specs=[col(TOP_K), col(TOP_K), col(N_EXPERTS), pl.BlockSpec((N_EXPERTS, LANES), lambda i: (0, 0))],
        out_shape=[jax.ShapeDtypeStruct((TOP_K, n), I32), jax.ShapeDtypeStruct((TOP_K, n), F32),
                   jax.ShapeDtypeStruct((N_EXPERTS, n), F32), jax.ShapeDtypeStruct((N_EXPERTS, LANES), F32)],
        compiler_params=_cparams(("arbitrary",)),
        name="router",
    )(h_all, wr_t, bias_b)


def _rank_kernel(mask_ref, e_ref, offs_ref, triu_ref, dest_ref, carry):
    tm = mask_ref.shape[1]

    @pl.when(pl.program_id(0) == 0)
    def _():
        carry[...] = jnp.zeros_like(carry)

    mk = mask_ref[...]
    rank = _dot(mk.astype(BF16), triu_ref[...])
    dest_full = rank + (offs_ref[...][:, 0:1] + carry[...][:, 0:1])
    rowi = lax.broadcasted_iota(I32, (N_EXPERTS, tm), 0)
    for j in range(TOP_K):
        d = jnp.sum(jnp.where(rowi == e_ref[j:j + 1, :], dest_full, 0.0), axis=0, keepdims=True)
        dest_ref[j:j + 1, :] = d.astype(I32)
    carry[...] = carry[...] + jnp.sum(mk, axis=1, keepdims=True)


def _rank(mask_t, e_t, offs_b, tm):
    n = mask_t.shape[1]
    r = np.arange(tm)
    triu = jnp.asarray((r[:, None] < r[None, :]).astype(np.float32), BF16)
    return pl.pallas_call(
        _rank_kernel,
        grid=(n // tm,),
        in_specs=[pl.BlockSpec((N_EXPERTS, tm), lambda i: (0, i)),
                  pl.BlockSpec((TOP_K, tm), lambda i: (0, i)),
                  pl.BlockSpec((N_EXPERTS, LANES), lambda i: (0, 0)),
                  pl.BlockSpec((tm, tm), lambda i: (0, 0))],
        out_specs=pl.BlockSpec((TOP_K, tm), lambda i: (0, i)),
        out_shape=jax.ShapeDtypeStruct((TOP_K, n), I32),
        scratch_shapes=[pltpu.VMEM((N_EXPERTS, LANES), F32)],
        compiler_params=_cparams(("arbitrary",)),
        name="rank",
    )(mask_t, e_t, offs_b, triu)


INV_COLS = 512
TOKEN_RADIX = 128


def _inverse_kernel(dest_ref, inv_ref):
    tm = dest_ref.shape[1]
    nq = inv_ref.shape[0]

    @pl.when(pl.program_id(0) == 0)
    def _():
        inv_ref[...] = jnp.zeros_like(inv_ref)

    tok = pl.program_id(0) * tm + lax.broadcasted_iota(I32, (1, tm), 1)
    t_hi = jnp.right_shift(tok, TOKEN_RADIX.bit_length() - 1).astype(F32)
    t_lo = jnp.bitwise_and(tok, TOKEN_RADIX - 1).astype(F32)
    qi = lax.broadcasted_iota(I32, (nq, tm), 0)
    si = lax.broadcasted_iota(I32, (INV_COLS, tm), 0)
    acc_hi = jnp.zeros(inv_ref.shape, F32)
    acc_lo = jnp.zeros(inv_ref.shape, F32)
    for j in range(TOP_K):
        d = dest_ref[j:j + 1, :]
        at_q = qi == jnp.right_shift(d, INV_COLS.bit_length() - 1)
        col = jnp.where(si == jnp.bitwise_and(d, INV_COLS - 1), 1.0, 0.0).astype(BF16)
        acc_hi = acc_hi + _dot_nt(jnp.where(at_q, t_hi, 0.0).astype(BF16), col)
        acc_lo = acc_lo + _dot_nt(jnp.where(at_q, t_lo, 0.0).astype(BF16), col)
    inv_ref[...] = inv_ref[...] + (acc_hi * float(TOKEN_RADIX) + acc_lo)


def _inverse_map(dest_t, tm):
    n = dest_t.shape[1]
    n_rows = n * TOP_K
    nq = -(-(n_rows // INV_COLS) // 8) * 8
    inv = pl.pallas_call(
        _inverse_kernel,
        grid=(n // tm,),
        in_specs=[pl.BlockSpec((TOP_K, tm), lambda i: (0, i))],
        out_specs=pl.BlockSpec((nq, INV_COLS), lambda i: (0, 0)),
        out_shape=jax.ShapeDtypeStruct((nq, INV_COLS), F32),
        compiler_params=_cparams(("arbitrary",)),
        name="inverse_map",
    )(dest_t)
    return inv.reshape(-1)[:n_rows].astype(I32).reshape(n_rows // EXPERT_ROW_TILE, 1, EXPERT_ROW_TILE)


SUBLANES = 8


def _tile_row_copy(src_ref, src_row, dst_ref, dst_group, dst_sub, sem):
    return pltpu.make_async_copy(
        src_ref.at[jnp.right_shift(src_row, 3), pl.ds(jnp.bitwise_and(src_row, SUBLANES - 1), 1)],
        dst_ref.at[dst_group, pl.ds(dst_sub, 1)], sem)


def _expert_kernel(n_tiles, tile_ref, exp_ref, valid_ref, first_ref, newexp_ref, lo_ref, hi_ref,
                   inv_ref, invn_ref, h_ref, wg_ref, wu_ref, wd_ref, y_ref, xbuf, wgb, wub, wdb, sem):
    i = pl.program_id(0)
    rows = xbuf.shape[1] * SUBLANES
    tile = tile_ref[i]
    slot = tile % 2
    valid = valid_ref[i] == 1
    first = first_ref[i] == 1

    def gather(idx_ref, s):
        def issue(k, c):
            for u in range(SUBLANES):
                _tile_row_copy(h_ref, idx_ref[0, 0, k * SUBLANES + u], xbuf.at[s], k, u, sem.at[s]).start()
            return c

        lax.fori_loop(0, rows // SUBLANES, issue, 0)

    def wait_rows(s):
        pltpu.make_async_copy(h_ref.at[pl.ds(0, rows // SUBLANES)], xbuf.at[s], sem.at[s]).wait()

    @pl.when(valid & first)
    def _():
        @pl.when(i == 0)
        def _():
            gather(inv_ref, 0)

        wait_rows(slot)

    @pl.when(valid & (newexp_ref[i] == 1))
    def _():
        wgb[...] = wg_ref[0].astype(BF16)
        wub[...] = wu_ref[0].astype(BF16)
        wdb[...] = wd_ref[0].astype(BF16)

    def compute(first_visit):
        xb = xbuf[slot].reshape(rows, D_MODEL).astype(BF16)
        if first_visit:
            for r in range(rows):
                _tile_row_copy(h_ref, invn_ref[0, 0, r], xbuf.at[1 - slot], r // SUBLANES, r % SUBLANES,
                               sem.at[1 - slot]).start()
        gate = _dot(xb, wgb[...])
        up = _dot(xb, wub[...])
        y = _dot((_silu(gate) * up).astype(BF16), wdb[...])
        rowi = lax.broadcasted_iota(I32, y.shape, 0)
        mine = (rowi >= lo_ref[i]) & (rowi < hi_ref[i])
        y_ref[...] = jnp.where(mine, y, 0.0 if first_visit else y_ref[...])

    @pl.when(valid & first)
    def _():
        compute(True)

    @pl.when(valid & jnp.logical_not(first))
    def _():
        compute(False)

    @pl.when(valid & first & (tile == n_tiles - 1))
    def _():
        wait_rows(1 - slot)


def _group_metadata(counts, n_rows):
    tmo = EXPERT_ROW_TILE
    n_tiles = n_rows // tmo
    ends = jnp.cumsum(counts)
    offs = ends - counts
    first_tile = offs // tmo
    n_t = jnp.where(counts > 0, (ends - 1) // tmo - first_tile + 1, 0)
    cum = jnp.cumsum(n_t)
    base = cum - n_t
    n_items = n_tiles + N_EXPERTS
    idx = jnp.arange(n_items, dtype=I32)
    valid = (idx < cum[-1]).astype(I32)
    idc = jnp.minimum(idx, cum[-1] - 1)
    e = jnp.minimum(jnp.sum((cum[None, :] <= idc[:, None]).astype(I32), axis=1), N_EXPERTS - 1)
    tile = (first_tile[e] + idc - base[e]).astype(I32)
    one = jnp.ones((1,), I32)
    first = jnp.concatenate([one, (tile[1:] != tile[:-1]).astype(I32)])
    new_expert = jnp.concatenate([one, (e[1:] != e[:-1]).astype(I32)])
    lo = jnp.clip(offs[e] - tile * tmo, 0, tmo).astype(I32)
    hi = jnp.clip(ends[e] - tile * tmo, 0, tmo).astype(I32)
    return tile, e, valid, first, new_expert, lo, hi


def _experts(l, meta, inv3, h_all, wg, wu, wd):
    tmo = EXPERT_ROW_TILE
    n_rows = inv3.shape[0] * tmo
    n_items = meta[0].shape[0]
    wspec = lambda shp: pl.BlockSpec((None, 1) + shp, lambda i, tl, ex, *_: (l, ex[i], 0, 0))
    n_tiles = inv3.shape[0]
    idx_spec = lambda nxt: pl.BlockSpec(
        (1, 1, tmo), lambda i, tl, *_: (jnp.minimum(tl[i] + nxt, n_tiles - 1), 0, 0), memory_space=pltpu.SMEM)
    return pl.pallas_call(
        functools.partial(_expert_kernel, n_tiles),
        grid_spec=pltpu.PrefetchScalarGridSpec(
            num_scalar_prefetch=7,
            grid=(n_items,),
            in_specs=[idx_spec(0), idx_spec(1), pl.BlockSpec(memory_space=pl.ANY),
                      wspec((D_MODEL, D_EXPERT)), wspec((D_MODEL, D_EXPERT)), wspec((D_EXPERT, D_MODEL))],
            out_specs=pl.BlockSpec((tmo, D_MODEL), lambda i, tl, *_: (tl[i], 0)),
            scratch_shapes=[pltpu.VMEM((2, tmo // SUBLANES, SUBLANES, D_MODEL), F32),
                            pltpu.VMEM((D_MODEL, D_EXPERT), BF16), pltpu.VMEM((D_MODEL, D_EXPERT), BF16),
                            pltpu.VMEM((D_EXPERT, D_MODEL), BF16), pltpu.SemaphoreType.DMA((2,))]),
        out_shape=jax.ShapeDtypeStruct((n_rows, D_MODEL), F32),
        compiler_params=_cparams(("arbitrary",), checked_dma=False),
        name="experts",
    )(*meta, inv3, inv3, h_all.reshape(-1, SUBLANES, D_MODEL), wg, wu, wd)


def _post_kernel(dest_ref, h_ref, wt_ref, y_ref, wsg_ref, wsu_ref, wsd_ref, g_ref, b_ref, out_ref, buf, sem):
    tm = h_ref.shape[0]

    def issue(k, c):
        for u in range(SUBLANES):
            for j in range(TOP_K):
                _tile_row_copy(y_ref, dest_ref[j, k * SUBLANES + u], buf.at[j], k, u, sem).start()
        return c

    lax.fori_loop(0, tm // SUBLANES, issue, 0)

    h = h_ref[...]
    hb = h.astype(BF16)
    shared = _dot((_silu(_dot(hb, wsg_ref[...])) * _dot(hb, wsu_ref[...])).astype(BF16), wsd_ref[...])

    for j in range(TOP_K):
        pltpu.make_async_copy(y_ref.at[pl.ds(0, tm // SUBLANES)], buf.at[j], sem).wait()

    wt = wt_ref[...]
    routed = buf[0].reshape(tm, D_MODEL) * wt[:, 0:1]
    for j in range(1, TOP_K):
        routed = routed + buf[j].reshape(tm, D_MODEL) * wt[:, j:j + 1]
    out_ref[...] = _layer_norm(DN_ALPHA * h + (routed + shared), g_ref[...], b_ref[...])


def _post(dest_t, h_all, w_tok, y, wsg, wsu, wsd, g, b, tm):
    n = h_all.shape[0]
    const = lambda a: pl.BlockSpec(a.shape, lambda i: (0, 0))
    return pl.pallas_call(
        _post_kernel,
        grid=(n // tm,),
        in_specs=[pl.BlockSpec((TOP_K, tm), lambda i: (0, i), memory_space=pltpu.SMEM),
                  pl.BlockSpec((tm, D_MODEL), lambda i: (i, 0)),
                  pl.BlockSpec((tm, TOP_K), lambda i: (i, 0)),
                  pl.BlockSpec(memory_space=pl.ANY),
                  const(wsg), const(wsu), const(wsd), const(g), const(b)],
        out_specs=pl.BlockSpec((tm, D_MODEL), lambda i: (i, 0)),
        out_shape=jax.ShapeDtypeStruct((n, D_MODEL), F32),
        scratch_shapes=[pltpu.VMEM((TOP_K, tm // SUBLANES, SUBLANES, D_MODEL), F32),
                        pltpu.SemaphoreType.DMA(())],
        compiler_params=_cparams(("arbitrary",), checked_dma=False),
        name="moe_post",
    )(dest_t, h_all, w_tok, y.reshape(-1, SUBLANES, D_MODEL), wsg, wsu, wsd, g, b)


def _moe_layer(l, h_all, prm):
    tm = MOE_ROW_TILE
    e_t, w_t, mask_t, cnt = _router(h_all, prm["wr_t"][l], prm["rbias"][l], tm)
    counts = jnp.sum(cnt, axis=1).astype(I32)
    offs = jnp.cumsum(counts) - counts
    offs_b = jnp.broadcast_to(offs.astype(F32)[:, None], (N_EXPERTS, LANES))
    dest_t = _rank(mask_t, e_t, offs_b, tm)
    inv3 = _inverse_map(dest_t, tm)
    meta = _group_metadata(counts, h_all.shape[0] * TOP_K)
    y = _experts(l, meta, inv3, h_all, prm["w_exp_gate"], prm["w_exp_up"], prm["w_exp_down"])
    return _post(dest_t, h_all, w_t.T, y, prm["wsg"][l], prm["wsu"][l], prm["wsd"][l],
                 prm["ln2_g"][l], prm["ln2_b"][l], tm)


def kernel(x_prompt, x_sample, cache_k, cache_v, state_hgrn, w_in, hgrn_lower_bounds, hgrn_norm_g, attn_sinks, w_branch_a, w_branch_b, w_out, ln1_g, ln1_b, w_router, router_bias, w_exp_gate, w_exp_up, w_exp_down, w_sh_gate, w_sh_up, w_sh_down, ln2_g, ln2_b):
    bsz, t, d = x_prompt.shape
    n_sample = x_sample.shape[0] * x_sample.shape[1]
    n_prompt = bsz * t
    depth = w_in.shape[0]
    assert d == D_MODEL and x_sample.shape[1] == 1 and n_prompt % ROW_TILE == 0 and t % ROW_TILE == 0
    assert n_prompt % n_sample == 0 and (n_prompt + n_sample) % MOE_ROW_TILE == 0
    assert ((n_prompt + n_sample) * TOP_K) % EXPERT_ROW_TILE == 0 and n_sample % SAMPLE_BLOCK == 0

    lb_prob = jax.nn.softmax(hgrn_lower_bounds.astype(F32), axis=0)
    lower = (jnp.cumsum(lb_prob, axis=0) - lb_prob[0])[:, None, :]
    cos_p, sin_p = _rope_tables(jnp.arange(t))
    cos_s, sin_s = _rope_tables(jnp.full((n_sample,), PAST_LEN))
    row = lambda a: a[:, None, :]
    prm = dict(
        w_in=w_in.astype(BF16), lower=lower, cos_p=cos_p, sin_p=sin_p, cos_s=cos_s, sin_s=sin_s,
        norm_g=row(hgrn_norm_g), sinks=attn_sinks,
        sinks8=jnp.broadcast_to(attn_sinks[:, :, None], (depth, B_Q_HEADS, LANES)),
        wa=w_branch_a.astype(BF16), wb=w_branch_b.astype(BF16), wo=w_out.astype(BF16),
        ln1_g=row(ln1_g), ln1_b=row(ln1_b), ln2_g=row(ln2_g), ln2_b=row(ln2_b),
        wr_t=jnp.swapaxes(w_router, 1, 2),
        rbias=jnp.broadcast_to(router_bias[:, :, None], (depth, N_EXPERTS, LANES)),
        w_exp_gate=w_exp_gate, w_exp_up=w_exp_up, w_exp_down=w_exp_down,
        wsg=w_sh_gate.astype(BF16), wsu=w_sh_up.astype(BF16), wsd=w_sh_down.astype(BF16),
        state=state_hgrn,
        cache_k=cache_k.reshape(depth, n_sample, WINDOW, B_KV_WIDTH),
        cache_v=cache_v.reshape(depth, n_sample, WINDOW, B_KV_WIDTH),
    )

    xp, xs, xs_off = x_prompt.reshape(n_prompt, d), x_sample.reshape(n_sample, d), 0
    per_layer = []
    for l in range(depth):
        h_all, outs = _mixer_layer(l, xp, xs, xs_off, n_prompt, bsz, t, n_sample, prm)
        y_all = _moe_layer(l, h_all, prm)
        per_layer.append(outs)
        xp, xs, xs_off = y_all, y_all, n_prompt // n_sample

    kv_shape = (n_sample, WINDOW, B_KV_HEADS, B_HEAD_DIM)
    stack = lambda k, f=lambda a: a: jnp.stack([f(o[k]) for o in per_layer])
    return (y_all[:n_prompt].reshape(bsz, t, d), y_all[n_prompt:].reshape(n_sample, 1, d),
            stack(0), stack(1), stack(2),
            stack(3, lambda a: a.reshape(kv_shape)), stack(4, lambda a: a.reshape(kv_shape)), stack(5))
```

```python
import functools

import numpy as np
import jax
import jax.numpy as jnp
from jax import lax
from jax.experimental import pallas as pl
from jax.experimental.pallas import tpu as pltpu

F32 = jnp.float32
BF16 = jnp.bfloat16
I32 = jnp.int32

D_MODEL = 1024
DEPTH = 2
PAST_LEN = 16384
A_HEADS = 4
A_DK = 128
A_DV = 128
A_KEY = A_HEADS * A_DK
A_WIDTH = A_HEADS * A_DV
B_Q_HEADS = 8
B_KV_HEADS = 2
B_HEAD_DIM = 64
B_GROUP = B_Q_HEADS // B_KV_HEADS
B_WIDTH = B_Q_HEADS * B_HEAD_DIM
B_KV_WIDTH = B_KV_HEADS * B_HEAD_DIM
WINDOW = 128
ROPE_THETA = 10000.0
ATTN_SCALE = B_HEAD_DIM ** -0.5
N_EXPERTS = 64
TOP_K = 8
N_GROUPS = 8
GROUP_SIZE = N_EXPERTS // N_GROUPS
TOPK_GROUPS = 4
D_EXPERT = D_MODEL // 4
D_SHARED = D_EXPERT
ROUTED_SCALE = 2.5
DN_ALPHA = (2 * DEPTH) ** 0.25
LN_EPS = 1e-5
RMS_EPS = 1e-6
NEG_BIG = -1e30
TINY = 1.1754944e-38
OFF_AF = A_KEY
OFF_AI = 2 * A_KEY
OFF_AG = OFF_AI + A_WIDTH
OFF_BQ = OFF_AG + A_WIDTH
OFF_BK = OFF_BQ + B_WIDTH
OFF_BV = OFF_BK + B_KV_WIDTH
OFF_GA = OFF_BV + B_KV_WIDTH
OFF_GB = OFF_GA + D_MODEL
IN_COLS = OFF_GB + D_MODEL

LANES = 128
HGRN_CHUNK = 128
HGRN_LEVELS = 7
ROW_TILE = 512
MOE_ROW_TILE = 384
EXPERT_ROW_TILE = 256
VMEM_LIMIT = 56 * 1024 * 1024


def _cparams(sem, vmem=VMEM_LIMIT, checked_dma=True):
    return pltpu.CompilerParams(dimension_semantics=sem, vmem_limit_bytes=vmem,
                                disable_bounds_checks=not checked_dma)


def _dot(a, b):
    return jnp.dot(a, b, preferred_element_type=F32)


def _dot_nt(a, b):
    return lax.dot_general(a, b, (((1,), (1,)), ((), ())), preferred_element_type=F32)


def _dot_tn(a, b):
    return lax.dot_general(a, b, (((0,), (0,)), ((), ())), preferred_element_type=F32)


def _sigmoid(x):
    return 1.0 / (1.0 + jnp.exp(-x))


def _silu(x):
    return x * _sigmoid(x)


def _split3(x):
    hi = x.astype(BF16)
    r1 = x - hi.astype(F32)
    mid = r1.astype(BF16)
    lo = (r1 - mid.astype(F32)).astype(BF16)
    return hi, mid, lo


def _layer_norm(y, g, b):
    mu = jnp.mean(y, axis=-1, keepdims=True)
    d = y - mu
    var = jnp.mean(d * d, axis=-1, keepdims=True)
    return d * lax.rsqrt(var + LN_EPS) * g + b


def _log_forget(af, lower):
    ls = jnp.minimum(af, 0.0) - jnp.log1p(jnp.exp(-jnp.abs(af)))
    a = jnp.log(jnp.maximum(lower, TINY))
    b = jnp.log1p(-lower) + ls
    mixed = jnp.maximum(a, b) + jnp.log1p(jnp.exp(-jnp.abs(a - b)))
    return jnp.where(lower > 0.0, mixed, ls)


def _proj_kernel(x_ref, w_ref, cos_ref, sin_ref, low_ref,
                 qa_ref, lf_ref, va_ref, ga_ref, qb_ref, kb_ref, vb_ref, sga_ref, sgb_ref):
    xb = x_ref[...].astype(BF16)

    def mm(c0, n):
        return _dot(xb, w_ref[:, c0:c0 + n])

    qa_ref[...] = _silu(mm(0, A_KEY)).astype(BF16)
    lf_ref[...] = _log_forget(mm(OFF_AF, A_KEY), low_ref[...])
    va_ref[...] = mm(OFF_AI, A_WIDTH).astype(BF16)
    ga_ref[...] = _silu(mm(OFF_AG, A_WIDTH)).astype(BF16)

    cos = cos_ref[...]
    sin = sin_ref[...]
    lane = lax.broadcasted_iota(I32, cos.shape, 1)
    first_half = (lane & (B_HEAD_DIM // 2)) == 0

    def rope(blk):
        partner = jnp.where(first_half, pltpu.roll(blk, LANES - B_HEAD_DIM // 2, 1),
                            pltpu.roll(blk, B_HEAD_DIM // 2, 1))
        return blk * cos + partner * sin

    bq = mm(OFF_BQ, B_WIDTH)
    for j in range(B_WIDTH // LANES):
        sl = slice(j * LANES, (j + 1) * LANES)
        qb_ref[:, sl] = (rope(bq[:, sl]) * ATTN_SCALE).astype(BF16)
    kb_ref[...] = rope(mm(OFF_BK, B_KV_WIDTH))
    vb_ref[...] = mm(OFF_BV, B_KV_WIDTH)
    sga_ref[...] = _sigmoid(mm(OFF_GA, D_MODEL)).astype(BF16)
    sgb_ref[...] = _sigmoid(mm(OFF_GB, D_MODEL)).astype(BF16)


def _proj(x, w_bf, cos_t, sin_t, lower, n_rows, row_off_blocks, tm):
    nt = n_rows // tm
    tab_blocks = cos_t.shape[0] // tm
    row = lambda w: pl.BlockSpec((tm, w), lambda i: (i, 0))
    outs = [(A_KEY, BF16), (A_KEY, F32), (A_WIDTH, BF16), (A_WIDTH, BF16), (B_WIDTH, BF16),
            (B_KV_WIDTH, F32), (B_KV_WIDTH, F32), (D_MODEL, BF16), (D_MODEL, BF16)]
    return pl.pallas_call(
        _proj_kernel,
        grid=(nt,),
        in_specs=[pl.BlockSpec((tm, D_MODEL), lambda i: (i + row_off_blocks, 0)),
                  pl.BlockSpec((D_MODEL, IN_COLS), lambda i: (0, 0)),
                  pl.BlockSpec((tm, LANES), lambda i: (i % tab_blocks, 0)),
                  pl.BlockSpec((tm, LANES), lambda i: (i % tab_blocks, 0)),
                  pl.BlockSpec((1, A_KEY), lambda i: (0, 0))],
        out_specs=[row(w) for w, _ in outs],
        out_shape=[jax.ShapeDtypeStruct((n_rows, w), dt) for w, dt in outs],
        compiler_params=_cparams(("arbitrary",)),
        name="proj",
    )(x, w_bf, cos_t, sin_t, lower)


def _rope_tables(pos):
    half = B_HEAD_DIM // 2
    inv = ROPE_THETA ** (-jnp.arange(half, dtype=F32) / half)
    ang = pos.astype(F32)[:, None] * inv[None, :]
    cos = jnp.cos(ang)
    sin = jnp.sin(ang)
    reps = LANES // B_HEAD_DIM
    cos_t = jnp.tile(jnp.concatenate([cos, cos], axis=1), (1, reps))
    sin_t = jnp.tile(jnp.concatenate([-sin, sin], axis=1), (1, reps))
    return cos_t, sin_t


def _hgrn_constants():
    c = HGRN_CHUNK
    r = np.arange(c)
    tri = (r[None, :] <= r[:, None]).astype(np.float32)
    sel = np.zeros((HGRN_LEVELS, c, c), np.float32)
    upper = np.zeros((HGRN_LEVELS, c, A_KEY), np.float32)
    pair = np.zeros((HGRN_LEVELS + 1, c, c), np.float32)
    for l in range(HGRN_LEVELS):
        b = c >> (l + 1)
        ref_row = (r // (2 * b)) * (2 * b) + b - 1
        sel[l, r, ref_row] = 1.0
        up = (r % (2 * b)) >= b
        upper[l] = up[:, None]
        same = (r[:, None] // (2 * b)) == (r[None, :] // (2 * b))
        pair[l] = (up[:, None] & ~up[None, :] & same)
    pair[HGRN_LEVELS] = np.eye(c)
    return (jnp.asarray(tri, BF16), jnp.asarray(sel.reshape(HGRN_LEVELS * c, c), BF16),
            jnp.asarray(upper), jnp.asarray(pair))


def _hgrn_kernel(qa_ref, lf_ref, va_ref, ga_ref, g_ref, tri_ref, sel_ref, up_ref, pair_ref,
                 oa_ref, st_ref, s_scr):
    c = HGRN_CHUNK
    step = pl.program_id(1)

    @pl.when(step == 0)
    def _():
        s_scr[...] = jnp.zeros_like(s_scr)

    lf = lf_ref[...]
    tri = tri_ref[...]
    sel = sel_ref[...]
    hi, mid, lo = _split3(lf)
    gcum = _dot(tri, hi) + _dot(tri, mid) + _dot(tri, lo)
    ghi, gmid, glo = _split3(gcum)
    gref = _dot(sel, ghi) + _dot(sel, gmid) + _dot(sel, glo)
    qb = qa_ref[...]
    qf = qb.astype(F32)
    kf = 1.0 - jnp.exp(lf)
    kb = kf.astype(BF16)
    vb = va_ref[...]
    ws = []
    for l in range(HGRN_LEVELS):
        e = jnp.exp(-jnp.abs(gcum - gref[l * c:(l + 1) * c]))
        ws.append((jnp.where(up_ref[l] > 0.5, qf, kf) * e).astype(BF16))
    qg = (qf * jnp.exp(gcum)).astype(BF16)
    gend = gcum[c - 1:c, :]
    kend = (kf * jnp.exp(gend - gcum)).astype(BF16)
    decay = jnp.exp(gend)
    gate = g_ref[...]

    for h in range(A_HEADS):
        sl = slice(h * A_DK, (h + 1) * A_DK)
        att = _dot_nt(qb[:, sl], kb[:, sl]) * pair_ref[HGRN_LEVELS]
        for l in range(HGRN_LEVELS):
            w = ws[l][:, sl]
            att = att + _dot_nt(w, w) * pair_ref[l]
        s_t = s_scr[h]
        o = _dot_nt(qg[:, sl], s_t.astype(BF16)) + _dot(att.astype(BF16), vb[:, sl])
        ms = jnp.mean(o * o, axis=-1, keepdims=True)
        on = o * lax.rsqrt(ms + RMS_EPS) * gate * ga_ref[:, sl].astype(F32)
        oa_ref[:, sl] = on.astype(BF16)
        s_scr[h] = s_t * decay[:, sl] + _dot_tn(vb[:, sl], kend[:, sl])

    @pl.when(step == pl.num_programs(1) - 1)
    def _():
        for h in range(A_HEADS):
            st_ref[0, h] = s_scr[h].T


def _hgrn_prompt(qa, lf, va, ga, norm_g, bsz, t):
    c = HGRN_CHUNK
    nc = t // c
    tri, sel, upper, pair = _hgrn_constants()
    blk = lambda: pl.BlockSpec((c, A_KEY), lambda b, i: (b * nc + i, 0))
    const = lambda a: pl.BlockSpec(a.shape, lambda b, i: (0,) * a.ndim)
    return pl.pallas_call(
        _hgrn_kernel,
        grid=(bsz, nc),
        in_specs=[blk(), blk(), blk(), blk(), pl.BlockSpec((1, A_DV), lambda b, i: (0, 0)),
                  const(tri), const(sel), const(upper), const(pair)],
        out_specs=[blk(), pl.BlockSpec((1, A_HEADS, A_DK, A_DV), lambda b, i: (b, 0, 0, 0))],
        out_shape=[jax.ShapeDtypeStruct((bsz * t, A_WIDTH), BF16),
                   jax.ShapeDtypeStruct((bsz, A_HEADS, A_DK, A_DV), F32)],
        scratch_shapes=[pltpu.VMEM((A_HEADS, A_DV, A_DK), F32)],
        compiler_params=_cparams(("arbitrary", "arbitrary")),
        name="hgrn_prompt",
    )(qa, lf, va, ga, norm_g, tri, sel, upper, pair)


def _swa_kernel(sink_ref, q_ref, k_ref, v_ref, o_ref, kprev, vprev):
    w = WINDOW
    i = pl.program_id(1)

    @pl.when(i == 0)
    def _():
        kprev[...] = jnp.zeros_like(kprev)
        vprev[...] = jnp.zeros_like(vprev)

    kc = k_ref[...]
    vc = v_ref[...]
    kk = jnp.concatenate([kprev[...], kc], axis=0)
    vv = jnp.concatenate([vprev[...], vc], axis=0)
    kr = pltpu.roll(kk, B_HEAD_DIM, 1)
    vr = pltpu.roll(vv, B_HEAD_DIM, 1)
    lo2 = lax.broadcasted_iota(I32, kk.shape, 1) < B_HEAD_DIM
    zero = jnp.zeros_like(kk)
    k_lo = [jnp.where(lo2, kk, zero).astype(BF16), jnp.where(lo2, kr, zero).astype(BF16)]
    k_hi = [jnp.where(lo2, zero, kr).astype(BF16), jnp.where(lo2, zero, kk).astype(BF16)]
    v_dup = [jnp.where(lo2, vv, vr).astype(BF16), jnp.where(lo2, vr, vv).astype(BF16)]

    qi = lax.broadcasted_iota(I32, (w, 2 * w), 0)
    kj = lax.broadcasted_iota(I32, (w, 2 * w), 1)
    valid = (kj >= qi) & (kj <= qi + w) & ((kj >= w) | (i > 0))
    lo1 = lax.broadcasted_iota(I32, (w, LANES), 1) < B_HEAD_DIM

    for j in range(B_WIDTH // LANES):
        g = (2 * j) // B_GROUP
        qblk = q_ref[:, j * LANES:(j + 1) * LANES]
        res = []
        for half, kmat in enumerate((k_lo[g], k_hi[g])):
            sk = sink_ref[2 * j + half]
            s = jnp.where(valid, _dot_nt(qblk, kmat), NEG_BIG)
            m = jnp.maximum(jnp.max(s, axis=-1, keepdims=True), sk)
            p = jnp.exp(s - m)
            denom = jnp.sum(p, axis=-1, keepdims=True) + jnp.exp(sk - m)
            res.append(_dot(p.astype(BF16), v_dup[g]) / denom)
        o_ref[:, j * LANES:(j + 1) * LANES] = jnp.where(lo1, res[0], res[1]).astype(BF16)

    kprev[...] = kc
    vprev[...] = vc


def _swa_prompt(sinks, qb, kb, vb, bsz, t):
    w = WINDOW
    nb = t // w
    return pl.pallas_call(
        _swa_kernel,
        grid_spec=pltpu.PrefetchScalarGridSpec(
            num_scalar_prefetch=1,
            grid=(bsz, nb),
            in_specs=[pl.BlockSpec((w, B_WIDTH), lambda b, i, s: (b * nb + i, 0)),
                      pl.BlockSpec((w, B_KV_WIDTH), lambda b, i, s: (b * nb + i, 0)),
                      pl.BlockSpec((w, B_KV_WIDTH), lambda b, i, s: (b * nb + i, 0))],
            out_specs=pl.BlockSpec((w, B_WIDTH), lambda b, i, s: (b * nb + i, 0)),
            scratch_shapes=[pltpu.VMEM((w, B_KV_WIDTH), F32), pltpu.VMEM((w, B_KV_WIDTH), F32)]),
        out_shape=jax.ShapeDtypeStruct((bsz * t, B_WIDTH), BF16),
        compiler_params=_cparams(("arbitrary", "arbitrary")),
        name="swa_prompt",
    )(sinks, qb, kb, vb)


SAMPLE_BLOCK = 8


def _sample_kernel(sink_ref, st_ref, lft_ref, qat_ref, va_ref, ga_ref, g_ref, q3_ref, kn_ref, vn_ref,
                   ck_ref, cv_ref, st_out, oa_ref, ob_ref, ck_out, cv_out, o_scr):
    w = WINDOW
    row = lax.broadcasted_iota(I32, (w, B_KV_WIDTH), 0)
    for i in range(SAMPLE_BLOCK):
        for h in range(A_HEADS):
            sl = slice(h * A_DV, (h + 1) * A_DV)
            fcol = jnp.exp(lft_ref[h, 0][:, i:i + 1])
            qcol = qat_ref[h, 0][:, i:i + 1]
            vrow = va_ref[i:i + 1, sl].astype(F32)
            s_new = st_ref[i, h] * fcol + (1.0 - fcol) * vrow
            st_out[i, h] = s_new
            o_scr[i:i + 1, sl] = jnp.sum(s_new * qcol, axis=0, keepdims=True)
        kc = ck_ref[i]
        vc = cv_ref[i]
        kn = kn_ref[i:i + 1, :]
        vn = vn_ref[i:i + 1, :]
        q3 = q3_ref[i]
        s = _dot_nt(q3.astype(BF16), kc.astype(BF16))
        s_new_key = jnp.sum(q3 * kn, axis=-1, keepdims=True)
        sk = sink_ref[...][:, 0:1]
        m = jnp.maximum(jnp.maximum(jnp.max(s, axis=-1, keepdims=True), s_new_key), sk)
        p = jnp.exp(s - m)
        pn = jnp.exp(s_new_key - m)
        denom = jnp.sum(p, axis=-1, keepdims=True) + pn + jnp.exp(sk - m)
        ob_ref[i] = (_dot(p.astype(BF16), vc.astype(BF16)) + pn * vn) / denom
        ck_out[i] = jnp.where(row == w - 1, kn, pltpu.roll(kc, w - 1, 0))
        cv_out[i] = jnp.where(row == w - 1, vn, pltpu.roll(vc, w - 1, 0))
    for h in range(A_HEADS):
        sl = slice(h * A_DV, (h + 1) * A_DV)
        o = o_scr[:, sl]
        ms = jnp.mean(o * o, axis=-1, keepdims=True)
        oa_ref[:, sl] = (o * lax.rsqrt(ms + RMS_EPS) * g_ref[...] * ga_ref[:, sl].astype(F32)).astype(BF16)


def _sample_step(l, sinks8, state, lft, qat, va, ga, norm_g, q3, kn, vn, ck, cv):
    nb = state.shape[1]
    sb = SAMPLE_BLOCK
    steps = nb // sb
    w = WINDOW
    b4 = lambda: pl.BlockSpec((sb, A_HEADS, A_DK, A_DV), lambda i: (i, 0, 0, 0))
    b4_in = pl.BlockSpec((None, sb, A_HEADS, A_DK, A_DV), lambda i: (l, i, 0, 0, 0))
    c3_in = lambda: pl.BlockSpec((None, sb, w, B_KV_WIDTH), lambda i: (l, i, 0, 0))
    t4 = lambda: pl.BlockSpec((A_HEADS, 1, A_DK, sb), lambda i: (0, i, 0, 0))
    r2 = lambda wd: pl.BlockSpec((sb, wd), lambda i: (i, 0))
    c3 = lambda: pl.BlockSpec((sb, w, B_KV_WIDTH), lambda i: (i, 0, 0))
    return pl.pallas_call(
        _sample_kernel,
        grid=(steps,),
        in_specs=[pl.BlockSpec((B_Q_HEADS, LANES), lambda i: (0, 0)),
                  b4_in, t4(), t4(), r2(A_WIDTH), r2(A_WIDTH), pl.BlockSpec((1, A_DV), lambda i: (0, 0)),
                  pl.BlockSpec((sb, B_Q_HEADS, LANES), lambda i: (i, 0, 0)), r2(B_KV_WIDTH), r2(B_KV_WIDTH),
                  c3_in(), c3_in()],
        out_specs=[b4(), r2(A_WIDTH), pl.BlockSpec((sb, B_Q_HEADS, LANES), lambda i: (i, 0, 0)), c3(), c3()],
        out_shape=[jax.ShapeDtypeStruct(state.shape[1:], F32),
                   jax.ShapeDtypeStruct((nb, A_WIDTH), BF16),
                   jax.ShapeDtypeStruct((nb, B_Q_HEADS, LANES), F32),
                   jax.ShapeDtypeStruct((nb, w, B_KV_WIDTH), F32),
                   jax.ShapeDtypeStruct((nb, w, B_KV_WIDTH), F32)],
        scratch_shapes=[pltpu.VMEM((sb, A_WIDTH), F32)],
        compiler_params=_cparams(("arbitrary",)),
        name="sample_step",
    )(sinks8, state, lft, qat, va, ga, norm_g, q3, kn, vn, ck, cv)


def _merge_kernel(x_ref, oa_ref, ob_ref, sga_ref, sgb_ref, wa_ref, wb_ref, wo_ref, g_ref, b_ref, h_ref):
    merged = (sga_ref[...].astype(F32) * _dot(oa_ref[...], wa_ref[...])
              + sgb_ref[...].astype(F32) * _dot(ob_ref[...], wb_ref[...]))
    mix = _dot(merged.astype(BF16), wo_ref[...])
    h_ref[...] = _layer_norm(DN_ALPHA * x_ref[...] + mix, g_ref[...], b_ref[...])


def _merge(x, oa, ob, sga, sgb, wa, wb, wo, g, b, n_rows, x_off_blocks, out_rows, out_off_blocks, tm, h_prev=None):
    nt = n_rows // tm
    row = lambda wd: pl.BlockSpec((tm, wd), lambda i: (i, 0))
    const = lambda a: pl.BlockSpec(a.shape, lambda i: (0, 0))
    args = [x, oa, ob, sga, sgb, wa, wb, wo, g, b]
    in_specs = [pl.BlockSpec((tm, D_MODEL), lambda i: (i + x_off_blocks, 0)),
                row(A_WIDTH), row(B_WIDTH), row(D_MODEL), row(D_MODEL),
                const(wa), const(wb), const(wo), const(g), const(b)]
    kern = _merge_kernel
    aliases = {}
    if h_prev is not None:
        args.append(h_prev)
        in_specs.append(pl.BlockSpec(memory_space=pl.ANY))
        aliases = {len(args) - 1: 0}
        kern = lambda *refs: _merge_kernel(*refs[:10], refs[11])
    return pl.pallas_call(
        kern,
        grid=(nt,),
        in_specs=in_specs,
        out_specs=pl.BlockSpec((tm, D_MODEL), lambda i: (i + out_off_blocks, 0)),
        out_shape=jax.ShapeDtypeStruct((out_rows, D_MODEL), F32),
        input_output_aliases=aliases,
        compiler_params=_cparams(("arbitrary",)),
        name="merge",
    )(*args)


def _mixer_layer(l, xp, xs, xs_off, n_prompt, bsz, t, n_sample, prm):
    tm = ROW_TILE
    p = _proj(xp, prm["w_in"][l], prm["cos_p"], prm["sin_p"], prm["lower"][l], n_prompt, 0, tm)
    qa, lf, va, ga, qb, kb, vb, sga, sgb = p
    oa, st_p = _hgrn_prompt(qa, lf, va, ga, prm["norm_g"][l], bsz, t)
    ob = _swa_prompt(prm["sinks"][l], qb, kb, vb, bsz, t)
    n_all = n_prompt + n_sample
    h_all = _merge(xp, oa, ob, sga, sgb, prm["wa"][l], prm["wb"][l], prm["wo"][l], prm["ln1_g"][l],
                   prm["ln1_b"][l], n_prompt, 0, n_all, 0, tm, h_prev=jnp.zeros((n_all, D_MODEL), F32))
    kp = kb.reshape(bsz, t, B_KV_HEADS, B_HEAD_DIM)[:, -WINDOW:]
    vp = vb.reshape(bsz, t, B_KV_HEADS, B_HEAD_DIM)[:, -WINDOW:]

    ts = n_sample
    ps = _proj(xs, prm["w_in"][l], prm["cos_s"], prm["sin_s"], prm["lower"][l], ts, xs_off, ts)
    qa_s, lf_s, va_s, ga_s, qb_s, kb_s, vb_s, sga_s, sgb_s = ps
    sb = SAMPLE_BLOCK
    to_t = lambda a: a.reshape(ts // sb, sb, A_HEADS, A_DK).transpose(2, 0, 3, 1)
    qh = qb_s.astype(F32).reshape(ts, B_Q_HEADS, B_HEAD_DIM)
    z = jnp.zeros_like(qh[:, :B_GROUP])
    q3 = jnp.concatenate([jnp.concatenate([qh[:, :B_GROUP], z], axis=-1),
                          jnp.concatenate([z, qh[:, B_GROUP:]], axis=-1)], axis=1)
    st_s, oa_s, ob3, ck_s, cv_s = _sample_step(
        l, prm["sinks8"][l], prm["state"], to_t(lf_s), to_t(qa_s.astype(F32)), va_s, ga_s, prm["norm_g"][l],
        q3, kb_s, vb_s, prm["cache_k"], prm["cache_v"])
    ob_s = jnp.concatenate([ob3[:, :B_GROUP, :B_HEAD_DIM], ob3[:, B_GROUP:, B_HEAD_DIM:]], axis=1)
    ob_s = ob_s.reshape(ts, B_WIDTH).astype(BF16)
    h_all = _merge(xs, oa_s, ob_s, sga_s, sgb_s, prm["wa"][l], prm["wb"][l], prm["wo"][l], prm["ln1_g"][l],
                   prm["ln1_b"][l], ts, xs_off, n_all, n_prompt // ts, ts, h_prev=h_all)
    return h_all, (kp, vp, st_p, ck_s, cv_s, st_s)


def _router_kernel(h_ref, wr_ref, bias_ref, e_ref, w_ref, mask_ref, cnt_ref):
    tm = h_ref.shape[0]
    gsz = GROUP_SIZE

    @pl.when(pl.program_id(0) == 0)
    def _():
        cnt_ref[...] = jnp.zeros_like(cnt_ref)

    logits = lax.dot_general(wr_ref[...], h_ref[...], (((1,), (1,)), ((), ())),
                             precision=lax.Precision.HIGHEST, preferred_element_type=F32)
    scores = _sigmoid(logits)
    sel = scores + bias_ref[...][:, 0:1]
    rowi = lax.broadcasted_iota(I32, (gsz, tm), 0)
    neg_inf = -jnp.inf
    blocks = [sel[g * gsz:(g + 1) * gsz] for g in range(N_GROUPS)]
    sblocks = [scores[g * gsz:(g + 1) * gsz] for g in range(N_GROUPS)]

    gscore = []
    for blk in blocks:
        m1 = jnp.max(blk, axis=0, keepdims=True)
        i1 = jnp.min(jnp.where(blk == m1, rowi, gsz), axis=0, keepdims=True)
        m2 = jnp.max(jnp.where(rowi == i1, neg_inf, blk), axis=0, keepdims=True)
        gscore.append(m1 + m2)
    work = []
    for g in range(N_GROUPS):
        ahead = jnp.zeros((1, tm), I32)
        for g2 in range(N_GROUPS):
            if g2 != g:
                beats = (gscore[g2] > gscore[g]) | ((gscore[g2] == gscore[g]) & (g2 < g))
                ahead = ahead + beats.astype(I32)
        work.append(jnp.where(ahead < TOPK_GROUPS, blocks[g], NEG_BIG))

    chosen = [jnp.zeros((gsz, tm), F32) for _ in range(N_GROUPS)]
    es, ws = [], []
    for _ in range(TOP_K):
        m = work[0]
        for g in range(1, N_GROUPS):
            m = jnp.maximum(m, work[g])
        m = jnp.max(m, axis=0, keepdims=True)
        cand = jnp.where(work[0] == m, rowi, N_EXPERTS)
        for g in range(1, N_GROUPS):
            cand = jnp.minimum(cand, jnp.where(work[g] == m, rowi + g * gsz, N_EXPERTS))
        idx = jnp.min(cand, axis=0, keepdims=True)
        wj = jnp.zeros((1, tm), F32)
        for g in range(N_GROUPS):
            hit = (rowi + g * gsz) == idx
            wj = wj + jnp.sum(jnp.where(hit, sblocks[g], 0.0), axis=0, keepdims=True)
            chosen[g] = jnp.where(hit, 1.0, chosen[g])
            work[g] = jnp.where(hit, neg_inf, work[g])
        es.append(idx)
        ws.append(wj)
    wsum = ws[0]
    for j in range(1, TOP_K):
        wsum = wsum + ws[j]
    for j in range(TOP_K):
        e_ref[j:j + 1, :] = es[j]
        w_ref[j:j + 1, :] = ws[j] / wsum * ROUTED_SCALE
    for g in range(N_GROUPS):
        rows = slice(g * gsz, (g + 1) * gsz)
        mask_ref[rows, :] = chosen[g]
        part = chosen[g][:, 0:LANES]
        for c in range(1, tm // LANES):
            part = part + chosen[g][:, c * LANES:(c + 1) * LANES]
        cnt_ref[rows, :] = cnt_ref[rows, :] + part


def _router(h_all, wr_t, bias_b, tm):
    n = h_all.shape[0]
    col = lambda r: pl.BlockSpec((r, tm), lambda i: (0, i))
    return pl.pallas_call(
        _router_kernel,
        grid=(n // tm,),
        in_specs=[pl.BlockSpec((tm, D_MODEL), lambda i: (i, 0)),
                  pl.BlockSpec((N_EXPERTS, D_MODEL), lambda i: (0, 0)),
                  pl.BlockSpec((N_EXPERTS, LANES), lambda i: (0, 0))],
        out_specs=[col(TOP_K), col(TOP_K), col(N_EXPERTS), pl.BlockSpec((N_EXPERTS, LANES), lambda i: (0, 0))],
        out_shape=[jax.ShapeDtypeStruct((TOP_K, n), I32), jax.ShapeDtypeStruct((TOP_K, n), F32),
                   jax.ShapeDtypeStruct((N_EXPERTS, n), F32), jax.ShapeDtypeStruct((N_EXPERTS, LANES), F32)],
        compiler_params=_cparams(("arbitrary",)),
        name="router",
    )(h_all, wr_t, bias_b)


def _rank_kernel(mask_ref, e_ref, offs_ref, triu_ref, dest_ref, carry):
    tm = mask_ref.shape[1]

    @pl.when(pl.program_id(0) == 0)
    def _():
        carry[...] = jnp.zeros_like(carry)

    mk = mask_ref[...]
    rank = _dot(mk.astype(BF16), triu_ref[...])
    dest_full = rank + (offs_ref[...][:, 0:1] + carry[...][:, 0:1])
    rowi = lax.broadcasted_iota(I32, (N_EXPERTS, tm), 0)
    for j in range(TOP_K):
        d = jnp.sum(jnp.where(rowi == e_ref[j:j + 1, :], dest_full, 0.0), axis=0, keepdims=True)
        dest_ref[j:j + 1, :] = d.astype(I32)
    carry[...] = carry[...] + jnp.sum(mk, axis=1, keepdims=True)


def _rank(mask_t, e_t, offs_b, tm):
    n = mask_t.shape[1]
    r = np.arange(tm)
    triu = jnp.asarray((r[:, None] < r[None, :]).astype(np.float32), BF16)
    return pl.pallas_call(
        _rank_kernel,
        grid=(n // tm,),
        in_specs=[pl.BlockSpec((N_EXPERTS, tm), lambda i: (0, i)),
                  pl.BlockSpec((TOP_K, tm), lambda i: (0, i)),
                  pl.BlockSpec((N_EXPERTS, LANES), lambda i: (0, 0)),
                  pl.BlockSpec((tm, tm), lambda i: (0, 0))],
        out_specs=pl.BlockSpec((TOP_K, tm), lambda i: (0, i)),
        out_shape=jax.ShapeDtypeStruct((TOP_K, n), I32),
        scratch_shapes=[pltpu.VMEM((N_EXPERTS, LANES), F32)],
        compiler_params=_cparams(("arbitrary",)),
        name="rank",
    )(mask_t, e_t, offs_b, triu)


INV_COLS = 512
TOKEN_RADIX = 128


def _inverse_kernel(dest_ref, inv_ref):
    tm = dest_ref.shape[1]
    nq = inv_ref.shape[0]

    @pl.when(pl.program_id(0) == 0)
    def _():
        inv_ref[...] = jnp.zeros_like(inv_ref)

    tok = pl.program_id(0) * tm + lax.broadcasted_iota(I32, (1, tm), 1)
    t_hi = jnp.right_shift(tok, TOKEN_RADIX.bit_length() - 1).astype(F32)
    t_lo = jnp.bitwise_and(tok, TOKEN_RADIX - 1).astype(F32)
    qi = lax.broadcasted_iota(I32, (nq, tm), 0)
    si = lax.broadcasted_iota(I32, (INV_COLS, tm), 0)
    acc_hi = jnp.zeros(inv_ref.shape, F32)
    acc_lo = jnp.zeros(inv_ref.shape, F32)
    for j in range(TOP_K):
        d = dest_ref[j:j + 1, :]
        at_q = qi == jnp.right_shift(d, INV_COLS.bit_length() - 1)
        col = jnp.where(si == jnp.bitwise_and(d, INV_COLS - 1), 1.0, 0.0).astype(BF16)
        acc_hi = acc_hi + _dot_nt(jnp.where(at_q, t_hi, 0.0).astype(BF16), col)
        acc_lo = acc_lo + _dot_nt(jnp.where(at_q, t_lo, 0.0).astype(BF16), col)
    inv_ref[...] = inv_ref[...] + (acc_hi * float(TOKEN_RADIX) + acc_lo)


def _inverse_map(dest_t, tm):
    n = dest_t.shape[1]
    n_rows = n * TOP_K
    nq = -(-(n_rows // INV_COLS) // 8) * 8
    inv = pl.pallas_call(
        _inverse_kernel,
        grid=(n // tm,),
        in_specs=[pl.BlockSpec((TOP_K, tm), lambda i: (0, i))],
        out_specs=pl.BlockSpec((nq, INV_COLS), lambda i: (0, 0)),
        out_shape=jax.ShapeDtypeStruct((nq, INV_COLS), F32),
        compiler_params=_cparams(("arbitrary",)),
        name="inverse_map",
    )(dest_t)
    return inv.reshape(-1)[:n_rows].astype(I32).reshape(n_rows // EXPERT_ROW_TILE, 1, EXPERT_ROW_TILE)


SUBLANES = 8
DMA_THREADS = 2


def _tile_row_copy(src_ref, src_row, dst_ref, dst_group, dst_sub, sem):
    return pltpu.make_async_copy(
        src_ref.at[jnp.right_shift(src_row, 3), pl.ds(jnp.bitwise_and(src_row, SUBLANES - 1), 1)],
        dst_ref.at[dst_group, pl.ds(dst_sub, 1)], sem)


def _expert_kernel(n_tiles, tile_ref, exp_ref, valid_ref, first_ref, newexp_ref, lo_ref, hi_ref,
                   inv_ref, invn_ref, h_ref, wg_ref, wu_ref, wd_ref, y_ref, xbuf, wgb, wub, wdb, sem):
    i = pl.program_id(0)
    rows = xbuf.shape[1] * SUBLANES
    tile = tile_ref[i]
    slot = tile % 2
    valid = valid_ref[i] == 1
    first = first_ref[i] == 1

    def gather(idx_ref, s):
        def issue(k, c):
            for u in range(SUBLANES):
                _tile_row_copy(h_ref, idx_ref[0, 0, k * SUBLANES + u], xbuf.at[s], k, u, sem.at[s]).start(
                    priority=u % DMA_THREADS)
            return c

        lax.fori_loop(0, rows // SUBLANES, issue, 0)

    def wait_rows(s):
        pltpu.make_async_copy(h_ref.at[pl.ds(0, rows // SUBLANES)], xbuf.at[s], sem.at[s]).wait()

    @pl.when(valid & first)
    def _():
        @pl.when(i == 0)
        def _():
            gather(inv_ref, 0)

        wait_rows(slot)

    @pl.when(valid & (newexp_ref[i] == 1))
    def _():
        wgb[...] = wg_ref[0].astype(BF16)
        wub[...] = wu_ref[0].astype(BF16)
        wdb[...] = wd_ref[0].astype(BF16)

    def compute(first_visit):
        xb = xbuf[slot].reshape(rows, D_MODEL).astype(BF16)
        if first_visit:
            for r in range(rows):
                _tile_row_copy(h_ref, invn_ref[0, 0, r], xbuf.at[1 - slot], r // SUBLANES, r % SUBLANES,
                               sem.at[1 - slot]).start(priority=r % DMA_THREADS)
        gate = _dot(xb, wgb[...])
        up = _dot(xb, wub[...])
        y = _dot((_silu(gate) * up).astype(BF16), wdb[...])
        rowi = lax.broadcasted_iota(I32, y.shape, 0)
        mine = (rowi >= lo_ref[i]) & (rowi < hi_ref[i])
        y_ref[...] = jnp.where(mine, y, 0.0 if first_visit else y_ref[...])

    @pl.when(valid & first)
    def _():
        compute(True)

    @pl.when(valid & jnp.logical_not(first))
    def _():
        compute(False)

    @pl.when(valid & first & (tile == n_tiles - 1))
    def _():
        wait_rows(1 - slot)


def _group_metadata(counts, n_rows):
    tmo = EXPERT_ROW_TILE
    n_tiles = n_rows // tmo
    ends = jnp.cumsum(counts)
    offs = ends - counts
    first_tile = offs // tmo
    n_t = jnp.where(counts > 0, (ends - 1) // tmo - first_tile + 1, 0)
    cum = jnp.cumsum(n_t)
    base = cum - n_t
    n_items = n_tiles + N_EXPERTS
    idx = jnp.arange(n_items, dtype=I32)
    valid = (idx < cum[-1]).astype(I32)
    idc = jnp.minimum(idx, cum[-1] - 1)
    e = jnp.minimum(jnp.sum((cum[None, :] <= idc[:, None]).astype(I32), axis=1), N_EXPERTS - 1)
    tile = (first_tile[e] + idc - base[e]).astype(I32)
    one = jnp.ones((1,), I32)
    first = jnp.concatenate([one, (tile[1:] != tile[:-1]).astype(I32)])
    new_expert = jnp.concatenate([one, (e[1:] != e[:-1]).astype(I32)])
    lo = jnp.clip(offs[e] - tile * tmo, 0, tmo).astype(I32)
    hi = jnp.clip(ends[e] - tile * tmo, 0, tmo).astype(I32)
    return tile, e, valid, first, new_expert, lo, hi


def _experts(l, meta, inv3, h_all, wg, wu, wd):
    tmo = EXPERT_ROW_TILE
    n_rows = inv3.shape[0] * tmo
    n_items = meta[0].shape[0]
    wspec = lambda shp: pl.BlockSpec((None, 1) + shp, lambda i, tl, ex, *_: (l, ex[i], 0, 0))
    n_tiles = inv3.shape[0]
    idx_spec = lambda nxt: pl.BlockSpec(
        (1, 1, tmo), lambda i, tl, *_: (jnp.minimum(tl[i] + nxt, n_tiles - 1), 0, 0), memory_space=pltpu.SMEM)
    return pl.pallas_call(
        functools.partial(_expert_kernel, n_tiles),
        grid_spec=pltpu.PrefetchScalarGridSpec(
            num_scalar_prefetch=7,
            grid=(n_items,),
            in_specs=[idx_spec(0), idx_spec(1), pl.BlockSpec(memory_space=pl.ANY),
                      wspec((D_MODEL, D_EXPERT)), wspec((D_MODEL, D_EXPERT)), wspec((D_EXPERT, D_MODEL))],
            out_specs=pl.BlockSpec((tmo, D_MODEL), lambda i, tl, *_: (tl[i], 0)),
            scratch_shapes=[pltpu.VMEM((2, tmo // SUBLANES, SUBLANES, D_MODEL), F32),
                            pltpu.VMEM((D_MODEL, D_EXPERT), BF16), pltpu.VMEM((D_MODEL, D_EXPERT), BF16),
                            pltpu.VMEM((D_EXPERT, D_MODEL), BF16), pltpu.SemaphoreType.DMA((2,))]),
        out_shape=jax.ShapeDtypeStruct((n_rows, D_MODEL), F32),
        compiler_params=_cparams(("arbitrary",), checked_dma=False),
        name="experts",
    )(*meta, inv3, inv3, h_all.reshape(-1, SUBLANES, D_MODEL), wg, wu, wd)


def _post_kernel(dest_ref, h_ref, wt_ref, y_ref, wsg_ref, wsu_ref, wsd_ref, g_ref, b_ref, out_ref, buf, sem):
    tm = h_ref.shape[0]

    def issue(k, c):
        for u in range(SUBLANES):
            for j in range(TOP_K):
                _tile_row_copy(y_ref, dest_ref[j, k * SUBLANES + u], buf.at[j], k, u, sem).start(
                    priority=j % DMA_THREADS)
        return c

    lax.fori_loop(0, tm // SUBLANES, issue, 0)

    h = h_ref[...]
    hb = h.astype(BF16)
    shared = _dot((_silu(_dot(hb, wsg_ref[...])) * _dot(hb, wsu_ref[...])).astype(BF16), wsd_ref[...])

    for j in range(TOP_K):
        pltpu.make_async_copy(y_ref.at[pl.ds(0, tm // SUBLANES)], buf.at[j], sem).wait()

    wt = wt_ref[...]
    routed = buf[0].reshape(tm, D_MODEL) * wt[:, 0:1]
    for j in range(1, TOP_K):
        routed = routed + buf[j].reshape(tm, D_MODEL) * wt[:, j:j + 1]
    out_ref[...] = _layer_norm(DN_ALPHA * h + (routed + shared), g_ref[...], b_ref[...])


def _post(dest_t, h_all, w_tok, y, wsg, wsu, wsd, g, b, tm):
    n = h_all.shape[0]
    const = lambda a: pl.BlockSpec(a.shape, lambda i: (0, 0))
    return pl.pallas_call(
        _post_kernel,
        grid=(n // tm,),
        in_specs=[pl.BlockSpec((TOP_K, tm), lambda i: (0, i), memory_space=pltpu.SMEM),
                  pl.BlockSpec((tm, D_MODEL), lambda i: (i, 0)),
                  pl.BlockSpec((tm, TOP_K), lambda i: (i, 0)),
                  pl.BlockSpec(memory_space=pl.ANY),
                  const(wsg), const(wsu), const(wsd), const(g), const(b)],
        out_specs=pl.BlockSpec((tm, D_MODEL), lambda i: (i, 0)),
        out_shape=jax.ShapeDtypeStruct((n, D_MODEL), F32),
        scratch_shapes=[pltpu.VMEM((TOP_K, tm // SUBLANES, SUBLANES, D_MODEL), F32),
                        pltpu.SemaphoreType.DMA(())],
        compiler_params=_cparams(("arbitrary",), checked_dma=False),
        name="moe_post",
    )(dest_t, h_all, w_tok, y.reshape(-1, SUBLANES, D_MODEL), wsg, wsu, wsd, g, b)


def _moe_layer(l, h_all, prm):
    tm = MOE_ROW_TILE
    e_t, w_t, mask_t, cnt = _router(h_all, prm["wr_t"][l], prm["rbias"][l], tm)
    counts = jnp.sum(cnt, axis=1).astype(I32)
    offs = jnp.cumsum(counts) - counts
    offs_b = jnp.broadcast_to(offs.astype(F32)[:, None], (N_EXPERTS, LANES))
    dest_t = _rank(mask_t, e_t, offs_b, tm)
    inv3 = _inverse_map(dest_t, tm)
    meta = _group_metadata(counts, h_all.shape[0] * TOP_K)
    y = _experts(l, meta, inv3, h_all, prm["w_exp_gate"], prm["w_exp_up"], prm["w_exp_down"])
    return _post(dest_t, h_all, w_t.T, y, prm["wsg"][l], prm["wsu"][l], prm["wsd"][l],
                 prm["ln2_g"][l], prm["ln2_b"][l], tm)


def kernel(x_prompt, x_sample, cache_k, cache_v, state_hgrn, w_in, hgrn_lower_bounds, hgrn_norm_g, attn_sinks, w_branch_a, w_branch_b, w_out, ln1_g, ln1_b, w_router, router_bias, w_exp_gate, w_exp_up, w_exp_down, w_sh_gate, w_sh_up, w_sh_down, ln2_g, ln2_b):
    bsz, t, d = x_prompt.shape
    n_sample = x_sample.shape[0] * x_sample.shape[1]
    n_prompt = bsz * t
    depth = w_in.shape[0]
    assert d == D_MODEL and x_sample.shape[1] == 1 and n_prompt % ROW_TILE == 0 and t % ROW_TILE == 0
    assert n_prompt % n_sample == 0 and (n_prompt + n_sample) % MOE_ROW_TILE == 0
    assert ((n_prompt + n_sample) * TOP_K) % EXPERT_ROW_TILE == 0 and n_sample % SAMPLE_BLOCK == 0

    lb_prob = jax.nn.softmax(hgrn_lower_bounds.astype(F32), axis=0)
    lower = (jnp.cumsum(lb_prob, axis=0) - lb_prob[0])[:, None, :]
    cos_p, sin_p = _rope_tables(jnp.arange(t))
    cos_s, sin_s = _rope_tables(jnp.full((n_sample,), PAST_LEN))
    row = lambda a: a[:, None, :]
    prm = dict(
        w_in=w_in.astype(BF16), lower=lower, cos_p=cos_p, sin_p=sin_p, cos_s=cos_s, sin_s=sin_s,
        norm_g=row(hgrn_norm_g), sinks=attn_sinks,
        sinks8=jnp.broadcast_to(attn_sinks[:, :, None], (depth, B_Q_HEADS, LANES)),
        wa=w_branch_a.astype(BF16), wb=w_branch_b.astype(BF16), wo=w_out.astype(BF16),
        ln1_g=row(ln1_g), ln1_b=row(ln1_b), ln2_g=row(ln2_g), ln2_b=row(ln2_b),
        wr_t=jnp.swapaxes(w_router, 1, 2),
        rbias=jnp.broadcast_to(router_bias[:, :, None], (depth, N_EXPERTS, LANES)),
        w_exp_gate=w_exp_gate, w_exp_up=w_exp_up, w_exp_down=w_exp_down,
        wsg=w_sh_gate.astype(BF16), wsu=w_sh_up.astype(BF16), wsd=w_sh_down.astype(BF16),
        state=state_hgrn,
        cache_k=cache_k.reshape(depth, n_sample, WINDOW, B_KV_WIDTH),
        cache_v=cache_v.reshape(depth, n_sample, WINDOW, B_KV_WIDTH),
    )

    xp, xs, xs_off = x_prompt.reshape(n_prompt, d), x_sample.reshape(n_sample, d), 0
    per_layer = []
    for l in range(depth):
        h_all, outs = _mixer_layer(l, xp, xs, xs_off, n_prompt, bsz, t, n_sample, prm)
        y_all = _moe_layer(l, h_all, prm)
        per_layer.append(outs)
        xp, xs, xs_off = y_all, y_all, n_prompt // n_sample

    kv_shape = (n_sample, WINDOW, B_KV_HEADS, B_HEAD_DIM)
    stack = lambda k, f=lambda a: a: jnp.stack([f(o[k]) for o in per_layer])
    return (y_all[:n_prompt].reshape(bsz, t, d), y_all[n_prompt:].reshape(n_sample, 1, d),
            stack(0), stack(1), stack(2),
            stack(3, lambda a: a.reshape(kv_shape)), stack(4, lambda a: a.reshape(kv_shape)), stack(5))
```

```python
import functools

import numpy as np
import jax
import jax.numpy as jnp
from jax import lax
from jax.experimental import pallas as pl
from jax.experimental.pallas import tpu as pltpu

F32 = jnp.float32
BF16 = jnp.bfloat16
I32 = jnp.int32

D_MODEL = 1024
DEPTH = 2
PAST_LEN = 16384
A_HEADS = 4
A_DK = 128
A_DV = 128
A_KEY = A_HEADS * A_DK
A_WIDTH = A_HEADS * A_DV
B_Q_HEADS = 8
B_KV_HEADS = 2
B_HEAD_DIM = 64
B_GROUP = B_Q_HEADS // B_KV_HEADS
B_WIDTH = B_Q_HEADS * B_HEAD_DIM
B_KV_WIDTH = B_KV_HEADS * B_HEAD_DIM
WINDOW = 128
ROPE_THETA = 10000.0
ATTN_SCALE = B_HEAD_DIM ** -0.5
N_EXPERTS = 64
TOP_K = 8
N_GROUPS = 8
GROUP_SIZE = N_EXPERTS // N_GROUPS
TOPK_GROUPS = 4
D_EXPERT = D_MODEL // 4
D_SHARED = D_EXPERT
ROUTED_SCALE = 2.5
DN_ALPHA = (2 * DEPTH) ** 0.25
LN_EPS = 1e-5
RMS_EPS = 1e-6
NEG_BIG = -1e30
TINY = 1.1754944e-38
OFF_AF = A_KEY
OFF_AI = 2 * A_KEY
OFF_AG = OFF_AI + A_WIDTH
OFF_BQ = OFF_AG + A_WIDTH
OFF_BK = OFF_BQ + B_WIDTH
OFF_BV = OFF_BK + B_KV_WIDTH
OFF_GA = OFF_BV + B_KV_WIDTH
OFF_GB = OFF_GA + D_MODEL
IN_COLS = OFF_GB + D_MODEL

LANES = 128
HGRN_CHUNK = 128
HGRN_LEVELS = 7
ROW_TILE = 512
MOE_ROW_TILE = 384
EXPERT_ROW_TILE = 256
VMEM_LIMIT = 56 * 1024 * 1024


def _cparams(sem, vmem=VMEM_LIMIT):
    return pltpu.CompilerParams(dimension_semantics=sem, vmem_limit_bytes=vmem)


def _dot(a, b):
    return jnp.dot(a, b, preferred_element_type=F32)


def _dot_nt(a, b):
    return lax.dot_general(a, b, (((1,), (1,)), ((), ())), preferred_element_type=F32)


def _dot_tn(a, b):
    return lax.dot_general(a, b, (((0,), (0,)), ((), ())), preferred_element_type=F32)


def _sigmoid(x):
    return 1.0 / (1.0 + jnp.exp(-x))


def _silu(x):
    return x * _sigmoid(x)


def _split3(x):
    hi = x.astype(BF16)
    r1 = x - hi.astype(F32)
    mid = r1.astype(BF16)
    lo = (r1 - mid.astype(F32)).astype(BF16)
    return hi, mid, lo


def _layer_norm(y, g, b):
    mu = jnp.mean(y, axis=-1, keepdims=True)
    d = y - mu
    var = jnp.mean(d * d, axis=-1, keepdims=True)
    return d * lax.rsqrt(var + LN_EPS) * g + b


SUBLANES = 8
assert D_MODEL == SUBLANES * LANES


def _store_tiled(ref, val):
    rows = val.shape[0]
    for c in range(SUBLANES):
        ref[pl.ds(c, rows, stride=SUBLANES), :] = val[:, c * LANES:(c + 1) * LANES]


def _load_tiled(ref, rows):
    return jnp.concatenate([ref[pl.ds(c, rows, stride=SUBLANES), :] for c in range(SUBLANES)], axis=1)


def _log_forget(af, lower):
    ls = jnp.minimum(af, 0.0) - jnp.log1p(jnp.exp(-jnp.abs(af)))
    a = jnp.log(jnp.maximum(lower, TINY))
    b = jnp.log1p(-lower) + ls
    mixed = jnp.maximum(a, b) + jnp.log1p(jnp.exp(-jnp.abs(a - b)))
    return jnp.where(lower > 0.0, mixed, ls)


def _proj_kernel(x_ref, w_ref, cos_ref, sin_ref, low_ref,
                 qa_ref, lf_ref, va_ref, ga_ref, qb_ref, kb_ref, vb_ref, sga_ref, sgb_ref):
    xb = x_ref[...].astype(BF16)

    def mm(c0, n):
        return _dot(xb, w_ref[:, c0:c0 + n])

    qa_ref[...] = _silu(mm(0, A_KEY)).astype(BF16)
    lf_ref[...] = _log_forget(mm(OFF_AF, A_KEY), low_ref[...])
    va_ref[...] = mm(OFF_AI, A_WIDTH).astype(BF16)
    ga_ref[...] = _silu(mm(OFF_AG, A_WIDTH)).astype(BF16)

    cos = cos_ref[...]
    sin = sin_ref[...]
    lane = lax.broadcasted_iota(I32, cos.shape, 1)
    first_half = (lane & (B_HEAD_DIM // 2)) == 0

    def rope(blk):
        partner = jnp.where(first_half, pltpu.roll(blk, LANES - B_HEAD_DIM // 2, 1),
                            pltpu.roll(blk, B_HEAD_DIM // 2, 1))
        return blk * cos + partner * sin

    bq = mm(OFF_BQ, B_WIDTH)
    for j in range(B_WIDTH // LANES):
        sl = slice(j * LANES, (j + 1) * LANES)
        qb_ref[:, sl] = (rope(bq[:, sl]) * ATTN_SCALE).astype(BF16)
    kb_ref[...] = rope(mm(OFF_BK, B_KV_WIDTH))
    vb_ref[...] = mm(OFF_BV, B_KV_WIDTH)
    sga_ref[...] = _sigmoid(mm(OFF_GA, D_MODEL)).astype(BF16)
    sgb_ref[...] = _sigmoid(mm(OFF_GB, D_MODEL)).astype(BF16)


def _proj(x, w_bf, cos_t, sin_t, lower, n_rows, row_off_blocks, tm):
    nt = n_rows // tm
    tab_blocks = cos_t.shape[0] // tm
    row = lambda w: pl.BlockSpec((tm, w), lambda i: (i, 0))
    outs = [(A_KEY, BF16), (A_KEY, F32), (A_WIDTH, BF16), (A_WIDTH, BF16), (B_WIDTH, BF16),
            (B_KV_WIDTH, F32), (B_KV_WIDTH, F32), (D_MODEL, BF16), (D_MODEL, BF16)]
    return pl.pallas_call(
        _proj_kernel,
        grid=(nt,),
        in_specs=[pl.BlockSpec((tm, D_MODEL), lambda i: (i + row_off_blocks, 0)),
                  pl.BlockSpec((D_MODEL, IN_COLS), lambda i: (0, 0)),
                  pl.BlockSpec((tm, LANES), lambda i: (i % tab_blocks, 0)),
                  pl.BlockSpec((tm, LANES), lambda i: (i % tab_blocks, 0)),
                  pl.BlockSpec((1, A_KEY), lambda i: (0, 0))],
        out_specs=[row(w) for w, _ in outs],
        out_shape=[jax.ShapeDtypeStruct((n_rows, w), dt) for w, dt in outs],
        compiler_params=_cparams(("arbitrary",)),
        name="proj",
    )(x, w_bf, cos_t, sin_t, lower)


def _rope_tables(pos):
    half = B_HEAD_DIM // 2
    inv = ROPE_THETA ** (-jnp.arange(half, dtype=F32) / half)
    ang = pos.astype(F32)[:, None] * inv[None, :]
    cos = jnp.cos(ang)
    sin = jnp.sin(ang)
    reps = LANES // B_HEAD_DIM
    cos_t = jnp.tile(jnp.concatenate([cos, cos], axis=1), (1, reps))
    sin_t = jnp.tile(jnp.concatenate([-sin, sin], axis=1), (1, reps))
    return cos_t, sin_t


def _hgrn_constants():
    c = HGRN_CHUNK
    r = np.arange(c)
    tri = (r[None, :] <= r[:, None]).astype(np.float32)
    sel = np.zeros((HGRN_LEVELS, c, c), np.float32)
    upper = np.zeros((HGRN_LEVELS, c, A_KEY), np.float32)
    pair = np.zeros((HGRN_LEVELS + 1, c, c), np.float32)
    for l in range(HGRN_LEVELS):
        b = c >> (l + 1)
        ref_row = (r // (2 * b)) * (2 * b) + b - 1
        sel[l, r, ref_row] = 1.0
        up = (r % (2 * b)) >= b
        upper[l] = up[:, None]
        same = (r[:, None] // (2 * b)) == (r[None, :] // (2 * b))
        pair[l] = (up[:, None] & ~up[None, :] & same)
    pair[HGRN_LEVELS] = np.eye(c)
    return (jnp.asarray(tri, BF16), jnp.asarray(sel.reshape(HGRN_LEVELS * c, c), BF16),
            jnp.asarray(upper), jnp.asarray(pair))


def _hgrn_kernel(qa_ref, lf_ref, va_ref, ga_ref, g_ref, tri_ref, sel_ref, up_ref, pair_ref,
                 oa_ref, st_ref, s_scr):
    c = HGRN_CHUNK
    step = pl.program_id(1)

    @pl.when(step == 0)
    def _():
        s_scr[...] = jnp.zeros_like(s_scr)

    lf = lf_ref[...]
    tri = tri_ref[...]
    sel = sel_ref[...]
    hi, mid, lo = _split3(lf)
    gcum = _dot(tri, hi) + _dot(tri, mid) + _dot(tri, lo)
    ghi, gmid, glo = _split3(gcum)
    gref = _dot(sel, ghi) + _dot(sel, gmid) + _dot(sel, glo)
    qb = qa_ref[...]
    qf = qb.astype(F32)
    kf = 1.0 - jnp.exp(lf)
    kb = kf.astype(BF16)
    vb = va_ref[...]
    ws = []
    for l in range(HGRN_LEVELS):
        e = jnp.exp(-jnp.abs(gcum - gref[l * c:(l + 1) * c]))
        ws.append((jnp.where(up_ref[l] > 0.5, qf, kf) * e).astype(BF16))
    qg = (qf * jnp.exp(gcum)).astype(BF16)
    gend = gcum[c - 1:c, :]
    kend = (kf * jnp.exp(gend - gcum)).astype(BF16)
    decay = jnp.exp(gend)
    gate = g_ref[...]

    for h in range(A_HEADS):
        sl = slice(h * A_DK, (h + 1) * A_DK)
        att = _dot_nt(qb[:, sl], kb[:, sl]) * pair_ref[HGRN_LEVELS]
        for l in range(HGRN_LEVELS):
            w = ws[l][:, sl]
            att = att + _dot_nt(w, w) * pair_ref[l]
        s_t = s_scr[h]
        o = _dot_nt(qg[:, sl], s_t.astype(BF16)) + _dot(att.astype(BF16), vb[:, sl])
        ms = jnp.mean(o * o, axis=-1, keepdims=True)
        on = o * lax.rsqrt(ms + RMS_EPS) * gate * ga_ref[:, sl].astype(F32)
        oa_ref[:, sl] = on.astype(BF16)
        s_scr[h] = s_t * decay[:, sl] + _dot_tn(vb[:, sl], kend[:, sl])

    @pl.when(step == pl.num_programs(1) - 1)
    def _():
        for h in range(A_HEADS):
            st_ref[0, h] = s_scr[h].T


def _hgrn_prompt(qa, lf, va, ga, norm_g, bsz, t):
    c = HGRN_CHUNK
    nc = t // c
    tri, sel, upper, pair = _hgrn_constants()
    blk = lambda: pl.BlockSpec((c, A_KEY), lambda b, i: (b * nc + i, 0))
    const = lambda a: pl.BlockSpec(a.shape, lambda b, i: (0,) * a.ndim)
    return pl.pallas_call(
        _hgrn_kernel,
        grid=(bsz, nc),
        in_specs=[blk(), blk(), blk(), blk(), pl.BlockSpec((1, A_DV), lambda b, i: (0, 0)),
                  const(tri), const(sel), const(upper), const(pair)],
        out_specs=[blk(), pl.BlockSpec((1, A_HEADS, A_DK, A_DV), lambda b, i: (b, 0, 0, 0))],
        out_shape=[jax.ShapeDtypeStruct((bsz * t, A_WIDTH), BF16),
                   jax.ShapeDtypeStruct((bsz, A_HEADS, A_DK, A_DV), F32)],
        scratch_shapes=[pltpu.VMEM((A_HEADS, A_DV, A_DK), F32)],
        compiler_params=_cparams(("arbitrary", "arbitrary")),
        name="hgrn_prompt",
    )(qa, lf, va, ga, norm_g, tri, sel, upper, pair)


def _swa_kernel(sink_ref, q_ref, k_ref, v_ref, o_ref, kprev, vprev):
    w = WINDOW
    i = pl.program_id(1)

    @pl.when(i == 0)
    def _():
        kprev[...] = jnp.zeros_like(kprev)
        vprev[...] = jnp.zeros_like(vprev)

    kc = k_ref[...]
    vc = v_ref[...]
    kk = jnp.concatenate([kprev[...], kc], axis=0)
    vv = jnp.concatenate([vprev[...], vc], axis=0)
    kr = pltpu.roll(kk, B_HEAD_DIM, 1)
    vr = pltpu.roll(vv, B_HEAD_DIM, 1)
    lo2 = lax.broadcasted_iota(I32, kk.shape, 1) < B_HEAD_DIM
    zero = jnp.zeros_like(kk)
    k_lo = [jnp.where(lo2, kk, zero).astype(BF16), jnp.where(lo2, kr, zero).astype(BF16)]
    k_hi = [jnp.where(lo2, zero, kr).astype(BF16), jnp.where(lo2, zero, kk).astype(BF16)]
    v_dup = [jnp.where(lo2, vv, vr).astype(BF16), jnp.where(lo2, vr, vv).astype(BF16)]

    qi = lax.broadcasted_iota(I32, (w, 2 * w), 0)
    kj = lax.broadcasted_iota(I32, (w, 2 * w), 1)
    valid = (kj >= qi) & (kj <= qi + w) & ((kj >= w) | (i > 0))
    lo1 = lax.broadcasted_iota(I32, (w, LANES), 1) < B_HEAD_DIM

    for j in range(B_WIDTH // LANES):
        g = (2 * j) // B_GROUP
        qblk = q_ref[:, j * LANES:(j + 1) * LANES]
        res = []
        for half, kmat in enumerate((k_lo[g], k_hi[g])):
            sk = sink_ref[2 * j + half]
            s = jnp.where(valid, _dot_nt(qblk, kmat), NEG_BIG)
            m = jnp.maximum(jnp.max(s, axis=-1, keepdims=True), sk)
            p = jnp.exp(s - m)
            denom = jnp.sum(p, axis=-1, keepdims=True) + jnp.exp(sk - m)
            res.append(_dot(p.astype(BF16), v_dup[g]) / denom)
        o_ref[:, j * LANES:(j + 1) * LANES] = jnp.where(lo1, res[0], res[1]).astype(BF16)

    kprev[...] = kc
    vprev[...] = vc


def _swa_prompt(sinks, qb, kb, vb, bsz, t):
    w = WINDOW
    nb = t // w
    return pl.pallas_call(
        _swa_kernel,
        grid_spec=pltpu.PrefetchScalarGridSpec(
            num_scalar_prefetch=1,
            grid=(bsz, nb),
            in_specs=[pl.BlockSpec((w, B_WIDTH), lambda b, i, s: (b * nb + i, 0)),
                      pl.BlockSpec((w, B_KV_WIDTH), lambda b, i, s: (b * nb + i, 0)),
                      pl.BlockSpec((w, B_KV_WIDTH), lambda b, i, s: (b * nb + i, 0))],
            out_specs=pl.BlockSpec((w, B_WIDTH), lambda b, i, s: (b * nb + i, 0)),
            scratch_shapes=[pltpu.VMEM((w, B_KV_WIDTH), F32), pltpu.VMEM((w, B_KV_WIDTH), F32)]),
        out_shape=jax.ShapeDtypeStruct((bsz * t, B_WIDTH), BF16),
        compiler_params=_cparams(("arbitrary", "arbitrary")),
        name="swa_prompt",
    )(sinks, qb, kb, vb)


SAMPLE_BLOCK = 8


def _sample_kernel(sink_ref, st_ref, lft_ref, qat_ref, va_ref, ga_ref, g_ref, q3_ref, kn_ref, vn_ref,
                   ck_ref, cv_ref, st_out, oa_ref, ob_ref, ck_out, cv_out, o_scr):
    w = WINDOW
    row = lax.broadcasted_iota(I32, (w, B_KV_WIDTH), 0)
    for i in range(SAMPLE_BLOCK):
        for h in range(A_HEADS):
            sl = slice(h * A_DV, (h + 1) * A_DV)
            fcol = jnp.exp(lft_ref[h, 0][:, i:i + 1])
            qcol = qat_ref[h, 0][:, i:i + 1]
            vrow = va_ref[i:i + 1, sl].astype(F32)
            s_new = st_ref[i, h] * fcol + (1.0 - fcol) * vrow
            st_out[i, h] = s_new
            o_scr[i:i + 1, sl] = jnp.sum(s_new * qcol, axis=0, keepdims=True)
        kc = ck_ref[i]
        vc = cv_ref[i]
        kn = kn_ref[i:i + 1, :]
        vn = vn_ref[i:i + 1, :]
        q3 = q3_ref[i]
        s = _dot_nt(q3.astype(BF16), kc.astype(BF16))
        s_new_key = jnp.sum(q3 * kn, axis=-1, keepdims=True)
        sk = sink_ref[...][:, 0:1]
        m = jnp.maximum(jnp.maximum(jnp.max(s, axis=-1, keepdims=True), s_new_key), sk)
        p = jnp.exp(s - m)
        pn = jnp.exp(s_new_key - m)
        denom = jnp.sum(p, axis=-1, keepdims=True) + pn + jnp.exp(sk - m)
        ob_ref[i] = (_dot(p.astype(BF16), vc.astype(BF16)) + pn * vn) / denom
        ck_out[i] = jnp.where(row == w - 1, kn, pltpu.roll(kc, w - 1, 0))
        cv_out[i] = jnp.where(row == w - 1, vn, pltpu.roll(vc, w - 1, 0))
    for h in range(A_HEADS):
        sl = slice(h * A_DV, (h + 1) * A_DV)
        o = o_scr[:, sl]
        ms = jnp.mean(o * o, axis=-1, keepdims=True)
        oa_ref[:, sl] = (o * lax.rsqrt(ms + RMS_EPS) * g_ref[...] * ga_ref[:, sl].astype(F32)).astype(BF16)


def _sample_step(l, sinks8, state, lft, qat, va, ga, norm_g, q3, kn, vn, ck, cv):
    nb = state.shape[1]
    sb = SAMPLE_BLOCK
    steps = nb // sb
    w = WINDOW
    b4 = lambda: pl.BlockSpec((sb, A_HEADS, A_DK, A_DV), lambda i: (i, 0, 0, 0))
    b4_in = pl.BlockSpec((None, sb, A_HEADS, A_DK, A_DV), lambda i: (l, i, 0, 0, 0))
    c3_in = lambda: pl.BlockSpec((None, sb, w, B_KV_WIDTH), lambda i: (l, i, 0, 0))
    t4 = lambda: pl.BlockSpec((A_HEADS, 1, A_DK, sb), lambda i: (0, i, 0, 0))
    r2 = lambda wd: pl.BlockSpec((sb, wd), lambda i: (i, 0))
    c3 = lambda: pl.BlockSpec((sb, w, B_KV_WIDTH), lambda i: (i, 0, 0))
    return pl.pallas_call(
        _sample_kernel,
        grid=(steps,),
        in_specs=[pl.BlockSpec((B_Q_HEADS, LANES), lambda i: (0, 0)),
                  b4_in, t4(), t4(), r2(A_WIDTH), r2(A_WIDTH), pl.BlockSpec((1, A_DV), lambda i: (0, 0)),
                  pl.BlockSpec((sb, B_Q_HEADS, LANES), lambda i: (i, 0, 0)), r2(B_KV_WIDTH), r2(B_KV_WIDTH),
                  c3_in(), c3_in()],
        out_specs=[b4(), r2(A_WIDTH), pl.BlockSpec((sb, B_Q_HEADS, LANES), lambda i: (i, 0, 0)), c3(), c3()],
        out_shape=[jax.ShapeDtypeStruct(state.shape[1:], F32),
                   jax.ShapeDtypeStruct((nb, A_WIDTH), BF16),
                   jax.ShapeDtypeStruct((nb, B_Q_HEADS, LANES), F32),
                   jax.ShapeDtypeStruct((nb, w, B_KV_WIDTH), F32),
                   jax.ShapeDtypeStruct((nb, w, B_KV_WIDTH), F32)],
        scratch_shapes=[pltpu.VMEM((sb, A_WIDTH), F32)],
        compiler_params=_cparams(("arbitrary",)),
        name="sample_step",
    )(sinks8, state, lft, qat, va, ga, norm_g, q3, kn, vn, ck, cv)


def _merge_kernel(x_ref, oa_ref, ob_ref, sga_ref, sgb_ref, wa_ref, wb_ref, wo_ref, g_ref, b_ref, h_ref):
    merged = (sga_ref[...].astype(F32) * _dot(oa_ref[...], wa_ref[...])
              + sgb_ref[...].astype(F32) * _dot(ob_ref[...], wb_ref[...]))
    mix = _dot(merged.astype(BF16), wo_ref[...])
    _store_tiled(h_ref, _layer_norm(DN_ALPHA * x_ref[...] + mix, g_ref[...], b_ref[...]))


def _merge(x, oa, ob, sga, sgb, wa, wb, wo, g, b, n_rows, x_off_blocks, out_rows, out_off_blocks, tm, h_prev=None):
    nt = n_rows // tm
    row = lambda wd: pl.BlockSpec((tm, wd), lambda i: (i, 0))
    const = lambda a: pl.BlockSpec(a.shape, lambda i: (0, 0))
    args = [x, oa, ob, sga, sgb, wa, wb, wo, g, b]
    in_specs = [pl.BlockSpec((tm, D_MODEL), lambda i: (i + x_off_blocks, 0)),
                row(A_WIDTH), row(B_WIDTH), row(D_MODEL), row(D_MODEL),
                const(wa), const(wb), const(wo), const(g), const(b)]
    kern = _merge_kernel
    aliases = {}
    if h_prev is not None:
        args.append(h_prev)
        in_specs.append(pl.BlockSpec(memory_space=pl.ANY))
        aliases = {len(args) - 1: 0}
        kern = lambda *refs: _merge_kernel(*refs[:10], refs[11])
    return pl.pallas_call(
        kern,
        grid=(nt,),
        in_specs=in_specs,
        out_specs=pl.BlockSpec((tm * SUBLANES, LANES), lambda i: (i + out_off_blocks, 0)),
        out_shape=jax.ShapeDtypeStruct((out_rows * SUBLANES, LANES), F32),
        input_output_aliases=aliases,
        compiler_params=_cparams(("arbitrary",)),
        name="merge",
    )(*args)


def _mixer_layer(l, xp, xs, xs_off, n_prompt, bsz, t, n_sample, prm):
    tm = ROW_TILE
    p = _proj(xp, prm["w_in"][l], prm["cos_p"], prm["sin_p"], prm["lower"][l], n_prompt, 0, tm)
    qa, lf, va, ga, qb, kb, vb, sga, sgb = p
    oa, st_p = _hgrn_prompt(qa, lf, va, ga, prm["norm_g"][l], bsz, t)
    ob = _swa_prompt(prm["sinks"][l], qb, kb, vb, bsz, t)
    n_all = n_prompt + n_sample
    h_all = _merge(xp, oa, ob, sga, sgb, prm["wa"][l], prm["wb"][l], prm["wo"][l], prm["ln1_g"][l],
                   prm["ln1_b"][l], n_prompt, 0, n_all, 0, tm, h_prev=jnp.zeros((n_all * SUBLANES, LANES), F32))
    kp = kb.reshape(bsz, t, B_KV_HEADS, B_HEAD_DIM)[:, -WINDOW:]
    vp = vb.reshape(bsz, t, B_KV_HEADS, B_HEAD_DIM)[:, -WINDOW:]

    ts = n_sample
    ps = _proj(xs, prm["w_in"][l], prm["cos_s"], prm["sin_s"], prm["lower"][l], ts, xs_off, ts)
    qa_s, lf_s, va_s, ga_s, qb_s, kb_s, vb_s, sga_s, sgb_s = ps
    sb = SAMPLE_BLOCK
    to_t = lambda a: a.reshape(ts // sb, sb, A_HEADS, A_DK).transpose(2, 0, 3, 1)
    qh = qb_s.astype(F32).reshape(ts, B_Q_HEADS, B_HEAD_DIM)
    z = jnp.zeros_like(qh[:, :B_GROUP])
    q3 = jnp.concatenate([jnp.concatenate([qh[:, :B_GROUP], z], axis=-1),
                          jnp.concatenate([z, qh[:, B_GROUP:]], axis=-1)], axis=1)
    st_s, oa_s, ob3, ck_s, cv_s = _sample_step(
        l, prm["sinks8"][l], prm["state"], to_t(lf_s), to_t(qa_s.astype(F32)), va_s, ga_s, prm["norm_g"][l],
        q3, kb_s, vb_s, prm["cache_k"], prm["cache_v"])
    ob_s = jnp.concatenate([ob3[:, :B_GROUP, :B_HEAD_DIM], ob3[:, B_GROUP:, B_HEAD_DIM:]], axis=1)
    ob_s = ob_s.reshape(ts, B_WIDTH).astype(BF16)
    h_all = _merge(xs, oa_s, ob_s, sga_s, sgb_s, prm["wa"][l], prm["wb"][l], prm["wo"][l], prm["ln1_g"][l],
                   prm["ln1_b"][l], ts, xs_off, n_all, n_prompt // ts, ts, h_prev=h_all)
    return h_all, (kp, vp, st_p, ck_s, cv_s, st_s)


def _router_kernel(h_ref, wr_ref, bias_ref, e_ref, w_ref, mask_ref, cnt_ref):
    tm = h_ref.shape[0] // SUBLANES
    gsz = GROUP_SIZE

    @pl.when(pl.program_id(0) == 0)
    def _():
        cnt_ref[...] = jnp.zeros_like(cnt_ref)

    logits = lax.dot_general(wr_ref[...], _load_tiled(h_ref, tm), (((1,), (1,)), ((), ())),
                             precision=lax.Precision.HIGHEST, preferred_element_type=F32)
    scores = _sigmoid(logits)
    sel = scores + bias_ref[...][:, 0:1]
    rowi = lax.broadcasted_iota(I32, (gsz, tm), 0)
    neg_inf = -jnp.inf
    blocks = [sel[g * gsz:(g + 1) * gsz] for g in range(N_GROUPS)]
    sblocks = [scores[g * gsz:(g + 1) * gsz] for g in range(N_GROUPS)]

    gscore = []
    for blk in blocks:
        m1 = jnp.max(blk, axis=0, keepdims=True)
        i1 = jnp.min(jnp.where(blk == m1, rowi, gsz), axis=0, keepdims=True)
        m2 = jnp.max(jnp.where(rowi == i1, neg_inf, blk), axis=0, keepdims=True)
        gscore.append(m1 + m2)
    work = []
    for g in range(N_GROUPS):
        ahead = jnp.zeros((1, tm), I32)
        for g2 in range(N_GROUPS):
            if g2 != g:
                beats = (gscore[g2] > gscore[g]) | ((gscore[g2] == gscore[g]) & (g2 < g))
                ahead = ahead + beats.astype(I32)
        work.append(jnp.where(ahead < TOPK_GROUPS, blocks[g], NEG_BIG))

    chosen = [jnp.zeros((gsz, tm), F32) for _ in range(N_GROUPS)]
    es, ws = [], []
    for _ in range(TOP_K):
        m = work[0]
        for g in range(1, N_GROUPS):
            m = jnp.maximum(m, work[g])
        m = jnp.max(m, axis=0, keepdims=True)
        cand = jnp.where(work[0] == m, rowi, N_EXPERTS)
        for g in range(1, N_GROUPS):
            cand = jnp.minimum(cand, jnp.where(work[g] == m, rowi + g * gsz, N_EXPERTS))
        idx = jnp.min(cand, axis=0, keepdims=True)
        wj = jnp.zeros((1, tm), F32)
        for g in range(N_GROUPS):
            hit = (rowi + g * gsz) == idx
            wj = wj + jnp.sum(jnp.where(hit, sblocks[g], 0.0), axis=0, keepdims=True)
            chosen[g] = jnp.where(hit, 1.0, chosen[g])
            work[g] = jnp.where(hit, neg_inf, work[g])
        es.append(idx)
        ws.append(wj)
    wsum = ws[0]
    for j in range(1, TOP_K):
        wsum = wsum + ws[j]
    for j in range(TOP_K):
        e_ref[j:j + 1, :] = es[j]
        w_ref[j:j + 1, :] = ws[j] / wsum * ROUTED_SCALE
    for g in range(N_GROUPS):
        rows = slice(g * gsz, (g + 1) * gsz)
        mask_ref[rows, :] = chosen[g]
        part = chosen[g][:, 0:LANES]
        for c in range(1, tm // LANES):
            part = part + chosen[g][:, c * LANES:(c + 1) * LANES]
        cnt_ref[rows, :] = cnt_ref[rows, :] + part


def _router(h_all, wr_t, bias_b, tm):
    n = h_all.shape[0] // SUBLANES
    col = lambda r: pl.BlockSpec((r, tm), lambda i: (0, i))
    return pl.pallas_call(
        _router_kernel,
        grid=(n // tm,),
        in_specs=[pl.BlockSpec((tm * SUBLANES, LANES), lambda i: (i, 0)),
                  pl.BlockSpec((N_EXPERTS, D_MODEL), lambda i: (0, 0)),
                  pl.BlockSpec((N_EXPERTS, LANES), lambda i: (0, 0))],
        out_specs=[col(TOP_K), col(TOP_K), col(N_EXPERTS), pl.BlockSpec((N_EXPERTS, LANES), lambda i: (0, 0))],
        out_shape=[jax.ShapeDtypeStruct((TOP_K, n), I32), jax.ShapeDtypeStruct((TOP_K, n), F32),
                   jax.ShapeDtypeStruct((N_EXPERTS, n), F32), jax.ShapeDtypeStruct((N_EXPERTS, LANES), F32)],
        compiler_params=_cparams(("arbitrary",)),
        name="router",
    )(h_all, wr_t, bias_b)


def _rank_kernel(mask_ref, e_ref, offs_ref, triu_ref, dest_ref, carry):
    tm = mask_ref.shape[1]

    @pl.when(pl.program_id(0) == 0)
    def _():
        carry[...] = jnp.zeros_like(carry)

    mk = mask_ref[...]
    rank = _dot(mk.astype(BF16), triu_ref[...])
    dest_full = rank + (offs_ref[...][:, 0:1] + carry[...][:, 0:1])
    rowi = lax.broadcasted_iota(I32, (N_EXPERTS, tm), 0)
    for j in range(TOP_K):
        d = jnp.sum(jnp.where(rowi == e_ref[j:j + 1, :], dest_full, 0.0), axis=0, keepdims=True)
        dest_ref[j:j + 1, :] = d.astype(I32)
    carry[...] = carry[...] + jnp.sum(mk, axis=1, keepdims=True)


def _rank(mask_t, e_t, offs_b, tm):
    n = mask_t.shape[1]
    r = np.arange(tm)
    triu = jnp.asarray((r[:, None] < r[None, :]).astype(np.float32), BF16)
    return pl.pallas_call(
        _rank_kernel,
        grid=(n // tm,),
        in_specs=[pl.BlockSpec((N_EXPERTS, tm), lambda i: (0, i)),
                  pl.BlockSpec((TOP_K, tm), lambda i: (0, i)),
                  pl.BlockSpec((N_EXPERTS, LANES), lambda i: (0, 0)),
                  pl.BlockSpec((tm, tm), lambda i: (0, 0))],
        out_specs=pl.BlockSpec((TOP_K, tm), lambda i: (0, i)),
        out_shape=jax.ShapeDtypeStruct((TOP_K, n), I32),
        scratch_shapes=[pltpu.VMEM((N_EXPERTS, LANES), F32)],
        compiler_params=_cparams(("arbitrary",)),
        name="rank",
    )(mask_t, e_t, offs_b, triu)


INV_COLS = 512
TOKEN_RADIX = 128


def _inverse_kernel(dest_ref, inv_ref):
    tm = dest_ref.shape[1]
    nq = inv_ref.shape[0]

    @pl.when(pl.program_id(0) == 0)
    def _():
        inv_ref[...] = jnp.zeros_like(inv_ref)

    tok = pl.program_id(0) * tm + lax.broadcasted_iota(I32, (1, tm), 1)
    t_hi = jnp.right_shift(tok, TOKEN_RADIX.bit_length() - 1).astype(F32)
    t_lo = jnp.bitwise_and(tok, TOKEN_RADIX - 1).astype(F32)
    qi = lax.broadcasted_iota(I32, (nq, tm), 0)
    si = lax.broadcasted_iota(I32, (INV_COLS, tm), 0)
    acc_hi = jnp.zeros(inv_ref.shape, F32)
    acc_lo = jnp.zeros(inv_ref.shape, F32)
    for j in range(TOP_K):
        d = dest_ref[j:j + 1, :]
        at_q = qi == jnp.right_shift(d, INV_COLS.bit_length() - 1)
        col = jnp.where(si == jnp.bitwise_and(d, INV_COLS - 1), 1.0, 0.0).astype(BF16)
        acc_hi = acc_hi + _dot_nt(jnp.where(at_q, t_hi, 0.0).astype(BF16), col)
        acc_lo = acc_lo + _dot_nt(jnp.where(at_q, t_lo, 0.0).astype(BF16), col)
    inv_ref[...] = inv_ref[...] + (acc_hi * float(TOKEN_RADIX) + acc_lo)


def _inverse_map(dest_t, tm):
    n = dest_t.shape[1]
    n_rows = n * TOP_K
    nq = -(-(n_rows // INV_COLS) // 8) * 8
    inv = pl.pallas_call(
        _inverse_kernel,
        grid=(n // tm,),
        in_specs=[pl.BlockSpec((TOP_K, tm), lambda i: (0, i))],
        out_specs=pl.BlockSpec((nq, INV_COLS), lambda i: (0, 0)),
        out_shape=jax.ShapeDtypeStruct((nq, INV_COLS), F32),
        compiler_params=_cparams(("arbitrary",)),
        name="inverse_map",
    )(dest_t)
    return inv.reshape(-1)[:n_rows].astype(I32).reshape(n_rows // EXPERT_ROW_TILE, 1, EXPERT_ROW_TILE)


DMA_THREADS = 2


def _token_copy(src_ref, token, dst_ref, dst_row, sem):
    return pltpu.make_async_copy(src_ref.at[pl.ds(pl.multiple_of(token * SUBLANES, SUBLANES), SUBLANES)],
                                 dst_ref.at[pl.ds(dst_row * SUBLANES, SUBLANES)], sem)


def _expert_kernel(n_tiles, tile_ref, exp_ref, valid_ref, first_ref, newexp_ref, lo_ref, hi_ref,
                   inv_ref, invn_ref, h_ref, wg_ref, wu_ref, wd_ref, y_ref, xbuf, wgb, wub, wdb, sem):
    i = pl.program_id(0)
    rows = xbuf.shape[1] // SUBLANES
    tile = tile_ref[i]
    slot = tile % 2
    valid = valid_ref[i] == 1
    first = first_ref[i] == 1

    def wait_rows(s):
        pltpu.make_async_copy(h_ref.at[pl.ds(0, rows * SUBLANES)], xbuf.at[s], sem.at[s]).wait()

    @pl.when(valid & first)
    def _():
        @pl.when(i == 0)
        def _():
            def issue(k, c):
                for u in range(SUBLANES):
                    r = k * SUBLANES + u
                    pltpu.make_async_copy(
                        h_ref.at[pl.ds(pl.multiple_of(inv_ref[0, 0, r] * SUBLANES, SUBLANES), SUBLANES)],
                        xbuf.at[0, pl.ds(pl.multiple_of(r * SUBLANES, SUBLANES), SUBLANES)], sem.at[0]).start()
                return c

            lax.fori_loop(0, rows // SUBLANES, issue, 0)

        wait_rows(slot)

    @pl.when(valid & (newexp_ref[i] == 1))
    def _():
        wgb[...] = wg_ref[0].astype(BF16)
        wub[...] = wu_ref[0].astype(BF16)
        wdb[...] = wd_ref[0].astype(BF16)

    def compute(first_visit):
        xb = _load_tiled(xbuf.at[slot], rows).astype(BF16)
        if first_visit:
            for r in range(rows):
                _token_copy(h_ref, invn_ref[0, 0, r], xbuf.at[1 - slot], r, sem.at[1 - slot]).start(
                    priority=r % DMA_THREADS)
        gate = _dot(xb, wgb[...])
        up = _dot(xb, wub[...])
        y = _dot((_silu(gate) * up).astype(BF16), wdb[...])
        rowi = lax.broadcasted_iota(I32, y.shape, 0)
        mine = (rowi >= lo_ref[i]) & (rowi < hi_ref[i])
        _store_tiled(y_ref, jnp.where(mine, y, 0.0 if first_visit else _load_tiled(y_ref, rows)))

    @pl.when(valid & first)
    def _():
        compute(True)

    @pl.when(valid & jnp.logical_not(first))
    def _():
        compute(False)

    @pl.when(valid & first & (tile == n_tiles - 1))
    def _():
        wait_rows(1 - slot)


def _group_metadata(counts, n_rows):
    tmo = EXPERT_ROW_TILE
    n_tiles = n_rows // tmo
    ends = jnp.cumsum(counts)
    offs = ends - counts
    first_tile = offs // tmo
    n_t = jnp.where(counts > 0, (ends - 1) // tmo - first_tile + 1, 0)
    cum = jnp.cumsum(n_t)
    base = cum - n_t
    n_items = n_tiles + N_EXPERTS
    idx = jnp.arange(n_items, dtype=I32)
    valid = (idx < cum[-1]).astype(I32)
    idc = jnp.minimum(idx, cum[-1] - 1)
    e = jnp.minimum(jnp.sum((cum[None, :] <= idc[:, None]).astype(I32), axis=1), N_EXPERTS - 1)
    tile = (first_tile[e] + idc - base[e]).astype(I32)
    one = jnp.ones((1,), I32)
    first = jnp.concatenate([one, (tile[1:] != tile[:-1]).astype(I32)])
    new_expert = jnp.concatenate([one, (e[1:] != e[:-1]).astype(I32)])
    lo = jnp.clip(offs[e] - tile * tmo, 0, tmo).astype(I32)
    hi = jnp.clip(ends[e] - tile * tmo, 0, tmo).astype(I32)
    return tile, e, valid, first, new_expert, lo, hi


def _experts(l, meta, inv3, h_all, wg, wu, wd):
    tmo = EXPERT_ROW_TILE
    n_rows = inv3.shape[0] * tmo
    n_items = meta[0].shape[0]
    wspec = lambda shp: pl.BlockSpec((None, 1) + shp, lambda i, tl, ex, *_: (l, ex[i], 0, 0))
    n_tiles = inv3.shape[0]
    idx_spec = lambda nxt: pl.BlockSpec(
        (1, 1, tmo), lambda i, tl, *_: (jnp.minimum(tl[i] + nxt, n_tiles - 1), 0, 0), memory_space=pltpu.SMEM)
    return pl.pallas_call(
        functools.partial(_expert_kernel, n_tiles),
        grid_spec=pltpu.PrefetchScalarGridSpec(
            num_scalar_prefetch=7,
            grid=(n_items,),
            in_specs=[idx_spec(0), idx_spec(1), pl.BlockSpec(memory_space=pl.ANY),
                      wspec((D_MODEL, D_EXPERT)), wspec((D_MODEL, D_EXPERT)), wspec((D_EXPERT, D_MODEL))],
            out_specs=pl.BlockSpec((tmo * SUBLANES, LANES), lambda i, tl, *_: (tl[i], 0)),
            scratch_shapes=[pltpu.VMEM((2, tmo * SUBLANES, LANES), F32),
                            pltpu.VMEM((D_MODEL, D_EXPERT), BF16), pltpu.VMEM((D_MODEL, D_EXPERT), BF16),
                            pltpu.VMEM((D_EXPERT, D_MODEL), BF16), pltpu.SemaphoreType.DMA((2,))]),
        out_shape=jax.ShapeDtypeStruct((n_rows * SUBLANES, LANES), F32),
        compiler_params=_cparams(("arbitrary",)),
        name="experts",
    )(*meta, inv3, inv3, h_all, wg, wu, wd)


def _post_kernel(dest_ref, h_ref, wt_ref, y_ref, wsg_ref, wsu_ref, wsd_ref, g_ref, b_ref, out_ref, buf, sem):
    tm = out_ref.shape[0]

    def issue(k, c):
        for u in range(SUBLANES):
            t = k * SUBLANES + u
            for j in range(TOP_K):
                pltpu.make_async_copy(
                    y_ref.at[pl.ds(pl.multiple_of(dest_ref[j, t] * SUBLANES, SUBLANES), SUBLANES)],
                    buf.at[j, pl.ds(pl.multiple_of(t * SUBLANES, SUBLANES), SUBLANES)], sem).start(
                        priority=j % DMA_THREADS)
        return c

    lax.fori_loop(0, tm // SUBLANES, issue, 0)

    h = _load_tiled(h_ref, tm)
    hb = h.astype(BF16)
    shared = _dot((_silu(_dot(hb, wsg_ref[...])) * _dot(hb, wsu_ref[...])).astype(BF16), wsd_ref[...])

    for j in range(TOP_K):
        pltpu.make_async_copy(y_ref.at[pl.ds(0, tm * SUBLANES)], buf.at[j], sem).wait()

    wt = wt_ref[...]
    routed = _load_tiled(buf.at[0], tm) * wt[:, 0:1]
    for j in range(1, TOP_K):
        routed = routed + _load_tiled(buf.at[j], tm) * wt[:, j:j + 1]
    out_ref[...] = _layer_norm(DN_ALPHA * h + (routed + shared), g_ref[...], b_ref[...])


def _post(dest_t, h_all, w_tok, y, wsg, wsu, wsd, g, b, tm):
    n = h_all.shape[0] // SUBLANES
    const = lambda a: pl.BlockSpec(a.shape, lambda i: (0, 0))
    return pl.pallas_call(
        _post_kernel,
        grid=(n // tm,),
        in_specs=[pl.BlockSpec((TOP_K, tm), lambda i: (0, i), memory_space=pltpu.SMEM),
                  pl.BlockSpec((tm * SUBLANES, LANES), lambda i: (i, 0)),
                  pl.BlockSpec((tm, TOP_K), lambda i: (i, 0)),
                  pl.BlockSpec(memory_space=pl.ANY),
                  const(wsg), const(wsu), const(wsd), const(g), const(b)],
        out_specs=pl.BlockSpec((tm, D_MODEL), lambda i: (i, 0)),
        out_shape=jax.ShapeDtypeStruct((n, D_MODEL), F32),
        scratch_shapes=[pltpu.VMEM((TOP_K, tm * SUBLANES, LANES), F32), pltpu.SemaphoreType.DMA(())],
        compiler_params=_cparams(("arbitrary",)),
        name="moe_post",
    )(dest_t, h_all, w_tok, y, wsg, wsu, wsd, g, b)


def _moe_layer(l, h_all, prm):
    tm = MOE_ROW_TILE
    e_t, w_t, mask_t, cnt = _router(h_all, prm["wr_t"][l], prm["rbias"][l], tm)
    counts = jnp.sum(cnt, axis=1).astype(I32)
    offs = jnp.cumsum(counts) - counts
    offs_b = jnp.broadcast_to(offs.astype(F32)[:, None], (N_EXPERTS, LANES))
    dest_t = _rank(mask_t, e_t, offs_b, tm)
    inv3 = _inverse_map(dest_t, tm)
    meta = _group_metadata(counts, h_all.shape[0] // SUBLANES * TOP_K)
    y = _experts(l, meta, inv3, h_all, prm["w_exp_gate"], prm["w_exp_up"], prm["w_exp_down"])
    return _post(dest_t, h_all, w_t.T, y, prm["wsg"][l], prm["wsu"][l], prm["wsd"][l],
                 prm["ln2_g"][l], prm["ln2_b"][l], tm)


def kernel(x_prompt, x_sample, cache_k, cache_v, state_hgrn, w_in, hgrn_lower_bounds, hgrn_norm_g, attn_sinks, w_branch_a, w_branch_b, w_out, ln1_g, ln1_b, w_router, router_bias, w_exp_gate, w_exp_up, w_exp_down, w_sh_gate, w_sh_up, w_sh_down, ln2_g, ln2_b):
    bsz, t, d = x_prompt.shape
    n_sample = x_sample.shape[0] * x_sample.shape[1]
    n_prompt = bsz * t
    depth = w_in.shape[0]
    assert d == D_MODEL and x_sample.shape[1] == 1 and n_prompt % ROW_TILE == 0 and t % ROW_TILE == 0
    assert n_prompt % n_sample == 0 and (n_prompt + n_sample) % MOE_ROW_TILE == 0
    assert ((n_prompt + n_sample) * TOP_K) % EXPERT_ROW_TILE == 0 and n_sample % SAMPLE_BLOCK == 0

    lb_prob = jax.nn.softmax(hgrn_lower_bounds.astype(F32), axis=0)
    lower = (jnp.cumsum(lb_prob, axis=0) - lb_prob[0])[:, None, :]
    cos_p, sin_p = _rope_tables(jnp.arange(t))
    cos_s, sin_s = _rope_tables(jnp.full((n_sample,), PAST_LEN))
    row = lambda a: a[:, None, :]
    prm = dict(
        w_in=w_in.astype(BF16), lower=lower, cos_p=cos_p, sin_p=sin_p, cos_s=cos_s, sin_s=sin_s,
        norm_g=row(hgrn_norm_g), sinks=attn_sinks,
        sinks8=jnp.broadcast_to(attn_sinks[:, :, None], (depth, B_Q_HEADS, LANES)),
        wa=w_branch_a.astype(BF16), wb=w_branch_b.astype(BF16), wo=w_out.astype(BF16),
        ln1_g=row(ln1_g), ln1_b=row(ln1_b), ln2_g=row(ln2_g), ln2_b=row(ln2_b),
        wr_t=jnp.swapaxes(w_router, 1, 2),
        rbias=jnp.broadcast_to(router_bias[:, :, None], (depth, N_EXPERTS, LANES)),
        w_exp_gate=w_exp_gate, w_exp_up=w_exp_up, w_exp_down=w_exp_down,
        wsg=w_sh_gate.astype(BF16), wsu=w_sh_up.astype(BF16), wsd=w_sh_down.astype(BF16),
        state=state_hgrn,
        cache_k=cache_k.reshape(depth, n_sample, WINDOW, B_KV_WIDTH),
        cache_v=cache_v.reshape(depth, n_sample, WINDOW, B_KV_WIDTH),
    )

    xp, xs, xs_off = x_prompt.reshape(n_prompt, d), x_sample.reshape(n_sample, d), 0
    per_layer = []
    for l in range(depth):
        h_all, outs = _mixer_layer(l, xp, xs, xs_off, n_prompt, bsz, t, n_sample, prm)
        y_all = _moe_layer(l, h_all, prm)
        per_layer.append(outs)
        xp, xs, xs_off = y_all, y_all, n_prompt // n_sample

    kv_shape = (n_sample, WINDOW, B_KV_HEADS, B_HEAD_DIM)
    stack = lambda k, f=lambda a: a: jnp.stack([f(o[k]) for o in per_layer])
    return (y_all[:n_prompt].reshape(bsz, t, d), y_all[n_prompt:].reshape(n_sample, 1, d),
            stack(0), stack(1), stack(2),
            stack(3, lambda a: a.reshape(kv_shape)), stack(4, lambda a: a.reshape(kv_shape)), stack(5))
```

```python
import functools

import numpy as np
import jax
import jax.numpy as jnp
from jax import lax
from jax.experimental import pallas as pl
from jax.experimental.pallas import tpu as pltpu

F32 = jnp.float32
BF16 = jnp.bfloat16
I32 = jnp.int32

D_MODEL = 1024
DEPTH = 2
PAST_LEN = 16384
A_HEADS = 4
A_DK = 128
A_DV = 128
A_KEY = A_HEADS * A_DK
A_WIDTH = A_HEADS * A_DV
B_Q_HEADS = 8
B_KV_HEADS = 2
B_HEAD_DIM = 64
B_GROUP = B_Q_HEADS // B_KV_HEADS
B_WIDTH = B_Q_HEADS * B_HEAD_DIM
B_KV_WIDTH = B_KV_HEADS * B_HEAD_DIM
WINDOW = 128
ROPE_THETA = 10000.0
ATTN_SCALE = B_HEAD_DIM ** -0.5
N_EXPERTS = 64
TOP_K = 8
N_GROUPS = 8
GROUP_SIZE = N_EXPERTS // N_GROUPS
TOPK_GROUPS = 4
D_EXPERT = D_MODEL // 4
D_SHARED = D_EXPERT
ROUTED_SCALE = 2.5
DN_ALPHA = (2 * DEPTH) ** 0.25
LN_EPS = 1e-5
RMS_EPS = 1e-6
NEG_BIG = -1e30
TINY = 1.1754944e-38
OFF_AF = A_KEY
OFF_AI = 2 * A_KEY
OFF_AG = OFF_AI + A_WIDTH
OFF_BQ = OFF_AG + A_WIDTH
OFF_BK = OFF_BQ + B_WIDTH
OFF_BV = OFF_BK + B_KV_WIDTH
OFF_GA = OFF_BV + B_KV_WIDTH
OFF_GB = OFF_GA + D_MODEL
IN_COLS = OFF_GB + D_MODEL

LANES = 128
HGRN_CHUNK = 128
HGRN_LEVELS = 7
ROW_TILE = 512
MOE_ROW_TILE = 384
EXPERT_ROW_TILE = 256
POST_ROW_TILE = 128
VMEM_LIMIT = 56 * 1024 * 1024


def _cparams(sem, vmem=VMEM_LIMIT):
    return pltpu.CompilerParams(dimension_semantics=sem, vmem_limit_bytes=vmem)


def _dot(a, b):
    return jnp.dot(a, b, preferred_element_type=F32)


def _dot_nt(a, b):
    return lax.dot_general(a, b, (((1,), (1,)), ((), ())), preferred_element_type=F32)


def _dot_tn(a, b):
    return lax.dot_general(a, b, (((0,), (0,)), ((), ())), preferred_element_type=F32)


def _sigmoid(x):
    return 1.0 / (1.0 + jnp.exp(-x))


def _silu(x):
    return x * _sigmoid(x)


def _split3(x):
    hi = x.astype(BF16)
    r1 = x - hi.astype(F32)
    mid = r1.astype(BF16)
    lo = (r1 - mid.astype(F32)).astype(BF16)
    return hi, mid, lo


def _layer_norm(y, g, b):
    mu = jnp.mean(y, axis=-1, keepdims=True)
    d = y - mu
    var = jnp.mean(d * d, axis=-1, keepdims=True)
    return d * lax.rsqrt(var + LN_EPS) * g + b


SUBLANES = 8
assert D_MODEL == SUBLANES * LANES


def _store_tiled(ref, val):
    rows = val.shape[0]
    for c in range(SUBLANES):
        ref[pl.ds(c, rows, stride=SUBLANES), :] = val[:, c * LANES:(c + 1) * LANES]


def _load_tiled(ref, rows):
    return jnp.concatenate([ref[pl.ds(c, rows, stride=SUBLANES), :] for c in range(SUBLANES)], axis=1)


def _log_forget(af, lower):
    ls = jnp.minimum(af, 0.0) - jnp.log1p(jnp.exp(-jnp.abs(af)))
    a = jnp.log(jnp.maximum(lower, TINY))
    b = jnp.log1p(-lower) + ls
    mixed = jnp.maximum(a, b) + jnp.log1p(jnp.exp(-jnp.abs(a - b)))
    return jnp.where(lower > 0.0, mixed, ls)


def _proj_kernel(x_ref, w_ref, cos_ref, sin_ref, low_ref,
                 qa_ref, lf_ref, va_ref, ga_ref, qb_ref, kb_ref, vb_ref, sga_ref, sgb_ref):
    xb = x_ref[...].astype(BF16)

    def mm(c0, n):
        return _dot(xb, w_ref[:, c0:c0 + n])

    qa_ref[...] = _silu(mm(0, A_KEY)).astype(BF16)
    lf_ref[...] = _log_forget(mm(OFF_AF, A_KEY), low_ref[...])
    va_ref[...] = mm(OFF_AI, A_WIDTH).astype(BF16)
    ga_ref[...] = _silu(mm(OFF_AG, A_WIDTH)).astype(BF16)

    cos = cos_ref[...]
    sin = sin_ref[...]
    lane = lax.broadcasted_iota(I32, cos.shape, 1)
    first_half = (lane & (B_HEAD_DIM // 2)) == 0

    def rope(blk):
        partner = jnp.where(first_half, pltpu.roll(blk, LANES - B_HEAD_DIM // 2, 1),
                            pltpu.roll(blk, B_HEAD_DIM // 2, 1))
        return blk * cos + partner * sin

    bq = mm(OFF_BQ, B_WIDTH)
    for j in range(B_WIDTH // LANES):
        sl = slice(j * LANES, (j + 1) * LANES)
        qb_ref[:, sl] = (rope(bq[:, sl]) * ATTN_SCALE).astype(BF16)
    kb_ref[...] = rope(mm(OFF_BK, B_KV_WIDTH))
    vb_ref[...] = mm(OFF_BV, B_KV_WIDTH)
    sga_ref[...] = _sigmoid(mm(OFF_GA, D_MODEL)).astype(BF16)
    sgb_ref[...] = _sigmoid(mm(OFF_GB, D_MODEL)).astype(BF16)


def _proj(x, w_bf, cos_t, sin_t, lower, n_rows, row_off_blocks, tm):
    nt = n_rows // tm
    tab_blocks = cos_t.shape[0] // tm
    row = lambda w: pl.BlockSpec((tm, w), lambda i: (i, 0))
    outs = [(A_KEY, BF16), (A_KEY, F32), (A_WIDTH, BF16), (A_WIDTH, BF16), (B_WIDTH, BF16),
            (B_KV_WIDTH, F32), (B_KV_WIDTH, F32), (D_MODEL, BF16), (D_MODEL, BF16)]
    return pl.pallas_call(
        _proj_kernel,
        grid=(nt,),
        in_specs=[pl.BlockSpec((tm, D_MODEL), lambda i: (i + row_off_blocks, 0)),
                  pl.BlockSpec((D_MODEL, IN_COLS), lambda i: (0, 0)),
                  pl.BlockSpec((tm, LANES), lambda i: (i % tab_blocks, 0)),
                  pl.BlockSpec((tm, LANES), lambda i: (i % tab_blocks, 0)),
                  pl.BlockSpec((1, A_KEY), lambda i: (0, 0))],
        out_specs=[row(w) for w, _ in outs],
        out_shape=[jax.ShapeDtypeStruct((n_rows, w), dt) for w, dt in outs],
        compiler_params=_cparams(("arbitrary",)),
        name="proj",
    )(x, w_bf, cos_t, sin_t, lower)


def _rope_tables(pos):
    half = B_HEAD_DIM // 2
    inv = ROPE_THETA ** (-jnp.arange(half, dtype=F32) / half)
    ang = pos.astype(F32)[:, None] * inv[None, :]
    cos = jnp.cos(ang)
    sin = jnp.sin(ang)
    reps = LANES // B_HEAD_DIM
    cos_t = jnp.tile(jnp.concatenate([cos, cos], axis=1), (1, reps))
    sin_t = jnp.tile(jnp.concatenate([-sin, sin], axis=1), (1, reps))
    return cos_t, sin_t


def _hgrn_constants():
    c = HGRN_CHUNK
    r = np.arange(c)
    tri = (r[None, :] <= r[:, None]).astype(np.float32)
    sel = np.zeros((HGRN_LEVELS, c, c), np.float32)
    upper = np.zeros((HGRN_LEVELS, c, A_KEY), np.float32)
    pair = np.zeros((HGRN_LEVELS + 1, c, c), np.float32)
    for l in range(HGRN_LEVELS):
        b = c >> (l + 1)
        ref_row = (r // (2 * b)) * (2 * b) + b - 1
        sel[l, r, ref_row] = 1.0
        up = (r % (2 * b)) >= b
        upper[l] = up[:, None]
        same = (r[:, None] // (2 * b)) == (r[None, :] // (2 * b))
        pair[l] = (up[:, None] & ~up[None, :] & same)
    pair[HGRN_LEVELS] = np.eye(c)
    return (jnp.asarray(tri, BF16), jnp.asarray(sel.reshape(HGRN_LEVELS * c, c), BF16),
            jnp.asarray(upper), jnp.asarray(pair))


def _hgrn_kernel(qa_ref, lf_ref, va_ref, ga_ref, g_ref, tri_ref, sel_ref, up_ref, pair_ref,
                 oa_ref, st_ref, s_scr):
    c = HGRN_CHUNK
    step = pl.program_id(1)

    @pl.when(step == 0)
    def _():
        s_scr[...] = jnp.zeros_like(s_scr)

    lf = lf_ref[...]
    tri = tri_ref[...]
    sel = sel_ref[...]
    hi, mid, lo = _split3(lf)
    gcum = _dot(tri, hi) + _dot(tri, mid) + _dot(tri, lo)
    ghi, gmid, glo = _split3(gcum)
    gref = _dot(sel, ghi) + _dot(sel, gmid) + _dot(sel, glo)
    qb = qa_ref[...]
    qf = qb.astype(F32)
    kf = 1.0 - jnp.exp(lf)
    kb = kf.astype(BF16)
    vb = va_ref[...]
    ws = []
    for l in range(HGRN_LEVELS):
        e = jnp.exp(-jnp.abs(gcum - gref[l * c:(l + 1) * c]))
        ws.append((jnp.where(up_ref[l] > 0.5, qf, kf) * e).astype(BF16))
    qg = (qf * jnp.exp(gcum)).astype(BF16)
    gend = gcum[c - 1:c, :]
    kend = (kf * jnp.exp(gend - gcum)).astype(BF16)
    decay = jnp.exp(gend)
    gate = g_ref[...]

    for h in range(A_HEADS):
        sl = slice(h * A_DK, (h + 1) * A_DK)
        att = _dot_nt(qb[:, sl], kb[:, sl]) * pair_ref[HGRN_LEVELS]
        for l in range(HGRN_LEVELS):
            w = ws[l][:, sl]
            att = att + _dot_nt(w, w) * pair_ref[l]
        s_t = s_scr[h]
        o = _dot_nt(qg[:, sl], s_t.astype(BF16)) + _dot(att.astype(BF16), vb[:, sl])
        ms = jnp.mean(o * o, axis=-1, keepdims=True)
        on = o * lax.rsqrt(ms + RMS_EPS) * gate * ga_ref[:, sl].astype(F32)
        oa_ref[:, sl] = on.astype(BF16)
        s_scr[h] = s_t * decay[:, sl] + _dot_tn(vb[:, sl], kend[:, sl])

    @pl.when(step == pl.num_programs(1) - 1)
    def _():
        for h in range(A_HEADS):
            st_ref[0, h] = s_scr[h].T


def _hgrn_prompt(qa, lf, va, ga, norm_g, bsz, t):
    c = HGRN_CHUNK
    nc = t // c
    tri, sel, upper, pair = _hgrn_constants()
    blk = lambda: pl.BlockSpec((c, A_KEY), lambda b, i: (b * nc + i, 0))
    const = lambda a: pl.BlockSpec(a.shape, lambda b, i: (0,) * a.ndim)
    return pl.pallas_call(
        _hgrn_kernel,
        grid=(bsz, nc),
        in_specs=[blk(), blk(), blk(), blk(), pl.BlockSpec((1, A_DV), lambda b, i: (0, 0)),
                  const(tri), const(sel), const(upper), const(pair)],
        out_specs=[blk(), pl.BlockSpec((1, A_HEADS, A_DK, A_DV), lambda b, i: (b, 0, 0, 0))],
        out_shape=[jax.ShapeDtypeStruct((bsz * t, A_WIDTH), BF16),
                   jax.ShapeDtypeStruct((bsz, A_HEADS, A_DK, A_DV), F32)],
        scratch_shapes=[pltpu.VMEM((A_HEADS, A_DV, A_DK), F32)],
        compiler_params=_cparams(("arbitrary", "arbitrary")),
        name="hgrn_prompt",
    )(qa, lf, va, ga, norm_g, tri, sel, upper, pair)


def _swa_kernel(sink_ref, q_ref, k_ref, v_ref, o_ref, kprev, vprev):
    w = WINDOW
    i = pl.program_id(1)

    @pl.when(i == 0)
    def _():
        kprev[...] = jnp.zeros_like(kprev)
        vprev[...] = jnp.zeros_like(vprev)

    kc = k_ref[...]
    vc = v_ref[...]
    kk = jnp.concatenate([kprev[...], kc], axis=0)
    vv = jnp.concatenate([vprev[...], vc], axis=0)
    kr = pltpu.roll(kk, B_HEAD_DIM, 1)
    vr = pltpu.roll(vv, B_HEAD_DIM, 1)
    lo2 = lax.broadcasted_iota(I32, kk.shape, 1) < B_HEAD_DIM
    zero = jnp.zeros_like(kk)
    k_lo = [jnp.where(lo2, kk, zero).astype(BF16), jnp.where(lo2, kr, zero).astype(BF16)]
    k_hi = [jnp.where(lo2, zero, kr).astype(BF16), jnp.where(lo2, zero, kk).astype(BF16)]
    v_dup = [jnp.where(lo2, vv, vr).astype(BF16), jnp.where(lo2, vr, vv).astype(BF16)]

    qi = lax.broadcasted_iota(I32, (w, 2 * w), 0)
    kj = lax.broadcasted_iota(I32, (w, 2 * w), 1)
    valid = (kj >= qi) & (kj <= qi + w) & ((kj >= w) | (i > 0))
    lo1 = lax.broadcasted_iota(I32, (w, LANES), 1) < B_HEAD_DIM

    for j in range(B_WIDTH // LANES):
        g = (2 * j) // B_GROUP
        qblk = q_ref[:, j * LANES:(j + 1) * LANES]
        res = []
        for half, kmat in enumerate((k_lo[g], k_hi[g])):
            sk = sink_ref[2 * j + half]
            s = jnp.where(valid, _dot_nt(qblk, kmat), NEG_BIG)
            m = jnp.maximum(jnp.max(s, axis=-1, keepdims=True), sk)
            p = jnp.exp(s - m)
            denom = jnp.sum(p, axis=-1, keepdims=True) + jnp.exp(sk - m)
            res.append(_dot(p.astype(BF16), v_dup[g]) / denom)
        o_ref[:, j * LANES:(j + 1) * LANES] = jnp.where(lo1, res[0], res[1]).astype(BF16)

    kprev[...] = kc
    vprev[...] = vc


def _swa_prompt(sinks, qb, kb, vb, bsz, t):
    w = WINDOW
    nb = t // w
    return pl.pallas_call(
        _swa_kernel,
        grid_spec=pltpu.PrefetchScalarGridSpec(
            num_scalar_prefetch=1,
            grid=(bsz, nb),
            in_specs=[pl.BlockSpec((w, B_WIDTH), lambda b, i, s: (b * nb + i, 0)),
                      pl.BlockSpec((w, B_KV_WIDTH), lambda b, i, s: (b * nb + i, 0)),
                      pl.BlockSpec((w, B_KV_WIDTH), lambda b, i, s: (b * nb + i, 0))],
            out_specs=pl.BlockSpec((w, B_WIDTH), lambda b, i, s: (b * nb + i, 0)),
            scratch_shapes=[pltpu.VMEM((w, B_KV_WIDTH), F32), pltpu.VMEM((w, B_KV_WIDTH), F32)]),
        out_shape=jax.ShapeDtypeStruct((bsz * t, B_WIDTH), BF16),
        compiler_params=_cparams(("arbitrary", "arbitrary")),
        name="swa_prompt",
    )(sinks, qb, kb, vb)


SAMPLE_BLOCK = 8


def _sample_kernel(sink_ref, st_ref, lft_ref, qat_ref, va_ref, ga_ref, g_ref, q3_ref, kn_ref, vn_ref,
                   ck_ref, cv_ref, st_out, oa_ref, ob_ref, ck_out, cv_out, o_scr):
    w = WINDOW
    row = lax.broadcasted_iota(I32, (w, B_KV_WIDTH), 0)
    for i in range(SAMPLE_BLOCK):
        for h in range(A_HEADS):
            sl = slice(h * A_DV, (h + 1) * A_DV)
            fcol = jnp.exp(lft_ref[h, 0][:, i:i + 1])
            qcol = qat_ref[h, 0][:, i:i + 1]
            vrow = va_ref[i:i + 1, sl].astype(F32)
            s_new = st_ref[i, h] * fcol + (1.0 - fcol) * vrow
            st_out[i, h] = s_new
            o_scr[i:i + 1, sl] = jnp.sum(s_new * qcol, axis=0, keepdims=True)
        kc = ck_ref[i]
        vc = cv_ref[i]
        kn = kn_ref[i:i + 1, :]
        vn = vn_ref[i:i + 1, :]
        q3 = q3_ref[i]
        s = _dot_nt(q3.astype(BF16), kc.astype(BF16))
        s_new_key = jnp.sum(q3 * kn, axis=-1, keepdims=True)
        sk = sink_ref[...][:, 0:1]
        m = jnp.maximum(jnp.maximum(jnp.max(s, axis=-1, keepdims=True), s_new_key), sk)
        p = jnp.exp(s - m)
        pn = jnp.exp(s_new_key - m)
        denom = jnp.sum(p, axis=-1, keepdims=True) + pn + jnp.exp(sk - m)
        ob_ref[i] = (_dot(p.astype(BF16), vc.astype(BF16)) + pn * vn) / denom
        ck_out[i] = jnp.where(row == w - 1, kn, pltpu.roll(kc, w - 1, 0))
        cv_out[i] = jnp.where(row == w - 1, vn, pltpu.roll(vc, w - 1, 0))
    for h in range(A_HEADS):
        sl = slice(h * A_DV, (h + 1) * A_DV)
        o = o_scr[:, sl]
        ms = jnp.mean(o * o, axis=-1, keepdims=True)
        oa_ref[:, sl] = (o * lax.rsqrt(ms + RMS_EPS) * g_ref[...] * ga_ref[:, sl].astype(F32)).astype(BF16)


def _sample_step(l, sinks8, state, lft, qat, va, ga, norm_g, q3, kn, vn, ck, cv):
    nb = state.shape[1]
    sb = SAMPLE_BLOCK
    steps = nb // sb
    w = WINDOW
    b4 = lambda: pl.BlockSpec((sb, A_HEADS, A_DK, A_DV), lambda i: (i, 0, 0, 0))
    b4_in = pl.BlockSpec((None, sb, A_HEADS, A_DK, A_DV), lambda i: (l, i, 0, 0, 0))
    c3_in = lambda: pl.BlockSpec((None, sb, w, B_KV_WIDTH), lambda i: (l, i, 0, 0))
    t4 = lambda: pl.BlockSpec((A_HEADS, 1, A_DK, sb), lambda i: (0, i, 0, 0))
    r2 = lambda wd: pl.BlockSpec((sb, wd), lambda i: (i, 0))
    c3 = lambda: pl.BlockSpec((sb, w, B_KV_WIDTH), lambda i: (i, 0, 0))
    return pl.pallas_call(
        _sample_kernel,
        grid=(steps,),
        in_specs=[pl.BlockSpec((B_Q_HEADS, LANES), lambda i: (0, 0)),
                  b4_in, t4(), t4(), r2(A_WIDTH), r2(A_WIDTH), pl.BlockSpec((1, A_DV), lambda i: (0, 0)),
                  pl.BlockSpec((sb, B_Q_HEADS, LANES), lambda i: (i, 0, 0)), r2(B_KV_WIDTH), r2(B_KV_WIDTH),
                  c3_in(), c3_in()],
        out_specs=[b4(), r2(A_WIDTH), pl.BlockSpec((sb, B_Q_HEADS, LANES), lambda i: (i, 0, 0)), c3(), c3()],
        out_shape=[jax.ShapeDtypeStruct(state.shape[1:], F32),
                   jax.ShapeDtypeStruct((nb, A_WIDTH), BF16),
                   jax.ShapeDtypeStruct((nb, B_Q_HEADS, LANES), F32),
                   jax.ShapeDtypeStruct((nb, w, B_KV_WIDTH), F32),
                   jax.ShapeDtypeStruct((nb, w, B_KV_WIDTH), F32)],
        scratch_shapes=[pltpu.VMEM((sb, A_WIDTH), F32)],
        compiler_params=_cparams(("arbitrary",)),
        name="sample_step",
    )(sinks8, state, lft, qat, va, ga, norm_g, q3, kn, vn, ck, cv)


def _merge_kernel(x_ref, oa_ref, ob_ref, sga_ref, sgb_ref, wa_ref, wb_ref, wo_ref, g_ref, b_ref, h_ref):
    merged = (sga_ref[...].astype(F32) * _dot(oa_ref[...], wa_ref[...])
              + sgb_ref[...].astype(F32) * _dot(ob_ref[...], wb_ref[...]))
    mix = _dot(merged.astype(BF16), wo_ref[...])
    _store_tiled(h_ref, _layer_norm(DN_ALPHA * x_ref[...] + mix, g_ref[...], b_ref[...]))


def _merge(x, oa, ob, sga, sgb, wa, wb, wo, g, b, n_rows, x_off_blocks, out_rows, out_off_blocks, tm, h_prev=None):
    nt = n_rows // tm
    row = lambda wd: pl.BlockSpec((tm, wd), lambda i: (i, 0))
    const = lambda a: pl.BlockSpec(a.shape, lambda i: (0, 0))
    args = [x, oa, ob, sga, sgb, wa, wb, wo, g, b]
    in_specs = [pl.BlockSpec((tm, D_MODEL), lambda i: (i + x_off_blocks, 0)),
                row(A_WIDTH), row(B_WIDTH), row(D_MODEL), row(D_MODEL),
                const(wa), const(wb), const(wo), const(g), const(b)]
    kern = _merge_kernel
    aliases = {}
    if h_prev is not None:
        args.append(h_prev)
        in_specs.append(pl.BlockSpec(memory_space=pl.ANY))
        aliases = {len(args) - 1: 0}
        kern = lambda *refs: _merge_kernel(*refs[:10], refs[11])
    return pl.pallas_call(
        kern,
        grid=(nt,),
        in_specs=in_specs,
        out_specs=pl.BlockSpec((tm * SUBLANES, LANES), lambda i: (i + out_off_blocks, 0)),
        out_shape=jax.ShapeDtypeStruct((out_rows * SUBLANES, LANES), F32),
        input_output_aliases=aliases,
        compiler_params=_cparams(("arbitrary",)),
        name="merge",
    )(*args)


def _mixer_layer(l, xp, xs, xs_off, n_prompt, bsz, t, n_sample, prm):
    tm = ROW_TILE
    p = _proj(xp, prm["w_in"][l], prm["cos_p"], prm["sin_p"], prm["lower"][l], n_prompt, 0, tm)
    qa, lf, va, ga, qb, kb, vb, sga, sgb = p
    oa, st_p = _hgrn_prompt(qa, lf, va, ga, prm["norm_g"][l], bsz, t)
    ob = _swa_prompt(prm["sinks"][l], qb, kb, vb, bsz, t)
    n_all = n_prompt + n_sample
    h_all = _merge(xp, oa, ob, sga, sgb, prm["wa"][l], prm["wb"][l], prm["wo"][l], prm["ln1_g"][l],
                   prm["ln1_b"][l], n_prompt, 0, n_all, 0, tm, h_prev=jnp.zeros((n_all * SUBLANES, LANES), F32))
    kp = kb.reshape(bsz, t, B_KV_HEADS, B_HEAD_DIM)[:, -WINDOW:]
    vp = vb.reshape(bsz, t, B_KV_HEADS, B_HEAD_DIM)[:, -WINDOW:]

    ts = n_sample
    ps = _proj(xs, prm["w_in"][l], prm["cos_s"], prm["sin_s"], prm["lower"][l], ts, xs_off, ts)
    qa_s, lf_s, va_s, ga_s, qb_s, kb_s, vb_s, sga_s, sgb_s = ps
    sb = SAMPLE_BLOCK
    to_t = lambda a: a.reshape(ts // sb, sb, A_HEADS, A_DK).transpose(2, 0, 3, 1)
    qh = qb_s.astype(F32).reshape(ts, B_Q_HEADS, B_HEAD_DIM)
    z = jnp.zeros_like(qh[:, :B_GROUP])
    q3 = jnp.concatenate([jnp.concatenate([qh[:, :B_GROUP], z], axis=-1),
                          jnp.concatenate([z, qh[:, B_GROUP:]], axis=-1)], axis=1)
    st_s, oa_s, ob3, ck_s, cv_s = _sample_step(
        l, prm["sinks8"][l], prm["state"], to_t(lf_s), to_t(qa_s.astype(F32)), va_s, ga_s, prm["norm_g"][l],
        q3, kb_s, vb_s, prm["cache_k"], prm["cache_v"])
    ob_s = jnp.concatenate([ob3[:, :B_GROUP, :B_HEAD_DIM], ob3[:, B_GROUP:, B_HEAD_DIM:]], axis=1)
    ob_s = ob_s.reshape(ts, B_WIDTH).astype(BF16)
    h_all = _merge(xs, oa_s, ob_s, sga_s, sgb_s, prm["wa"][l], prm["wb"][l], prm["wo"][l], prm["ln1_g"][l],
                   prm["ln1_b"][l], ts, xs_off, n_all, n_prompt // ts, ts, h_prev=h_all)
    return h_all, (kp, vp, st_p, ck_s, cv_s, st_s)


def _router_kernel(h_ref, wr_ref, bias_ref, e_ref, w_ref, mask_ref, cnt_ref):
    tm = h_ref.shape[0] // SUBLANES
    gsz = GROUP_SIZE

    @pl.when(pl.program_id(0) == 0)
    def _():
        cnt_ref[...] = jnp.zeros_like(cnt_ref)

    logits = lax.dot_general(wr_ref[...], _load_tiled(h_ref, tm), (((1,), (1,)), ((), ())),
                             precision=lax.Precision.HIGHEST, preferred_element_type=F32)
    scores = _sigmoid(logits)
    sel = scores + bias_ref[...][:, 0:1]
    rowi = lax.broadcasted_iota(I32, (gsz, tm), 0)
    neg_inf = -jnp.inf
    blocks = [sel[g * gsz:(g + 1) * gsz] for g in range(N_GROUPS)]
    sblocks = [scores[g * gsz:(g + 1) * gsz] for g in range(N_GROUPS)]

    gscore = []
    for blk in blocks:
        m1 = jnp.max(blk, axis=0, keepdims=True)
        i1 = jnp.min(jnp.where(blk == m1, rowi, gsz), axis=0, keepdims=True)
        m2 = jnp.max(jnp.where(rowi == i1, neg_inf, blk), axis=0, keepdims=True)
        gscore.append(m1 + m2)
    work = []
    for g in range(N_GROUPS):
        ahead = jnp.zeros((1, tm), I32)
        for g2 in range(N_GROUPS):
            if g2 != g:
                beats = (gscore[g2] > gscore[g]) | ((gscore[g2] == gscore[g]) & (g2 < g))
                ahead = ahead + beats.astype(I32)
        work.append(jnp.where(ahead < TOPK_GROUPS, blocks[g], NEG_BIG))

    chosen = [jnp.zeros((gsz, tm), F32) for _ in range(N_GROUPS)]
    es, ws = [], []
    for _ in range(TOP_K):
        m = work[0]
        for g in range(1, N_GROUPS):
            m = jnp.maximum(m, work[g])
        m = jnp.max(m, axis=0, keepdims=True)
        cand = jnp.where(work[0] == m, rowi, N_EXPERTS)
        for g in range(1, N_GROUPS):
            cand = jnp.minimum(cand, jnp.where(work[g] == m, rowi + g * gsz, N_EXPERTS))
        idx = jnp.min(cand, axis=0, keepdims=True)
        wj = jnp.zeros((1, tm), F32)
        for g in range(N_GROUPS):
            hit = (rowi + g * gsz) == idx
            wj = wj + jnp.sum(jnp.where(hit, sblocks[g], 0.0), axis=0, keepdims=True)
            chosen[g] = jnp.where(hit, 1.0, chosen[g])
            work[g] = jnp.where(hit, neg_inf, work[g])
        es.append(idx)
        ws.append(wj)
    wsum = ws[0]
    for j in range(1, TOP_K):
        wsum = wsum + ws[j]
    for j in range(TOP_K):
        e_ref[j:j + 1, :] = es[j]
        w_ref[j:j + 1, :] = ws[j] / wsum * ROUTED_SCALE
    for g in range(N_GROUPS):
        rows = slice(g * gsz, (g + 1) * gsz)
        mask_ref[rows, :] = chosen[g]
        part = chosen[g][:, 0:LANES]
        for c in range(1, tm // LANES):
            part = part + chosen[g][:, c * LANES:(c + 1) * LANES]
        cnt_ref[rows, :] = cnt_ref[rows, :] + part


def _router(h_all, wr_t, bias_b, tm):
    n = h_all.shape[0] // SUBLANES
    col = lambda r: pl.BlockSpec((r, tm), lambda i: (0, i))
    return pl.pallas_call(
        _router_kernel,
        grid=(n // tm,),
        in_specs=[pl.BlockSpec((tm * SUBLANES, LANES), lambda i: (i, 0)),
                  pl.BlockSpec((N_EXPERTS, D_MODEL), lambda i: (0, 0)),
                  pl.BlockSpec((N_EXPERTS, LANES), lambda i: (0, 0))],
        out_specs=[col(TOP_K), col(TOP_K), col(N_EXPERTS), pl.BlockSpec((N_EXPERTS, LANES), lambda i: (0, 0))],
        out_shape=[jax.ShapeDtypeStruct((TOP_K, n), I32), jax.ShapeDtypeStruct((TOP_K, n), F32),
                   jax.ShapeDtypeStruct((N_EXPERTS, n), F32), jax.ShapeDtypeStruct((N_EXPERTS, LANES), F32)],
        compiler_params=_cparams(("arbitrary",)),
        name="router",
    )(h_all, wr_t, bias_b)


def _rank_kernel(mask_ref, e_ref, offs_ref, triu_ref, dest_ref, carry):
    tm = mask_ref.shape[1]

    @pl.when(pl.program_id(0) == 0)
    def _():
        carry[...] = jnp.zeros_like(carry)

    mk = mask_ref[...]
    rank = _dot(mk.astype(BF16), triu_ref[...])
    dest_full = rank + (offs_ref[...][:, 0:1] + carry[...][:, 0:1])
    rowi = lax.broadcasted_iota(I32, (N_EXPERTS, tm), 0)
    for j in range(TOP_K):
        d = jnp.sum(jnp.where(rowi == e_ref[j:j + 1, :], dest_full, 0.0), axis=0, keepdims=True)
        dest_ref[j:j + 1, :] = d.astype(I32)
    carry[...] = carry[...] + jnp.sum(mk, axis=1, keepdims=True)


def _rank(mask_t, e_t, offs_b, tm):
    n = mask_t.shape[1]
    r = np.arange(tm)
    triu = jnp.asarray((r[:, None] < r[None, :]).astype(np.float32), BF16)
    return pl.pallas_call(
        _rank_kernel,
        grid=(n // tm,),
        in_specs=[pl.BlockSpec((N_EXPERTS, tm), lambda i: (0, i)),
                  pl.BlockSpec((TOP_K, tm), lambda i: (0, i)),
                  pl.BlockSpec((N_EXPERTS, LANES), lambda i: (0, 0)),
                  pl.BlockSpec((tm, tm), lambda i: (0, 0))],
        out_specs=pl.BlockSpec((TOP_K, tm), lambda i: (0, i)),
        out_shape=jax.ShapeDtypeStruct((TOP_K, n), I32),
        scratch_shapes=[pltpu.VMEM((N_EXPERTS, LANES), F32)],
        compiler_params=_cparams(("arbitrary",)),
        name="rank",
    )(mask_t, e_t, offs_b, triu)


INV_COLS = 512
TOKEN_RADIX = 128


def _inverse_kernel(dest_ref, inv_ref):
    tm = dest_ref.shape[1]
    nq = inv_ref.shape[0]

    @pl.when(pl.program_id(0) == 0)
    def _():
        inv_ref[...] = jnp.zeros_like(inv_ref)

    tok = pl.program_id(0) * tm + lax.broadcasted_iota(I32, (1, tm), 1)
    t_hi = jnp.right_shift(tok, TOKEN_RADIX.bit_length() - 1).astype(F32)
    t_lo = jnp.bitwise_and(tok, TOKEN_RADIX - 1).astype(F32)
    qi = lax.broadcasted_iota(I32, (nq, tm), 0)
    si = lax.broadcasted_iota(I32, (INV_COLS, tm), 0)
    acc_hi = jnp.zeros(inv_ref.shape, F32)
    acc_lo = jnp.zeros(inv_ref.shape, F32)
    for j in range(TOP_K):
        d = dest_ref[j:j + 1, :]
        at_q = qi == jnp.right_shift(d, INV_COLS.bit_length() - 1)
        col = jnp.where(si == jnp.bitwise_and(d, INV_COLS - 1), 1.0, 0.0).astype(BF16)
        acc_hi = acc_hi + _dot_nt(jnp.where(at_q, t_hi, 0.0).astype(BF16), col)
        acc_lo = acc_lo + _dot_nt(jnp.where(at_q, t_lo, 0.0).astype(BF16), col)
    inv_ref[...] = inv_ref[...] + (acc_hi * float(TOKEN_RADIX) + acc_lo)


def _inverse_map(dest_t, tm):
    n = dest_t.shape[1]
    n_rows = n * TOP_K
    nq = -(-(n_rows // INV_COLS) // 8) * 8
    inv = pl.pallas_call(
        _inverse_kernel,
        grid=(n // tm,),
        in_specs=[pl.BlockSpec((TOP_K, tm), lambda i: (0, i))],
        out_specs=pl.BlockSpec((nq, INV_COLS), lambda i: (0, 0)),
        out_shape=jax.ShapeDtypeStruct((nq, INV_COLS), F32),
        compiler_params=_cparams(("arbitrary",)),
        name="inverse_map",
    )(dest_t)
    return inv.reshape(-1)[:n_rows].astype(I32).reshape(n_rows // EXPERT_ROW_TILE, 1, EXPERT_ROW_TILE)


DMA_THREADS = 2


def _token_copy(src_ref, token, dst_ref, dst_row, sem):
    return pltpu.make_async_copy(src_ref.at[pl.ds(pl.multiple_of(token * SUBLANES, SUBLANES), SUBLANES)],
                                 dst_ref.at[pl.ds(dst_row * SUBLANES, SUBLANES)], sem)


GATHER_AHEAD = 2
GATHER_SLOTS = GATHER_AHEAD + 1


def _expert_kernel(n_tiles, tile_ref, exp_ref, valid_ref, first_ref, newexp_ref, lo_ref, hi_ref,
                   inv0_ref, inv1_ref, inva_ref, h_ref, wg_ref, wu_ref, wd_ref, y_ref, xbuf, wgb, wub, wdb, sem):
    i = pl.program_id(0)
    rows = xbuf.shape[1] // SUBLANES
    tile = tile_ref[i]
    slot = tile % GATHER_SLOTS
    ahead_slot = (tile + GATHER_AHEAD) % GATHER_SLOTS
    valid = valid_ref[i] == 1
    first = first_ref[i] == 1

    def wait_rows(s):
        pltpu.make_async_copy(h_ref.at[pl.ds(0, rows * SUBLANES)], xbuf.at[s], sem.at[s]).wait()

    def gather_loop(idx_ref, s):
        def issue(k, c):
            for u in range(SUBLANES):
                r = k * SUBLANES + u
                pltpu.make_async_copy(
                    h_ref.at[pl.ds(pl.multiple_of(idx_ref[0, 0, r] * SUBLANES, SUBLANES), SUBLANES)],
                    xbuf.at[s, pl.ds(pl.multiple_of(r * SUBLANES, SUBLANES), SUBLANES)], sem.at[s]).start()
            return c

        lax.fori_loop(0, rows // SUBLANES, issue, 0)

    @pl.when(valid & first)
    def _():
        @pl.when(i == 0)
        def _():
            gather_loop(inv0_ref, 0)
            gather_loop(inv1_ref, 1)

        wait_rows(slot)

    @pl.when(valid & (newexp_ref[i] == 1))
    def _():
        wgb[...] = wg_ref[0].astype(BF16)
        wub[...] = wu_ref[0].astype(BF16)
        wdb[...] = wd_ref[0].astype(BF16)

    def compute(first_visit):
        xb = _load_tiled(xbuf.at[slot], rows).astype(BF16)
        if first_visit:
            for r in range(rows):
                _token_copy(h_ref, inva_ref[0, 0, r], xbuf.at[ahead_slot], r, sem.at[ahead_slot]).start(
                    priority=r % DMA_THREADS)
        gate = _dot(xb, wgb[...])
        up = _dot(xb, wub[...])
        y = _dot((_silu(gate) * up).astype(BF16), wdb[...])
        rowi = lax.broadcasted_iota(I32, y.shape, 0)
        mine = (rowi >= lo_ref[i]) & (rowi < hi_ref[i])
        _store_tiled(y_ref, jnp.where(mine, y, 0.0 if first_visit else _load_tiled(y_ref, rows)))

    @pl.when(valid & first)
    def _():
        compute(True)

    @pl.when(valid & jnp.logical_not(first))
    def _():
        compute(False)

    @pl.when(valid & first & (tile >= n_tiles - GATHER_AHEAD))
    def _():
        wait_rows(ahead_slot)


def _group_metadata(counts, n_rows):
    tmo = EXPERT_ROW_TILE
    n_tiles = n_rows // tmo
    ends = jnp.cumsum(counts)
    offs = ends - counts
    first_tile = offs // tmo
    n_t = jnp.where(counts > 0, (ends - 1) // tmo - first_tile + 1, 0)
    cum = jnp.cumsum(n_t)
    base = cum - n_t
    n_items = n_tiles + N_EXPERTS
    idx = jnp.arange(n_items, dtype=I32)
    valid = (idx < cum[-1]).astype(I32)
    idc = jnp.minimum(idx, cum[-1] - 1)
    e = jnp.minimum(jnp.sum((cum[None, :] <= idc[:, None]).astype(I32), axis=1), N_EXPERTS - 1)
    tile = (first_tile[e] + idc - base[e]).astype(I32)
    one = jnp.ones((1,), I32)
    first = jnp.concatenate([one, (tile[1:] != tile[:-1]).astype(I32)])
    new_expert = jnp.concatenate([one, (e[1:] != e[:-1]).astype(I32)])
    lo = jnp.clip(offs[e] - tile * tmo, 0, tmo).astype(I32)
    hi = jnp.clip(ends[e] - tile * tmo, 0, tmo).astype(I32)
    return tile, e, valid, first, new_expert, lo, hi


def _experts(l, meta, inv3, h_all, wg, wu, wd):
    tmo = EXPERT_ROW_TILE
    n_rows = inv3.shape[0] * tmo
    n_items = meta[0].shape[0]
    wspec = lambda shp: pl.BlockSpec((None, 1) + shp, lambda i, tl, ex, *_: (l, ex[i], 0, 0))
    n_tiles = inv3.shape[0]
    assert n_tiles > GATHER_AHEAD
    idx_spec = lambda nxt: pl.BlockSpec(
        (1, 1, tmo), lambda i, tl, *_: (jnp.minimum(tl[i] + nxt, n_tiles - 1), 0, 0), memory_space=pltpu.SMEM)
    return pl.pallas_call(
        functools.partial(_expert_kernel, n_tiles),
        grid_spec=pltpu.PrefetchScalarGridSpec(
            num_scalar_prefetch=7,
            grid=(n_items,),
            in_specs=[idx_spec(0), idx_spec(1), idx_spec(GATHER_AHEAD), pl.BlockSpec(memory_space=pl.ANY),
                      wspec((D_MODEL, D_EXPERT)), wspec((D_MODEL, D_EXPERT)), wspec((D_EXPERT, D_MODEL))],
            out_specs=pl.BlockSpec((tmo * SUBLANES, LANES), lambda i, tl, *_: (tl[i], 0)),
            scratch_shapes=[pltpu.VMEM((GATHER_SLOTS, tmo * SUBLANES, LANES), F32),
                            pltpu.VMEM((D_MODEL, D_EXPERT), BF16), pltpu.VMEM((D_MODEL, D_EXPERT), BF16),
                            pltpu.VMEM((D_EXPERT, D_MODEL), BF16),
                            pltpu.SemaphoreType.DMA((GATHER_SLOTS,))]),
        out_shape=jax.ShapeDtypeStruct((n_rows * SUBLANES, LANES), F32),
        compiler_params=_cparams(("arbitrary",)),
        name="experts",
    )(*meta, inv3, inv3, inv3, h_all, wg, wu, wd)


def _post_kernel(n_steps, dest0_ref, dest1_ref, desta_ref, h_ref, wt_ref, y_ref, wsg_ref, wsu_ref, wsd_ref,
                 g_ref, b_ref, out_ref, buf, sem):
    i = pl.program_id(0)
    tm = out_ref.shape[0]
    slot = i % GATHER_SLOTS
    ahead_slot = (i + GATHER_AHEAD) % GATHER_SLOTS

    def wait_rows(s):
        for j in range(TOP_K):
            pltpu.make_async_copy(y_ref.at[pl.ds(0, tm * SUBLANES)], buf.at[s, j], sem.at[s]).wait()

    def gather_loop(idx_ref, s):
        def issue(k, c):
            for u in range(SUBLANES):
                t = k * SUBLANES + u
                for j in range(TOP_K):
                    pltpu.make_async_copy(
                        y_ref.at[pl.ds(pl.multiple_of(idx_ref[j, t] * SUBLANES, SUBLANES), SUBLANES)],
                        buf.at[s, j, pl.ds(pl.multiple_of(t * SUBLANES, SUBLANES), SUBLANES)], sem.at[s]).start()
            return c

        lax.fori_loop(0, tm // SUBLANES, issue, 0)

    @pl.when(i == 0)
    def _():
        gather_loop(dest0_ref, 0)
        gather_loop(dest1_ref, 1)

    wait_rows(slot)

    h = _load_tiled(h_ref, tm)
    hb = h.astype(BF16)
    for t in range(tm):
        for j in range(TOP_K):
            _token_copy(y_ref, desta_ref[j, t], buf.at[ahead_slot, j], t, sem.at[ahead_slot]).start(
                priority=j % DMA_THREADS)
    shared = _dot((_silu(_dot(hb, wsg_ref[...])) * _dot(hb, wsu_ref[...])).astype(BF16), wsd_ref[...])
    wt = wt_ref[...]
    routed = _load_tiled(buf.at[slot, 0], tm) * wt[:, 0:1]
    for j in range(1, TOP_K):
        routed = routed + _load_tiled(buf.at[slot, j], tm) * wt[:, j:j + 1]
    out_ref[...] = _layer_norm(DN_ALPHA * h + (routed + shared), g_ref[...], b_ref[...])

    @pl.when(i >= n_steps - GATHER_AHEAD)
    def _():
        wait_rows(ahead_slot)


def _post(dest_t, h_all, w_tok, y, wsg, wsu, wsd, g, b, tm):
    n = h_all.shape[0] // SUBLANES
    n_steps = n // tm
    assert n_steps > GATHER_AHEAD
    const = lambda a: pl.BlockSpec(a.shape, lambda i: (0, 0))
    idx_spec = lambda nxt: pl.BlockSpec((TOP_K, tm), lambda i: (0, jnp.minimum(i + nxt, n_steps - 1)),
                                        memory_space=pltpu.SMEM)
    return pl.pallas_call(
        functools.partial(_post_kernel, n_steps),
        grid=(n_steps,),
        in_specs=[idx_spec(0), idx_spec(1), idx_spec(GATHER_AHEAD),
                  pl.BlockSpec((tm * SUBLANES, LANES), lambda i: (i, 0)),
                  pl.BlockSpec((tm, TOP_K), lambda i: (i, 0)),
                  pl.BlockSpec(memory_space=pl.ANY),
                  const(wsg), const(wsu), const(wsd), const(g), const(b)],
        out_specs=pl.BlockSpec((tm, D_MODEL), lambda i: (i, 0)),
        out_shape=jax.ShapeDtypeStruct((n, D_MODEL), F32),
        scratch_shapes=[pltpu.VMEM((GATHER_SLOTS, TOP_K, tm * SUBLANES, LANES), F32),
                        pltpu.SemaphoreType.DMA((GATHER_SLOTS,))],
        compiler_params=_cparams(("arbitrary",)),
        name="moe_post",
    )(dest_t, dest_t, dest_t, h_all, w_tok, y, wsg, wsu, wsd, g, b)


def _moe_layer(l, h_all, prm):
    tm = MOE_ROW_TILE
    e_t, w_t, mask_t, cnt = _router(h_all, prm["wr_t"][l], prm["rbias"][l], tm)
    counts = jnp.sum(cnt, axis=1).astype(I32)
    offs = jnp.cumsum(counts) - counts
    offs_b = jnp.broadcast_to(offs.astype(F32)[:, None], (N_EXPERTS, LANES))
    dest_t = _rank(mask_t, e_t, offs_b, tm)
    inv3 = _inverse_map(dest_t, tm)
    meta = _group_metadata(counts, h_all.shape[0] // SUBLANES * TOP_K)
    y = _experts(l, meta, inv3, h_all, prm["w_exp_gate"], prm["w_exp_up"], prm["w_exp_down"])
    return _post(dest_t, h_all, w_t.T, y, prm["wsg"][l], prm["wsu"][l], prm["wsd"][l],
                 prm["ln2_g"][l], prm["ln2_b"][l], POST_ROW_TILE)


def kernel(x_prompt, x_sample, cache_k, cache_v, state_hgrn, w_in, hgrn_lower_bounds, hgrn_norm_g, attn_sinks, w_branch_a, w_branch_b, w_out, ln1_g, ln1_b, w_router, router_bias, w_exp_gate, w_exp_up, w_exp_down, w_sh_gate, w_sh_up, w_sh_down, ln2_g, ln2_b):
    bsz, t, d = x_prompt.shape
    n_sample = x_sample.shape[0] * x_sample.shape[1]
    n_prompt = bsz * t
    depth = w_in.shape[0]
    assert d == D_MODEL and x_sample.shape[1] == 1 and n_prompt % ROW_TILE == 0 and t % ROW_TILE == 0
    assert n_prompt % n_sample == 0 and (n_prompt + n_sample) % MOE_ROW_TILE == 0
    assert (n_prompt + n_sample) % POST_ROW_TILE == 0
    assert ((n_prompt + n_sample) * TOP_K) % EXPERT_ROW_TILE == 0 and n_sample % SAMPLE_BLOCK == 0

    lb_prob = jax.nn.softmax(hgrn_lower_bounds.astype(F32), axis=0)
    lower = (jnp.cumsum(lb_prob, axis=0) - lb_prob[0])[:, None, :]
    cos_p, sin_p = _rope_tables(jnp.arange(t))
    cos_s, sin_s = _rope_tables(jnp.full((n_sample,), PAST_LEN))
    row = lambda a: a[:, None, :]
    prm = dict(
        w_in=w_in.astype(BF16), lower=lower, cos_p=cos_p, sin_p=sin_p, cos_s=cos_s, sin_s=sin_s,
        norm_g=row(hgrn_norm_g), sinks=attn_sinks,
        sinks8=jnp.broadcast_to(attn_sinks[:, :, None], (depth, B_Q_HEADS, LANES)),
        wa=w_branch_a.astype(BF16), wb=w_branch_b.astype(BF16), wo=w_out.astype(BF16),
        ln1_g=row(ln1_g), ln1_b=row(ln1_b), ln2_g=row(ln2_g), ln2_b=row(ln2_b),
        wr_t=jnp.swapaxes(w_router, 1, 2),
        rbias=jnp.broadcast_to(router_bias[:, :, None], (depth, N_EXPERTS, LANES)),
        w_exp_gate=w_exp_gate, w_exp_up=w_exp_up, w_exp_down=w_exp_down,
        wsg=w_sh_gate.astype(BF16), wsu=w_sh_up.astype(BF16), wsd=w_sh_down.astype(BF16),
        state=state_hgrn,
        cache_k=cache_k.reshape(depth, n_sample, WINDOW, B_KV_WIDTH),
        cache_v=cache_v.reshape(depth, n_sample, WINDOW, B_KV_WIDTH),
    )

    xp, xs, xs_off = x_prompt.reshape(n_prompt, d), x_sample.reshape(n_sample, d), 0
    per_layer = []
    for l in range(depth):
        h_all, outs = _mixer_layer(l, xp, xs, xs_off, n_prompt, bsz, t, n_sample, prm)
        y_all = _moe_layer(l, h_all, prm)
        per_layer.append(outs)
        xp, xs, xs_off = y_all, y_all, n_prompt // n_sample

    kv_shape = (n_sample, WINDOW, B_KV_HEADS, B_HEAD_DIM)
    stack = lambda k, f=lambda a: a: jnp.stack([f(o[k]) for o in per_layer])
    return (y_all[:n_prompt].reshape(bsz, t, d), y_all[n_prompt:].reshape(n_sample, 1, d),
            stack(0), stack(1), stack(2),
            stack(3, lambda a: a.reshape(kv_shape)), stack(4, lambda a: a.reshape(kv_shape)), stack(5))
```

```python
import functools

import numpy as np
import jax
import jax.numpy as jnp
from jax import lax
from jax.experimental import pallas as pl
from jax.experimental.pallas import tpu as pltpu

F32 = jnp.float32
BF16 = jnp.bfloat16
I32 = jnp.int32

D_MODEL = 1024
DEPTH = 2
PAST_LEN = 16384
A_HEADS = 4
A_DK = 128
A_DV = 128
A_KEY = A_HEADS * A_DK
A_WIDTH = A_HEADS * A_DV
B_Q_HEADS = 8
B_KV_HEADS = 2
B_HEAD_DIM = 64
B_GROUP = B_Q_HEADS // B_KV_HEADS
B_WIDTH = B_Q_HEADS * B_HEAD_DIM
B_KV_WIDTH = B_KV_HEADS * B_HEAD_DIM
WINDOW = 128
ROPE_THETA = 10000.0
ATTN_SCALE = B_HEAD_DIM ** -0.5
N_EXPERTS = 64
TOP_K = 8
N_GROUPS = 8
GROUP_SIZE = N_EXPERTS // N_GROUPS
TOPK_GROUPS = 4
D_EXPERT = D_MODEL // 4
D_SHARED = D_EXPERT
ROUTED_SCALE = 2.5
DN_ALPHA = (2 * DEPTH) ** 0.25
LN_EPS = 1e-5
RMS_EPS = 1e-6
NEG_BIG = -1e30
TINY = 1.1754944e-38
OFF_AF = A_KEY
OFF_AI = 2 * A_KEY
OFF_AG = OFF_AI + A_WIDTH
OFF_BQ = OFF_AG + A_WIDTH
OFF_BK = OFF_BQ + B_WIDTH
OFF_BV = OFF_BK + B_KV_WIDTH
OFF_GA = OFF_BV + B_KV_WIDTH
OFF_GB = OFF_GA + D_MODEL
IN_COLS = OFF_GB + D_MODEL

LANES = 128
HGRN_CHUNK = 128
HGRN_LEVELS = 7
ROW_TILE = 512
MOE_ROW_TILE = 384
EXPERT_ROW_TILE = 512
POST_ROW_TILE = 128
VMEM_LIMIT = 56 * 1024 * 1024


def _cparams(sem, vmem=VMEM_LIMIT):
    return pltpu.CompilerParams(dimension_semantics=sem, vmem_limit_bytes=vmem)


def _dot(a, b):
    return jnp.dot(a, b, preferred_element_type=F32)


def _dot_nt(a, b):
    return lax.dot_general(a, b, (((1,), (1,)), ((), ())), preferred_element_type=F32)


def _dot_tn(a, b):
    return lax.dot_general(a, b, (((0,), (0,)), ((), ())), preferred_element_type=F32)


def _sigmoid(x):
    return 1.0 / (1.0 + jnp.exp(-x))


def _silu(x):
    return x * _sigmoid(x)


def _split3(x):
    hi = x.astype(BF16)
    r1 = x - hi.astype(F32)
    mid = r1.astype(BF16)
    lo = (r1 - mid.astype(F32)).astype(BF16)
    return hi, mid, lo


def _layer_norm(y, g, b):
    mu = jnp.mean(y, axis=-1, keepdims=True)
    d = y - mu
    var = jnp.mean(d * d, axis=-1, keepdims=True)
    return d * lax.rsqrt(var + LN_EPS) * g + b


SUBLANES = 8
assert D_MODEL == SUBLANES * LANES


def _store_tiled(ref, val):
    rows = val.shape[0]
    for c in range(SUBLANES):
        ref[pl.ds(c, rows, stride=SUBLANES), :] = val[:, c * LANES:(c + 1) * LANES]


def _load_tiled(ref, rows):
    return jnp.concatenate([ref[pl.ds(c, rows, stride=SUBLANES), :] for c in range(SUBLANES)], axis=1)


def _log_forget(af, lower):
    ls = jnp.minimum(af, 0.0) - jnp.log1p(jnp.exp(-jnp.abs(af)))
    a = jnp.log(jnp.maximum(lower, TINY))
    b = jnp.log1p(-lower) + ls
    mixed = jnp.maximum(a, b) + jnp.log1p(jnp.exp(-jnp.abs(a - b)))
    return jnp.where(lower > 0.0, mixed, ls)


def _proj_kernel(x_ref, w_ref, cos_ref, sin_ref, low_ref,
                 qa_ref, lf_ref, va_ref, ga_ref, qb_ref, kb_ref, vb_ref, sga_ref, sgb_ref):
    xb = x_ref[...].astype(BF16)

    def mm(c0, n):
        return _dot(xb, w_ref[:, c0:c0 + n])

    qa_ref[...] = _silu(mm(0, A_KEY)).astype(BF16)
    lf_ref[...] = _log_forget(mm(OFF_AF, A_KEY), low_ref[...])
    va_ref[...] = mm(OFF_AI, A_WIDTH).astype(BF16)
    ga_ref[...] = _silu(mm(OFF_AG, A_WIDTH)).astype(BF16)

    cos = cos_ref[...]
    sin = sin_ref[...]
    lane = lax.broadcasted_iota(I32, cos.shape, 1)
    first_half = (lane & (B_HEAD_DIM // 2)) == 0

    def rope(blk):
        partner = jnp.where(first_half, pltpu.roll(blk, LANES - B_HEAD_DIM // 2, 1),
                            pltpu.roll(blk, B_HEAD_DIM // 2, 1))
        return blk * cos + partner * sin

    bq = mm(OFF_BQ, B_WIDTH)
    for j in range(B_WIDTH // LANES):
        sl = slice(j * LANES, (j + 1) * LANES)
        qb_ref[:, sl] = (rope(bq[:, sl]) * ATTN_SCALE).astype(BF16)
    kb_ref[...] = rope(mm(OFF_BK, B_KV_WIDTH))
    vb_ref[...] = mm(OFF_BV, B_KV_WIDTH)
    sga_ref[...] = _sigmoid(mm(OFF_GA, D_MODEL)).astype(BF16)
    sgb_ref[...] = _sigmoid(mm(OFF_GB, D_MODEL)).astype(BF16)


def _proj(x, w_bf, cos_t, sin_t, lower, n_rows, row_off_blocks, tm):
    nt = n_rows // tm
    tab_blocks = cos_t.shape[0] // tm
    row = lambda w: pl.BlockSpec((tm, w), lambda i: (i, 0))
    outs = [(A_KEY, BF16), (A_KEY, F32), (A_WIDTH, BF16), (A_WIDTH, BF16), (B_WIDTH, BF16),
            (B_KV_WIDTH, F32), (B_KV_WIDTH, F32), (D_MODEL, BF16), (D_MODEL, BF16)]
    return pl.pallas_call(
        _proj_kernel,
        grid=(nt,),
        in_specs=[pl.BlockSpec((tm, D_MODEL), lambda i: (i + row_off_blocks, 0)),
                  pl.BlockSpec((D_MODEL, IN_COLS), lambda i: (0, 0)),
                  pl.BlockSpec((tm, LANES), lambda i: (i % tab_blocks, 0)),
                  pl.BlockSpec((tm, LANES), lambda i: (i % tab_blocks, 0)),
                  pl.BlockSpec((1, A_KEY), lambda i: (0, 0))],
        out_specs=[row(w) for w, _ in outs],
        out_shape=[jax.ShapeDtypeStruct((n_rows, w), dt) for w, dt in outs],
        compiler_params=_cparams(("arbitrary",)),
        name="proj",
    )(x, w_bf, cos_t, sin_t, lower)


def _rope_tables(pos):
    half = B_HEAD_DIM // 2
    inv = ROPE_THETA ** (-jnp.arange(half, dtype=F32) / half)
    ang = pos.astype(F32)[:, None] * inv[None, :]
    cos = jnp.cos(ang)
    sin = jnp.sin(ang)
    reps = LANES // B_HEAD_DIM
    cos_t = jnp.tile(jnp.concatenate([cos, cos], axis=1), (1, reps))
    sin_t = jnp.tile(jnp.concatenate([-sin, sin], axis=1), (1, reps))
    return cos_t, sin_t


def _hgrn_constants():
    c = HGRN_CHUNK
    r = np.arange(c)
    tri = (r[None, :] <= r[:, None]).astype(np.float32)
    sel = np.zeros((HGRN_LEVELS, c, c), np.float32)
    upper = np.zeros((HGRN_LEVELS, c, A_KEY), np.float32)
    pair = np.zeros((HGRN_LEVELS + 1, c, c), np.float32)
    for l in range(HGRN_LEVELS):
        b = c >> (l + 1)
        ref_row = (r // (2 * b)) * (2 * b) + b - 1
        sel[l, r, ref_row] = 1.0
        up = (r % (2 * b)) >= b
        upper[l] = up[:, None]
        same = (r[:, None] // (2 * b)) == (r[None, :] // (2 * b))
        pair[l] = (up[:, None] & ~up[None, :] & same)
    pair[HGRN_LEVELS] = np.eye(c)
    return (jnp.asarray(tri, BF16), jnp.asarray(sel.reshape(HGRN_LEVELS * c, c), BF16),
            jnp.asarray(upper), jnp.asarray(pair))


def _hgrn_kernel(qa_ref, lf_ref, va_ref, ga_ref, g_ref, tri_ref, sel_ref, up_ref, pair_ref,
                 oa_ref, st_ref, s_scr):
    c = HGRN_CHUNK
    step = pl.program_id(1)

    @pl.when(step == 0)
    def _():
        s_scr[...] = jnp.zeros_like(s_scr)

    lf = lf_ref[...]
    tri = tri_ref[...]
    sel = sel_ref[...]
    hi, mid, lo = _split3(lf)
    gcum = _dot(tri, hi) + _dot(tri, mid) + _dot(tri, lo)
    ghi, gmid, glo = _split3(gcum)
    gref = _dot(sel, ghi) + _dot(sel, gmid) + _dot(sel, glo)
    qb = qa_ref[...]
    qf = qb.astype(F32)
    kf = 1.0 - jnp.exp(lf)
    kb = kf.astype(BF16)
    vb = va_ref[...]
    ws = []
    for l in range(HGRN_LEVELS):
        e = jnp.exp(-jnp.abs(gcum - gref[l * c:(l + 1) * c]))
        ws.append((jnp.where(up_ref[l] > 0.5, qf, kf) * e).astype(BF16))
    qg = (qf * jnp.exp(gcum)).astype(BF16)
    gend = gcum[c - 1:c, :]
    kend = (kf * jnp.exp(gend - gcum)).astype(BF16)
    decay = jnp.exp(gend)
    gate = g_ref[...]

    for h in range(A_HEADS):
        sl = slice(h * A_DK, (h + 1) * A_DK)
        att = _dot_nt(qb[:, sl], kb[:, sl]) * pair_ref[HGRN_LEVELS]
        for l in range(HGRN_LEVELS):
            w = ws[l][:, sl]
            att = att + _dot_nt(w, w) * pair_ref[l]
        s_t = s_scr[h]
        o = _dot_nt(qg[:, sl], s_t.astype(BF16)) + _dot(att.astype(BF16), vb[:, sl])
        ms = jnp.mean(o * o, axis=-1, keepdims=True)
        on = o * lax.rsqrt(ms + RMS_EPS) * gate * ga_ref[:, sl].astype(F32)
        oa_ref[:, sl] = on.astype(BF16)
        s_scr[h] = s_t * decay[:, sl] + _dot_tn(vb[:, sl], kend[:, sl])

    @pl.when(step == pl.num_programs(1) - 1)
    def _():
        for h in range(A_HEADS):
            st_ref[0, h] = s_scr[h].T


def _hgrn_prompt(qa, lf, va, ga, norm_g, bsz, t):
    c = HGRN_CHUNK
    nc = t // c
    tri, sel, upper, pair = _hgrn_constants()
    blk = lambda: pl.BlockSpec((c, A_KEY), lambda b, i: (b * nc + i, 0))
    const = lambda a: pl.BlockSpec(a.shape, lambda b, i: (0,) * a.ndim)
    return pl.pallas_call(
        _hgrn_kernel,
        grid=(bsz, nc),
        in_specs=[blk(), blk(), blk(), blk(), pl.BlockSpec((1, A_DV), lambda b, i: (0, 0)),
                  const(tri), const(sel), const(upper), const(pair)],
        out_specs=[blk(), pl.BlockSpec((1, A_HEADS, A_DK, A_DV), lambda b, i: (b, 0, 0, 0))],
        out_shape=[jax.ShapeDtypeStruct((bsz * t, A_WIDTH), BF16),
                   jax.ShapeDtypeStruct((bsz, A_HEADS, A_DK, A_DV), F32)],
        scratch_shapes=[pltpu.VMEM((A_HEADS, A_DV, A_DK), F32)],
        compiler_params=_cparams(("arbitrary", "arbitrary")),
        name="hgrn_prompt",
    )(qa, lf, va, ga, norm_g, tri, sel, upper, pair)


def _swa_kernel(sink_ref, q_ref, k_ref, v_ref, o_ref, kprev, vprev):
    w = WINDOW
    i = pl.program_id(1)

    @pl.when(i == 0)
    def _():
        kprev[...] = jnp.zeros_like(kprev)
        vprev[...] = jnp.zeros_like(vprev)

    kc = k_ref[...]
    vc = v_ref[...]
    kk = jnp.concatenate([kprev[...], kc], axis=0)
    vv = jnp.concatenate([vprev[...], vc], axis=0)
    kr = pltpu.roll(kk, B_HEAD_DIM, 1)
    vr = pltpu.roll(vv, B_HEAD_DIM, 1)
    lo2 = lax.broadcasted_iota(I32, kk.shape, 1) < B_HEAD_DIM
    zero = jnp.zeros_like(kk)
    k_lo = [jnp.where(lo2, kk, zero).astype(BF16), jnp.where(lo2, kr, zero).astype(BF16)]
    k_hi = [jnp.where(lo2, zero, kr).astype(BF16), jnp.where(lo2, zero, kk).astype(BF16)]
    v_dup = [jnp.where(lo2, vv, vr).astype(BF16), jnp.where(lo2, vr, vv).astype(BF16)]

    qi = lax.broadcasted_iota(I32, (w, 2 * w), 0)
    kj = lax.broadcasted_iota(I32, (w, 2 * w), 1)
    valid = (kj >= qi) & (kj <= qi + w) & ((kj >= w) | (i > 0))
    lo1 = lax.broadcasted_iota(I32, (w, LANES), 1) < B_HEAD_DIM

    for j in range(B_WIDTH // LANES):
        g = (2 * j) // B_GROUP
        qblk = q_ref[:, j * LANES:(j + 1) * LANES]
        res = []
        for half, kmat in enumerate((k_lo[g], k_hi[g])):
            sk = sink_ref[2 * j + half]
            s = jnp.where(valid, _dot_nt(qblk, kmat), NEG_BIG)
            m = jnp.maximum(jnp.max(s, axis=-1, keepdims=True), sk)
            p = jnp.exp(s - m)
            denom = jnp.sum(p, axis=-1, keepdims=True) + jnp.exp(sk - m)
            res.append(_dot(p.astype(BF16), v_dup[g]) / denom)
        o_ref[:, j * LANES:(j + 1) * LANES] = jnp.where(lo1, res[0], res[1]).astype(BF16)

    kprev[...] = kc
    vprev[...] = vc


def _swa_prompt(sinks, qb, kb, vb, bsz, t):
    w = WINDOW
    nb = t // w
    return pl.pallas_call(
        _swa_kernel,
        grid_spec=pltpu.PrefetchScalarGridSpec(
            num_scalar_prefetch=1,
            grid=(bsz, nb),
            in_specs=[pl.BlockSpec((w, B_WIDTH), lambda b, i, s: (b * nb + i, 0)),
                      pl.BlockSpec((w, B_KV_WIDTH), lambda b, i, s: (b * nb + i, 0)),
                      pl.BlockSpec((w, B_KV_WIDTH), lambda b, i, s: (b * nb + i, 0))],
            out_specs=pl.BlockSpec((w, B_WIDTH), lambda b, i, s: (b * nb + i, 0)),
            scratch_shapes=[pltpu.VMEM((w, B_KV_WIDTH), F32), pltpu.VMEM((w, B_KV_WIDTH), F32)]),
        out_shape=jax.ShapeDtypeStruct((bsz * t, B_WIDTH), BF16),
        compiler_params=_cparams(("arbitrary", "arbitrary")),
        name="swa_prompt",
    )(sinks, qb, kb, vb)


SAMPLE_BLOCK = 8


def _sample_kernel(sink_ref, st_ref, lft_ref, qat_ref, va_ref, ga_ref, g_ref, q3_ref, kn_ref, vn_ref,
                   ck_ref, cv_ref, st_out, oa_ref, ob_ref, ck_out, cv_out, o_scr):
    w = WINDOW
    row = lax.broadcasted_iota(I32, (w, B_KV_WIDTH), 0)
    for i in range(SAMPLE_BLOCK):
        for h in range(A_HEADS):
            sl = slice(h * A_DV, (h + 1) * A_DV)
            fcol = jnp.exp(lft_ref[h, 0][:, i:i + 1])
            qcol = qat_ref[h, 0][:, i:i + 1]
            vrow = va_ref[i:i + 1, sl].astype(F32)
            s_new = st_ref[i, h] * fcol + (1.0 - fcol) * vrow
            st_out[i, h] = s_new
            o_scr[i:i + 1, sl] = jnp.sum(s_new * qcol, axis=0, keepdims=True)
        kc = ck_ref[i]
        vc = cv_ref[i]
        kn = kn_ref[i:i + 1, :]
        vn = vn_ref[i:i + 1, :]
        q3 = q3_ref[i]
        s = _dot_nt(q3.astype(BF16), kc.astype(BF16))
        s_new_key = jnp.sum(q3 * kn, axis=-1, keepdims=True)
        sk = sink_ref[...][:, 0:1]
        m = jnp.maximum(jnp.maximum(jnp.max(s, axis=-1, keepdims=True), s_new_key), sk)
        p = jnp.exp(s - m)
        pn = jnp.exp(s_new_key - m)
        denom = jnp.sum(p, axis=-1, keepdims=True) + pn + jnp.exp(sk - m)
        ob_ref[i] = (_dot(p.astype(BF16), vc.astype(BF16)) + pn * vn) / denom
        ck_out[i] = jnp.where(row == w - 1, kn, pltpu.roll(kc, w - 1, 0))
        cv_out[i] = jnp.where(row == w - 1, vn, pltpu.roll(vc, w - 1, 0))
    for h in range(A_HEADS):
        sl = slice(h * A_DV, (h + 1) * A_DV)
        o = o_scr[:, sl]
        ms = jnp.mean(o * o, axis=-1, keepdims=True)
        oa_ref[:, sl] = (o * lax.rsqrt(ms + RMS_EPS) * g_ref[...] * ga_ref[:, sl].astype(F32)).astype(BF16)


def _sample_step(l, sinks8, state, lft, qat, va, ga, norm_g, q3, kn, vn, ck, cv):
    nb = state.shape[1]
    sb = SAMPLE_BLOCK
    steps = nb // sb
    w = WINDOW
    b4 = lambda: pl.BlockSpec((sb, A_HEADS, A_DK, A_DV), lambda i: (i, 0, 0, 0))
    b4_in = pl.BlockSpec((None, sb, A_HEADS, A_DK, A_DV), lambda i: (l, i, 0, 0, 0))
    c3_in = lambda: pl.BlockSpec((None, sb, w, B_KV_WIDTH), lambda i: (l, i, 0, 0))
    t4 = lambda: pl.BlockSpec((A_HEADS, 1, A_DK, sb), lambda i: (0, i, 0, 0))
    r2 = lambda wd: pl.BlockSpec((sb, wd), lambda i: (i, 0))
    c3 = lambda: pl.BlockSpec((sb, w, B_KV_WIDTH), lambda i: (i, 0, 0))
    return pl.pallas_call(
        _sample_kernel,
        grid=(steps,),
        in_specs=[pl.BlockSpec((B_Q_HEADS, LANES), lambda i: (0, 0)),
                  b4_in, t4(), t4(), r2(A_WIDTH), r2(A_WIDTH), pl.BlockSpec((1, A_DV), lambda i: (0, 0)),
                  pl.BlockSpec((sb, B_Q_HEADS, LANES), lambda i: (i, 0, 0)), r2(B_KV_WIDTH), r2(B_KV_WIDTH),
                  c3_in(), c3_in()],
        out_specs=[b4(), r2(A_WIDTH), pl.BlockSpec((sb, B_Q_HEADS, LANES), lambda i: (i, 0, 0)), c3(), c3()],
        out_shape=[jax.ShapeDtypeStruct(state.shape[1:], F32),
                   jax.ShapeDtypeStruct((nb, A_WIDTH), BF16),
                   jax.ShapeDtypeStruct((nb, B_Q_HEADS, LANES), F32),
                   jax.ShapeDtypeStruct((nb, w, B_KV_WIDTH), F32),
                   jax.ShapeDtypeStruct((nb, w, B_KV_WIDTH), F32)],
        scratch_shapes=[pltpu.VMEM((sb, A_WIDTH), F32)],
        compiler_params=_cparams(("arbitrary",)),
        name="sample_step",
    )(sinks8, state, lft, qat, va, ga, norm_g, q3, kn, vn, ck, cv)


def _merge_kernel(x_ref, oa_ref, ob_ref, sga_ref, sgb_ref, wa_ref, wb_ref, wo_ref, g_ref, b_ref, h_ref):
    merged = (sga_ref[...].astype(F32) * _dot(oa_ref[...], wa_ref[...])
              + sgb_ref[...].astype(F32) * _dot(ob_ref[...], wb_ref[...]))
    mix = _dot(merged.astype(BF16), wo_ref[...])
    _store_tiled(h_ref, _layer_norm(DN_ALPHA * x_ref[...] + mix, g_ref[...], b_ref[...]))


def _merge(x, oa, ob, sga, sgb, wa, wb, wo, g, b, n_rows, x_off_blocks, out_rows, out_off_blocks, tm, h_prev=None):
    nt = n_rows // tm
    row = lambda wd: pl.BlockSpec((tm, wd), lambda i: (i, 0))
    const = lambda a: pl.BlockSpec(a.shape, lambda i: (0, 0))
    args = [x, oa, ob, sga, sgb, wa, wb, wo, g, b]
    in_specs = [pl.BlockSpec((tm, D_MODEL), lambda i: (i + x_off_blocks, 0)),
                row(A_WIDTH), row(B_WIDTH), row(D_MODEL), row(D_MODEL),
                const(wa), const(wb), const(wo), const(g), const(b)]
    kern = _merge_kernel
    aliases = {}
    if h_prev is not None:
        args.append(h_prev)
        in_specs.append(pl.BlockSpec(memory_space=pl.ANY))
        aliases = {len(args) - 1: 0}
        kern = lambda *refs: _merge_kernel(*refs[:10], refs[11])
    return pl.pallas_call(
        kern,
        grid=(nt,),
        in_specs=in_specs,
        out_specs=pl.BlockSpec((tm * SUBLANES, LANES), lambda i: (i + out_off_blocks, 0)),
        out_shape=jax.ShapeDtypeStruct((out_rows * SUBLANES, LANES), F32),
        input_output_aliases=aliases,
        compiler_params=_cparams(("arbitrary",)),
        name="merge",
    )(*args)


def _mixer_layer(l, xp, xs, xs_off, n_prompt, bsz, t, n_sample, prm):
    tm = ROW_TILE
    p = _proj(xp, prm["w_in"][l], prm["cos_p"], prm["sin_p"], prm["lower"][l], n_prompt, 0, tm)
    qa, lf, va, ga, qb, kb, vb, sga, sgb = p
    oa, st_p = _hgrn_prompt(qa, lf, va, ga, prm["norm_g"][l], bsz, t)
    ob = _swa_prompt(prm["sinks"][l], qb, kb, vb, bsz, t)
    n_all = n_prompt + n_sample
    h_all = _merge(xp, oa, ob, sga, sgb, prm["wa"][l], prm["wb"][l], prm["wo"][l], prm["ln1_g"][l],
                   prm["ln1_b"][l], n_prompt, 0, n_all, 0, tm, h_prev=jnp.zeros((n_all * SUBLANES, LANES), F32))
    kp = kb.reshape(bsz, t, B_KV_HEADS, B_HEAD_DIM)[:, -WINDOW:]
    vp = vb.reshape(bsz, t, B_KV_HEADS, B_HEAD_DIM)[:, -WINDOW:]

    ts = n_sample
    ps = _proj(xs, prm["w_in"][l], prm["cos_s"], prm["sin_s"], prm["lower"][l], ts, xs_off, ts)
    qa_s, lf_s, va_s, ga_s, qb_s, kb_s, vb_s, sga_s, sgb_s = ps
    sb = SAMPLE_BLOCK
    to_t = lambda a: a.reshape(ts // sb, sb, A_HEADS, A_DK).transpose(2, 0, 3, 1)
    qh = qb_s.astype(F32).reshape(ts, B_Q_HEADS, B_HEAD_DIM)
    z = jnp.zeros_like(qh[:, :B_GROUP])
    q3 = jnp.concatenate([jnp.concatenate([qh[:, :B_GROUP], z], axis=-1),
                          jnp.concatenate([z, qh[:, B_GROUP:]], axis=-1)], axis=1)
    st_s, oa_s, ob3, ck_s, cv_s = _sample_step(
        l, prm["sinks8"][l], prm["state"], to_t(lf_s), to_t(qa_s.astype(F32)), va_s, ga_s, prm["norm_g"][l],
        q3, kb_s, vb_s, prm["cache_k"], prm["cache_v"])
    ob_s = jnp.concatenate([ob3[:, :B_GROUP, :B_HEAD_DIM], ob3[:, B_GROUP:, B_HEAD_DIM:]], axis=1)
    ob_s = ob_s.reshape(ts, B_WIDTH).astype(BF16)
    h_all = _merge(xs, oa_s, ob_s, sga_s, sgb_s, prm["wa"][l], prm["wb"][l], prm["wo"][l], prm["ln1_g"][l],
                   prm["ln1_b"][l], ts, xs_off, n_all, n_prompt // ts, ts, h_prev=h_all)
    return h_all, (kp, vp, st_p, ck_s, cv_s, st_s)


def _router_kernel(h_ref, wr_ref, bias_ref, e_ref, w_ref, mask_ref, cnt_ref):
    tm = h_ref.shape[0] // SUBLANES
    gsz = GROUP_SIZE

    @pl.when(pl.program_id(0) == 0)
    def _():
        cnt_ref[...] = jnp.zeros_like(cnt_ref)

    logits = lax.dot_general(wr_ref[...], _load_tiled(h_ref, tm), (((1,), (1,)), ((), ())),
                             precision=lax.Precision.HIGHEST, preferred_element_type=F32)
    scores = _sigmoid(logits)
    sel = scores + bias_ref[...][:, 0:1]
    rowi = lax.broadcasted_iota(I32, (gsz, tm), 0)
    neg_inf = -jnp.inf
    blocks = [sel[g * gsz:(g + 1) * gsz] for g in range(N_GROUPS)]
    sblocks = [scores[g * gsz:(g + 1) * gsz] for g in range(N_GROUPS)]

    gscore = []
    for blk in blocks:
        m1 = jnp.max(blk, axis=0, keepdims=True)
        i1 = jnp.min(jnp.where(blk == m1, rowi, gsz), axis=0, keepdims=True)
        m2 = jnp.max(jnp.where(rowi == i1, neg_inf, blk), axis=0, keepdims=True)
        gscore.append(m1 + m2)
    work = []
    for g in range(N_GROUPS):
        ahead = jnp.zeros((1, tm), I32)
        for g2 in range(N_GROUPS):
            if g2 != g:
                beats = (gscore[g2] > gscore[g]) | ((gscore[g2] == gscore[g]) & (g2 < g))
                ahead = ahead + beats.astype(I32)
        work.append(jnp.where(ahead < TOPK_GROUPS, blocks[g], NEG_BIG))

    chosen = [jnp.zeros((gsz, tm), F32) for _ in range(N_GROUPS)]
    es, ws = [], []
    for _ in range(TOP_K):
        m = work[0]
        for g in range(1, N_GROUPS):
            m = jnp.maximum(m, work[g])
        m = jnp.max(m, axis=0, keepdims=True)
        cand = jnp.where(work[0] == m, rowi, N_EXPERTS)
        for g in range(1, N_GROUPS):
            cand = jnp.minimum(cand, jnp.where(work[g] == m, rowi + g * gsz, N_EXPERTS))
        idx = jnp.min(cand, axis=0, keepdims=True)
        wj = jnp.zeros((1, tm), F32)
        for g in range(N_GROUPS):
            hit = (rowi + g * gsz) == idx
            wj = wj + jnp.sum(jnp.where(hit, sblocks[g], 0.0), axis=0, keepdims=True)
            chosen[g] = jnp.where(hit, 1.0, chosen[g])
            work[g] = jnp.where(hit, neg_inf, work[g])
        es.append(idx)
        ws.append(wj)
    wsum = ws[0]
    for j in range(1, TOP_K):
        wsum = wsum + ws[j]
    for j in range(TOP_K):
        e_ref[j:j + 1, :] = es[j]
        w_ref[j:j + 1, :] = ws[j] / wsum * ROUTED_SCALE
    for g in range(N_GROUPS):
        rows = slice(g * gsz, (g + 1) * gsz)
        mask_ref[rows, :] = chosen[g]
        part = chosen[g][:, 0:LANES]
        for c in range(1, tm // LANES):
            part = part + chosen[g][:, c * LANES:(c + 1) * LANES]
        cnt_ref[rows, :] = cnt_ref[rows, :] + part


def _router(h_all, wr_t, bias_b, tm):
    n = h_all.shape[0] // SUBLANES
    col = lambda r: pl.BlockSpec((r, tm), lambda i: (0, i))
    return pl.pallas_call(
        _router_kernel,
        grid=(n // tm,),
        in_specs=[pl.BlockSpec((tm * SUBLANES, LANES), lambda i: (i, 0)),
                  pl.BlockSpec((N_EXPERTS, D_MODEL), lambda i: (0, 0)),
                  pl.BlockSpec((N_EXPERTS, LANES), lambda i: (0, 0))],
        out_specs=[col(TOP_K), col(TOP_K), col(N_EXPERTS), pl.BlockSpec((N_EXPERTS, LANES), lambda i: (0, 0))],
        out_shape=[jax.ShapeDtypeStruct((TOP_K, n), I32), jax.ShapeDtypeStruct((TOP_K, n), F32),
                   jax.ShapeDtypeStruct((N_EXPERTS, n), F32), jax.ShapeDtypeStruct((N_EXPERTS, LANES), F32)],
        compiler_params=_cparams(("arbitrary",)),
        name="router",
    )(h_all, wr_t, bias_b)


def _rank_kernel(mask_ref, e_ref, offs_ref, triu_ref, dest_ref, carry):
    tm = mask_ref.shape[1]

    @pl.when(pl.program_id(0) == 0)
    def _():
        carry[...] = jnp.zeros_like(carry)

    mk = mask_ref[...]
    rank = _dot(mk.astype(BF16), triu_ref[...])
    dest_full = rank + (offs_ref[...][:, 0:1] + carry[...][:, 0:1])
    rowi = lax.broadcasted_iota(I32, (N_EXPERTS, tm), 0)
    for j in range(TOP_K):
        d = jnp.sum(jnp.where(rowi == e_ref[j:j + 1, :], dest_full, 0.0), axis=0, keepdims=True)
        dest_ref[j:j + 1, :] = d.astype(I32)
    carry[...] = carry[...] + jnp.sum(mk, axis=1, keepdims=True)


def _rank(mask_t, e_t, offs_b, tm):
    n = mask_t.shape[1]
    r = np.arange(tm)
    triu = jnp.asarray((r[:, None] < r[None, :]).astype(np.float32), BF16)
    return pl.pallas_call(
        _rank_kernel,
        grid=(n // tm,),
        in_specs=[pl.BlockSpec((N_EXPERTS, tm), lambda i: (0, i)),
                  pl.BlockSpec((TOP_K, tm), lambda i: (0, i)),
                  pl.BlockSpec((N_EXPERTS, LANES), lambda i: (0, 0)),
                  pl.BlockSpec((tm, tm), lambda i: (0, 0))],
        out_specs=pl.BlockSpec((TOP_K, tm), lambda i: (0, i)),
        out_shape=jax.ShapeDtypeStruct((TOP_K, n), I32),
        scratch_shapes=[pltpu.VMEM((N_EXPERTS, LANES), F32)],
        compiler_params=_cparams(("arbitrary",)),
        name="rank",
    )(mask_t, e_t, offs_b, triu)


INV_COLS = 512
TOKEN_RADIX = 128


def _inverse_kernel(dest_ref, inv_ref):
    tm = dest_ref.shape[1]
    nq = inv_ref.shape[0]

    @pl.when(pl.program_id(0) == 0)
    def _():
        inv_ref[...] = jnp.zeros_like(inv_ref)

    tok = pl.program_id(0) * tm + lax.broadcasted_iota(I32, (1, tm), 1)
    t_hi = jnp.right_shift(tok, TOKEN_RADIX.bit_length() - 1).astype(F32)
    t_lo = jnp.bitwise_and(tok, TOKEN_RADIX - 1).astype(F32)
    qi = lax.broadcasted_iota(I32, (nq, tm), 0)
    si = lax.broadcasted_iota(I32, (INV_COLS, tm), 0)
    acc_hi = jnp.zeros(inv_ref.shape, F32)
    acc_lo = jnp.zeros(inv_ref.shape, F32)
    for j in range(TOP_K):
        d = dest_ref[j:j + 1, :]
        at_q = qi == jnp.right_shift(d, INV_COLS.bit_length() - 1)
        col = jnp.where(si == jnp.bitwise_and(d, INV_COLS - 1), 1.0, 0.0).astype(BF16)
        acc_hi = acc_hi + _dot_nt(jnp.where(at_q, t_hi, 0.0).astype(BF16), col)
        acc_lo = acc_lo + _dot_nt(jnp.where(at_q, t_lo, 0.0).astype(BF16), col)
    inv_ref[...] = inv_ref[...] + (acc_hi * float(TOKEN_RADIX) + acc_lo)


def _inverse_map(dest_t, tm):
    n = dest_t.shape[1]
    n_rows = n * TOP_K
    nq = -(-(n_rows // INV_COLS) // 8) * 8
    inv = pl.pallas_call(
        _inverse_kernel,
        grid=(n // tm,),
        in_specs=[pl.BlockSpec((TOP_K, tm), lambda i: (0, i))],
        out_specs=pl.BlockSpec((nq, INV_COLS), lambda i: (0, 0)),
        out_shape=jax.ShapeDtypeStruct((nq, INV_COLS), F32),
        compiler_params=_cparams(("arbitrary",)),
        name="inverse_map",
    )(dest_t)
    return inv.reshape(-1)[:n_rows].astype(I32).reshape(n_rows // EXPERT_ROW_TILE, 1, EXPERT_ROW_TILE)


DMA_THREADS = 2


def _token_copy(src_ref, token, dst_ref, dst_row, sem):
    return pltpu.make_async_copy(src_ref.at[pl.ds(pl.multiple_of(token * SUBLANES, SUBLANES), SUBLANES)],
                                 dst_ref.at[pl.ds(dst_row * SUBLANES, SUBLANES)], sem)


GATHER_AHEAD = 2
GATHER_SLOTS = GATHER_AHEAD + 1


def _expert_kernel(n_tiles, tile_ref, exp_ref, valid_ref, first_ref, newexp_ref, lo_ref, hi_ref,
                   inv0_ref, inv1_ref, inva_ref, h_ref, wg_ref, wu_ref, wd_ref, y_ref, xbuf, wgb, wub, wdb, sem):
    i = pl.program_id(0)
    rows = xbuf.shape[1] // SUBLANES
    tile = tile_ref[i]
    slot = tile % GATHER_SLOTS
    ahead_slot = (tile + GATHER_AHEAD) % GATHER_SLOTS
    valid = valid_ref[i] == 1
    first = first_ref[i] == 1

    def wait_rows(s):
        pltpu.make_async_copy(h_ref.at[pl.ds(0, rows * SUBLANES)], xbuf.at[s], sem.at[s]).wait()

    def gather_loop(idx_ref, s):
        def issue(k, c):
            for u in range(SUBLANES):
                r = k * SUBLANES + u
                pltpu.make_async_copy(
                    h_ref.at[pl.ds(pl.multiple_of(idx_ref[0, 0, r] * SUBLANES, SUBLANES), SUBLANES)],
                    xbuf.at[s, pl.ds(pl.multiple_of(r * SUBLANES, SUBLANES), SUBLANES)], sem.at[s]).start()
            return c

        lax.fori_loop(0, rows // SUBLANES, issue, 0)

    @pl.when(valid & first)
    def _():
        @pl.when(i == 0)
        def _():
            gather_loop(inv0_ref, 0)
            gather_loop(inv1_ref, 1)

        wait_rows(slot)

    @pl.when(valid & (newexp_ref[i] == 1))
    def _():
        wgb[...] = wg_ref[0].astype(BF16)
        wub[...] = wu_ref[0].astype(BF16)
        wdb[...] = wd_ref[0].astype(BF16)

    def compute(first_visit):
        xb = _load_tiled(xbuf.at[slot], rows).astype(BF16)
        if first_visit:
            for r in range(rows):
                _token_copy(h_ref, inva_ref[0, 0, r], xbuf.at[ahead_slot], r, sem.at[ahead_slot]).start(
                    priority=r % DMA_THREADS)
        gate = _dot(xb, wgb[...])
        up = _dot(xb, wub[...])
        y = _dot((_silu(gate) * up).astype(BF16), wdb[...])
        rowi = lax.broadcasted_iota(I32, y.shape, 0)
        mine = (rowi >= lo_ref[i]) & (rowi < hi_ref[i])
        _store_tiled(y_ref, jnp.where(mine, y, 0.0 if first_visit else _load_tiled(y_ref, rows)))

    @pl.when(valid & first)
    def _():
        compute(True)

    @pl.when(valid & jnp.logical_not(first))
    def _():
        compute(False)

    @pl.when(valid & first & (tile >= n_tiles - GATHER_AHEAD))
    def _():
        wait_rows(ahead_slot)


def _group_metadata(counts, n_rows):
    tmo = EXPERT_ROW_TILE
    n_tiles = n_rows // tmo
    ends = jnp.cumsum(counts)
    offs = ends - counts
    first_tile = offs // tmo
    n_t = jnp.where(counts > 0, (ends - 1) // tmo - first_tile + 1, 0)
    cum = jnp.cumsum(n_t)
    base = cum - n_t
    n_items = n_tiles + N_EXPERTS
    idx = jnp.arange(n_items, dtype=I32)
    valid = (idx < cum[-1]).astype(I32)
    idc = jnp.minimum(idx, cum[-1] - 1)
    e = jnp.minimum(jnp.sum((cum[None, :] <= idc[:, None]).astype(I32), axis=1), N_EXPERTS - 1)
    tile = (first_tile[e] + idc - base[e]).astype(I32)
    one = jnp.ones((1,), I32)
    first = jnp.concatenate([one, (tile[1:] != tile[:-1]).astype(I32)])
    new_expert = jnp.concatenate([one, (e[1:] != e[:-1]).astype(I32)])
    lo = jnp.clip(offs[e] - tile * tmo, 0, tmo).astype(I32)
    hi = jnp.clip(ends[e] - tile * tmo, 0, tmo).astype(I32)
    return tile, e, valid, first, new_expert, lo, hi


def _experts(l, meta, inv3, h_all, wg, wu, wd):
    tmo = EXPERT_ROW_TILE
    n_rows = inv3.shape[0] * tmo
    n_items = meta[0].shape[0]
    wspec = lambda shp: pl.BlockSpec((None, 1) + shp, lambda i, tl, ex, *_: (l, ex[i], 0, 0))
    n_tiles = inv3.shape[0]
    assert n_tiles > GATHER_AHEAD
    idx_spec = lambda nxt: pl.BlockSpec(
        (1, 1, tmo), lambda i, tl, *_: (jnp.minimum(tl[i] + nxt, n_tiles - 1), 0, 0), memory_space=pltpu.SMEM)
    return pl.pallas_call(
        functools.partial(_expert_kernel, n_tiles),
        grid_spec=pltpu.PrefetchScalarGridSpec(
            num_scalar_prefetch=7,
            grid=(n_items,),
            in_specs=[idx_spec(0), idx_spec(1), idx_spec(GATHER_AHEAD), pl.BlockSpec(memory_space=pl.ANY),
                      wspec((D_MODEL, D_EXPERT)), wspec((D_MODEL, D_EXPERT)), wspec((D_EXPERT, D_MODEL))],
            out_specs=pl.BlockSpec((tmo * SUBLANES, LANES), lambda i, tl, *_: (tl[i], 0)),
            scratch_shapes=[pltpu.VMEM((GATHER_SLOTS, tmo * SUBLANES, LANES), F32),
                            pltpu.VMEM((D_MODEL, D_EXPERT), BF16), pltpu.VMEM((D_MODEL, D_EXPERT), BF16),
                            pltpu.VMEM((D_EXPERT, D_MODEL), BF16),
                            pltpu.SemaphoreType.DMA((GATHER_SLOTS,))]),
        out_shape=jax.ShapeDtypeStruct((n_rows * SUBLANES, LANES), F32),
        compiler_params=_cparams(("arbitrary",)),
        name="experts",
    )(*meta, inv3, inv3, inv3, h_all, wg, wu, wd)


def _post_kernel(n_steps, dest0_ref, dest1_ref, desta_ref, h_ref, wt_ref, y_ref, wsg_ref, wsu_ref, wsd_ref,
                 g_ref, b_ref, out_ref, buf, sem):
    i = pl.program_id(0)
    tm = out_ref.shape[0]
    slot = i % GATHER_SLOTS
    ahead_slot = (i + GATHER_AHEAD) % GATHER_SLOTS

    def wait_rows(s):
        for j in range(TOP_K):
            pltpu.make_async_copy(y_ref.at[pl.ds(0, tm * SUBLANES)], buf.at[s, j], sem.at[s]).wait()

    def gather_loop(idx_ref, s):
        def issue(k, c):
            for u in range(SUBLANES):
                t = k * SUBLANES + u
                for j in range(TOP_K):
                    pltpu.make_async_copy(
                        y_ref.at[pl.ds(pl.multiple_of(idx_ref[j, t] * SUBLANES, SUBLANES), SUBLANES)],
                        buf.at[s, j, pl.ds(pl.multiple_of(t * SUBLANES, SUBLANES), SUBLANES)], sem.at[s]).start()
            return c

        lax.fori_loop(0, tm // SUBLANES, issue, 0)

    @pl.when(i == 0)
    def _():
        gather_loop(dest0_ref, 0)
        gather_loop(dest1_ref, 1)

    wait_rows(slot)

    h = _load_tiled(h_ref, tm)
    hb = h.astype(BF16)
    for t in range(tm):
        for j in range(TOP_K):
            _token_copy(y_ref, desta_ref[j, t], buf.at[ahead_slot, j], t, sem.at[ahead_slot]).start(
                priority=j % DMA_THREADS)
    shared = _dot((_silu(_dot(hb, wsg_ref[...])) * _dot(hb, wsu_ref[...])).astype(BF16), wsd_ref[...])
    wt = wt_ref[...]
    routed = _load_tiled(buf.at[slot, 0], tm) * wt[:, 0:1]
    for j in range(1, TOP_K):
        routed = routed + _load_tiled(buf.at[slot, j], tm) * wt[:, j:j + 1]
    out_ref[...] = _layer_norm(DN_ALPHA * h + (routed + shared), g_ref[...], b_ref[...])

    @pl.when(i >= n_steps - GATHER_AHEAD)
    def _():
        wait_rows(ahead_slot)


def _post(dest_t, h_all, w_tok, y, wsg, wsu, wsd, g, b, tm):
    n = h_all.shape[0] // SUBLANES
    n_steps = n // tm
    assert n_steps > GATHER_AHEAD
    const = lambda a: pl.BlockSpec(a.shape, lambda i: (0, 0))
    idx_spec = lambda nxt: pl.BlockSpec((TOP_K, tm), lambda i: (0, jnp.minimum(i + nxt, n_steps - 1)),
                                        memory_space=pltpu.SMEM)
    return pl.pallas_call(
        functools.partial(_post_kernel, n_steps),
        grid=(n_steps,),
        in_specs=[idx_spec(0), idx_spec(1), idx_spec(GATHER_AHEAD),
                  pl.BlockSpec((tm * SUBLANES, LANES), lambda i: (i, 0)),
                  pl.BlockSpec((tm, TOP_K), lambda i: (i, 0)),
                  pl.BlockSpec(memory_space=pl.ANY),
                  const(wsg), const(wsu), const(wsd), const(g), const(b)],
        out_specs=pl.BlockSpec((tm, D_MODEL), lambda i: (i, 0)),
        out_shape=jax.ShapeDtypeStruct((n, D_MODEL), F32),
        scratch_shapes=[pltpu.VMEM((GATHER_SLOTS, TOP_K, tm * SUBLANES, LANES), F32),
                        pltpu.SemaphoreType.DMA((GATHER_SLOTS,))],
        compiler_params=_cparams(("arbitrary",)),
        name="moe_post",
    )(dest_t, dest_t, dest_t, h_all, w_tok, y, wsg, wsu, wsd, g, b)


def _moe_layer(l, h_all, prm):
    tm = MOE_ROW_TILE
    e_t, w_t, mask_t, cnt = _router(h_all, prm["wr_t"][l], prm["rbias"][l], tm)
    counts = jnp.sum(cnt, axis=1).astype(I32)
    offs = jnp.cumsum(counts) - counts
    offs_b = jnp.broadcast_to(offs.astype(F32)[:, None], (N_EXPERTS, LANES))
    dest_t = _rank(mask_t, e_t, offs_b, tm)
    inv3 = _inverse_map(dest_t, tm)
    meta = _group_metadata(counts, h_all.shape[0] // SUBLANES * TOP_K)
    y = _experts(l, meta, inv3, h_all, prm["w_exp_gate"], prm["w_exp_up"], prm["w_exp_down"])
    return _post(dest_t, h_all, w_t.T, y, prm["wsg"][l], prm["wsu"][l], prm["wsd"][l],
                 prm["ln2_g"][l], prm["ln2_b"][l], POST_ROW_TILE)


def kernel(x_prompt, x_sample, cache_k, cache_v, state_hgrn, w_in, hgrn_lower_bounds, hgrn_norm_g, attn_sinks, w_branch_a, w_branch_b, w_out, ln1_g, ln1_b, w_router, router_bias, w_exp_gate, w_exp_up, w_exp_down, w_sh_gate, w_sh_up, w_sh_down, ln2_g, ln2_b):
    bsz, t, d = x_prompt.shape
    n_sample = x_sample.shape[0] * x_sample.shape[1]
    n_prompt = bsz * t
    depth = w_in.shape[0]
    assert d == D_MODEL and x_sample.shape[1] == 1 and n_prompt % ROW_TILE == 0 and t % ROW_TILE == 0
    assert n_prompt % n_sample == 0 and (n_prompt + n_sample) % MOE_ROW_TILE == 0
    assert (n_prompt + n_sample) % POST_ROW_TILE == 0
    assert ((n_prompt + n_sample) * TOP_K) % EXPERT_ROW_TILE == 0 and n_sample % SAMPLE_BLOCK == 0

    lb_prob = jax.nn.softmax(hgrn_lower_bounds.astype(F32), axis=0)
    lower = (jnp.cumsum(lb_prob, axis=0) - lb_prob[0])[:, None, :]
    cos_p, sin_p = _rope_tables(jnp.arange(t))
    cos_s, sin_s = _rope_tables(jnp.full((n_sample,), PAST_LEN))
    row = lambda a: a[:, None, :]
    prm = dict(
        w_in=w_in.astype(BF16), lower=lower, cos_p=cos_p, sin_p=sin_p, cos_s=cos_s, sin_s=sin_s,
        norm_g=row(hgrn_norm_g), sinks=attn_sinks,
        sinks8=jnp.broadcast_to(attn_sinks[:, :, None], (depth, B_Q_HEADS, LANES)),
        wa=w_branch_a.astype(BF16), wb=w_branch_b.astype(BF16), wo=w_out.astype(BF16),
        ln1_g=row(ln1_g), ln1_b=row(ln1_b), ln2_g=row(ln2_g), ln2_b=row(ln2_b),
        wr_t=jnp.swapaxes(w_router, 1, 2),
        rbias=jnp.broadcast_to(router_bias[:, :, None], (depth, N_EXPERTS, LANES)),
        w_exp_gate=w_exp_gate, w_exp_up=w_exp_up, w_exp_down=w_exp_down,
        wsg=w_sh_gate.astype(BF16), wsu=w_sh_up.astype(BF16), wsd=w_sh_down.astype(BF16),
        state=state_hgrn,
        cache_k=cache_k.reshape(depth, n_sample, WINDOW, B_KV_WIDTH),
        cache_v=cache_v.reshape(depth, n_sample, WINDOW, B_KV_WIDTH),
    )

    xp, xs, xs_off = x_prompt.reshape(n_prompt, d), x_sample.reshape(n_sample, d), 0
    per_layer = []
    for l in range(depth):
        h_all, outs = _mixer_layer(l, xp, xs, xs_off, n_prompt, bsz, t, n_sample, prm)
        y_all = _moe_layer(l, h_all, prm)
        per_layer.append(outs)
        xp, xs, xs_off = y_all, y_all, n_prompt // n_sample

    kv_shape = (n_sample, WINDOW, B_KV_HEADS, B_HEAD_DIM)
    stack = lambda k, f=lambda a: a: jnp.stack([f(o[k]) for o in per_layer])
    return (y_all[:n_prompt].reshape(bsz, t, d), y_all[n_prompt:].reshape(n_sample, 1, d),
            stack(0), stack(1), stack(2),
            stack(3, lambda a: a.reshape(kv_shape)), stack(4, lambda a: a.reshape(kv_shape)), stack(5))
```

```python
import functools

import numpy as np
import jax
import jax.numpy as jnp
from jax import lax
from jax.experimental import pallas as pl
from jax.experimental.pallas import tpu as pltpu

F32 = jnp.float32
BF16 = jnp.bfloat16
I32 = jnp.int32

D_MODEL = 1024
DEPTH = 2
PAST_LEN = 16384
A_HEADS = 4
A_DK = 128
A_DV = 128
A_KEY = A_HEADS * A_DK
A_WIDTH = A_HEADS * A_DV
B_Q_HEADS = 8
B_KV_HEADS = 2
B_HEAD_DIM = 64
B_GROUP = B_Q_HEADS // B_KV_HEADS
B_WIDTH = B_Q_HEADS * B_HEAD_DIM
B_KV_WIDTH = B_KV_HEADS * B_HEAD_DIM
WINDOW = 128
ROPE_THETA = 10000.0
ATTN_SCALE = B_HEAD_DIM ** -0.5
N_EXPERTS = 64
TOP_K = 8
N_GROUPS = 8
GROUP_SIZE = N_EXPERTS // N_GROUPS
TOPK_GROUPS = 4
D_EXPERT = D_MODEL // 4
D_SHARED = D_EXPERT
ROUTED_SCALE = 2.5
DN_ALPHA = (2 * DEPTH) ** 0.25
LN_EPS = 1e-5
RMS_EPS = 1e-6
NEG_BIG = -1e30
TINY = 1.1754944e-38
OFF_AF = A_KEY
OFF_AI = 2 * A_KEY
OFF_AG = OFF_AI + A_WIDTH
OFF_BQ = OFF_AG + A_WIDTH
OFF_BK = OFF_BQ + B_WIDTH
OFF_BV = OFF_BK + B_KV_WIDTH
OFF_GA = OFF_BV + B_KV_WIDTH
OFF_GB = OFF_GA + D_MODEL
IN_COLS = OFF_GB + D_MODEL

LANES = 128
HGRN_CHUNK = 128
HGRN_LEVELS = 7
ROW_TILE = 512
MOE_ROW_TILE = 384
EXPERT_ROW_TILE = 512
POST_ROW_TILE = 128
VMEM_LIMIT = 56 * 1024 * 1024


def _cparams(sem, vmem=VMEM_LIMIT):
    return pltpu.CompilerParams(dimension_semantics=sem, vmem_limit_bytes=vmem)


def _dot(a, b):
    return jnp.dot(a, b, preferred_element_type=F32)


def _dot_nt(a, b):
    return lax.dot_general(a, b, (((1,), (1,)), ((), ())), preferred_element_type=F32)


def _dot_tn(a, b):
    return lax.dot_general(a, b, (((0,), (0,)), ((), ())), preferred_element_type=F32)


def _sigmoid(x):
    return 1.0 / (1.0 + jnp.exp(-x))


def _silu(x):
    return x * _sigmoid(x)


def _split3(x):
    hi = x.astype(BF16)
    r1 = x - hi.astype(F32)
    mid = r1.astype(BF16)
    lo = (r1 - mid.astype(F32)).astype(BF16)
    return hi, mid, lo


def _layer_norm(y, g, b):
    mu = jnp.mean(y, axis=-1, keepdims=True)
    d = y - mu
    var = jnp.mean(d * d, axis=-1, keepdims=True)
    return d * lax.rsqrt(var + LN_EPS) * g + b


SUBLANES = 8
assert D_MODEL == SUBLANES * LANES


def _store_tiled(ref, val):
    rows = val.shape[0]
    for c in range(SUBLANES):
        ref[pl.ds(c, rows, stride=SUBLANES), :] = val[:, c * LANES:(c + 1) * LANES]


def _load_tiled(ref, rows):
    return jnp.concatenate([ref[pl.ds(c, rows, stride=SUBLANES), :] for c in range(SUBLANES)], axis=1)


def _log_forget(af, lower):
    ls = jnp.minimum(af, 0.0) - jnp.log1p(jnp.exp(-jnp.abs(af)))
    a = jnp.log(jnp.maximum(lower, TINY))
    b = jnp.log1p(-lower) + ls
    mixed = jnp.maximum(a, b) + jnp.log1p(jnp.exp(-jnp.abs(a - b)))
    return jnp.where(lower > 0.0, mixed, ls)


def _proj_kernel(x_ref, w_ref, cos_ref, sin_ref, low_ref,
                 qa_ref, lf_ref, va_ref, ga_ref, qb_ref, kb_ref, vb_ref, sga_ref, sgb_ref):
    xb = x_ref[...].astype(BF16)

    def mm(c0, n):
        return _dot(xb, w_ref[:, c0:c0 + n])

    qa_ref[...] = _silu(mm(0, A_KEY)).astype(BF16)
    lf_ref[...] = _log_forget(mm(OFF_AF, A_KEY), low_ref[...])
    va_ref[...] = mm(OFF_AI, A_WIDTH).astype(BF16)
    ga_ref[...] = _silu(mm(OFF_AG, A_WIDTH)).astype(BF16)

    cos = cos_ref[...]
    sin = sin_ref[...]
    lane = lax.broadcasted_iota(I32, cos.shape, 1)
    first_half = (lane & (B_HEAD_DIM // 2)) == 0

    def rope(blk):
        partner = jnp.where(first_half, pltpu.roll(blk, LANES - B_HEAD_DIM // 2, 1),
                            pltpu.roll(blk, B_HEAD_DIM // 2, 1))
        return blk * cos + partner * sin

    bq = mm(OFF_BQ, B_WIDTH)
    for j in range(B_WIDTH // LANES):
        sl = slice(j * LANES, (j + 1) * LANES)
        qb_ref[:, sl] = (rope(bq[:, sl]) * ATTN_SCALE).astype(BF16)
    kb_ref[...] = rope(mm(OFF_BK, B_KV_WIDTH))
    vb_ref[...] = mm(OFF_BV, B_KV_WIDTH)
    sga_ref[...] = _sigmoid(mm(OFF_GA, D_MODEL)).astype(BF16)
    sgb_ref[...] = _sigmoid(mm(OFF_GB, D_MODEL)).astype(BF16)


def _proj(x, w_bf, cos_t, sin_t, lower, n_rows, row_off_blocks, tm):
    nt = n_rows // tm
    tab_blocks = cos_t.shape[0] // tm
    row = lambda w: pl.BlockSpec((tm, w), lambda i: (i, 0))
    outs = [(A_KEY, BF16), (A_KEY, F32), (A_WIDTH, BF16), (A_WIDTH, BF16), (B_WIDTH, BF16),
            (B_KV_WIDTH, F32), (B_KV_WIDTH, F32), (D_MODEL, BF16), (D_MODEL, BF16)]
    return pl.pallas_call(
        _proj_kernel,
        grid=(nt,),
        in_specs=[pl.BlockSpec((tm, D_MODEL), lambda i: (i + row_off_blocks, 0)),
                  pl.BlockSpec((D_MODEL, IN_COLS), lambda i: (0, 0)),
                  pl.BlockSpec((tm, LANES), lambda i: (i % tab_blocks, 0)),
                  pl.BlockSpec((tm, LANES), lambda i: (i % tab_blocks, 0)),
                  pl.BlockSpec((1, A_KEY), lambda i: (0, 0))],
        out_specs=[row(w) for w, _ in outs],
        out_shape=[jax.ShapeDtypeStruct((n_rows, w), dt) for w, dt in outs],
        compiler_params=_cparams(("arbitrary",)),
        name="proj",
    )(x, w_bf, cos_t, sin_t, lower)


def _rope_tables(pos):
    half = B_HEAD_DIM // 2
    inv = ROPE_THETA ** (-jnp.arange(half, dtype=F32) / half)
    ang = pos.astype(F32)[:, None] * inv[None, :]
    cos = jnp.cos(ang)
    sin = jnp.sin(ang)
    reps = LANES // B_HEAD_DIM
    cos_t = jnp.tile(jnp.concatenate([cos, cos], axis=1), (1, reps))
    sin_t = jnp.tile(jnp.concatenate([-sin, sin], axis=1), (1, reps))
    return cos_t, sin_t


def _hgrn_constants():
    c = HGRN_CHUNK
    r = np.arange(c)
    tri = (r[None, :] <= r[:, None]).astype(np.float32)
    sel = np.zeros((HGRN_LEVELS, c, c), np.float32)
    upper = np.zeros((HGRN_LEVELS, c, A_KEY), np.float32)
    pair = np.zeros((HGRN_LEVELS + 1, c, c), np.float32)
    for l in range(HGRN_LEVELS):
        b = c >> (l + 1)
        ref_row = (r // (2 * b)) * (2 * b) + b - 1
        sel[l, r, ref_row] = 1.0
        up = (r % (2 * b)) >= b
        upper[l] = up[:, None]
        same = (r[:, None] // (2 * b)) == (r[None, :] // (2 * b))
        pair[l] = (up[:, None] & ~up[None, :] & same)
    pair[HGRN_LEVELS] = np.eye(c)
    return (jnp.asarray(tri, BF16), jnp.asarray(sel.reshape(HGRN_LEVELS * c, c), BF16),
            jnp.asarray(upper), jnp.asarray(pair))


def _hgrn_kernel(qa_ref, lf_ref, va_ref, ga_ref, g_ref, tri_ref, sel_ref, up_ref, pair_ref,
                 oa_ref, st_ref, s_scr):
    c = HGRN_CHUNK
    step = pl.program_id(1)

    @pl.when(step == 0)
    def _():
        s_scr[...] = jnp.zeros_like(s_scr)

    lf = lf_ref[...]
    tri = tri_ref[...]
    sel = sel_ref[...]
    hi, mid, lo = _split3(lf)
    gcum = _dot(tri, hi) + _dot(tri, mid) + _dot(tri, lo)
    ghi, gmid, glo = _split3(gcum)
    gref = _dot(sel, ghi) + _dot(sel, gmid) + _dot(sel, glo)
    qb = qa_ref[...]
    qf = qb.astype(F32)
    kf = 1.0 - jnp.exp(lf)
    kb = kf.astype(BF16)
    vb = va_ref[...]
    ws = []
    for l in range(HGRN_LEVELS):
        e = jnp.exp(-jnp.abs(gcum - gref[l * c:(l + 1) * c]))
        ws.append((jnp.where(up_ref[l] > 0.5, qf, kf) * e).astype(BF16))
    qg = (qf * jnp.exp(gcum)).astype(BF16)
    gend = gcum[c - 1:c, :]
    kend = (kf * jnp.exp(gend - gcum)).astype(BF16)
    decay = jnp.exp(gend)
    gate = g_ref[...]

    for h in range(A_HEADS):
        sl = slice(h * A_DK, (h + 1) * A_DK)
        att = _dot_nt(qb[:, sl], kb[:, sl]) * pair_ref[HGRN_LEVELS]
        for l in range(HGRN_LEVELS):
            w = ws[l][:, sl]
            att = att + _dot_nt(w, w) * pair_ref[l]
        s_t = s_scr[h]
        o = _dot_nt(qg[:, sl], s_t.astype(BF16)) + _dot(att.astype(BF16), vb[:, sl])
        ms = jnp.mean(o * o, axis=-1, keepdims=True)
        on = o * lax.rsqrt(ms + RMS_EPS) * gate * ga_ref[:, sl].astype(F32)
        oa_ref[:, sl] = on.astype(BF16)
        s_scr[h] = s_t * decay[:, sl] + _dot_tn(vb[:, sl], kend[:, sl])

    @pl.when(step == pl.num_programs(1) - 1)
    def _():
        for h in range(A_HEADS):
            st_ref[0, h] = s_scr[h].T


def _hgrn_prompt(qa, lf, va, ga, norm_g, bsz, t):
    c = HGRN_CHUNK
    nc = t // c
    tri, sel, upper, pair = _hgrn_constants()
    blk = lambda: pl.BlockSpec((c, A_KEY), lambda b, i: (b * nc + i, 0))
    const = lambda a: pl.BlockSpec(a.shape, lambda b, i: (0,) * a.ndim)
    return pl.pallas_call(
        _hgrn_kernel,
        grid=(bsz, nc),
        in_specs=[blk(), blk(), blk(), blk(), pl.BlockSpec((1, A_DV), lambda b, i: (0, 0)),
                  const(tri), const(sel), const(upper), const(pair)],
        out_specs=[blk(), pl.BlockSpec((1, A_HEADS, A_DK, A_DV), lambda b, i: (b, 0, 0, 0))],
        out_shape=[jax.ShapeDtypeStruct((bsz * t, A_WIDTH), BF16),
                   jax.ShapeDtypeStruct((bsz, A_HEADS, A_DK, A_DV), F32)],
        scratch_shapes=[pltpu.VMEM((A_HEADS, A_DV, A_DK), F32)],
        compiler_params=_cparams(("arbitrary", "arbitrary")),
        name="hgrn_prompt",
    )(qa, lf, va, ga, norm_g, tri, sel, upper, pair)


def _swa_kernel(sink_ref, q_ref, k_ref, v_ref, o_ref, kprev, vprev):
    w = WINDOW
    i = pl.program_id(1)

    @pl.when(i == 0)
    def _():
        kprev[...] = jnp.zeros_like(kprev)
        vprev[...] = jnp.zeros_like(vprev)

    kc = k_ref[...]
    vc = v_ref[...]
    kk = jnp.concatenate([kprev[...], kc], axis=0)
    vv = jnp.concatenate([vprev[...], vc], axis=0)
    kr = pltpu.roll(kk, B_HEAD_DIM, 1)
    vr = pltpu.roll(vv, B_HEAD_DIM, 1)
    lo2 = lax.broadcasted_iota(I32, kk.shape, 1) < B_HEAD_DIM
    zero = jnp.zeros_like(kk)
    k_lo = [jnp.where(lo2, kk, zero).astype(BF16), jnp.where(lo2, kr, zero).astype(BF16)]
    k_hi = [jnp.where(lo2, zero, kr).astype(BF16), jnp.where(lo2, zero, kk).astype(BF16)]
    v_dup = [jnp.where(lo2, vv, vr).astype(BF16), jnp.where(lo2, vr, vv).astype(BF16)]

    qi = lax.broadcasted_iota(I32, (w, 2 * w), 0)
    kj = lax.broadcasted_iota(I32, (w, 2 * w), 1)
    valid = (kj >= qi) & (kj <= qi + w) & ((kj >= w) | (i > 0))
    lo1 = lax.broadcasted_iota(I32, (w, LANES), 1) < B_HEAD_DIM

    for j in range(B_WIDTH // LANES):
        g = (2 * j) // B_GROUP
        qblk = q_ref[:, j * LANES:(j + 1) * LANES]
        res = []
        for half, kmat in enumerate((k_lo[g], k_hi[g])):
            sk = sink_ref[2 * j + half]
            s = jnp.where(valid, _dot_nt(qblk, kmat), NEG_BIG)
            m = jnp.maximum(jnp.max(s, axis=-1, keepdims=True), sk)
            p = jnp.exp(s - m)
            denom = jnp.sum(p, axis=-1, keepdims=True) + jnp.exp(sk - m)
            res.append(_dot(p.astype(BF16), v_dup[g]) / denom)
        o_ref[:, j * LANES:(j + 1) * LANES] = jnp.where(lo1, res[0], res[1]).astype(BF16)

    kprev[...] = kc
    vprev[...] = vc


def _swa_prompt(sinks, qb, kb, vb, bsz, t):
    w = WINDOW
    nb = t // w
    return pl.pallas_call(
        _swa_kernel,
        grid_spec=pltpu.PrefetchScalarGridSpec(
            num_scalar_prefetch=1,
            grid=(bsz, nb),
            in_specs=[pl.BlockSpec((w, B_WIDTH), lambda b, i, s: (b * nb + i, 0)),
                      pl.BlockSpec((w, B_KV_WIDTH), lambda b, i, s: (b * nb + i, 0)),
                      pl.BlockSpec((w, B_KV_WIDTH), lambda b, i, s: (b * nb + i, 0))],
            out_specs=pl.BlockSpec((w, B_WIDTH), lambda b, i, s: (b * nb + i, 0)),
            scratch_shapes=[pltpu.VMEM((w, B_KV_WIDTH), F32), pltpu.VMEM((w, B_KV_WIDTH), F32)]),
        out_shape=jax.ShapeDtypeStruct((bsz * t, B_WIDTH), BF16),
        compiler_params=_cparams(("arbitrary", "arbitrary")),
        name="swa_prompt",
    )(sinks, qb, kb, vb)


SAMPLE_BLOCK = 8


def _sample_kernel(sink_ref, st_ref, lft_ref, qat_ref, va_ref, ga_ref, g_ref, q3_ref, kn_ref, vn_ref,
                   ck_ref, cv_ref, st_out, oa_ref, ob_ref, ck_out, cv_out, o_scr):
    w = WINDOW
    row = lax.broadcasted_iota(I32, (w, B_KV_WIDTH), 0)
    for i in range(SAMPLE_BLOCK):
        for h in range(A_HEADS):
            sl = slice(h * A_DV, (h + 1) * A_DV)
            fcol = jnp.exp(lft_ref[h, 0][:, i:i + 1])
            qcol = qat_ref[h, 0][:, i:i + 1]
            vrow = va_ref[i:i + 1, sl].astype(F32)
            s_new = st_ref[i, h] * fcol + (1.0 - fcol) * vrow
            st_out[i, h] = s_new
            o_scr[i:i + 1, sl] = jnp.sum(s_new * qcol, axis=0, keepdims=True)
        kc = ck_ref[i]
        vc = cv_ref[i]
        kn = kn_ref[i:i + 1, :]
        vn = vn_ref[i:i + 1, :]
        q3 = q3_ref[i]
        s = _dot_nt(q3.astype(BF16), kc.astype(BF16))
        s_new_key = jnp.sum(q3 * kn, axis=-1, keepdims=True)
        sk = sink_ref[...][:, 0:1]
        m = jnp.maximum(jnp.maximum(jnp.max(s, axis=-1, keepdims=True), s_new_key), sk)
        p = jnp.exp(s - m)
        pn = jnp.exp(s_new_key - m)
        denom = jnp.sum(p, axis=-1, keepdims=True) + pn + jnp.exp(sk - m)
        ob_ref[i] = (_dot(p.astype(BF16), vc.astype(BF16)) + pn * vn) / denom
        ck_out[i] = jnp.where(row == w - 1, kn, pltpu.roll(kc, w - 1, 0))
        cv_out[i] = jnp.where(row == w - 1, vn, pltpu.roll(vc, w - 1, 0))
    for h in range(A_HEADS):
        sl = slice(h * A_DV, (h + 1) * A_DV)
        o = o_scr[:, sl]
        ms = jnp.mean(o * o, axis=-1, keepdims=True)
        oa_ref[:, sl] = (o * lax.rsqrt(ms + RMS_EPS) * g_ref[...] * ga_ref[:, sl].astype(F32)).astype(BF16)


def _sample_step(l, sinks8, state, lft, qat, va, ga, norm_g, q3, kn, vn, ck, cv):
    nb = state.shape[1]
    sb = SAMPLE_BLOCK
    steps = nb // sb
    w = WINDOW
    b4 = lambda: pl.BlockSpec((sb, A_HEADS, A_DK, A_DV), lambda i: (i, 0, 0, 0))
    b4_in = pl.BlockSpec((None, sb, A_HEADS, A_DK, A_DV), lambda i: (l, i, 0, 0, 0))
    c3_in = lambda: pl.BlockSpec((None, sb, w, B_KV_WIDTH), lambda i: (l, i, 0, 0))
    t4 = lambda: pl.BlockSpec((A_HEADS, 1, A_DK, sb), lambda i: (0, i, 0, 0))
    r2 = lambda wd: pl.BlockSpec((sb, wd), lambda i: (i, 0))
    c3 = lambda: pl.BlockSpec((sb, w, B_KV_WIDTH), lambda i: (i, 0, 0))
    return pl.pallas_call(
        _sample_kernel,
        grid=(steps,),
        in_specs=[pl.BlockSpec((B_Q_HEADS, LANES), lambda i: (0, 0)),
                  b4_in, t4(), t4(), r2(A_WIDTH), r2(A_WIDTH), pl.BlockSpec((1, A_DV), lambda i: (0, 0)),
                  pl.BlockSpec((sb, B_Q_HEADS, LANES), lambda i: (i, 0, 0)), r2(B_KV_WIDTH), r2(B_KV_WIDTH),
                  c3_in(), c3_in()],
        out_specs=[b4(), r2(A_WIDTH), pl.BlockSpec((sb, B_Q_HEADS, LANES), lambda i: (i, 0, 0)), c3(), c3()],
        out_shape=[jax.ShapeDtypeStruct(state.shape[1:], F32),
                   jax.ShapeDtypeStruct((nb, A_WIDTH), BF16),
                   jax.ShapeDtypeStruct((nb, B_Q_HEADS, LANES), F32),
                   jax.ShapeDtypeStruct((nb, w, B_KV_WIDTH), F32),
                   jax.ShapeDtypeStruct((nb, w, B_KV_WIDTH), F32)],
        scratch_shapes=[pltpu.VMEM((sb, A_WIDTH), F32)],
        compiler_params=_cparams(("arbitrary",)),
        name="sample_step",
    )(sinks8, state, lft, qat, va, ga, norm_g, q3, kn, vn, ck, cv)


def _merge_kernel(x_ref, oa_ref, ob_ref, sga_ref, sgb_ref, wa_ref, wb_ref, wo_ref, g_ref, b_ref, h_ref):
    merged = (sga_ref[...].astype(F32) * _dot(oa_ref[...], wa_ref[...])
              + sgb_ref[...].astype(F32) * _dot(ob_ref[...], wb_ref[...]))
    mix = _dot(merged.astype(BF16), wo_ref[...])
    _store_tiled(h_ref, _layer_norm(DN_ALPHA * x_ref[...] + mix, g_ref[...], b_ref[...]))


def _merge(x, oa, ob, sga, sgb, wa, wb, wo, g, b, n_rows, x_off_blocks, out_rows, out_off_blocks, tm, h_prev=None):
    nt = n_rows // tm
    row = lambda wd: pl.BlockSpec((tm, wd), lambda i: (i, 0))
    const = lambda a: pl.BlockSpec(a.shape, lambda i: (0, 0))
    args = [x, oa, ob, sga, sgb, wa, wb, wo, g, b]
    in_specs = [pl.BlockSpec((tm, D_MODEL), lambda i: (i + x_off_blocks, 0)),
                row(A_WIDTH), row(B_WIDTH), row(D_MODEL), row(D_MODEL),
                const(wa), const(wb), const(wo), const(g), const(b)]
    kern = _merge_kernel
    aliases = {}
    if h_prev is not None:
        args.append(h_prev)
        in_specs.append(pl.BlockSpec(memory_space=pl.ANY))
        aliases = {len(args) - 1: 0}
        kern = lambda *refs: _merge_kernel(*refs[:10], refs[11])
    return pl.pallas_call(
        kern,
        grid=(nt,),
        in_specs=in_specs,
        out_specs=pl.BlockSpec((tm * SUBLANES, LANES), lambda i: (i + out_off_blocks, 0)),
        out_shape=jax.ShapeDtypeStruct((out_rows * SUBLANES, LANES), F32),
        input_output_aliases=aliases,
        compiler_params=_cparams(("arbitrary",)),
        name="merge",
    )(*args)


def _mixer_layer(l, xp, xs, xs_off, n_prompt, bsz, t, n_sample, prm):
    tm = ROW_TILE
    p = _proj(xp, prm["w_in"][l], prm["cos_p"], prm["sin_p"], prm["lower"][l], n_prompt, 0, tm)
    qa, lf, va, ga, qb, kb, vb, sga, sgb = p
    oa, st_p = _hgrn_prompt(qa, lf, va, ga, prm["norm_g"][l], bsz, t)
    ob = _swa_prompt(prm["sinks"][l], qb, kb, vb, bsz, t)
    n_all = n_prompt + n_sample
    h_all = _merge(xp, oa, ob, sga, sgb, prm["wa"][l], prm["wb"][l], prm["wo"][l], prm["ln1_g"][l],
                   prm["ln1_b"][l], n_prompt, 0, n_all, 0, tm, h_prev=jnp.zeros((n_all * SUBLANES, LANES), F32))
    kp = kb.reshape(bsz, t, B_KV_HEADS, B_HEAD_DIM)[:, -WINDOW:]
    vp = vb.reshape(bsz, t, B_KV_HEADS, B_HEAD_DIM)[:, -WINDOW:]

    ts = n_sample
    ps = _proj(xs, prm["w_in"][l], prm["cos_s"], prm["sin_s"], prm["lower"][l], ts, xs_off, ts)
    qa_s, lf_s, va_s, ga_s, qb_s, kb_s, vb_s, sga_s, sgb_s = ps
    sb = SAMPLE_BLOCK
    to_t = lambda a: a.reshape(ts // sb, sb, A_HEADS, A_DK).transpose(2, 0, 3, 1)
    qh = qb_s.astype(F32).reshape(ts, B_Q_HEADS, B_HEAD_DIM)
    z = jnp.zeros_like(qh[:, :B_GROUP])
    q3 = jnp.concatenate([jnp.concatenate([qh[:, :B_GROUP], z], axis=-1),
                          jnp.concatenate([z, qh[:, B_GROUP:]], axis=-1)], axis=1)
    st_s, oa_s, ob3, ck_s, cv_s = _sample_step(
        l, prm["sinks8"][l], prm["state"], to_t(lf_s), to_t(qa_s.astype(F32)), va_s, ga_s, prm["norm_g"][l],
        q3, kb_s, vb_s, prm["cache_k"], prm["cache_v"])
    ob_s = jnp.concatenate([ob3[:, :B_GROUP, :B_HEAD_DIM], ob3[:, B_GROUP:, B_HEAD_DIM:]], axis=1)
    ob_s = ob_s.reshape(ts, B_WIDTH).astype(BF16)
    h_all = _merge(xs, oa_s, ob_s, sga_s, sgb_s, prm["wa"][l], prm["wb"][l], prm["wo"][l], prm["ln1_g"][l],
                   prm["ln1_b"][l], ts, xs_off, n_all, n_prompt // ts, ts, h_prev=h_all)
    return h_all, (kp, vp, st_p, ck_s, cv_s, st_s)


def _router_kernel(h_ref, wr_ref, bias_ref, e_ref, w_ref, mask_ref, cnt_ref):
    tm = h_ref.shape[0] // SUBLANES
    gsz = GROUP_SIZE

    @pl.when(pl.program_id(0) == 0)
    def _():
        cnt_ref[...] = jnp.zeros_like(cnt_ref)

    logits = lax.dot_general(wr_ref[...], _load_tiled(h_ref, tm), (((1,), (1,)), ((), ())),
                             precision=lax.Precision.HIGHEST, preferred_element_type=F32)
    scores = _sigmoid(logits)
    sel = scores + bias_ref[...][:, 0:1]
    rowi = lax.broadcasted_iota(I32, (gsz, tm), 0)
    neg_inf = -jnp.inf
    blocks = [sel[g * gsz:(g + 1) * gsz] for g in range(N_GROUPS)]
    sblocks = [scores[g * gsz:(g + 1) * gsz] for g in range(N_GROUPS)]

    gscore = []
    for blk in blocks:
        m1 = jnp.max(blk, axis=0, keepdims=True)
        i1 = jnp.min(jnp.where(blk == m1, rowi, gsz), axis=0, keepdims=True)
        m2 = jnp.max(jnp.where(rowi == i1, neg_inf, blk), axis=0, keepdims=True)
        gscore.append(m1 + m2)
    work = []
    for g in range(N_GROUPS):
        ahead = jnp.zeros((1, tm), I32)
        for g2 in range(N_GROUPS):
            if g2 != g:
                beats = (gscore[g2] > gscore[g]) | ((gscore[g2] == gscore[g]) & (g2 < g))
                ahead = ahead + beats.astype(I32)
        work.append(jnp.where(ahead < TOPK_GROUPS, blocks[g], NEG_BIG))

    chosen = [jnp.zeros((gsz, tm), F32) for _ in range(N_GROUPS)]
    es, ws = [], []
    for _ in range(TOP_K):
        m = work[0]
        for g in range(1, N_GROUPS):
            m = jnp.maximum(m, work[g])
        m = jnp.max(m, axis=0, keepdims=True)
        cand = jnp.where(work[0] == m, rowi, N_EXPERTS)
        for g in range(1, N_GROUPS):
            cand = jnp.minimum(cand, jnp.where(work[g] == m, rowi + g * gsz, N_EXPERTS))
        idx = jnp.min(cand, axis=0, keepdims=True)
        wj = jnp.zeros((1, tm), F32)
        for g in range(N_GROUPS):
            hit = (rowi + g * gsz) == idx
            wj = wj + jnp.sum(jnp.where(hit, sblocks[g], 0.0), axis=0, keepdims=True)
            chosen[g] = jnp.where(hit, 1.0, chosen[g])
            work[g] = jnp.where(hit, neg_inf, work[g])
        es.append(idx)
        ws.append(wj)
    wsum = ws[0]
    for j in range(1, TOP_K):
        wsum = wsum + ws[j]
    for j in range(TOP_K):
        e_ref[j:j + 1, :] = es[j]
        w_ref[j:j + 1, :] = ws[j] / wsum * ROUTED_SCALE
    for g in range(N_GROUPS):
        rows = slice(g * gsz, (g + 1) * gsz)
        mask_ref[rows, :] = chosen[g]
        part = chosen[g][:, 0:LANES]
        for c in range(1, tm // LANES):
            part = part + chosen[g][:, c * LANES:(c + 1) * LANES]
        cnt_ref[rows, :] = cnt_ref[rows, :] + part


def _router(h_all, wr_t, bias_b, tm):
    n = h_all.shape[0] // SUBLANES
    col = lambda r: pl.BlockSpec((r, tm), lambda i: (0, i))
    return pl.pallas_call(
        _router_kernel,
        grid=(n // tm,),
        in_specs=[pl.BlockSpec((tm * SUBLANES, LANES), lambda i: (i, 0)),
                  pl.BlockSpec((N_EXPERTS, D_MODEL), lambda i: (0, 0)),
                  pl.BlockSpec((N_EXPERTS, LANES), lambda i: (0, 0))],
        out_specs=[col(TOP_K), col(TOP_K), col(N_EXPERTS), pl.BlockSpec((N_EXPERTS, LANES), lambda i: (0, 0))],
        out_shape=[jax.ShapeDtypeStruct((TOP_K, n), I32), jax.ShapeDtypeStruct((TOP_K, n), F32),
                   jax.ShapeDtypeStruct((N_EXPERTS, n), F32), jax.ShapeDtypeStruct((N_EXPERTS, LANES), F32)],
        compiler_params=_cparams(("arbitrary",)),
        name="router",
    )(h_all, wr_t, bias_b)


def _rank_kernel(mask_ref, e_ref, offs_ref, triu_ref, dest_ref, carry):
    tm = mask_ref.shape[1]

    @pl.when(pl.program_id(0) == 0)
    def _():
        carry[...] = jnp.zeros_like(carry)

    mk = mask_ref[...]
    rank = _dot(mk.astype(BF16), triu_ref[...])
    dest_full = rank + (offs_ref[...][:, 0:1] + carry[...][:, 0:1])
    rowi = lax.broadcasted_iota(I32, (N_EXPERTS, tm), 0)
    for j in range(TOP_K):
        d = jnp.sum(jnp.where(rowi == e_ref[j:j + 1, :], dest_full, 0.0), axis=0, keepdims=True)
        dest_ref[j:j + 1, :] = d.astype(I32)
    carry[...] = carry[...] + jnp.sum(mk, axis=1, keepdims=True)


def _rank(mask_t, e_t, offs_b, tm):
    n = mask_t.shape[1]
    r = np.arange(tm)
    triu = jnp.asarray((r[:, None] < r[None, :]).astype(np.float32), BF16)
    return pl.pallas_call(
        _rank_kernel,
        grid=(n // tm,),
        in_specs=[pl.BlockSpec((N_EXPERTS, tm), lambda i: (0, i)),
                  pl.BlockSpec((TOP_K, tm), lambda i: (0, i)),
                  pl.BlockSpec((N_EXPERTS, LANES), lambda i: (0, 0)),
                  pl.BlockSpec((tm, tm), lambda i: (0, 0))],
        out_specs=pl.BlockSpec((TOP_K, tm), lambda i: (0, i)),
        out_shape=jax.ShapeDtypeStruct((TOP_K, n), I32),
        scratch_shapes=[pltpu.VMEM((N_EXPERTS, LANES), F32)],
        compiler_params=_cparams(("arbitrary",)),
        name="rank",
    )(mask_t, e_t, offs_b, triu)


INV_COLS = 512
TOKEN_RADIX = 128


def _inverse_kernel(dest_ref, inv_ref):
    tm = dest_ref.shape[1]
    nq = inv_ref.shape[0]

    @pl.when(pl.program_id(0) == 0)
    def _():
        inv_ref[...] = jnp.zeros_like(inv_ref)

    tok = pl.program_id(0) * tm + lax.broadcasted_iota(I32, (1, tm), 1)
    t_hi = jnp.right_shift(tok, TOKEN_RADIX.bit_length() - 1).astype(F32)
    t_lo = jnp.bitwise_and(tok, TOKEN_RADIX - 1).astype(F32)
    qi = lax.broadcasted_iota(I32, (nq, tm), 0)
    si = lax.broadcasted_iota(I32, (INV_COLS, tm), 0)
    acc_hi = jnp.zeros(inv_ref.shape, F32)
    acc_lo = jnp.zeros(inv_ref.shape, F32)
    for j in range(TOP_K):
        d = dest_ref[j:j + 1, :]
        at_q = qi == jnp.right_shift(d, INV_COLS.bit_length() - 1)
        col = jnp.where(si == jnp.bitwise_and(d, INV_COLS - 1), 1.0, 0.0).astype(BF16)
        acc_hi = acc_hi + _dot_nt(jnp.where(at_q, t_hi, 0.0).astype(BF16), col)
        acc_lo = acc_lo + _dot_nt(jnp.where(at_q, t_lo, 0.0).astype(BF16), col)
    inv_ref[...] = inv_ref[...] + (acc_hi * float(TOKEN_RADIX) + acc_lo)


def _inverse_map(dest_t, tm):
    n = dest_t.shape[1]
    n_rows = n * TOP_K
    nq = -(-(n_rows // INV_COLS) // 8) * 8
    inv = pl.pallas_call(
        _inverse_kernel,
        grid=(n // tm,),
        in_specs=[pl.BlockSpec((TOP_K, tm), lambda i: (0, i))],
        out_specs=pl.BlockSpec((nq, INV_COLS), lambda i: (0, 0)),
        out_shape=jax.ShapeDtypeStruct((nq, INV_COLS), F32),
        compiler_params=_cparams(("arbitrary",)),
        name="inverse_map",
    )(dest_t)
    return inv.reshape(-1)[:n_rows].astype(I32).reshape(n_rows // EXPERT_ROW_TILE, 1, EXPERT_ROW_TILE)


DMA_THREADS = 2


def _token_copy(src_ref, token, dst_ref, dst_row, sem):
    return pltpu.make_async_copy(src_ref.at[pl.ds(pl.multiple_of(token * SUBLANES, SUBLANES), SUBLANES)],
                                 dst_ref.at[pl.ds(dst_row * SUBLANES, SUBLANES)], sem)


GATHER_AHEAD = 2
GATHER_SLOTS = GATHER_AHEAD + 1


def _expert_kernel(n_tiles, tile_ref, exp_ref, valid_ref, first_ref, newexp_ref, lo_ref, hi_ref,
                   inv0_ref, inv1_ref, inva_ref, h_ref, wg_ref, wu_ref, wd_ref, y_ref, xbuf, wgb, wub, wdb, sem):
    i = pl.program_id(0)
    rows = xbuf.shape[1] // SUBLANES
    tile = tile_ref[i]
    slot = tile % GATHER_SLOTS
    ahead_slot = (tile + GATHER_AHEAD) % GATHER_SLOTS
    valid = valid_ref[i] == 1
    first = first_ref[i] == 1

    def wait_rows(s):
        pltpu.make_async_copy(h_ref.at[pl.ds(0, rows * SUBLANES)], xbuf.at[s], sem.at[s]).wait()

    def gather_loop(idx_ref, s):
        def issue(k, c):
            for u in range(SUBLANES):
                r = k * SUBLANES + u
                pltpu.make_async_copy(
                    h_ref.at[pl.ds(pl.multiple_of(idx_ref[0, 0, r] * SUBLANES, SUBLANES), SUBLANES)],
                    xbuf.at[s, pl.ds(pl.multiple_of(r * SUBLANES, SUBLANES), SUBLANES)], sem.at[s]).start()
            return c

        lax.fori_loop(0, rows // SUBLANES, issue, 0)

    @pl.when(valid & first)
    def _():
        @pl.when(i == 0)
        def _():
            gather_loop(inv0_ref, 0)
            gather_loop(inv1_ref, 1)

        wait_rows(slot)

    @pl.when(valid & (newexp_ref[i] == 1))
    def _():
        wgb[...] = wg_ref[0].astype(BF16)
        wub[...] = wu_ref[0].astype(BF16)
        wdb[...] = wd_ref[0].astype(BF16)

    def compute(first_visit):
        xb = _load_tiled(xbuf.at[slot], rows).astype(BF16)
        if first_visit:
            for r in range(rows):
                _token_copy(h_ref, inva_ref[0, 0, r], xbuf.at[ahead_slot], r, sem.at[ahead_slot]).start(
                    priority=r % DMA_THREADS)
        gate = _dot(xb, wgb[...])
        up = _dot(xb, wub[...])
        y = _dot((_silu(gate) * up).astype(BF16), wdb[...])
        rowi = lax.broadcasted_iota(I32, y.shape, 0)
        mine = (rowi >= lo_ref[i]) & (rowi < hi_ref[i])
        _store_tiled(y_ref, jnp.where(mine, y, 0.0 if first_visit else _load_tiled(y_ref, rows)))

    @pl.when(valid & first)
    def _():
        compute(True)

    @pl.when(valid & jnp.logical_not(first))
    def _():
        compute(False)

    @pl.when(valid & first & (tile >= n_tiles - GATHER_AHEAD))
    def _():
        wait_rows(ahead_slot)


def _group_metadata(counts, n_rows):
    tmo = EXPERT_ROW_TILE
    n_tiles = n_rows // tmo
    ends = jnp.cumsum(counts)
    offs = ends - counts
    first_tile = offs // tmo
    n_t = jnp.where(counts > 0, (ends - 1) // tmo - first_tile + 1, 0)
    cum = jnp.cumsum(n_t)
    base = cum - n_t
    n_items = n_tiles + N_EXPERTS
    idx = jnp.arange(n_items, dtype=I32)
    valid = (idx < cum[-1]).astype(I32)
    idc = jnp.minimum(idx, cum[-1] - 1)
    e = jnp.minimum(jnp.sum((cum[None, :] <= idc[:, None]).astype(I32), axis=1), N_EXPERTS - 1)
    tile = (first_tile[e] + idc - base[e]).astype(I32)
    one = jnp.ones((1,), I32)
    first = jnp.concatenate([one, (tile[1:] != tile[:-1]).astype(I32)])
    new_expert = jnp.concatenate([one, (e[1:] != e[:-1]).astype(I32)])
    lo = jnp.clip(offs[e] - tile * tmo, 0, tmo).astype(I32)
    hi = jnp.clip(ends[e] - tile * tmo, 0, tmo).astype(I32)
    return tile, e, valid, first, new_expert, lo, hi


def _experts(l, meta, inv3, h_all, wg, wu, wd):
    tmo = EXPERT_ROW_TILE
    n_rows = inv3.shape[0] * tmo
    n_items = meta[0].shape[0]
    wspec = lambda shp: pl.BlockSpec((None, 1) + shp, lambda i, tl, ex, *_: (l, ex[i], 0, 0))
    n_tiles = inv3.shape[0]
    assert n_tiles > GATHER_AHEAD
    idx_spec = lambda nxt: pl.BlockSpec(
        (1, 1, tmo), lambda i, tl, *_: (jnp.minimum(tl[i] + nxt, n_tiles - 1), 0, 0), memory_space=pltpu.SMEM)
    return pl.pallas_call(
        functools.partial(_expert_kernel, n_tiles),
        grid_spec=pltpu.PrefetchScalarGridSpec(
            num_scalar_prefetch=7,
            grid=(n_items,),
            in_specs=[idx_spec(0), idx_spec(1), idx_spec(GATHER_AHEAD), pl.BlockSpec(memory_space=pl.ANY),
                      wspec((D_MODEL, D_EXPERT)), wspec((D_MODEL, D_EXPERT)), wspec((D_EXPERT, D_MODEL))],
            out_specs=pl.BlockSpec((tmo * SUBLANES, LANES), lambda i, tl, *_: (tl[i], 0)),
            scratch_shapes=[pltpu.VMEM((GATHER_SLOTS, tmo * SUBLANES, LANES), F32),
                            pltpu.VMEM((D_MODEL, D_EXPERT), BF16), pltpu.VMEM((D_MODEL, D_EXPERT), BF16),
                            pltpu.VMEM((D_EXPERT, D_MODEL), BF16),
                            pltpu.SemaphoreType.DMA((GATHER_SLOTS,))]),
        out_shape=jax.ShapeDtypeStruct((n_rows * SUBLANES, LANES), F32),
        compiler_params=_cparams(("arbitrary",)),
        name="experts",
    )(*meta, inv3, inv3, inv3, h_all, wg, wu, wd)


def _post_kernel(n_steps, dest0_ref, dest1_ref, desta_ref, h_ref, wt_ref, y_ref, wsg_ref, wsu_ref, wsd_ref,
                 g_ref, b_ref, out_ref, *scratch):
    bufs, sem = scratch[:GATHER_SLOTS], scratch[GATHER_SLOTS]
    i = pl.program_id(0)
    tm = out_ref.shape[0]

    def wait_rows(s):
        for j in range(TOP_K):
            pltpu.make_async_copy(y_ref.at[pl.ds(0, tm * SUBLANES)], bufs[s].at[j], sem.at[s]).wait()

    def gather_loop(idx_ref, s):
        def issue(k, c):
            for u in range(SUBLANES):
                t = k * SUBLANES + u
                for j in range(TOP_K):
                    pltpu.make_async_copy(
                        y_ref.at[pl.ds(pl.multiple_of(idx_ref[j, t] * SUBLANES, SUBLANES), SUBLANES)],
                        bufs[s].at[j, pl.ds(pl.multiple_of(t * SUBLANES, SUBLANES), SUBLANES)], sem.at[s]).start()
            return c

        lax.fori_loop(0, tm // SUBLANES, issue, 0)

    @pl.when(i == 0)
    def _():
        gather_loop(dest0_ref, 0)
        gather_loop(dest1_ref, 1)

    def step(s):
        a = (s + GATHER_AHEAD) % GATHER_SLOTS
        wait_rows(s)
        h = _load_tiled(h_ref, tm)
        hb = h.astype(BF16)
        for t in range(tm):
            for j in range(TOP_K):
                _token_copy(y_ref, desta_ref[j, t], bufs[a].at[j], t, sem.at[a]).start(priority=j % DMA_THREADS)
        shared = _dot((_silu(_dot(hb, wsg_ref[...])) * _dot(hb, wsu_ref[...])).astype(BF16), wsd_ref[...])
        wt = wt_ref[...]
        routed = _load_tiled(bufs[s].at[0], tm) * wt[:, 0:1]
        for j in range(1, TOP_K):
            routed = routed + _load_tiled(bufs[s].at[j], tm) * wt[:, j:j + 1]
        out_ref[...] = _layer_norm(DN_ALPHA * h + (routed + shared), g_ref[...], b_ref[...])

        @pl.when(i >= n_steps - GATHER_AHEAD)
        def _():
            wait_rows(a)

    for s in range(GATHER_SLOTS):
        pl.when(i % GATHER_SLOTS == s)(functools.partial(step, s))


def _post(dest_t, h_all, w_tok, y, wsg, wsu, wsd, g, b, tm):
    n = h_all.shape[0] // SUBLANES
    n_steps = n // tm
    assert n_steps > GATHER_AHEAD
    const = lambda a: pl.BlockSpec(a.shape, lambda i: (0, 0))
    idx_spec = lambda nxt: pl.BlockSpec((TOP_K, tm), lambda i: (0, jnp.minimum(i + nxt, n_steps - 1)),
                                        memory_space=pltpu.SMEM)
    return pl.pallas_call(
        functools.partial(_post_kernel, n_steps),
        grid=(n_steps,),
        in_specs=[idx_spec(0), idx_spec(1), idx_spec(GATHER_AHEAD),
                  pl.BlockSpec((tm * SUBLANES, LANES), lambda i: (i, 0)),
                  pl.BlockSpec((tm, TOP_K), lambda i: (i, 0)),
                  pl.BlockSpec(memory_space=pl.ANY),
                  const(wsg), const(wsu), const(wsd), const(g), const(b)],
        out_specs=pl.BlockSpec((tm, D_MODEL), lambda i: (i, 0)),
        out_shape=jax.ShapeDtypeStruct((n, D_MODEL), F32),
        scratch_shapes=[pltpu.VMEM((TOP_K, tm * SUBLANES, LANES), F32) for _ in range(GATHER_SLOTS)]
        + [pltpu.SemaphoreType.DMA((GATHER_SLOTS,))],
        compiler_params=_cparams(("arbitrary",)),
        name="moe_post",
    )(dest_t, dest_t, dest_t, h_all, w_tok, y, wsg, wsu, wsd, g, b)


def _moe_layer(l, h_all, prm):
    tm = MOE_ROW_TILE
    e_t, w_t, mask_t, cnt = _router(h_all, prm["wr_t"][l], prm["rbias"][l], tm)
    counts = jnp.sum(cnt, axis=1).astype(I32)
    offs = jnp.cumsum(counts) - counts
    offs_b = jnp.broadcast_to(offs.astype(F32)[:, None], (N_EXPERTS, LANES))
    dest_t = _rank(mask_t, e_t, offs_b, tm)
    inv3 = _inverse_map(dest_t, tm)
    meta = _group_metadata(counts, h_all.shape[0] // SUBLANES * TOP_K)
    y = _experts(l, meta, inv3, h_all, prm["w_exp_gate"], prm["w_exp_up"], prm["w_exp_down"])
    return _post(dest_t, h_all, w_t.T, y, prm["wsg"][l], prm["wsu"][l], prm["wsd"][l],
                 prm["ln2_g"][l], prm["ln2_b"][l], POST_ROW_TILE)


def kernel(x_prompt, x_sample, cache_k, cache_v, state_hgrn, w_in, hgrn_lower_bounds, hgrn_norm_g, attn_sinks, w_branch_a, w_branch_b, w_out, ln1_g, ln1_b, w_router, router_bias, w_exp_gate, w_exp_up, w_exp_down, w_sh_gate, w_sh_up, w_sh_down, ln2_g, ln2_b):
    bsz, t, d = x_prompt.shape
    n_sample = x_sample.shape[0] * x_sample.shape[1]
    n_prompt = bsz * t
    depth = w_in.shape[0]
    assert d == D_MODEL and x_sample.shape[1] == 1 and n_prompt % ROW_TILE == 0 and t % ROW_TILE == 0
    assert n_prompt % n_sample == 0 and (n_prompt + n_sample) % MOE_ROW_TILE == 0
    assert (n_prompt + n_sample) % POST_ROW_TILE == 0
    assert ((n_prompt + n_sample) * TOP_K) % EXPERT_ROW_TILE == 0 and n_sample % SAMPLE_BLOCK == 0

    lb_prob = jax.nn.softmax(hgrn_lower_bounds.astype(F32), axis=0)
    lower = (jnp.cumsum(lb_prob, axis=0) - lb_prob[0])[:, None, :]
    cos_p, sin_p = _rope_tables(jnp.arange(t))
    cos_s, sin_s = _rope_tables(jnp.full((n_sample,), PAST_LEN))
    row = lambda a: a[:, None, :]
    prm = dict(
        w_in=w_in.astype(BF16), lower=lower, cos_p=cos_p, sin_p=sin_p, cos_s=cos_s, sin_s=sin_s,
        norm_g=row(hgrn_norm_g), sinks=attn_sinks,
        sinks8=jnp.broadcast_to(attn_sinks[:, :, None], (depth, B_Q_HEADS, LANES)),
        wa=w_branch_a.astype(BF16), wb=w_branch_b.astype(BF16), wo=w_out.astype(BF16),
        ln1_g=row(ln1_g), ln1_b=row(ln1_b), ln2_g=row(ln2_g), ln2_b=row(ln2_b),
        wr_t=jnp.swapaxes(w_router, 1, 2),
        rbias=jnp.broadcast_to(router_bias[:, :, None], (depth, N_EXPERTS, LANES)),
        w_exp_gate=w_exp_gate, w_exp_up=w_exp_up, w_exp_down=w_exp_down,
        wsg=w_sh_gate.astype(BF16), wsu=w_sh_up.astype(BF16), wsd=w_sh_down.astype(BF16),
        state=state_hgrn,
        cache_k=cache_k.reshape(depth, n_sample, WINDOW, B_KV_WIDTH),
        cache_v=cache_v.reshape(depth, n_sample, WINDOW, B_KV_WIDTH),
    )

    xp, xs, xs_off = x_prompt.reshape(n_prompt, d), x_sample.reshape(n_sample, d), 0
    per_layer = []
    for l in range(depth):
        h_all, outs = _mixer_layer(l, xp, xs, xs_off, n_prompt, bsz, t, n_sample, prm)
        y_all = _moe_layer(l, h_all, prm)
        per_layer.append(outs)
        xp, xs, xs_off = y_all, y_all, n_prompt // n_sample

    kv_shape = (n_sample, WINDOW, B_KV_HEADS, B_HEAD_DIM)
    stack = lambda k, f=lambda a: a: jnp.stack([f(o[k]) for o in per_layer])
    return (y_all[:n_prompt].reshape(bsz, t, d), y_all[n_prompt:].reshape(n_sample, 1, d),
            stack(0), stack(1), stack(2),
            stack(3, lambda a: a.reshape(kv_shape)), stack(4, lambda a: a.reshape(kv_shape)), stack(5))
```

```python
import functools

import numpy as np
import jax
import jax.numpy as jnp
from jax import lax
from jax.experimental import pallas as pl
from jax.experimental.pallas import tpu as pltpu

F32 = jnp.float32
BF16 = jnp.bfloat16
I32 = jnp.int32

D_MODEL = 1024
DEPTH = 2
PAST_LEN = 16384
A_HEADS = 4
A_DK = 128
A_DV = 128
A_KEY = A_HEADS * A_DK
A_WIDTH = A_HEADS * A_DV
B_Q_HEADS = 8
B_KV_HEADS = 2
B_HEAD_DIM = 64
B_GROUP = B_Q_HEADS // B_KV_HEADS
B_WIDTH = B_Q_HEADS * B_HEAD_DIM
B_KV_WIDTH = B_KV_HEADS * B_HEAD_DIM
WINDOW = 128
ROPE_THETA = 10000.0
ATTN_SCALE = B_HEAD_DIM ** -0.5
N_EXPERTS = 64
TOP_K = 8
N_GROUPS = 8
GROUP_SIZE = N_EXPERTS // N_GROUPS
TOPK_GROUPS = 4
D_EXPERT = D_MODEL // 4
D_SHARED = D_EXPERT
ROUTED_SCALE = 2.5
DN_ALPHA = (2 * DEPTH) ** 0.25
LN_EPS = 1e-5
RMS_EPS = 1e-6
NEG_BIG = -1e30
TINY = 1.1754944e-38
OFF_AF = A_KEY
OFF_AI = 2 * A_KEY
OFF_AG = OFF_AI + A_WIDTH
OFF_BQ = OFF_AG + A_WIDTH
OFF_BK = OFF_BQ + B_WIDTH
OFF_BV = OFF_BK + B_KV_WIDTH
OFF_GA = OFF_BV + B_KV_WIDTH
OFF_GB = OFF_GA + D_MODEL
IN_COLS = OFF_GB + D_MODEL

LANES = 128
HGRN_CHUNK = 128
HGRN_LEVELS = 7
ROW_TILE = 512
MOE_ROW_TILE = 384
EXPERT_ROW_TILE = 512
POST_ROW_TILE = 128
VMEM_LIMIT = 56 * 1024 * 1024


def _cparams(sem, vmem=VMEM_LIMIT):
    return pltpu.CompilerParams(dimension_semantics=sem, vmem_limit_bytes=vmem)


def _dot(a, b):
    return jnp.dot(a, b, preferred_element_type=F32)


def _dot_nt(a, b):
    return lax.dot_general(a, b, (((1,), (1,)), ((), ())), preferred_element_type=F32)


def _dot_tn(a, b):
    return lax.dot_general(a, b, (((0,), (0,)), ((), ())), preferred_element_type=F32)


def _sigmoid(x):
    return 1.0 / (1.0 + jnp.exp(-x))


def _silu(x):
    return x * _sigmoid(x)


def _split3(x):
    hi = x.astype(BF16)
    r1 = x - hi.astype(F32)
    mid = r1.astype(BF16)
    lo = (r1 - mid.astype(F32)).astype(BF16)
    return hi, mid, lo


def _layer_norm(y, g, b):
    mu = jnp.mean(y, axis=-1, keepdims=True)
    d = y - mu
    var = jnp.mean(d * d, axis=-1, keepdims=True)
    return d * lax.rsqrt(var + LN_EPS) * g + b


SUBLANES = 8
assert D_MODEL == SUBLANES * LANES


def _store_tiled(ref, val):
    rows = val.shape[0]
    for c in range(SUBLANES):
        ref[pl.ds(c, rows, stride=SUBLANES), :] = val[:, c * LANES:(c + 1) * LANES]


def _load_tiled(ref, rows):
    return jnp.concatenate([ref[pl.ds(c, rows, stride=SUBLANES), :] for c in range(SUBLANES)], axis=1)


def _log_forget(af, lower):
    ls = jnp.minimum(af, 0.0) - jnp.log1p(jnp.exp(-jnp.abs(af)))
    a = jnp.log(jnp.maximum(lower, TINY))
    b = jnp.log1p(-lower) + ls
    mixed = jnp.maximum(a, b) + jnp.log1p(jnp.exp(-jnp.abs(a - b)))
    return jnp.where(lower > 0.0, mixed, ls)


def _proj_kernel(x_ref, w_ref, cos_ref, sin_ref, low_ref,
                 qa_ref, lf_ref, va_ref, ga_ref, qb_ref, kb_ref, vb_ref, sga_ref, sgb_ref):
    xb = x_ref[...].astype(BF16)

    def mm(c0, n):
        return _dot(xb, w_ref[:, c0:c0 + n])

    qa_ref[...] = _silu(mm(0, A_KEY)).astype(BF16)
    lf_ref[...] = _log_forget(mm(OFF_AF, A_KEY), low_ref[...])
    va_ref[...] = mm(OFF_AI, A_WIDTH).astype(BF16)
    ga_ref[...] = _silu(mm(OFF_AG, A_WIDTH)).astype(BF16)

    cos = cos_ref[...]
    sin = sin_ref[...]
    lane = lax.broadcasted_iota(I32, cos.shape, 1)
    first_half = (lane & (B_HEAD_DIM // 2)) == 0

    def rope(blk):
        partner = jnp.where(first_half, pltpu.roll(blk, LANES - B_HEAD_DIM // 2, 1),
                            pltpu.roll(blk, B_HEAD_DIM // 2, 1))
        return blk * cos + partner * sin

    bq = mm(OFF_BQ, B_WIDTH)
    for j in range(B_WIDTH // LANES):
        sl = slice(j * LANES, (j + 1) * LANES)
        qb_ref[:, sl] = (rope(bq[:, sl]) * ATTN_SCALE).astype(BF16)
    kb_ref[...] = rope(mm(OFF_BK, B_KV_WIDTH))
    vb_ref[...] = mm(OFF_BV, B_KV_WIDTH)
    sga_ref[...] = _sigmoid(mm(OFF_GA, D_MODEL)).astype(BF16)
    sgb_ref[...] = _sigmoid(mm(OFF_GB, D_MODEL)).astype(BF16)


def _proj(x, w_bf, cos_t, sin_t, lower, n_rows, row_off_blocks, tm):
    nt = n_rows // tm
    tab_blocks = cos_t.shape[0] // tm
    row = lambda w: pl.BlockSpec((tm, w), lambda i: (i, 0))
    outs = [(A_KEY, BF16), (A_KEY, F32), (A_WIDTH, BF16), (A_WIDTH, BF16), (B_WIDTH, BF16),
            (B_KV_WIDTH, F32), (B_KV_WIDTH, F32), (D_MODEL, BF16), (D_MODEL, BF16)]
    return pl.pallas_call(
        _proj_kernel,
        grid=(nt,),
        in_specs=[pl.BlockSpec((tm, D_MODEL), lambda i: (i + row_off_blocks, 0)),
                  pl.BlockSpec((D_MODEL, IN_COLS), lambda i: (0, 0)),
                  pl.BlockSpec((tm, LANES), lambda i: (i % tab_blocks, 0)),
                  pl.BlockSpec((tm, LANES), lambda i: (i % tab_blocks, 0)),
                  pl.BlockSpec((1, A_KEY), lambda i: (0, 0))],
        out_specs=[row(w) for w, _ in outs],
        out_shape=[jax.ShapeDtypeStruct((n_rows, w), dt) for w, dt in outs],
        compiler_params=_cparams(("arbitrary",)),
        name="proj",
    )(x, w_bf, cos_t, sin_t, lower)


def _rope_tables(pos):
    half = B_HEAD_DIM // 2
    inv = ROPE_THETA ** (-jnp.arange(half, dtype=F32) / half)
    ang = pos.astype(F32)[:, None] * inv[None, :]
    cos = jnp.cos(ang)
    sin = jnp.sin(ang)
    reps = LANES // B_HEAD_DIM
    cos_t = jnp.tile(jnp.concatenate([cos, cos], axis=1), (1, reps))
    sin_t = jnp.tile(jnp.concatenate([-sin, sin], axis=1), (1, reps))
    return cos_t, sin_t


def _hgrn_constants():
    c = HGRN_CHUNK
    r = np.arange(c)
    tri = (r[None, :] <= r[:, None]).astype(np.float32)
    upper = np.zeros((HGRN_LEVELS, c, A_KEY), np.float32)
    pair = np.zeros((HGRN_LEVELS + 1, c, c), np.float32)
    for l in range(HGRN_LEVELS):
        b = c >> (l + 1)
        up = (r % (2 * b)) >= b
        upper[l] = up[:, None]
        same = (r[:, None] // (2 * b)) == (r[None, :] // (2 * b))
        pair[l] = (up[:, None] & ~up[None, :] & same)
    pair[HGRN_LEVELS] = np.eye(c)
    return jnp.asarray(tri, BF16), jnp.asarray(upper), jnp.asarray(pair)


def _hgrn_kernel(qa_ref, lf_ref, va_ref, ga_ref, g_ref, tri_ref, up_ref, pair_ref,
                 oa_ref, st_ref, s_scr):
    c = HGRN_CHUNK
    step = pl.program_id(1)

    @pl.when(step == 0)
    def _():
        s_scr[...] = jnp.zeros_like(s_scr)

    lf = lf_ref[...]
    tri = tri_ref[...]
    hi, mid, lo = _split3(lf)
    gcum = _dot(tri, hi) + _dot(tri, mid) + _dot(tri, lo)
    qb = qa_ref[...]
    qf = qb.astype(F32)
    kf = 1.0 - jnp.exp(lf)
    kb = kf.astype(BF16)
    vb = va_ref[...]
    ws = []
    g8 = gcum.reshape(c // SUBLANES, SUBLANES, A_KEY)
    sub = lax.broadcasted_iota(I32, g8.shape, 1)
    for l in range(HGRN_LEVELS):
        half = c >> (l + 1)
        if 2 * half >= SUBLANES:
            g3 = gcum.reshape(c // (2 * half), 2 * half, A_KEY)
            gref_l = jnp.broadcast_to(g3[:, half - 1:half, :], g3.shape).reshape(c, A_KEY)
        else:
            gref_g = jnp.broadcast_to(g8[:, half - 1:half, :], g8.shape)
            for first in range(2 * half, SUBLANES, 2 * half):
                pick = jnp.broadcast_to(g8[:, first + half - 1:first + half, :], g8.shape)
                gref_g = jnp.where(sub >= first, pick, gref_g)
            gref_l = gref_g.reshape(c, A_KEY)
        e = jnp.exp(-jnp.abs(gcum - gref_l))
        ws.append((jnp.where(up_ref[l] > 0.5, qf, kf) * e).astype(BF16))
    qg = (qf * jnp.exp(gcum)).astype(BF16)
    gend = gcum[c - 1:c, :]
    kend = (kf * jnp.exp(gend - gcum)).astype(BF16)
    decay = jnp.exp(gend)
    gate = g_ref[...]

    for h in range(A_HEADS):
        sl = slice(h * A_DK, (h + 1) * A_DK)
        att = _dot_nt(qb[:, sl], kb[:, sl]) * pair_ref[HGRN_LEVELS]
        for l in range(HGRN_LEVELS):
            w = ws[l][:, sl]
            att = att + _dot_nt(w, w) * pair_ref[l]
        s_t = s_scr[h]
        o = _dot_nt(qg[:, sl], s_t.astype(BF16)) + _dot(att.astype(BF16), vb[:, sl])
        ms = jnp.mean(o * o, axis=-1, keepdims=True)
        on = o * lax.rsqrt(ms + RMS_EPS) * gate * ga_ref[:, sl].astype(F32)
        oa_ref[:, sl] = on.astype(BF16)
        s_scr[h] = s_t * decay[:, sl] + _dot_tn(vb[:, sl], kend[:, sl])

    @pl.when(step == pl.num_programs(1) - 1)
    def _():
        for h in range(A_HEADS):
            st_ref[0, h] = s_scr[h].T


def _hgrn_prompt(qa, lf, va, ga, norm_g, bsz, t):
    c = HGRN_CHUNK
    nc = t // c
    tri, upper, pair = _hgrn_constants()
    blk = lambda: pl.BlockSpec((c, A_KEY), lambda b, i: (b * nc + i, 0))
    const = lambda a: pl.BlockSpec(a.shape, lambda b, i: (0,) * a.ndim)
    return pl.pallas_call(
        _hgrn_kernel,
        grid=(bsz, nc),
        in_specs=[blk(), blk(), blk(), blk(), pl.BlockSpec((1, A_DV), lambda b, i: (0, 0)),
                  const(tri), const(upper), const(pair)],
        out_specs=[blk(), pl.BlockSpec((1, A_HEADS, A_DK, A_DV), lambda b, i: (b, 0, 0, 0))],
        out_shape=[jax.ShapeDtypeStruct((bsz * t, A_WIDTH), BF16),
                   jax.ShapeDtypeStruct((bsz, A_HEADS, A_DK, A_DV), F32)],
        scratch_shapes=[pltpu.VMEM((A_HEADS, A_DV, A_DK), F32)],
        compiler_params=_cparams(("arbitrary", "arbitrary")),
        name="hgrn_prompt",
    )(qa, lf, va, ga, norm_g, tri, upper, pair)


def _swa_kernel(sink_ref, q_ref, k_ref, v_ref, o_ref, kprev, vprev):
    w = WINDOW
    i = pl.program_id(1)

    @pl.when(i == 0)
    def _():
        kprev[...] = jnp.zeros_like(kprev)
        vprev[...] = jnp.zeros_like(vprev)

    kc = k_ref[...]
    vc = v_ref[...]
    kk = jnp.concatenate([kprev[...], kc], axis=0)
    vv = jnp.concatenate([vprev[...], vc], axis=0)
    kr = pltpu.roll(kk, B_HEAD_DIM, 1)
    vr = pltpu.roll(vv, B_HEAD_DIM, 1)
    lo2 = lax.broadcasted_iota(I32, kk.shape, 1) < B_HEAD_DIM
    zero = jnp.zeros_like(kk)
    k_lo = [jnp.where(lo2, kk, zero).astype(BF16), jnp.where(lo2, kr, zero).astype(BF16)]
    k_hi = [jnp.where(lo2, zero, kr).astype(BF16), jnp.where(lo2, zero, kk).astype(BF16)]
    v_dup = [jnp.where(lo2, vv, vr).astype(BF16), jnp.where(lo2, vr, vv).astype(BF16)]

    qi = lax.broadcasted_iota(I32, (w, 2 * w), 0)
    kj = lax.broadcasted_iota(I32, (w, 2 * w), 1)
    valid = (kj >= qi) & (kj <= qi + w) & ((kj >= w) | (i > 0))
    lo1 = lax.broadcasted_iota(I32, (w, LANES), 1) < B_HEAD_DIM

    for j in range(B_WIDTH // LANES):
        g = (2 * j) // B_GROUP
        qblk = q_ref[:, j * LANES:(j + 1) * LANES]
        res = []
        for half, kmat in enumerate((k_lo[g], k_hi[g])):
            sk = sink_ref[2 * j + half]
            s = jnp.where(valid, _dot_nt(qblk, kmat), NEG_BIG)
            m = jnp.maximum(jnp.max(s, axis=-1, keepdims=True), sk)
            p = jnp.exp(s - m)
            denom = jnp.sum(p, axis=-1, keepdims=True) + jnp.exp(sk - m)
            res.append(_dot(p.astype(BF16), v_dup[g]) / denom)
        o_ref[:, j * LANES:(j + 1) * LANES] = jnp.where(lo1, res[0], res[1]).astype(BF16)

    kprev[...] = kc
    vprev[...] = vc


def _swa_prompt(sinks, qb, kb, vb, bsz, t):
    w = WINDOW
    nb = t // w
    return pl.pallas_call(
        _swa_kernel,
        grid_spec=pltpu.PrefetchScalarGridSpec(
            num_scalar_prefetch=1,
            grid=(bsz, nb),
            in_specs=[pl.BlockSpec((w, B_WIDTH), lambda b, i, s: (b * nb + i, 0)),
                      pl.BlockSpec((w, B_KV_WIDTH), lambda b, i, s: (b * nb + i, 0)),
                      pl.BlockSpec((w, B_KV_WIDTH), lambda b, i, s: (b * nb + i, 0))],
            out_specs=pl.BlockSpec((w, B_WIDTH), lambda b, i, s: (b * nb + i, 0)),
            scratch_shapes=[pltpu.VMEM((w, B_KV_WIDTH), F32), pltpu.VMEM((w, B_KV_WIDTH), F32)]),
        out_shape=jax.ShapeDtypeStruct((bsz * t, B_WIDTH), BF16),
        compiler_params=_cparams(("arbitrary", "arbitrary")),
        name="swa_prompt",
    )(sinks, qb, kb, vb)


SAMPLE_BLOCK = 8


def _sample_kernel(sink_ref, st_ref, lft_ref, qat_ref, va_ref, ga_ref, g_ref, q3_ref, kn_ref, vn_ref,
                   ck_ref, cv_ref, st_out, oa_ref, ob_ref, ck_out, cv_out, o_scr):
    w = WINDOW
    row = lax.broadcasted_iota(I32, (w, B_KV_WIDTH), 0)
    for i in range(SAMPLE_BLOCK):
        for h in range(A_HEADS):
            sl = slice(h * A_DV, (h + 1) * A_DV)
            fcol = jnp.exp(lft_ref[h, 0][:, i:i + 1])
            qcol = qat_ref[h, 0][:, i:i + 1]
            vrow = va_ref[i:i + 1, sl].astype(F32)
            s_new = st_ref[i, h] * fcol + (1.0 - fcol) * vrow
            st_out[i, h] = s_new
            o_scr[i:i + 1, sl] = jnp.sum(s_new * qcol, axis=0, keepdims=True)
        kc = ck_ref[i]
        vc = cv_ref[i]
        kn = kn_ref[i:i + 1, :]
        vn = vn_ref[i:i + 1, :]
        q3 = q3_ref[i]
        s = _dot_nt(q3.astype(BF16), kc.astype(BF16))
        s_new_key = jnp.sum(q3 * kn, axis=-1, keepdims=True)
        sk = sink_ref[...][:, 0:1]
        m = jnp.maximum(jnp.maximum(jnp.max(s, axis=-1, keepdims=True), s_new_key), sk)
        p = jnp.exp(s - m)
        pn = jnp.exp(s_new_key - m)
        denom = jnp.sum(p, axis=-1, keepdims=True) + pn + jnp.exp(sk - m)
        ob_ref[i] = (_dot(p.astype(BF16), vc.astype(BF16)) + pn * vn) / denom
        ck_out[i] = jnp.where(row == w - 1, kn, pltpu.roll(kc, w - 1, 0))
        cv_out[i] = jnp.where(row == w - 1, vn, pltpu.roll(vc, w - 1, 0))
    for h in range(A_HEADS):
        sl = slice(h * A_DV, (h + 1) * A_DV)
        o = o_scr[:, sl]
        ms = jnp.mean(o * o, axis=-1, keepdims=True)
        oa_ref[:, sl] = (o * lax.rsqrt(ms + RMS_EPS) * g_ref[...] * ga_ref[:, sl].astype(F32)).astype(BF16)


def _sample_step(l, sinks8, state, lft, qat, va, ga, norm_g, q3, kn, vn, ck, cv):
    nb = state.shape[1]
    sb = SAMPLE_BLOCK
    steps = nb // sb
    w = WINDOW
    b4 = lambda: pl.BlockSpec((sb, A_HEADS, A_DK, A_DV), lambda i: (i, 0, 0, 0))
    b4_in = pl.BlockSpec((None, sb, A_HEADS, A_DK, A_DV), lambda i: (l, i, 0, 0, 0))
    c3_in = lambda: pl.BlockSpec((None, sb, w, B_KV_WIDTH), lambda i: (l, i, 0, 0))
    t4 = lambda: pl.BlockSpec((A_HEADS, 1, A_DK, sb), lambda i: (0, i, 0, 0))
    r2 = lambda wd: pl.BlockSpec((sb, wd), lambda i: (i, 0))
    c3 = lambda: pl.BlockSpec((sb, w, B_KV_WIDTH), lambda i: (i, 0, 0))
    return pl.pallas_call(
        _sample_kernel,
        grid=(steps,),
        in_specs=[pl.BlockSpec((B_Q_HEADS, LANES), lambda i: (0, 0)),
                  b4_in, t4(), t4(), r2(A_WIDTH), r2(A_WIDTH), pl.BlockSpec((1, A_DV), lambda i: (0, 0)),
                  pl.BlockSpec((sb, B_Q_HEADS, LANES), lambda i: (i, 0, 0)), r2(B_KV_WIDTH), r2(B_KV_WIDTH),
                  c3_in(), c3_in()],
        out_specs=[b4(), r2(A_WIDTH), pl.BlockSpec((sb, B_Q_HEADS, LANES), lambda i: (i, 0, 0)), c3(), c3()],
        out_shape=[jax.ShapeDtypeStruct(state.shape[1:], F32),
                   jax.ShapeDtypeStruct((nb, A_WIDTH), BF16),
                   jax.ShapeDtypeStruct((nb, B_Q_HEADS, LANES), F32),
                   jax.ShapeDtypeStruct((nb, w, B_KV_WIDTH), F32),
                   jax.ShapeDtypeStruct((nb, w, B_KV_WIDTH), F32)],
        scratch_shapes=[pltpu.VMEM((sb, A_WIDTH), F32)],
        compiler_params=_cparams(("arbitrary",)),
        name="sample_step",
    )(sinks8, state, lft, qat, va, ga, norm_g, q3, kn, vn, ck, cv)


def _merge_kernel(x_ref, oa_ref, ob_ref, sga_ref, sgb_ref, wa_ref, wb_ref, wo_ref, g_ref, b_ref, h_ref):
    merged = (sga_ref[...].astype(F32) * _dot(oa_ref[...], wa_ref[...])
              + sgb_ref[...].astype(F32) * _dot(ob_ref[...], wb_ref[...]))
    mix = _dot(merged.astype(BF16), wo_ref[...])
    _store_tiled(h_ref, _layer_norm(DN_ALPHA * x_ref[...] + mix, g_ref[...], b_ref[...]))


def _merge(x, oa, ob, sga, sgb, wa, wb, wo, g, b, n_rows, x_off_blocks, out_rows, out_off_blocks, tm, h_prev=None):
    nt = n_rows // tm
    row = lambda wd: pl.BlockSpec((tm, wd), lambda i: (i, 0))
    const = lambda a: pl.BlockSpec(a.shape, lambda i: (0, 0))
    args = [x, oa, ob, sga, sgb, wa, wb, wo, g, b]
    in_specs = [pl.BlockSpec((tm, D_MODEL), lambda i: (i + x_off_blocks, 0)),
                row(A_WIDTH), row(B_WIDTH), row(D_MODEL), row(D_MODEL),
                const(wa), const(wb), const(wo), const(g), const(b)]
    kern = _merge_kernel
    aliases = {}
    if h_prev is not None:
        args.append(h_prev)
        in_specs.append(pl.BlockSpec(memory_space=pl.ANY))
        aliases = {len(args) - 1: 0}
        kern = lambda *refs: _merge_kernel(*refs[:10], refs[11])
    return pl.pallas_call(
        kern,
        grid=(nt,),
        in_specs=in_specs,
        out_specs=pl.BlockSpec((tm * SUBLANES, LANES), lambda i: (i + out_off_blocks, 0)),
        out_shape=jax.ShapeDtypeStruct((out_rows * SUBLANES, LANES), F32),
        input_output_aliases=aliases,
        compiler_params=_cparams(("arbitrary",)),
        name="merge",
    )(*args)


def _mixer_layer(l, xp, xs, xs_off, n_prompt, bsz, t, n_sample, prm):
    tm = ROW_TILE
    p = _proj(xp, prm["w_in"][l], prm["cos_p"], prm["sin_p"], prm["lower"][l], n_prompt, 0, tm)
    qa, lf, va, ga, qb, kb, vb, sga, sgb = p
    oa, st_p = _hgrn_prompt(qa, lf, va, ga, prm["norm_g"][l], bsz, t)
    ob = _swa_prompt(prm["sinks"][l], qb, kb, vb, bsz, t)
    n_all = n_prompt + n_sample
    h_all = _merge(xp, oa, ob, sga, sgb, prm["wa"][l], prm["wb"][l], prm["wo"][l], prm["ln1_g"][l],
                   prm["ln1_b"][l], n_prompt, 0, n_all, 0, tm, h_prev=jnp.zeros((n_all * SUBLANES, LANES), F32))
    kp = kb.reshape(bsz, t, B_KV_HEADS, B_HEAD_DIM)[:, -WINDOW:]
    vp = vb.reshape(bsz, t, B_KV_HEADS, B_HEAD_DIM)[:, -WINDOW:]

    ts = n_sample
    ps = _proj(xs, prm["w_in"][l], prm["cos_s"], prm["sin_s"], prm["lower"][l], ts, xs_off, ts)
    qa_s, lf_s, va_s, ga_s, qb_s, kb_s, vb_s, sga_s, sgb_s = ps
    sb = SAMPLE_BLOCK
    to_t = lambda a: a.reshape(ts // sb, sb, A_HEADS, A_DK).transpose(2, 0, 3, 1)
    qh = qb_s.astype(F32).reshape(ts, B_Q_HEADS, B_HEAD_DIM)
    z = jnp.zeros_like(qh[:, :B_GROUP])
    q3 = jnp.concatenate([jnp.concatenate([qh[:, :B_GROUP], z], axis=-1),
                          jnp.concatenate([z, qh[:, B_GROUP:]], axis=-1)], axis=1)
    st_s, oa_s, ob3, ck_s, cv_s = _sample_step(
        l, prm["sinks8"][l], prm["state"], to_t(lf_s), to_t(qa_s.astype(F32)), va_s, ga_s, prm["norm_g"][l],
        q3, kb_s, vb_s, prm["cache_k"], prm["cache_v"])
    ob_s = jnp.concatenate([ob3[:, :B_GROUP, :B_HEAD_DIM], ob3[:, B_GROUP:, B_HEAD_DIM:]], axis=1)
    ob_s = ob_s.reshape(ts, B_WIDTH).astype(BF16)
    h_all = _merge(xs, oa_s, ob_s, sga_s, sgb_s, prm["wa"][l], prm["wb"][l], prm["wo"][l], prm["ln1_g"][l],
                   prm["ln1_b"][l], ts, xs_off, n_all, n_prompt // ts, ts, h_prev=h_all)
    return h_all, (kp, vp, st_p, ck_s, cv_s, st_s)


def _router_kernel(h_ref, wr_ref, bias_ref, e_ref, w_ref, mask_ref, cnt_ref):
    tm = h_ref.shape[0] // SUBLANES
    gsz = GROUP_SIZE

    @pl.when(pl.program_id(0) == 0)
    def _():
        cnt_ref[...] = jnp.zeros_like(cnt_ref)

    logits = lax.dot_general(wr_ref[...], _load_tiled(h_ref, tm), (((1,), (1,)), ((), ())),
                             precision=lax.Precision.HIGHEST, preferred_element_type=F32)
    scores = _sigmoid(logits)
    sel = scores + bias_ref[...][:, 0:1]
    rowi = lax.broadcasted_iota(I32, (gsz, tm), 0)
    neg_inf = -jnp.inf
    blocks = [sel[g * gsz:(g + 1) * gsz] for g in range(N_GROUPS)]
    sblocks = [scores[g * gsz:(g + 1) * gsz] for g in range(N_GROUPS)]

    gscore = []
    for blk in blocks:
        m1 = jnp.max(blk, axis=0, keepdims=True)
        i1 = jnp.min(jnp.where(blk == m1, rowi, gsz), axis=0, keepdims=True)
        m2 = jnp.max(jnp.where(rowi == i1, neg_inf, blk), axis=0, keepdims=True)
        gscore.append(m1 + m2)
    work = []
    for g in range(N_GROUPS):
        ahead = jnp.zeros((1, tm), I32)
        for g2 in range(N_GROUPS):
            if g2 != g:
                beats = (gscore[g2] > gscore[g]) | ((gscore[g2] == gscore[g]) & (g2 < g))
                ahead = ahead + beats.astype(I32)
        work.append(jnp.where(ahead < TOPK_GROUPS, blocks[g], NEG_BIG))

    chosen = [jnp.zeros((gsz, tm), F32) for _ in range(N_GROUPS)]
    es, ws = [], []
    for _ in range(TOP_K):
        m = work[0]
        for g in range(1, N_GROUPS):
            m = jnp.maximum(m, work[g])
        m = jnp.max(m, axis=0, keepdims=True)
        cand = jnp.where(work[0] == m, rowi, N_EXPERTS)
        for g in range(1, N_GROUPS):
            cand = jnp.minimum(cand, jnp.where(work[g] == m, rowi + g * gsz, N_EXPERTS))
        idx = jnp.min(cand, axis=0, keepdims=True)
        wj = jnp.zeros((1, tm), F32)
        for g in range(N_GROUPS):
            hit = (rowi + g * gsz) == idx
            wj = wj + jnp.sum(jnp.where(hit, sblocks[g], 0.0), axis=0, keepdims=True)
            chosen[g] = jnp.where(hit, 1.0, chosen[g])
            work[g] = jnp.where(hit, neg_inf, work[g])
        es.append(idx)
        ws.append(wj)
    wsum = ws[0]
    for j in range(1, TOP_K):
        wsum = wsum + ws[j]
    for j in range(TOP_K):
        e_ref[j:j + 1, :] = es[j]
        w_ref[j:j + 1, :] = ws[j] / wsum * ROUTED_SCALE
    for g in range(N_GROUPS):
        rows = slice(g * gsz, (g + 1) * gsz)
        mask_ref[rows, :] = chosen[g]
        part = chosen[g][:, 0:LANES]
        for c in range(1, tm // LANES):
            part = part + chosen[g][:, c * LANES:(c + 1) * LANES]
        cnt_ref[rows, :] = cnt_ref[rows, :] + part


def _router(h_all, wr_t, bias_b, tm):
    n = h_all.shape[0] // SUBLANES
    col = lambda r: pl.BlockSpec((r, tm), lambda i: (0, i))
    return pl.pallas_call(
        _router_kernel,
        grid=(n // tm,),
        in_specs=[pl.BlockSpec((tm * SUBLANES, LANES), lambda i: (i, 0)),
                  pl.BlockSpec((N_EXPERTS, D_MODEL), lambda i: (0, 0)),
                  pl.BlockSpec((N_EXPERTS, LANES), lambda i: (0, 0))],
        out_specs=[col(TOP_K), col(TOP_K), col(N_EXPERTS), pl.BlockSpec((N_EXPERTS, LANES), lambda i: (0, 0))],
        out_shape=[jax.ShapeDtypeStruct((TOP_K, n), I32), jax.ShapeDtypeStruct((TOP_K, n), F32),
                   jax.ShapeDtypeStruct((N_EXPERTS, n), F32), jax.ShapeDtypeStruct((N_EXPERTS, LANES), F32)],
        compiler_params=_cparams(("arbitrary",)),
        name="router",
    )(h_all, wr_t, bias_b)


def _rank_kernel(mask_ref, e_ref, offs_ref, triu_ref, dest_ref, carry):
    tm = mask_ref.shape[1]

    @pl.when(pl.program_id(0) == 0)
    def _():
        carry[...] = jnp.zeros_like(carry)

    mk = mask_ref[...]
    rank = _dot(mk.astype(BF16), triu_ref[...])
    dest_full = rank + (offs_ref[...][:, 0:1] + carry[...][:, 0:1])
    rowi = lax.broadcasted_iota(I32, (N_EXPERTS, tm), 0)
    for j in range(TOP_K):
        d = jnp.sum(jnp.where(rowi == e_ref[j:j + 1, :], dest_full, 0.0), axis=0, keepdims=True)
        dest_ref[j:j + 1, :] = d.astype(I32)
    carry[...] = carry[...] + jnp.sum(mk, axis=1, keepdims=True)


def _rank(mask_t, e_t, offs_b, tm):
    n = mask_t.shape[1]
    r = np.arange(tm)
    triu = jnp.asarray((r[:, None] < r[None, :]).astype(np.float32), BF16)
    return pl.pallas_call(
        _rank_kernel,
        grid=(n // tm,),
        in_specs=[pl.BlockSpec((N_EXPERTS, tm), lambda i: (0, i)),
                  pl.BlockSpec((TOP_K, tm), lambda i: (0, i)),
                  pl.BlockSpec((N_EXPERTS, LANES), lambda i: (0, 0)),
                  pl.BlockSpec((tm, tm), lambda i: (0, 0))],
        out_specs=pl.BlockSpec((TOP_K, tm), lambda i: (0, i)),
        out_shape=jax.ShapeDtypeStruct((TOP_K, n), I32),
        scratch_shapes=[pltpu.VMEM((N_EXPERTS, LANES), F32)],
        compiler_params=_cparams(("arbitrary",)),
        name="rank",
    )(mask_t, e_t, offs_b, triu)


INV_COLS = 512
TOKEN_RADIX = 128


def _inverse_kernel(dest_ref, inv_ref):
    tm = dest_ref.shape[1]
    nq = inv_ref.shape[0]

    @pl.when(pl.program_id(0) == 0)
    def _():
        inv_ref[...] = jnp.zeros_like(inv_ref)

    tok = pl.program_id(0) * tm + lax.broadcasted_iota(I32, (1, tm), 1)
    t_hi = jnp.right_shift(tok, TOKEN_RADIX.bit_length() - 1).astype(F32)
    t_lo = jnp.bitwise_and(tok, TOKEN_RADIX - 1).astype(F32)
    qi = lax.broadcasted_iota(I32, (nq, tm), 0)
    si = lax.broadcasted_iota(I32, (INV_COLS, tm), 0)
    acc_hi = jnp.zeros(inv_ref.shape, F32)
    acc_lo = jnp.zeros(inv_ref.shape, F32)
    for j in range(TOP_K):
        d = dest_ref[j:j + 1, :]
        at_q = qi == jnp.right_shift(d, INV_COLS.bit_length() - 1)
        col = jnp.where(si == jnp.bitwise_and(d, INV_COLS - 1), 1.0, 0.0).astype(BF16)
        acc_hi = acc_hi + _dot_nt(jnp.where(at_q, t_hi, 0.0).astype(BF16), col)
        acc_lo = acc_lo + _dot_nt(jnp.where(at_q, t_lo, 0.0).astype(BF16), col)
    inv_ref[...] = inv_ref[...] + (acc_hi * float(TOKEN_RADIX) + acc_lo)


def _inverse_map(dest_t, tm):
    n = dest_t.shape[1]
    n_rows = n * TOP_K
    nq = -(-(n_rows // INV_COLS) // 8) * 8
    inv = pl.pallas_call(
        _inverse_kernel,
        grid=(n // tm,),
        in_specs=[pl.BlockSpec((TOP_K, tm), lambda i: (0, i))],
        out_specs=pl.BlockSpec((nq, INV_COLS), lambda i: (0, 0)),
        out_shape=jax.ShapeDtypeStruct((nq, INV_COLS), F32),
        compiler_params=_cparams(("arbitrary",)),
        name="inverse_map",
    )(dest_t)
    return inv.reshape(-1)[:n_rows].astype(I32).reshape(n_rows // EXPERT_ROW_TILE, 1, EXPERT_ROW_TILE)


DMA_THREADS = 2


def _token_copy(src_ref, token, dst_ref, dst_row, sem):
    return pltpu.make_async_copy(src_ref.at[pl.ds(pl.multiple_of(token * SUBLANES, SUBLANES), SUBLANES)],
                                 dst_ref.at[pl.ds(dst_row * SUBLANES, SUBLANES)], sem)


GATHER_AHEAD = 2
GATHER_SLOTS = GATHER_AHEAD + 1


def _expert_kernel(n_tiles, tile_ref, exp_ref, valid_ref, first_ref, newexp_ref, lo_ref, hi_ref,
                   inv0_ref, inv1_ref, inva_ref, h_ref, wg_ref, wu_ref, wd_ref, y_ref, xbuf, wgb, wub, wdb, sem):
    i = pl.program_id(0)
    rows = xbuf.shape[1] // SUBLANES
    tile = tile_ref[i]
    slot = tile % GATHER_SLOTS
    ahead_slot = (tile + GATHER_AHEAD) % GATHER_SLOTS
    valid = valid_ref[i] == 1
    first = first_ref[i] == 1

    def wait_rows(s):
        pltpu.make_async_copy(h_ref.at[pl.ds(0, rows * SUBLANES)], xbuf.at[s], sem.at[s]).wait()

    def gather_loop(idx_ref, s):
        def issue(k, c):
            for u in range(SUBLANES):
                r = k * SUBLANES + u
                pltpu.make_async_copy(
                    h_ref.at[pl.ds(pl.multiple_of(idx_ref[0, 0, r] * SUBLANES, SUBLANES), SUBLANES)],
                    xbuf.at[s, pl.ds(pl.multiple_of(r * SUBLANES, SUBLANES), SUBLANES)], sem.at[s]).start()
            return c

        lax.fori_loop(0, rows // SUBLANES, issue, 0)

    @pl.when(valid & first)
    def _():
        @pl.when(i == 0)
        def _():
            gather_loop(inv0_ref, 0)
            gather_loop(inv1_ref, 1)

        wait_rows(slot)

    @pl.when(valid & (newexp_ref[i] == 1))
    def _():
        wgb[...] = wg_ref[0].astype(BF16)
        wub[...] = wu_ref[0].astype(BF16)
        wdb[...] = wd_ref[0].astype(BF16)

    def compute(first_visit):
        xb = _load_tiled(xbuf.at[slot], rows).astype(BF16)
        if first_visit:
            for r in range(rows):
                _token_copy(h_ref, inva_ref[0, 0, r], xbuf.at[ahead_slot], r, sem.at[ahead_slot]).start(
                    priority=r % DMA_THREADS)
        gate = _dot(xb, wgb[...])
        up = _dot(xb, wub[...])
        y = _dot((_silu(gate) * up).astype(BF16), wdb[...])
        rowi = lax.broadcasted_iota(I32, y.shape, 0)
        mine = (rowi >= lo_ref[i]) & (rowi < hi_ref[i])
        _store_tiled(y_ref, jnp.where(mine, y, 0.0 if first_visit else _load_tiled(y_ref, rows)))

    @pl.when(valid & first)
    def _():
        compute(True)

    @pl.when(valid & jnp.logical_not(first))
    def _():
        compute(False)

    @pl.when(valid & first & (tile >= n_tiles - GATHER_AHEAD))
    def _():
        wait_rows(ahead_slot)


def _group_metadata(counts, n_rows):
    tmo = EXPERT_ROW_TILE
    n_tiles = n_rows // tmo
    ends = jnp.cumsum(counts)
    offs = ends - counts
    first_tile = offs // tmo
    n_t = jnp.where(counts > 0, (ends - 1) // tmo - first_tile + 1, 0)
    cum = jnp.cumsum(n_t)
    base = cum - n_t
    n_items = n_tiles + N_EXPERTS
    idx = jnp.arange(n_items, dtype=I32)
    valid = (idx < cum[-1]).astype(I32)
    idc = jnp.minimum(idx, cum[-1] - 1)
    e = jnp.minimum(jnp.sum((cum[None, :] <= idc[:, None]).astype(I32), axis=1), N_EXPERTS - 1)
    tile = (first_tile[e] + idc - base[e]).astype(I32)
    one = jnp.ones((1,), I32)
    first = jnp.concatenate([one, (tile[1:] != tile[:-1]).astype(I32)])
    new_expert = jnp.concatenate([one, (e[1:] != e[:-1]).astype(I32)])
    lo = jnp.clip(offs[e] - tile * tmo, 0, tmo).astype(I32)
    hi = jnp.clip(ends[e] - tile * tmo, 0, tmo).astype(I32)
    return tile, e, valid, first, new_expert, lo, hi


def _experts(l, meta, inv3, h_all, wg, wu, wd):
    tmo = EXPERT_ROW_TILE
    n_rows = inv3.shape[0] * tmo
    n_items = meta[0].shape[0]
    wspec = lambda shp: pl.BlockSpec((None, 1) + shp, lambda i, tl, ex, *_: (l, ex[i], 0, 0))
    n_tiles = inv3.shape[0]
    assert n_tiles > GATHER_AHEAD
    idx_spec = lambda nxt: pl.BlockSpec(
        (1, 1, tmo), lambda i, tl, *_: (jnp.minimum(tl[i] + nxt, n_tiles - 1), 0, 0), memory_space=pltpu.SMEM)
    return pl.pallas_call(
        functools.partial(_expert_kernel, n_tiles),
        grid_spec=pltpu.PrefetchScalarGridSpec(
            num_scalar_prefetch=7,
            grid=(n_items,),
            in_specs=[idx_spec(0), idx_spec(1), idx_spec(GATHER_AHEAD), pl.BlockSpec(memory_space=pl.ANY),
                      wspec((D_MODEL, D_EXPERT)), wspec((D_MODEL, D_EXPERT)), wspec((D_EXPERT, D_MODEL))],
            out_specs=pl.BlockSpec((tmo * SUBLANES, LANES), lambda i, tl, *_: (tl[i], 0)),
            scratch_shapes=[pltpu.VMEM((GATHER_SLOTS, tmo * SUBLANES, LANES), F32),
                            pltpu.VMEM((D_MODEL, D_EXPERT), BF16), pltpu.VMEM((D_MODEL, D_EXPERT), BF16),
                            pltpu.VMEM((D_EXPERT, D_MODEL), BF16),
                            pltpu.SemaphoreType.DMA((GATHER_SLOTS,))]),
        out_shape=jax.ShapeDtypeStruct((n_rows * SUBLANES, LANES), F32),
        compiler_params=_cparams(("arbitrary",)),
        name="experts",
    )(*meta, inv3, inv3, inv3, h_all, wg, wu, wd)


def _post_kernel(n_steps, dest0_ref, dest1_ref, desta_ref, h_ref, wt_ref, y_ref, wsg_ref, wsu_ref, wsd_ref,
                 g_ref, b_ref, out_ref, *scratch):
    bufs, sem = scratch[:GATHER_SLOTS], scratch[GATHER_SLOTS]
    i = pl.program_id(0)
    tm = out_ref.shape[0]

    def wait_rows(s):
        for j in range(TOP_K):
            pltpu.make_async_copy(y_ref.at[pl.ds(0, tm * SUBLANES)], bufs[s].at[j], sem.at[s]).wait()

    def gather_loop(idx_ref, s):
        def issue(k, c):
            for u in range(SUBLANES):
                t = k * SUBLANES + u
                for j in range(TOP_K):
                    pltpu.make_async_copy(
                        y_ref.at[pl.ds(pl.multiple_of(idx_ref[j, t] * SUBLANES, SUBLANES), SUBLANES)],
                        bufs[s].at[j, pl.ds(pl.multiple_of(t * SUBLANES, SUBLANES), SUBLANES)], sem.at[s]).start()
            return c

        lax.fori_loop(0, tm // SUBLANES, issue, 0)

    @pl.when(i == 0)
    def _():
        gather_loop(dest0_ref, 0)
        gather_loop(dest1_ref, 1)

    def step(s):
        a = (s + GATHER_AHEAD) % GATHER_SLOTS
        wait_rows(s)
        h = _load_tiled(h_ref, tm)
        hb = h.astype(BF16)
        for t in range(tm):
            for j in range(TOP_K):
                _token_copy(y_ref, desta_ref[j, t], bufs[a].at[j], t, sem.at[a]).start(priority=j % DMA_THREADS)
        shared = _dot((_silu(_dot(hb, wsg_ref[...])) * _dot(hb, wsu_ref[...])).astype(BF16), wsd_ref[...])
        wt = wt_ref[...]
        routed = _load_tiled(bufs[s].at[0], tm) * wt[:, 0:1]
        for j in range(1, TOP_K):
            routed = routed + _load_tiled(bufs[s].at[j], tm) * wt[:, j:j + 1]
        out_ref[...] = _layer_norm(DN_ALPHA * h + (routed + shared), g_ref[...], b_ref[...])

        @pl.when(i >= n_steps - GATHER_AHEAD)
        def _():
            wait_rows(a)

    for s in range(GATHER_SLOTS):
        pl.when(i % GATHER_SLOTS == s)(functools.partial(step, s))


def _post(dest_t, h_all, w_tok, y, wsg, wsu, wsd, g, b, tm):
    n = h_all.shape[0] // SUBLANES
    n_steps = n // tm
    assert n_steps > GATHER_AHEAD
    const = lambda a: pl.BlockSpec(a.shape, lambda i: (0, 0))
    idx_spec = lambda nxt: pl.BlockSpec((TOP_K, tm), lambda i: (0, jnp.minimum(i + nxt, n_steps - 1)),
                                        memory_space=pltpu.SMEM)
    return pl.pallas_call(
        functools.partial(_post_kernel, n_steps),
        grid=(n_steps,),
        in_specs=[idx_spec(0), idx_spec(1), idx_spec(GATHER_AHEAD),
                  pl.BlockSpec((tm * SUBLANES, LANES), lambda i: (i, 0)),
                  pl.BlockSpec((tm, TOP_K), lambda i: (i, 0)),
                  pl.BlockSpec(memory_space=pl.ANY),
                  const(wsg), const(wsu), const(wsd), const(g), const(b)],
        out_specs=pl.BlockSpec((tm, D_MODEL), lambda i: (i, 0)),
        out_shape=jax.ShapeDtypeStruct((n, D_MODEL), F32),
        scratch_shapes=[pltpu.VMEM((TOP_K, tm * SUBLANES, LANES), F32) for _ in range(GATHER_SLOTS)]
        + [pltpu.SemaphoreType.DMA((GATHER_SLOTS,))],
        compiler_params=_cparams(("arbitrary",)),
        name="moe_post",
    )(dest_t, dest_t, dest_t, h_all, w_tok, y, wsg, wsu, wsd, g, b)


def _moe_layer(l, h_all, prm):
    tm = MOE_ROW_TILE
    e_t, w_t, mask_t, cnt = _router(h_all, prm["wr_t"][l], prm["rbias"][l], tm)
    counts = jnp.sum(cnt, axis=1).astype(I32)
    offs = jnp.cumsum(counts) - counts
    offs_b = jnp.broadcast_to(offs.astype(F32)[:, None], (N_EXPERTS, LANES))
    dest_t = _rank(mask_t, e_t, offs_b, tm)
    inv3 = _inverse_map(dest_t, tm)
    meta = _group_metadata(counts, h_all.shape[0] // SUBLANES * TOP_K)
    y = _experts(l, meta, inv3, h_all, prm["w_exp_gate"], prm["w_exp_up"], prm["w_exp_down"])
    return _post(dest_t, h_all, w_t.T, y, prm["wsg"][l], prm["wsu"][l], prm["wsd"][l],
                 prm["ln2_g"][l], prm["ln2_b"][l], POST_ROW_TILE)


def kernel(x_prompt, x_sample, cache_k, cache_v, state_hgrn, w_in, hgrn_lower_bounds, hgrn_norm_g, attn_sinks, w_branch_a, w_branch_b, w_out, ln1_g, ln1_b, w_router, router_bias, w_exp_gate, w_exp_up, w_exp_down, w_sh_gate, w_sh_up, w_sh_down, ln2_g, ln2_b):
    bsz, t, d = x_prompt.shape
    n_sample = x_sample.shape[0] * x_sample.shape[1]
    n_prompt = bsz * t
    depth = w_in.shape[0]
    assert d == D_MODEL and x_sample.shape[1] == 1 and n_prompt % ROW_TILE == 0 and t % ROW_TILE == 0
    assert n_prompt % n_sample == 0 and (n_prompt + n_sample) % MOE_ROW_TILE == 0
    assert (n_prompt + n_sample) % POST_ROW_TILE == 0
    assert ((n_prompt + n_sample) * TOP_K) % EXPERT_ROW_TILE == 0 and n_sample % SAMPLE_BLOCK == 0

    lb_prob = jax.nn.softmax(hgrn_lower_bounds.astype(F32), axis=0)
    lower = (jnp.cumsum(lb_prob, axis=0) - lb_prob[0])[:, None, :]
    cos_p, sin_p = _rope_tables(jnp.arange(t))
    cos_s, sin_s = _rope_tables(jnp.full((n_sample,), PAST_LEN))
    row = lambda a: a[:, None, :]
    prm = dict(
        w_in=w_in.astype(BF16), lower=lower, cos_p=cos_p, sin_p=sin_p, cos_s=cos_s, sin_s=sin_s,
        norm_g=row(hgrn_norm_g), sinks=attn_sinks,
        sinks8=jnp.broadcast_to(attn_sinks[:, :, None], (depth, B_Q_HEADS, LANES)),
        wa=w_branch_a.astype(BF16), wb=w_branch_b.astype(BF16), wo=w_out.astype(BF16),
        ln1_g=row(ln1_g), ln1_b=row(ln1_b), ln2_g=row(ln2_g), ln2_b=row(ln2_b),
        wr_t=jnp.swapaxes(w_router, 1, 2),
        rbias=jnp.broadcast_to(router_bias[:, :, None], (depth, N_EXPERTS, LANES)),
        w_exp_gate=w_exp_gate, w_exp_up=w_exp_up, w_exp_down=w_exp_down,
        wsg=w_sh_gate.astype(BF16), wsu=w_sh_up.astype(BF16), wsd=w_sh_down.astype(BF16),
        state=state_hgrn,
        cache_k=cache_k.reshape(depth, n_sample, WINDOW, B_KV_WIDTH),
        cache_v=cache_v.reshape(depth, n_sample, WINDOW, B_KV_WIDTH),
    )

    xp, xs, xs_off = x_prompt.reshape(n_prompt, d), x_sample.reshape(n_sample, d), 0
    per_layer = []
    for l in range(depth):
        h_all, outs = _mixer_layer(l, xp, xs, xs_off, n_prompt, bsz, t, n_sample, prm)
        y_all = _moe_layer(l, h_all, prm)
        per_layer.append(outs)
        xp, xs, xs_off = y_all, y_all, n_prompt // n_sample

    kv_shape = (n_sample, WINDOW, B_KV_HEADS, B_HEAD_DIM)
    stack = lambda k, f=lambda a: a: jnp.stack([f(o[k]) for o in per_layer])
    return (y_all[:n_prompt].reshape(bsz, t, d), y_all[n_prompt:].reshape(n_sample, 1, d),
            stack(0), stack(1), stack(2),
            stack(3, lambda a: a.reshape(kv_shape)), stack(4, lambda a: a.reshape(kv_shape)), stack(5))
```

```python
import functools

import numpy as np
import jax
import jax.numpy as jnp
from jax import lax
from jax.experimental import pallas as pl
from jax.experimental.pallas import tpu as pltpu

F32 = jnp.float32
BF16 = jnp.bfloat16
I32 = jnp.int32

D_MODEL = 1024
DEPTH = 2
PAST_LEN = 16384
A_HEADS = 4
A_DK = 128
A_DV = 128
A_KEY = A_HEADS * A_DK
A_WIDTH = A_HEADS * A_DV
B_Q_HEADS = 8
B_KV_HEADS = 2
B_HEAD_DIM = 64
B_GROUP = B_Q_HEADS // B_KV_HEADS
B_WIDTH = B_Q_HEADS * B_HEAD_DIM
B_KV_WIDTH = B_KV_HEADS * B_HEAD_DIM
WINDOW = 128
ROPE_THETA = 10000.0
ATTN_SCALE = B_HEAD_DIM ** -0.5
N_EXPERTS = 64
TOP_K = 8
N_GROUPS = 8
GROUP_SIZE = N_EXPERTS // N_GROUPS
TOPK_GROUPS = 4
D_EXPERT = D_MODEL // 4
D_SHARED = D_EXPERT
ROUTED_SCALE = 2.5
DN_ALPHA = (2 * DEPTH) ** 0.25
LN_EPS = 1e-5
RMS_EPS = 1e-6
NEG_BIG = -1e30
TINY = 1.1754944e-38
OFF_AF = A_KEY
OFF_AI = 2 * A_KEY
OFF_AG = OFF_AI + A_WIDTH
OFF_BQ = OFF_AG + A_WIDTH
OFF_BK = OFF_BQ + B_WIDTH
OFF_BV = OFF_BK + B_KV_WIDTH
OFF_GA = OFF_BV + B_KV_WIDTH
OFF_GB = OFF_GA + D_MODEL
IN_COLS = OFF_GB + D_MODEL

LANES = 128
HGRN_CHUNK = 128
HGRN_LEVELS = 7
ROW_TILE = 512
MOE_ROW_TILE = 384
EXPERT_ROW_TILE = 512
POST_ROW_TILE = 128
VMEM_LIMIT = 56 * 1024 * 1024


def _cparams(sem, vmem=VMEM_LIMIT):
    return pltpu.CompilerParams(dimension_semantics=sem, vmem_limit_bytes=vmem)


def _dot(a, b):
    return jnp.dot(a, b, preferred_element_type=F32)


def _dot_nt(a, b):
    return lax.dot_general(a, b, (((1,), (1,)), ((), ())), preferred_element_type=F32)


def _dot_tn(a, b):
    return lax.dot_general(a, b, (((0,), (0,)), ((), ())), preferred_element_type=F32)


def _sigmoid(x):
    return 0.5 * jnp.tanh(0.5 * x) + 0.5


def _silu(x):
    return x * _sigmoid(x)


def _split3(x):
    hi = x.astype(BF16)
    r1 = x - hi.astype(F32)
    mid = r1.astype(BF16)
    lo = (r1 - mid.astype(F32)).astype(BF16)
    return hi, mid, lo


def _layer_norm(y, g, b):
    mu = jnp.mean(y, axis=-1, keepdims=True)
    d = y - mu
    var = jnp.mean(d * d, axis=-1, keepdims=True)
    return d * lax.rsqrt(var + LN_EPS) * g + b


SUBLANES = 8
assert D_MODEL == SUBLANES * LANES


def _store_tiled(ref, val):
    rows = val.shape[0]
    for c in range(SUBLANES):
        ref[pl.ds(c, rows, stride=SUBLANES), :] = val[:, c * LANES:(c + 1) * LANES]


def _load_tiled(ref, rows):
    return jnp.concatenate([ref[pl.ds(c, rows, stride=SUBLANES), :] for c in range(SUBLANES)], axis=1)


def _log_forget(af, lower):
    ls = jnp.minimum(af, 0.0) - jnp.log1p(jnp.exp(-jnp.abs(af)))
    a = jnp.log(jnp.maximum(lower, TINY))
    b = jnp.log1p(-lower) + ls
    mixed = jnp.maximum(a, b) + jnp.log1p(jnp.exp(-jnp.abs(a - b)))
    return jnp.where(lower > 0.0, mixed, ls)


def _proj_kernel(x_ref, w_ref, cos_ref, sin_ref, low_ref,
                 qa_ref, lf_ref, va_ref, ga_ref, qb_ref, kb_ref, vb_ref, sga_ref, sgb_ref):
    xb = x_ref[...].astype(BF16)

    def mm(c0, n):
        return _dot(xb, w_ref[:, c0:c0 + n])

    qa_ref[...] = _silu(mm(0, A_KEY)).astype(BF16)
    lf_ref[...] = _log_forget(mm(OFF_AF, A_KEY), low_ref[...])
    va_ref[...] = mm(OFF_AI, A_WIDTH).astype(BF16)
    ga_ref[...] = _silu(mm(OFF_AG, A_WIDTH)).astype(BF16)

    cos = cos_ref[...]
    sin = sin_ref[...]
    lane = lax.broadcasted_iota(I32, cos.shape, 1)
    first_half = (lane & (B_HEAD_DIM // 2)) == 0

    def rope(blk):
        partner = jnp.where(first_half, pltpu.roll(blk, LANES - B_HEAD_DIM // 2, 1),
                            pltpu.roll(blk, B_HEAD_DIM // 2, 1))
        return blk * cos + partner * sin

    bq = mm(OFF_BQ, B_WIDTH)
    for j in range(B_WIDTH // LANES):
        sl = slice(j * LANES, (j + 1) * LANES)
        qb_ref[:, sl] = (rope(bq[:, sl]) * ATTN_SCALE).astype(BF16)
    kb_ref[...] = rope(mm(OFF_BK, B_KV_WIDTH))
    vb_ref[...] = mm(OFF_BV, B_KV_WIDTH)
    sga_ref[...] = _sigmoid(mm(OFF_GA, D_MODEL)).astype(BF16)
    sgb_ref[...] = _sigmoid(mm(OFF_GB, D_MODEL)).astype(BF16)


def _proj(x, w_bf, cos_t, sin_t, lower, n_rows, row_off_blocks, tm):
    nt = n_rows // tm
    tab_blocks = cos_t.shape[0] // tm
    row = lambda w: pl.BlockSpec((tm, w), lambda i: (i, 0))
    outs = [(A_KEY, BF16), (A_KEY, F32), (A_WIDTH, BF16), (A_WIDTH, BF16), (B_WIDTH, BF16),
            (B_KV_WIDTH, F32), (B_KV_WIDTH, F32), (D_MODEL, BF16), (D_MODEL, BF16)]
    return pl.pallas_call(
        _proj_kernel,
        grid=(nt,),
        in_specs=[pl.BlockSpec((tm, D_MODEL), lambda i: (i + row_off_blocks, 0)),
                  pl.BlockSpec((D_MODEL, IN_COLS), lambda i: (0, 0)),
                  pl.BlockSpec((tm, LANES), lambda i: (i % tab_blocks, 0)),
                  pl.BlockSpec((tm, LANES), lambda i: (i % tab_blocks, 0)),
                  pl.BlockSpec((1, A_KEY), lambda i: (0, 0))],
        out_specs=[row(w) for w, _ in outs],
        out_shape=[jax.ShapeDtypeStruct((n_rows, w), dt) for w, dt in outs],
        compiler_params=_cparams(("arbitrary",)),
        name="proj",
    )(x, w_bf, cos_t, sin_t, lower)


def _rope_tables(pos):
    half = B_HEAD_DIM // 2
    inv = ROPE_THETA ** (-jnp.arange(half, dtype=F32) / half)
    ang = pos.astype(F32)[:, None] * inv[None, :]
    cos = jnp.cos(ang)
    sin = jnp.sin(ang)
    reps = LANES // B_HEAD_DIM
    cos_t = jnp.tile(jnp.concatenate([cos, cos], axis=1), (1, reps))
    sin_t = jnp.tile(jnp.concatenate([-sin, sin], axis=1), (1, reps))
    return cos_t, sin_t


def _hgrn_constants():
    c = HGRN_CHUNK
    r = np.arange(c)
    tri = (r[None, :] <= r[:, None]).astype(np.float32)
    upper = np.zeros((HGRN_LEVELS, c, A_KEY), np.float32)
    pair = np.zeros((HGRN_LEVELS + 1, c, c), np.float32)
    for l in range(HGRN_LEVELS):
        b = c >> (l + 1)
        up = (r % (2 * b)) >= b
        upper[l] = up[:, None]
        same = (r[:, None] // (2 * b)) == (r[None, :] // (2 * b))
        pair[l] = (up[:, None] & ~up[None, :] & same)
    pair[HGRN_LEVELS] = np.eye(c)
    return jnp.asarray(tri, BF16), jnp.asarray(upper), jnp.asarray(pair)


def _hgrn_kernel(qa_ref, lf_ref, va_ref, ga_ref, g_ref, tri_ref, up_ref, pair_ref,
                 oa_ref, st_ref, s_scr):
    c = HGRN_CHUNK
    step = pl.program_id(1)

    @pl.when(step == 0)
    def _():
        s_scr[...] = jnp.zeros_like(s_scr)

    lf = lf_ref[...]
    tri = tri_ref[...]
    hi, mid, lo = _split3(lf)
    gcum = _dot(tri, hi) + _dot(tri, mid) + _dot(tri, lo)
    qb = qa_ref[...]
    qf = qb.astype(F32)
    kf = 1.0 - jnp.exp(lf)
    kb = kf.astype(BF16)
    vb = va_ref[...]
    ws = []
    g8 = gcum.reshape(c // SUBLANES, SUBLANES, A_KEY)
    sub = lax.broadcasted_iota(I32, g8.shape, 1)
    for l in range(HGRN_LEVELS):
        half = c >> (l + 1)
        if 2 * half >= SUBLANES:
            g3 = gcum.reshape(c // (2 * half), 2 * half, A_KEY)
            gref_l = jnp.broadcast_to(g3[:, half - 1:half, :], g3.shape).reshape(c, A_KEY)
        else:
            gref_g = jnp.broadcast_to(g8[:, half - 1:half, :], g8.shape)
            for first in range(2 * half, SUBLANES, 2 * half):
                pick = jnp.broadcast_to(g8[:, first + half - 1:first + half, :], g8.shape)
                gref_g = jnp.where(sub >= first, pick, gref_g)
            gref_l = gref_g.reshape(c, A_KEY)
        e = jnp.exp(-jnp.abs(gcum - gref_l))
        ws.append((jnp.where(up_ref[l] > 0.5, qf, kf) * e).astype(BF16))
    qg = (qf * jnp.exp(gcum)).astype(BF16)
    gend = gcum[c - 1:c, :]
    kend = (kf * jnp.exp(gend - gcum)).astype(BF16)
    decay = jnp.exp(gend)
    gate = g_ref[...]

    for h in range(A_HEADS):
        sl = slice(h * A_DK, (h + 1) * A_DK)
        att = _dot_nt(qb[:, sl], kb[:, sl]) * pair_ref[HGRN_LEVELS]
        for l in range(HGRN_LEVELS):
            w = ws[l][:, sl]
            att = att + _dot_nt(w, w) * pair_ref[l]
        s_t = s_scr[h]
        o = _dot_nt(qg[:, sl], s_t.astype(BF16)) + _dot(att.astype(BF16), vb[:, sl])
        ms = jnp.mean(o * o, axis=-1, keepdims=True)
        on = o * lax.rsqrt(ms + RMS_EPS) * gate * ga_ref[:, sl].astype(F32)
        oa_ref[:, sl] = on.astype(BF16)
        s_scr[h] = s_t * decay[:, sl] + _dot_tn(vb[:, sl], kend[:, sl])

    @pl.when(step == pl.num_programs(1) - 1)
    def _():
        for h in range(A_HEADS):
            st_ref[0, h] = s_scr[h].T


def _hgrn_prompt(qa, lf, va, ga, norm_g, bsz, t):
    c = HGRN_CHUNK
    nc = t // c
    tri, upper, pair = _hgrn_constants()
    blk = lambda: pl.BlockSpec((c, A_KEY), lambda b, i: (b * nc + i, 0))
    const = lambda a: pl.BlockSpec(a.shape, lambda b, i: (0,) * a.ndim)
    return pl.pallas_call(
        _hgrn_kernel,
        grid=(bsz, nc),
        in_specs=[blk(), blk(), blk(), blk(), pl.BlockSpec((1, A_DV), lambda b, i: (0, 0)),
                  const(tri), const(upper), const(pair)],
        out_specs=[blk(), pl.BlockSpec((1, A_HEADS, A_DK, A_DV), lambda b, i: (b, 0, 0, 0))],
        out_shape=[jax.ShapeDtypeStruct((bsz * t, A_WIDTH), BF16),
                   jax.ShapeDtypeStruct((bsz, A_HEADS, A_DK, A_DV), F32)],
        scratch_shapes=[pltpu.VMEM((A_HEADS, A_DV, A_DK), F32)],
        compiler_params=_cparams(("arbitrary", "arbitrary")),
        name="hgrn_prompt",
    )(qa, lf, va, ga, norm_g, tri, upper, pair)


def _swa_kernel(sink_ref, q_ref, k_ref, v_ref, o_ref, kprev, vprev):
    w = WINDOW
    i = pl.program_id(1)

    @pl.when(i == 0)
    def _():
        kprev[...] = jnp.zeros_like(kprev)
        vprev[...] = jnp.zeros_like(vprev)

    kc = k_ref[...]
    vc = v_ref[...]
    kk = jnp.concatenate([kprev[...], kc], axis=0)
    vv = jnp.concatenate([vprev[...], vc], axis=0)
    kr = pltpu.roll(kk, B_HEAD_DIM, 1)
    vr = pltpu.roll(vv, B_HEAD_DIM, 1)
    lo2 = lax.broadcasted_iota(I32, kk.shape, 1) < B_HEAD_DIM
    zero = jnp.zeros_like(kk)
    k_lo = [jnp.where(lo2, kk, zero).astype(BF16), jnp.where(lo2, kr, zero).astype(BF16)]
    k_hi = [jnp.where(lo2, zero, kr).astype(BF16), jnp.where(lo2, zero, kk).astype(BF16)]
    v_dup = [jnp.where(lo2, vv, vr).astype(BF16), jnp.where(lo2, vr, vv).astype(BF16)]

    qi = lax.broadcasted_iota(I32, (w, 2 * w), 0)
    kj = lax.broadcasted_iota(I32, (w, 2 * w), 1)
    valid = (kj >= qi) & (kj <= qi + w) & ((kj >= w) | (i > 0))
    lo1 = lax.broadcasted_iota(I32, (w, LANES), 1) < B_HEAD_DIM

    for j in range(B_WIDTH // LANES):
        g = (2 * j) // B_GROUP
        qblk = q_ref[:, j * LANES:(j + 1) * LANES]
        res = []
        for half, kmat in enumerate((k_lo[g], k_hi[g])):
            sk = sink_ref[2 * j + half]
            s = jnp.where(valid, _dot_nt(qblk, kmat), NEG_BIG)
            m = jnp.maximum(jnp.max(s, axis=-1, keepdims=True), sk)
            p = jnp.exp(s - m)
            denom = jnp.sum(p, axis=-1, keepdims=True) + jnp.exp(sk - m)
            res.append(_dot(p.astype(BF16), v_dup[g]) / denom)
        o_ref[:, j * LANES:(j + 1) * LANES] = jnp.where(lo1, res[0], res[1]).astype(BF16)

    kprev[...] = kc
    vprev[...] = vc


def _swa_prompt(sinks, qb, kb, vb, bsz, t):
    w = WINDOW
    nb = t // w
    return pl.pallas_call(
        _swa_kernel,
        grid_spec=pltpu.PrefetchScalarGridSpec(
            num_scalar_prefetch=1,
            grid=(bsz, nb),
            in_specs=[pl.BlockSpec((w, B_WIDTH), lambda b, i, s: (b * nb + i, 0)),
                      pl.BlockSpec((w, B_KV_WIDTH), lambda b, i, s: (b * nb + i, 0)),
                      pl.BlockSpec((w, B_KV_WIDTH), lambda b, i, s: (b * nb + i, 0))],
            out_specs=pl.BlockSpec((w, B_WIDTH), lambda b, i, s: (b * nb + i, 0)),
            scratch_shapes=[pltpu.VMEM((w, B_KV_WIDTH), F32), pltpu.VMEM((w, B_KV_WIDTH), F32)]),
        out_shape=jax.ShapeDtypeStruct((bsz * t, B_WIDTH), BF16),
        compiler_params=_cparams(("arbitrary", "arbitrary")),
        name="swa_prompt",
    )(sinks, qb, kb, vb)


SAMPLE_BLOCK = 8


def _sample_kernel(sink_ref, st_ref, lft_ref, qat_ref, va_ref, ga_ref, g_ref, q3_ref, kn_ref, vn_ref,
                   ck_ref, cv_ref, st_out, oa_ref, ob_ref, ck_out, cv_out, o_scr):
    w = WINDOW
    row = lax.broadcasted_iota(I32, (w, B_KV_WIDTH), 0)
    for i in range(SAMPLE_BLOCK):
        for h in range(A_HEADS):
            sl = slice(h * A_DV, (h + 1) * A_DV)
            fcol = jnp.exp(lft_ref[h, 0][:, i:i + 1])
            qcol = qat_ref[h, 0][:, i:i + 1]
            vrow = va_ref[i:i + 1, sl].astype(F32)
            s_new = st_ref[i, h] * fcol + (1.0 - fcol) * vrow
            st_out[i, h] = s_new
            o_scr[i:i + 1, sl] = jnp.sum(s_new * qcol, axis=0, keepdims=True)
        kc = ck_ref[i]
        vc = cv_ref[i]
        kn = kn_ref[i:i + 1, :]
        vn = vn_ref[i:i + 1, :]
        q3 = q3_ref[i]
        s = _dot_nt(q3.astype(BF16), kc.astype(BF16))
        s_new_key = jnp.sum(q3 * kn, axis=-1, keepdims=True)
        sk = sink_ref[...][:, 0:1]
        m = jnp.maximum(jnp.maximum(jnp.max(s, axis=-1, keepdims=True), s_new_key), sk)
        p = jnp.exp(s - m)
        pn = jnp.exp(s_new_key - m)
        denom = jnp.sum(p, axis=-1, keepdims=True) + pn + jnp.exp(sk - m)
        ob_ref[i] = (_dot(p.astype(BF16), vc.astype(BF16)) + pn * vn) / denom
        ck_out[i] = jnp.where(row == w - 1, kn, pltpu.roll(kc, w - 1, 0))
        cv_out[i] = jnp.where(row == w - 1, vn, pltpu.roll(vc, w - 1, 0))
    for h in range(A_HEADS):
        sl = slice(h * A_DV, (h + 1) * A_DV)
        o = o_scr[:, sl]
        ms = jnp.mean(o * o, axis=-1, keepdims=True)
        oa_ref[:, sl] = (o * lax.rsqrt(ms + RMS_EPS) * g_ref[...] * ga_ref[:, sl].astype(F32)).astype(BF16)


def _sample_step(l, sinks8, state, lft, qat, va, ga, norm_g, q3, kn, vn, ck, cv):
    nb = state.shape[1]
    sb = SAMPLE_BLOCK
    steps = nb // sb
    w = WINDOW
    b4 = lambda: pl.BlockSpec((sb, A_HEADS, A_DK, A_DV), lambda i: (i, 0, 0, 0))
    b4_in = pl.BlockSpec((None, sb, A_HEADS, A_DK, A_DV), lambda i: (l, i, 0, 0, 0))
    c3_in = lambda: pl.BlockSpec((None, sb, w, B_KV_WIDTH), lambda i: (l, i, 0, 0))
    t4 = lambda: pl.BlockSpec((A_HEADS, 1, A_DK, sb), lambda i: (0, i, 0, 0))
    r2 = lambda wd: pl.BlockSpec((sb, wd), lambda i: (i, 0))
    c3 = lambda: pl.BlockSpec((sb, w, B_KV_WIDTH), lambda i: (i, 0, 0))
    return pl.pallas_call(
        _sample_kernel,
        grid=(steps,),
        in_specs=[pl.BlockSpec((B_Q_HEADS, LANES), lambda i: (0, 0)),
                  b4_in, t4(), t4(), r2(A_WIDTH), r2(A_WIDTH), pl.BlockSpec((1, A_DV), lambda i: (0, 0)),
                  pl.BlockSpec((sb, B_Q_HEADS, LANES), lambda i: (i, 0, 0)), r2(B_KV_WIDTH), r2(B_KV_WIDTH),
                  c3_in(), c3_in()],
        out_specs=[b4(), r2(A_WIDTH), pl.BlockSpec((sb, B_Q_HEADS, LANES), lambda i: (i, 0, 0)), c3(), c3()],
        out_shape=[jax.ShapeDtypeStruct(state.shape[1:], F32),
                   jax.ShapeDtypeStruct((nb, A_WIDTH), BF16),
                   jax.ShapeDtypeStruct((nb, B_Q_HEADS, LANES), F32),
                   jax.ShapeDtypeStruct((nb, w, B_KV_WIDTH), F32),
                   jax.ShapeDtypeStruct((nb, w, B_KV_WIDTH), F32)],
        scratch_shapes=[pltpu.VMEM((sb, A_WIDTH), F32)],
        compiler_params=_cparams(("arbitrary",)),
        name="sample_step",
    )(sinks8, state, lft, qat, va, ga, norm_g, q3, kn, vn, ck, cv)


def _merge_kernel(x_ref, oa_ref, ob_ref, sga_ref, sgb_ref, wa_ref, wb_ref, wo_ref, g_ref, b_ref, h_ref):
    merged = (sga_ref[...].astype(F32) * _dot(oa_ref[...], wa_ref[...])
              + sgb_ref[...].astype(F32) * _dot(ob_ref[...], wb_ref[...]))
    mix = _dot(merged.astype(BF16), wo_ref[...])
    _store_tiled(h_ref, _layer_norm(DN_ALPHA * x_ref[...] + mix, g_ref[...], b_ref[...]))


def _merge(x, oa, ob, sga, sgb, wa, wb, wo, g, b, n_rows, x_off_blocks, out_rows, out_off_blocks, tm, h_prev=None):
    nt = n_rows // tm
    row = lambda wd: pl.BlockSpec((tm, wd), lambda i: (i, 0))
    const = lambda a: pl.BlockSpec(a.shape, lambda i: (0, 0))
    args = [x, oa, ob, sga, sgb, wa, wb, wo, g, b]
    in_specs = [pl.BlockSpec((tm, D_MODEL), lambda i: (i + x_off_blocks, 0)),
                row(A_WIDTH), row(B_WIDTH), row(D_MODEL), row(D_MODEL),
                const(wa), const(wb), const(wo), const(g), const(b)]
    kern = _merge_kernel
    aliases = {}
    if h_prev is not None:
        args.append(h_prev)
        in_specs.append(pl.BlockSpec(memory_space=pl.ANY))
        aliases = {len(args) - 1: 0}
        kern = lambda *refs: _merge_kernel(*refs[:10], refs[11])
    return pl.pallas_call(
        kern,
        grid=(nt,),
        in_specs=in_specs,
        out_specs=pl.BlockSpec((tm * SUBLANES, LANES), lambda i: (i + out_off_blocks, 0)),
        out_shape=jax.ShapeDtypeStruct((out_rows * SUBLANES, LANES), F32),
        input_output_aliases=aliases,
        compiler_params=_cparams(("arbitrary",)),
        name="merge",
    )(*args)


def _mixer_layer(l, xp, xs, xs_off, n_prompt, bsz, t, n_sample, prm, h_buf):
    tm = ROW_TILE
    p = _proj(xp, prm["w_in"][l], prm["cos_p"], prm["sin_p"], prm["lower"][l], n_prompt, 0, tm)
    qa, lf, va, ga, qb, kb, vb, sga, sgb = p
    oa, st_p = _hgrn_prompt(qa, lf, va, ga, prm["norm_g"][l], bsz, t)
    ob = _swa_prompt(prm["sinks"][l], qb, kb, vb, bsz, t)
    n_all = n_prompt + n_sample
    h_all = _merge(xp, oa, ob, sga, sgb, prm["wa"][l], prm["wb"][l], prm["wo"][l], prm["ln1_g"][l],
                   prm["ln1_b"][l], n_prompt, 0, n_all, 0, tm, h_prev=h_buf)
    kp = kb.reshape(bsz, t, B_KV_HEADS, B_HEAD_DIM)[:, -WINDOW:]
    vp = vb.reshape(bsz, t, B_KV_HEADS, B_HEAD_DIM)[:, -WINDOW:]

    ts = n_sample
    ps = _proj(xs, prm["w_in"][l], prm["cos_s"], prm["sin_s"], prm["lower"][l], ts, xs_off, ts)
    qa_s, lf_s, va_s, ga_s, qb_s, kb_s, vb_s, sga_s, sgb_s = ps
    sb = SAMPLE_BLOCK
    to_t = lambda a: a.reshape(ts // sb, sb, A_HEADS, A_DK).transpose(2, 0, 3, 1)
    qh = qb_s.astype(F32).reshape(ts, B_Q_HEADS, B_HEAD_DIM)
    z = jnp.zeros_like(qh[:, :B_GROUP])
    q3 = jnp.concatenate([jnp.concatenate([qh[:, :B_GROUP], z], axis=-1),
                          jnp.concatenate([z, qh[:, B_GROUP:]], axis=-1)], axis=1)
    st_s, oa_s, ob3, ck_s, cv_s = _sample_step(
        l, prm["sinks8"][l], prm["state"], to_t(lf_s), to_t(qa_s.astype(F32)), va_s, ga_s, prm["norm_g"][l],
        q3, kb_s, vb_s, prm["cache_k"], prm["cache_v"])
    ob_s = jnp.concatenate([ob3[:, :B_GROUP, :B_HEAD_DIM], ob3[:, B_GROUP:, B_HEAD_DIM:]], axis=1)
    ob_s = ob_s.reshape(ts, B_WIDTH).astype(BF16)
    h_all = _merge(xs, oa_s, ob_s, sga_s, sgb_s, prm["wa"][l], prm["wb"][l], prm["wo"][l], prm["ln1_g"][l],
                   prm["ln1_b"][l], ts, xs_off, n_all, n_prompt // ts, ts, h_prev=h_all)
    return h_all, (kp, vp, st_p, ck_s, cv_s, st_s)


def _router_kernel(h_ref, wr_ref, bias_ref, e_ref, w_ref, mask_ref, cnt_ref):
    tm = h_ref.shape[0] // SUBLANES
    gsz = GROUP_SIZE

    @pl.when(pl.program_id(0) == 0)
    def _():
        cnt_ref[...] = jnp.zeros_like(cnt_ref)

    logits = lax.dot_general(wr_ref[...], _load_tiled(h_ref, tm), (((1,), (1,)), ((), ())),
                             precision=lax.Precision.HIGHEST, preferred_element_type=F32)
    scores = _sigmoid(logits)
    sel = scores + bias_ref[...][:, 0:1]
    rowi = lax.broadcasted_iota(I32, (gsz, tm), 0)
    neg_inf = -jnp.inf
    blocks = [sel[g * gsz:(g + 1) * gsz] for g in range(N_GROUPS)]
    sblocks = [scores[g * gsz:(g + 1) * gsz] for g in range(N_GROUPS)]

    gscore = []
    for blk in blocks:
        m1 = jnp.max(blk, axis=0, keepdims=True)
        i1 = jnp.min(jnp.where(blk == m1, rowi, gsz), axis=0, keepdims=True)
        m2 = jnp.max(jnp.where(rowi == i1, neg_inf, blk), axis=0, keepdims=True)
        gscore.append(m1 + m2)
    work = []
    for g in range(N_GROUPS):
        ahead = jnp.zeros((1, tm), I32)
        for g2 in range(N_GROUPS):
            if g2 != g:
                beats = (gscore[g2] > gscore[g]) | ((gscore[g2] == gscore[g]) & (g2 < g))
                ahead = ahead + beats.astype(I32)
        work.append(jnp.where(ahead < TOPK_GROUPS, blocks[g], NEG_BIG))

    chosen = [jnp.zeros((gsz, tm), F32) for _ in range(N_GROUPS)]
    es, ws = [], []
    for _ in range(TOP_K):
        m = work[0]
        for g in range(1, N_GROUPS):
            m = jnp.maximum(m, work[g])
        m = jnp.max(m, axis=0, keepdims=True)
        cand = jnp.where(work[0] == m, rowi, N_EXPERTS)
        for g in range(1, N_GROUPS):
            cand = jnp.minimum(cand, jnp.where(work[g] == m, rowi + g * gsz, N_EXPERTS))
        idx = jnp.min(cand, axis=0, keepdims=True)
        wj = jnp.zeros((1, tm), F32)
        for g in range(N_GROUPS):
            hit = (rowi + g * gsz) == idx
            wj = wj + jnp.sum(jnp.where(hit, sblocks[g], 0.0), axis=0, keepdims=True)
            chosen[g] = jnp.where(hit, 1.0, chosen[g])
            work[g] = jnp.where(hit, neg_inf, work[g])
        es.append(idx)
        ws.append(wj)
    wsum = ws[0]
    for j in range(1, TOP_K):
        wsum = wsum + ws[j]
    for j in range(TOP_K):
        e_ref[j:j + 1, :] = es[j]
        w_ref[j:j + 1, :] = ws[j] / wsum * ROUTED_SCALE
    for g in range(N_GROUPS):
        rows = slice(g * gsz, (g + 1) * gsz)
        mask_ref[rows, :] = chosen[g]
        part = chosen[g][:, 0:LANES]
        for c in range(1, tm // LANES):
            part = part + chosen[g][:, c * LANES:(c + 1) * LANES]
        cnt_ref[rows, :] = cnt_ref[rows, :] + part


def _router(h_all, wr_t, bias_b, tm):
    n = h_all.shape[0] // SUBLANES
    col = lambda r: pl.BlockSpec((r, tm), lambda i: (0, i))
    return pl.pallas_call(
        _router_kernel,
        grid=(n // tm,),
        in_specs=[pl.BlockSpec((tm * SUBLANES, LANES), lambda i: (i, 0)),
                  pl.BlockSpec((N_EXPERTS, D_MODEL), lambda i: (0, 0)),
                  pl.BlockSpec((N_EXPERTS, LANES), lambda i: (0, 0))],
        out_specs=[col(TOP_K), col(TOP_K), col(N_EXPERTS), pl.BlockSpec((N_EXPERTS, LANES), lambda i: (0, 0))],
        out_shape=[jax.ShapeDtypeStruct((TOP_K, n), I32), jax.ShapeDtypeStruct((TOP_K, n), F32),
                   jax.ShapeDtypeStruct((N_EXPERTS, n), F32), jax.ShapeDtypeStruct((N_EXPERTS, LANES), F32)],
        compiler_params=_cparams(("arbitrary",)),
        name="router",
    )(h_all, wr_t, bias_b)


def _rank_kernel(mask_ref, e_ref, offs_ref, triu_ref, dest_ref, carry):
    tm = mask_ref.shape[1]

    @pl.when(pl.program_id(0) == 0)
    def _():
        carry[...] = jnp.zeros_like(carry)

    mk = mask_ref[...]
    rank = _dot(mk.astype(BF16), triu_ref[...])
    dest_full = rank + (offs_ref[...][:, 0:1] + carry[...][:, 0:1])
    rowi = lax.broadcasted_iota(I32, (N_EXPERTS, tm), 0)
    for j in range(TOP_K):
        d = jnp.sum(jnp.where(rowi == e_ref[j:j + 1, :], dest_full, 0.0), axis=0, keepdims=True)
        dest_ref[j:j + 1, :] = d.astype(I32)
    carry[...] = carry[...] + jnp.sum(mk, axis=1, keepdims=True)


def _rank(mask_t, e_t, offs_b, tm):
    n = mask_t.shape[1]
    r = np.arange(tm)
    triu = jnp.asarray((r[:, None] < r[None, :]).astype(np.float32), BF16)
    return pl.pallas_call(
        _rank_kernel,
        grid=(n // tm,),
        in_specs=[pl.BlockSpec((N_EXPERTS, tm), lambda i: (0, i)),
                  pl.BlockSpec((TOP_K, tm), lambda i: (0, i)),
                  pl.BlockSpec((N_EXPERTS, LANES), lambda i: (0, 0)),
                  pl.BlockSpec((tm, tm), lambda i: (0, 0))],
        out_specs=pl.BlockSpec((TOP_K, tm), lambda i: (0, i)),
        out_shape=jax.ShapeDtypeStruct((TOP_K, n), I32),
        scratch_shapes=[pltpu.VMEM((N_EXPERTS, LANES), F32)],
        compiler_params=_cparams(("arbitrary",)),
        name="rank",
    )(mask_t, e_t, offs_b, triu)


INV_COLS = 512
TOKEN_RADIX = 128


def _inverse_kernel(dest_ref, inv_ref):
    tm = dest_ref.shape[1]
    nq = inv_ref.shape[0]

    @pl.when(pl.program_id(0) == 0)
    def _():
        inv_ref[...] = jnp.zeros_like(inv_ref)

    tok = pl.program_id(0) * tm + lax.broadcasted_iota(I32, (1, tm), 1)
    t_hi = jnp.right_shift(tok, TOKEN_RADIX.bit_length() - 1).astype(F32)
    t_lo = jnp.bitwise_and(tok, TOKEN_RADIX - 1).astype(F32)
    qi = lax.broadcasted_iota(I32, (nq, tm), 0)
    si = lax.broadcasted_iota(I32, (INV_COLS, tm), 0)
    acc_hi = jnp.zeros(inv_ref.shape, F32)
    acc_lo = jnp.zeros(inv_ref.shape, F32)
    for j in range(TOP_K):
        d = dest_ref[j:j + 1, :]
        at_q = qi == jnp.right_shift(d, INV_COLS.bit_length() - 1)
        col = jnp.where(si == jnp.bitwise_and(d, INV_COLS - 1), 1.0, 0.0).astype(BF16)
        acc_hi = acc_hi + _dot_nt(jnp.where(at_q, t_hi, 0.0).astype(BF16), col)
        acc_lo = acc_lo + _dot_nt(jnp.where(at_q, t_lo, 0.0).astype(BF16), col)
    inv_ref[...] = inv_ref[...] + (acc_hi * float(TOKEN_RADIX) + acc_lo)


def _inverse_map(dest_t, tm):
    n = dest_t.shape[1]
    n_rows = n * TOP_K
    nq = -(-(n_rows // INV_COLS) // 8) * 8
    inv = pl.pallas_call(
        _inverse_kernel,
        grid=(n // tm,),
        in_specs=[pl.BlockSpec((TOP_K, tm), lambda i: (0, i))],
        out_specs=pl.BlockSpec((nq, INV_COLS), lambda i: (0, 0)),
        out_shape=jax.ShapeDtypeStruct((nq, INV_COLS), F32),
        compiler_params=_cparams(("arbitrary",)),
        name="inverse_map",
    )(dest_t)
    return inv.reshape(-1)[:n_rows].astype(I32).reshape(n_rows // EXPERT_ROW_TILE, 1, EXPERT_ROW_TILE)


DMA_THREADS = 2


def _token_copy(src_ref, token, dst_ref, dst_row, sem):
    return pltpu.make_async_copy(src_ref.at[pl.ds(pl.multiple_of(token * SUBLANES, SUBLANES), SUBLANES)],
                                 dst_ref.at[pl.ds(dst_row * SUBLANES, SUBLANES)], sem)


GATHER_AHEAD = 2
GATHER_SLOTS = GATHER_AHEAD + 1


def _expert_kernel(n_tiles, tile_ref, exp_ref, valid_ref, first_ref, newexp_ref, lo_ref, hi_ref,
                   inv0_ref, inv1_ref, inva_ref, h_ref, wg_ref, wu_ref, wd_ref, y_ref, xbuf, wgb, wub, wdb, sem):
    i = pl.program_id(0)
    rows = xbuf.shape[1] // SUBLANES
    tile = tile_ref[i]
    slot = tile % GATHER_SLOTS
    ahead_slot = (tile + GATHER_AHEAD) % GATHER_SLOTS
    valid = valid_ref[i] == 1
    first = first_ref[i] == 1

    def wait_rows(s):
        pltpu.make_async_copy(h_ref.at[pl.ds(0, rows * SUBLANES)], xbuf.at[s], sem.at[s]).wait()

    def gather_loop(idx_ref, s):
        def issue(k, c):
            for u in range(SUBLANES):
                r = k * SUBLANES + u
                pltpu.make_async_copy(
                    h_ref.at[pl.ds(pl.multiple_of(idx_ref[0, 0, r] * SUBLANES, SUBLANES), SUBLANES)],
                    xbuf.at[s, pl.ds(pl.multiple_of(r * SUBLANES, SUBLANES), SUBLANES)], sem.at[s]).start()
            return c

        lax.fori_loop(0, rows // SUBLANES, issue, 0)

    @pl.when(valid & first)
    def _():
        @pl.when(i == 0)
        def _():
            gather_loop(inv0_ref, 0)
            gather_loop(inv1_ref, 1)

        wait_rows(slot)

    @pl.when(valid & (newexp_ref[i] == 1))
    def _():
        wgb[...] = wg_ref[0].astype(BF16)
        wub[...] = wu_ref[0].astype(BF16)
        wdb[...] = wd_ref[0].astype(BF16)

    def compute(first_visit):
        xb = _load_tiled(xbuf.at[slot], rows).astype(BF16)
        if first_visit:
            for r in range(rows):
                _token_copy(h_ref, inva_ref[0, 0, r], xbuf.at[ahead_slot], r, sem.at[ahead_slot]).start(
                    priority=r % DMA_THREADS)
        gate = _dot(xb, wgb[...])
        up = _dot(xb, wub[...])
        y = _dot((_silu(gate) * up).astype(BF16), wdb[...])
        rowi = lax.broadcasted_iota(I32, y.shape, 0)
        mine = (rowi >= lo_ref[i]) & (rowi < hi_ref[i])
        _store_tiled(y_ref, jnp.where(mine, y, 0.0 if first_visit else _load_tiled(y_ref, rows)))

    @pl.when(valid & first)
    def _():
        compute(True)

    @pl.when(valid & jnp.logical_not(first))
    def _():
        compute(False)

    @pl.when(valid & first & (tile >= n_tiles - GATHER_AHEAD))
    def _():
        wait_rows(ahead_slot)


def _group_metadata(counts, n_rows):
    tmo = EXPERT_ROW_TILE
    n_tiles = n_rows // tmo
    ends = jnp.cumsum(counts)
    offs = ends - counts
    first_tile = offs // tmo
    n_t = jnp.where(counts > 0, (ends - 1) // tmo - first_tile + 1, 0)
    cum = jnp.cumsum(n_t)
    base = cum - n_t
    n_items = n_tiles + N_EXPERTS
    idx = jnp.arange(n_items, dtype=I32)
    valid = (idx < cum[-1]).astype(I32)
    idc = jnp.minimum(idx, cum[-1] - 1)
    e = jnp.minimum(jnp.sum((cum[None, :] <= idc[:, None]).astype(I32), axis=1), N_EXPERTS - 1)
    tile = (first_tile[e] + idc - base[e]).astype(I32)
    one = jnp.ones((1,), I32)
    first = jnp.concatenate([one, (tile[1:] != tile[:-1]).astype(I32)])
    new_expert = jnp.concatenate([one, (e[1:] != e[:-1]).astype(I32)])
    lo = jnp.clip(offs[e] - tile * tmo, 0, tmo).astype(I32)
    hi = jnp.clip(ends[e] - tile * tmo, 0, tmo).astype(I32)
    return tile, e, valid, first, new_expert, lo, hi


def _experts(l, meta, inv3, h_all, wg, wu, wd):
    tmo = EXPERT_ROW_TILE
    n_rows = inv3.shape[0] * tmo
    n_items = meta[0].shape[0]
    wspec = lambda shp: pl.BlockSpec((None, 1) + shp, lambda i, tl, ex, *_: (l, ex[i], 0, 0))
    n_tiles = inv3.shape[0]
    assert n_tiles > GATHER_AHEAD
    idx_spec = lambda nxt: pl.BlockSpec(
        (1, 1, tmo), lambda i, tl, *_: (jnp.minimum(tl[i] + nxt, n_tiles - 1), 0, 0), memory_space=pltpu.SMEM)
    return pl.pallas_call(
        functools.partial(_expert_kernel, n_tiles),
        grid_spec=pltpu.PrefetchScalarGridSpec(
            num_scalar_prefetch=7,
            grid=(n_items,),
            in_specs=[idx_spec(0), idx_spec(1), idx_spec(GATHER_AHEAD), pl.BlockSpec(memory_space=pl.ANY),
                      wspec((D_MODEL, D_EXPERT)), wspec((D_MODEL, D_EXPERT)), wspec((D_EXPERT, D_MODEL))],
            out_specs=pl.BlockSpec((tmo * SUBLANES, LANES), lambda i, tl, *_: (tl[i], 0)),
            scratch_shapes=[pltpu.VMEM((GATHER_SLOTS, tmo * SUBLANES, LANES), F32),
                            pltpu.VMEM((D_MODEL, D_EXPERT), BF16), pltpu.VMEM((D_MODEL, D_EXPERT), BF16),
                            pltpu.VMEM((D_EXPERT, D_MODEL), BF16),
                            pltpu.SemaphoreType.DMA((GATHER_SLOTS,))]),
        out_shape=jax.ShapeDtypeStruct((n_rows * SUBLANES, LANES), F32),
        compiler_params=_cparams(("arbitrary",)),
        name="experts",
    )(*meta, inv3, inv3, inv3, h_all, wg, wu, wd)


def _post_kernel(n_steps, n_prompt_steps, dest0_ref, dest1_ref, desta_ref, h_ref, wt_ref, y_ref, wsg_ref, wsu_ref,
                 wsd_ref, g_ref, b_ref, outp_ref, outs_ref, *scratch):
    bufs, sem = scratch[:GATHER_SLOTS], scratch[GATHER_SLOTS]
    i = pl.program_id(0)
    tm = outp_ref.shape[0]

    def wait_rows(s):
        for j in range(TOP_K):
            pltpu.make_async_copy(y_ref.at[pl.ds(0, tm * SUBLANES)], bufs[s].at[j], sem.at[s]).wait()

    def gather_loop(idx_ref, s):
        def issue(k, c):
            for u in range(SUBLANES):
                t = k * SUBLANES + u
                for j in range(TOP_K):
                    pltpu.make_async_copy(
                        y_ref.at[pl.ds(pl.multiple_of(idx_ref[j, t] * SUBLANES, SUBLANES), SUBLANES)],
                        bufs[s].at[j, pl.ds(pl.multiple_of(t * SUBLANES, SUBLANES), SUBLANES)], sem.at[s]).start()
            return c

        lax.fori_loop(0, tm // SUBLANES, issue, 0)

    @pl.when(i == 0)
    def _():
        gather_loop(dest0_ref, 0)
        gather_loop(dest1_ref, 1)

    def step(s):
        a = (s + GATHER_AHEAD) % GATHER_SLOTS
        wait_rows(s)
        h = _load_tiled(h_ref, tm)
        hb = h.astype(BF16)
        for t in range(tm):
            for j in range(TOP_K):
                _token_copy(y_ref, desta_ref[j, t], bufs[a].at[j], t, sem.at[a]).start(priority=j % DMA_THREADS)
        shared = _dot((_silu(_dot(hb, wsg_ref[...])) * _dot(hb, wsu_ref[...])).astype(BF16), wsd_ref[...])
        wt = wt_ref[...]
        routed = _load_tiled(bufs[s].at[0], tm) * wt[:, 0:1]
        for j in range(1, TOP_K):
            routed = routed + _load_tiled(bufs[s].at[j], tm) * wt[:, j:j + 1]
        res = _layer_norm(DN_ALPHA * h + (routed + shared), g_ref[...], b_ref[...])

        @pl.when(i < n_prompt_steps)
        def _():
            outp_ref[...] = res

        @pl.when(i >= n_prompt_steps)
        def _():
            outs_ref[...] = res

        @pl.when(i >= n_steps - GATHER_AHEAD)
        def _():
            wait_rows(a)

    for s in range(GATHER_SLOTS):
        pl.when(i % GATHER_SLOTS == s)(functools.partial(step, s))


def _post(dest_t, h_all, w_tok, y, wsg, wsu, wsd, g, b, tm, n_prompt):
    n = h_all.shape[0] // SUBLANES
    n_steps = n // tm
    n_prompt_steps = n_prompt // tm
    assert n_steps > GATHER_AHEAD and n_prompt % tm == 0 and 0 < n_prompt_steps < n_steps
    const = lambda a: pl.BlockSpec(a.shape, lambda i: (0, 0))
    idx_spec = lambda nxt: pl.BlockSpec((TOP_K, tm), lambda i: (0, jnp.minimum(i + nxt, n_steps - 1)),
                                        memory_space=pltpu.SMEM)
    return pl.pallas_call(
        functools.partial(_post_kernel, n_steps, n_prompt_steps),
        grid=(n_steps,),
        in_specs=[idx_spec(0), idx_spec(1), idx_spec(GATHER_AHEAD),
                  pl.BlockSpec((tm * SUBLANES, LANES), lambda i: (i, 0)),
                  pl.BlockSpec((tm, TOP_K), lambda i: (i, 0)),
                  pl.BlockSpec(memory_space=pl.ANY),
                  const(wsg), const(wsu), const(wsd), const(g), const(b)],
        out_specs=[pl.BlockSpec((tm, D_MODEL), lambda i: (jnp.minimum(i, n_prompt_steps - 1), 0)),
                   pl.BlockSpec((tm, D_MODEL), lambda i: (jnp.maximum(i - n_prompt_steps, 0), 0))],
        out_shape=[jax.ShapeDtypeStruct((n_prompt, D_MODEL), F32),
                   jax.ShapeDtypeStruct((n - n_prompt, D_MODEL), F32)],
        scratch_shapes=[pltpu.VMEM((TOP_K, tm * SUBLANES, LANES), F32) for _ in range(GATHER_SLOTS)]
        + [pltpu.SemaphoreType.DMA((GATHER_SLOTS,))],
        compiler_params=_cparams(("arbitrary",)),
        name="moe_post",
    )(dest_t, dest_t, dest_t, h_all, w_tok, y, wsg, wsu, wsd, g, b)


def _moe_layer(l, h_all, n_prompt, prm):
    tm = MOE_ROW_TILE
    e_t, w_t, mask_t, cnt = _router(h_all, prm["wr_t"][l], prm["rbias"][l], tm)
    counts = jnp.sum(cnt, axis=1).astype(I32)
    offs = jnp.cumsum(counts) - counts
    offs_b = jnp.broadcast_to(offs.astype(F32)[:, None], (N_EXPERTS, LANES))
    dest_t = _rank(mask_t, e_t, offs_b, tm)
    inv3 = _inverse_map(dest_t, tm)
    meta = _group_metadata(counts, h_all.shape[0] // SUBLANES * TOP_K)
    y = _experts(l, meta, inv3, h_all, prm["w_exp_gate"], prm["w_exp_up"], prm["w_exp_down"])
    return _post(dest_t, h_all, w_t.T, y, prm["wsg"][l], prm["wsu"][l], prm["wsd"][l],
                 prm["ln2_g"][l], prm["ln2_b"][l], POST_ROW_TILE, n_prompt)


def kernel(x_prompt, x_sample, cache_k, cache_v, state_hgrn, w_in, hgrn_lower_bounds, hgrn_norm_g, attn_sinks, w_branch_a, w_branch_b, w_out, ln1_g, ln1_b, w_router, router_bias, w_exp_gate, w_exp_up, w_exp_down, w_sh_gate, w_sh_up, w_sh_down, ln2_g, ln2_b):
    bsz, t, d = x_prompt.shape
    n_sample = x_sample.shape[0] * x_sample.shape[1]
    n_prompt = bsz * t
    depth = w_in.shape[0]
    assert d == D_MODEL and x_sample.shape[1] == 1 and n_prompt % ROW_TILE == 0 and t % ROW_TILE == 0
    assert n_prompt % n_sample == 0 and (n_prompt + n_sample) % MOE_ROW_TILE == 0
    assert n_prompt % POST_ROW_TILE == 0 and n_sample % POST_ROW_TILE == 0
    assert ((n_prompt + n_sample) * TOP_K) % EXPERT_ROW_TILE == 0 and n_sample % SAMPLE_BLOCK == 0

    lb_prob = jax.nn.softmax(hgrn_lower_bounds.astype(F32), axis=0)
    lower = (jnp.cumsum(lb_prob, axis=0) - lb_prob[0])[:, None, :]
    cos_p, sin_p = _rope_tables(jnp.arange(t))
    cos_s, sin_s = _rope_tables(jnp.full((n_sample,), PAST_LEN))
    row = lambda a: a[:, None, :]
    prm = dict(
        w_in=w_in.astype(BF16), lower=lower, cos_p=cos_p, sin_p=sin_p, cos_s=cos_s, sin_s=sin_s,
        norm_g=row(hgrn_norm_g), sinks=attn_sinks,
        sinks8=jnp.broadcast_to(attn_sinks[:, :, None], (depth, B_Q_HEADS, LANES)),
        wa=w_branch_a.astype(BF16), wb=w_branch_b.astype(BF16), wo=w_out.astype(BF16),
        ln1_g=row(ln1_g), ln1_b=row(ln1_b), ln2_g=row(ln2_g), ln2_b=row(ln2_b),
        wr_t=jnp.swapaxes(w_router, 1, 2),
        rbias=jnp.broadcast_to(router_bias[:, :, None], (depth, N_EXPERTS, LANES)),
        w_exp_gate=w_exp_gate, w_exp_up=w_exp_up, w_exp_down=w_exp_down,
        wsg=w_sh_gate.astype(BF16), wsu=w_sh_up.astype(BF16), wsd=w_sh_down.astype(BF16),
        state=state_hgrn,
        cache_k=cache_k.reshape(depth, n_sample, WINDOW, B_KV_WIDTH),
        cache_v=cache_v.reshape(depth, n_sample, WINDOW, B_KV_WIDTH),
    )

    xp, xs, xs_off = x_prompt.reshape(n_prompt, d), x_sample.reshape(n_sample, d), 0
    per_layer = []
    h_all = jnp.zeros(((n_prompt + n_sample) * SUBLANES, LANES), F32)
    for l in range(depth):
        h_all, outs = _mixer_layer(l, xp, xs, xs_off, n_prompt, bsz, t, n_sample, prm, h_all)
        xp, xs = _moe_layer(l, h_all, n_prompt, prm)
        per_layer.append(outs)

    kv_shape = (n_sample, WINDOW, B_KV_HEADS, B_HEAD_DIM)
    stack = lambda k, f=lambda a: a: jnp.stack([f(o[k]) for o in per_layer])
    return (xp.reshape(bsz, t, d), xs.reshape(n_sample, 1, d),
            stack(0), stack(1), stack(2),
            stack(3, lambda a: a.reshape(kv_shape)), stack(4, lambda a: a.reshape(kv_shape)), stack(5))
```

```python
import functools

import numpy as np
import jax
import jax.numpy as jnp
from jax import lax
from jax.experimental import pallas as pl
from jax.experimental.pallas import tpu as pltpu

F32 = jnp.float32
BF16 = jnp.bfloat16
I32 = jnp.int32

D_MODEL = 1024
DEPTH = 2
PAST_LEN = 16384
A_HEADS = 4
A_DK = 128
A_DV = 128
A_KEY = A_HEADS * A_DK
A_WIDTH = A_HEADS * A_DV
B_Q_HEADS = 8
B_KV_HEADS = 2
B_HEAD_DIM = 64
B_GROUP = B_Q_HEADS // B_KV_HEADS
B_WIDTH = B_Q_HEADS * B_HEAD_DIM
B_KV_WIDTH = B_KV_HEADS * B_HEAD_DIM
WINDOW = 128
ROPE_THETA = 10000.0
ATTN_SCALE = B_HEAD_DIM ** -0.5
N_EXPERTS = 64
TOP_K = 8
N_GROUPS = 8
GROUP_SIZE = N_EXPERTS // N_GROUPS
TOPK_GROUPS = 4
D_EXPERT = D_MODEL // 4
D_SHARED = D_EXPERT
ROUTED_SCALE = 2.5
DN_ALPHA = (2 * DEPTH) ** 0.25
LN_EPS = 1e-5
RMS_EPS = 1e-6
NEG_BIG = -1e30
TINY = 1.1754944e-38
OFF_AF = A_KEY
OFF_AI = 2 * A_KEY
OFF_AG = OFF_AI + A_WIDTH
OFF_BQ = OFF_AG + A_WIDTH
OFF_BK = OFF_BQ + B_WIDTH
OFF_BV = OFF_BK + B_KV_WIDTH
OFF_GA = OFF_BV + B_KV_WIDTH
OFF_GB = OFF_GA + D_MODEL
IN_COLS = OFF_GB + D_MODEL

LANES = 128
HGRN_CHUNK = 128
HGRN_LEVELS = 7
ROW_TILE = 512
MOE_ROW_TILE = 384
EXPERT_ROW_TILE = 512
POST_ROW_TILE = 128
VMEM_LIMIT = 56 * 1024 * 1024


def _cparams(sem, vmem=VMEM_LIMIT):
    return pltpu.CompilerParams(dimension_semantics=sem, vmem_limit_bytes=vmem)


def _dot(a, b):
    return jnp.dot(a, b, preferred_element_type=F32)


def _dot_nt(a, b):
    return lax.dot_general(a, b, (((1,), (1,)), ((), ())), preferred_element_type=F32)


def _dot_tn(a, b):
    return lax.dot_general(a, b, (((0,), (0,)), ((), ())), preferred_element_type=F32)


def _sigmoid(x):
    return 0.5 * jnp.tanh(0.5 * x) + 0.5


def _silu(x):
    return x * _sigmoid(x)


def _split3(x):
    hi = x.astype(BF16)
    r1 = x - hi.astype(F32)
    mid = r1.astype(BF16)
    lo = (r1 - mid.astype(F32)).astype(BF16)
    return hi, mid, lo


def _layer_norm(y, g, b):
    mu = jnp.mean(y, axis=-1, keepdims=True)
    d = y - mu
    var = jnp.mean(d * d, axis=-1, keepdims=True)
    return d * lax.rsqrt(var + LN_EPS) * g + b


SUBLANES = 8
assert D_MODEL == SUBLANES * LANES


def _store_tiled(ref, val):
    rows = val.shape[0]
    for c in range(SUBLANES):
        ref[pl.ds(c, rows, stride=SUBLANES), :] = val[:, c * LANES:(c + 1) * LANES]


def _load_tiled(ref, rows):
    return jnp.concatenate([ref[pl.ds(c, rows, stride=SUBLANES), :] for c in range(SUBLANES)], axis=1)


def _log_forget(af, lower):
    ls = jnp.minimum(af, 0.0) - jnp.log1p(jnp.exp(-jnp.abs(af)))
    a = jnp.log(jnp.maximum(lower, TINY))
    b = jnp.log1p(-lower) + ls
    mixed = jnp.maximum(a, b) + jnp.log1p(jnp.exp(-jnp.abs(a - b)))
    return jnp.where(lower > 0.0, mixed, ls)


def _proj_kernel(x_ref, w_ref, cos_ref, sin_ref, low_ref,
                 qa_ref, lf_ref, va_ref, ga_ref, qb_ref, kb_ref, vb_ref, sga_ref, sgb_ref):
    xb = x_ref[...].astype(BF16)

    def mm(c0, n):
        return _dot(xb, w_ref[:, c0:c0 + n])

    qa_ref[...] = _silu(mm(0, A_KEY)).astype(BF16)
    lf_ref[...] = _log_forget(mm(OFF_AF, A_KEY), low_ref[...])
    va_ref[...] = mm(OFF_AI, A_WIDTH).astype(BF16)
    ga_ref[...] = _silu(mm(OFF_AG, A_WIDTH)).astype(BF16)

    cos = cos_ref[...]
    sin = sin_ref[...]
    lane = lax.broadcasted_iota(I32, cos.shape, 1)
    first_half = (lane & (B_HEAD_DIM // 2)) == 0

    def rope(blk):
        partner = jnp.where(first_half, pltpu.roll(blk, LANES - B_HEAD_DIM // 2, 1),
                            pltpu.roll(blk, B_HEAD_DIM // 2, 1))
        return blk * cos + partner * sin

    bq = mm(OFF_BQ, B_WIDTH)
    for j in range(B_WIDTH // LANES):
        sl = slice(j * LANES, (j + 1) * LANES)
        qb_ref[:, sl] = (rope(bq[:, sl]) * ATTN_SCALE).astype(BF16)
    kb_ref[...] = rope(mm(OFF_BK, B_KV_WIDTH))
    vb_ref[...] = mm(OFF_BV, B_KV_WIDTH)
    sga_ref[...] = _sigmoid(mm(OFF_GA, D_MODEL)).astype(BF16)
    sgb_ref[...] = _sigmoid(mm(OFF_GB, D_MODEL)).astype(BF16)


def _proj(x, w_bf, cos_t, sin_t, lower, n_rows, row_off_blocks, tm):
    nt = n_rows // tm
    tab_blocks = cos_t.shape[0] // tm
    row = lambda w: pl.BlockSpec((tm, w), lambda i: (i, 0))
    outs = [(A_KEY, BF16), (A_KEY, F32), (A_WIDTH, BF16), (A_WIDTH, BF16), (B_WIDTH, BF16),
            (B_KV_WIDTH, F32), (B_KV_WIDTH, F32), (D_MODEL, BF16), (D_MODEL, BF16)]
    return pl.pallas_call(
        _proj_kernel,
        grid=(nt,),
        in_specs=[pl.BlockSpec((tm, D_MODEL), lambda i: (i + row_off_blocks, 0)),
                  pl.BlockSpec((D_MODEL, IN_COLS), lambda i: (0, 0)),
                  pl.BlockSpec((tm, LANES), lambda i: (i % tab_blocks, 0)),
                  pl.BlockSpec((tm, LANES), lambda i: (i % tab_blocks, 0)),
                  pl.BlockSpec((1, A_KEY), lambda i: (0, 0))],
        out_specs=[row(w) for w, _ in outs],
        out_shape=[jax.ShapeDtypeStruct((n_rows, w), dt) for w, dt in outs],
        compiler_params=_cparams(("arbitrary",)),
        name="proj",
    )(x, w_bf, cos_t, sin_t, lower)


def _rope_tables(pos):
    half = B_HEAD_DIM // 2
    inv = np.float32(ROPE_THETA) ** (-np.arange(half, dtype=np.float32) / np.float32(half))
    ang = np.asarray(pos, np.float32)[:, None] * inv[None, :]
    cos = np.cos(ang)
    sin = np.sin(ang)
    reps = LANES // B_HEAD_DIM
    cos_t = np.tile(np.concatenate([cos, cos], axis=1), (1, reps))
    sin_t = np.tile(np.concatenate([-sin, sin], axis=1), (1, reps))
    return jnp.asarray(cos_t, F32), jnp.asarray(sin_t, F32)


def _hgrn_constants():
    c = HGRN_CHUNK
    r = np.arange(c)
    tri = (r[None, :] <= r[:, None]).astype(np.float32)
    upper = np.zeros((HGRN_LEVELS, c, A_KEY), np.float32)
    pair = np.zeros((HGRN_LEVELS + 1, c, c), np.float32)
    for l in range(HGRN_LEVELS):
        b = c >> (l + 1)
        up = (r % (2 * b)) >= b
        upper[l] = up[:, None]
        same = (r[:, None] // (2 * b)) == (r[None, :] // (2 * b))
        pair[l] = (up[:, None] & ~up[None, :] & same)
    pair[HGRN_LEVELS] = np.eye(c)
    return jnp.asarray(tri, BF16), jnp.asarray(upper), jnp.asarray(pair)


def _hgrn_kernel(qa_ref, lf_ref, va_ref, ga_ref, g_ref, tri_ref, up_ref, pair_ref,
                 oa_ref, st_ref, s_scr):
    c = HGRN_CHUNK
    step = pl.program_id(1)

    @pl.when(step == 0)
    def _():
        s_scr[...] = jnp.zeros_like(s_scr)

    lf = lf_ref[...]
    tri = tri_ref[...]
    hi, mid, lo = _split3(lf)
    gcum = _dot(tri, hi) + _dot(tri, mid) + _dot(tri, lo)
    qb = qa_ref[...]
    qf = qb.astype(F32)
    kf = 1.0 - jnp.exp(lf)
    kb = kf.astype(BF16)
    vb = va_ref[...]
    ws = []
    g8 = gcum.reshape(c // SUBLANES, SUBLANES, A_KEY)
    sub = lax.broadcasted_iota(I32, g8.shape, 1)
    for l in range(HGRN_LEVELS):
        half = c >> (l + 1)
        if 2 * half >= SUBLANES:
            g3 = gcum.reshape(c // (2 * half), 2 * half, A_KEY)
            gref_l = jnp.broadcast_to(g3[:, half - 1:half, :], g3.shape).reshape(c, A_KEY)
        else:
            gref_g = jnp.broadcast_to(g8[:, half - 1:half, :], g8.shape)
            for first in range(2 * half, SUBLANES, 2 * half):
                pick = jnp.broadcast_to(g8[:, first + half - 1:first + half, :], g8.shape)
                gref_g = jnp.where(sub >= first, pick, gref_g)
            gref_l = gref_g.reshape(c, A_KEY)
        e = jnp.exp(-jnp.abs(gcum - gref_l))
        ws.append((jnp.where(up_ref[l] > 0.5, qf, kf) * e).astype(BF16))
    qg = (qf * jnp.exp(gcum)).astype(BF16)
    gend = gcum[c - 1:c, :]
    kend = (kf * jnp.exp(gend - gcum)).astype(BF16)
    decay = jnp.exp(gend)
    gate = g_ref[...]

    for h in range(A_HEADS):
        sl = slice(h * A_DK, (h + 1) * A_DK)
        att = _dot_nt(qb[:, sl], kb[:, sl]) * pair_ref[HGRN_LEVELS]
        for l in range(HGRN_LEVELS):
            w = ws[l][:, sl]
            att = att + _dot_nt(w, w) * pair_ref[l]
        s_t = s_scr[h]
        o = _dot_nt(qg[:, sl], s_t.astype(BF16)) + _dot(att.astype(BF16), vb[:, sl])
        ms = jnp.mean(o * o, axis=-1, keepdims=True)
        on = o * lax.rsqrt(ms + RMS_EPS) * gate * ga_ref[:, sl].astype(F32)
        oa_ref[:, sl] = on.astype(BF16)
        s_scr[h] = s_t * decay[:, sl] + _dot_tn(vb[:, sl], kend[:, sl])

    @pl.when(step == pl.num_programs(1) - 1)
    def _():
        for h in range(A_HEADS):
            st_ref[0, h] = s_scr[h].T


def _hgrn_prompt(qa, lf, va, ga, norm_g, bsz, t):
    c = HGRN_CHUNK
    nc = t // c
    tri, upper, pair = _hgrn_constants()
    blk = lambda: pl.BlockSpec((c, A_KEY), lambda b, i: (b * nc + i, 0))
    const = lambda a: pl.BlockSpec(a.shape, lambda b, i: (0,) * a.ndim)
    return pl.pallas_call(
        _hgrn_kernel,
        grid=(bsz, nc),
        in_specs=[blk(), blk(), blk(), blk(), pl.BlockSpec((1, A_DV), lambda b, i: (0, 0)),
                  const(tri), const(upper), const(pair)],
        out_specs=[blk(), pl.BlockSpec((1, A_HEADS, A_DK, A_DV), lambda b, i: (b, 0, 0, 0))],
        out_shape=[jax.ShapeDtypeStruct((bsz * t, A_WIDTH), BF16),
                   jax.ShapeDtypeStruct((bsz, A_HEADS, A_DK, A_DV), F32)],
        scratch_shapes=[pltpu.VMEM((A_HEADS, A_DV, A_DK), F32)],
        compiler_params=_cparams(("arbitrary", "arbitrary")),
        name="hgrn_prompt",
    )(qa, lf, va, ga, norm_g, tri, upper, pair)


def _swa_kernel(sink_ref, q_ref, k_ref, v_ref, o_ref, kprev, vprev):
    w = WINDOW
    i = pl.program_id(1)

    @pl.when(i == 0)
    def _():
        kprev[...] = jnp.zeros_like(kprev)
        vprev[...] = jnp.zeros_like(vprev)

    kc = k_ref[...]
    vc = v_ref[...]
    kk = jnp.concatenate([kprev[...], kc], axis=0)
    vv = jnp.concatenate([vprev[...], vc], axis=0)
    kr = pltpu.roll(kk, B_HEAD_DIM, 1)
    vr = pltpu.roll(vv, B_HEAD_DIM, 1)
    lo2 = lax.broadcasted_iota(I32, kk.shape, 1) < B_HEAD_DIM
    zero = jnp.zeros_like(kk)
    k_lo = [jnp.where(lo2, kk, zero).astype(BF16), jnp.where(lo2, kr, zero).astype(BF16)]
    k_hi = [jnp.where(lo2, zero, kr).astype(BF16), jnp.where(lo2, zero, kk).astype(BF16)]
    v_dup = [jnp.where(lo2, vv, vr).astype(BF16), jnp.where(lo2, vr, vv).astype(BF16)]

    qi = lax.broadcasted_iota(I32, (w, 2 * w), 0)
    kj = lax.broadcasted_iota(I32, (w, 2 * w), 1)
    valid = (kj >= qi) & (kj <= qi + w) & ((kj >= w) | (i > 0))
    lo1 = lax.broadcasted_iota(I32, (w, LANES), 1) < B_HEAD_DIM

    for j in range(B_WIDTH // LANES):
        g = (2 * j) // B_GROUP
        qblk = q_ref[:, j * LANES:(j + 1) * LANES]
        res = []
        for half, kmat in enumerate((k_lo[g], k_hi[g])):
            sk = sink_ref[2 * j + half]
            s = jnp.where(valid, _dot_nt(qblk, kmat), NEG_BIG)
            m = jnp.maximum(jnp.max(s, axis=-1, keepdims=True), sk)
            p = jnp.exp(s - m)
            denom = jnp.sum(p, axis=-1, keepdims=True) + jnp.exp(sk - m)
            res.append(_dot(p.astype(BF16), v_dup[g]) / denom)
        o_ref[:, j * LANES:(j + 1) * LANES] = jnp.where(lo1, res[0], res[1]).astype(BF16)

    kprev[...] = kc
    vprev[...] = vc


def _swa_prompt(sinks, qb, kb, vb, bsz, t):
    w = WINDOW
    nb = t // w
    return pl.pallas_call(
        _swa_kernel,
        grid_spec=pltpu.PrefetchScalarGridSpec(
            num_scalar_prefetch=1,
            grid=(bsz, nb),
            in_specs=[pl.BlockSpec((w, B_WIDTH), lambda b, i, s: (b * nb + i, 0)),
                      pl.BlockSpec((w, B_KV_WIDTH), lambda b, i, s: (b * nb + i, 0)),
                      pl.BlockSpec((w, B_KV_WIDTH), lambda b, i, s: (b * nb + i, 0))],
            out_specs=pl.BlockSpec((w, B_WIDTH), lambda b, i, s: (b * nb + i, 0)),
            scratch_shapes=[pltpu.VMEM((w, B_KV_WIDTH), F32), pltpu.VMEM((w, B_KV_WIDTH), F32)]),
        out_shape=jax.ShapeDtypeStruct((bsz * t, B_WIDTH), BF16),
        compiler_params=_cparams(("arbitrary", "arbitrary")),
        name="swa_prompt",
    )(sinks, qb, kb, vb)


SAMPLE_BLOCK = 8


def _sample_kernel(sink_ref, st_ref, lft_ref, qat_ref, va_ref, ga_ref, g_ref, q3_ref, kn_ref, vn_ref,
                   ck_ref, cv_ref, st_out, oa_ref, ob_ref, ck_out, cv_out, o_scr):
    w = WINDOW
    row = lax.broadcasted_iota(I32, (w, B_KV_WIDTH), 0)
    for i in range(SAMPLE_BLOCK):
        for h in range(A_HEADS):
            sl = slice(h * A_DV, (h + 1) * A_DV)
            fcol = jnp.exp(lft_ref[h, 0][:, i:i + 1])
            qcol = qat_ref[h, 0][:, i:i + 1]
            vrow = va_ref[i:i + 1, sl].astype(F32)
            s_new = st_ref[i, h] * fcol + (1.0 - fcol) * vrow
            st_out[i, h] = s_new
            o_scr[i:i + 1, sl] = jnp.sum(s_new * qcol, axis=0, keepdims=True)
        kc = ck_ref[i]
        vc = cv_ref[i]
        kn = kn_ref[i:i + 1, :]
        vn = vn_ref[i:i + 1, :]
        q3 = q3_ref[i]
        s = _dot_nt(q3.astype(BF16), kc.astype(BF16))
        s_new_key = jnp.sum(q3 * kn, axis=-1, keepdims=True)
        sk = sink_ref[...][:, 0:1]
        m = jnp.maximum(jnp.maximum(jnp.max(s, axis=-1, keepdims=True), s_new_key), sk)
        p = jnp.exp(s - m)
        pn = jnp.exp(s_new_key - m)
        denom = jnp.sum(p, axis=-1, keepdims=True) + pn + jnp.exp(sk - m)
        ob_ref[i] = (_dot(p.astype(BF16), vc.astype(BF16)) + pn * vn) / denom
        ck_out[i] = jnp.where(row == w - 1, kn, pltpu.roll(kc, w - 1, 0))
        cv_out[i] = jnp.where(row == w - 1, vn, pltpu.roll(vc, w - 1, 0))
    for h in range(A_HEADS):
        sl = slice(h * A_DV, (h + 1) * A_DV)
        o = o_scr[:, sl]
        ms = jnp.mean(o * o, axis=-1, keepdims=True)
        oa_ref[:, sl] = (o * lax.rsqrt(ms + RMS_EPS) * g_ref[...] * ga_ref[:, sl].astype(F32)).astype(BF16)


def _sample_step(l, sinks8, state, lft, qat, va, ga, norm_g, q3, kn, vn, ck, cv):
    nb = state.shape[1]
    sb = SAMPLE_BLOCK
    steps = nb // sb
    w = WINDOW
    b4 = lambda: pl.BlockSpec((sb, A_HEADS, A_DK, A_DV), lambda i: (i, 0, 0, 0))
    b4_in = pl.BlockSpec((None, sb, A_HEADS, A_DK, A_DV), lambda i: (l, i, 0, 0, 0))
    c3_in = lambda: pl.BlockSpec((None, sb, w, B_KV_WIDTH), lambda i: (l, i, 0, 0))
    t4 = lambda: pl.BlockSpec((A_HEADS, 1, A_DK, sb), lambda i: (0, i, 0, 0))
    r2 = lambda wd: pl.BlockSpec((sb, wd), lambda i: (i, 0))
    c3 = lambda: pl.BlockSpec((sb, w, B_KV_WIDTH), lambda i: (i, 0, 0))
    return pl.pallas_call(
        _sample_kernel,
        grid=(steps,),
        in_specs=[pl.BlockSpec((B_Q_HEADS, LANES), lambda i: (0, 0)),
                  b4_in, t4(), t4(), r2(A_WIDTH), r2(A_WIDTH), pl.BlockSpec((1, A_DV), lambda i: (0, 0)),
                  pl.BlockSpec((sb, B_Q_HEADS, LANES), lambda i: (i, 0, 0)), r2(B_KV_WIDTH), r2(B_KV_WIDTH),
                  c3_in(), c3_in()],
        out_specs=[b4(), r2(A_WIDTH), pl.BlockSpec((sb, B_Q_HEADS, LANES), lambda i: (i, 0, 0)), c3(), c3()],
        out_shape=[jax.ShapeDtypeStruct(state.shape[1:], F32),
                   jax.ShapeDtypeStruct((nb, A_WIDTH), BF16),
                   jax.ShapeDtypeStruct((nb, B_Q_HEADS, LANES), F32),
                   jax.ShapeDtypeStruct((nb, w, B_KV_WIDTH), F32),
                   jax.ShapeDtypeStruct((nb, w, B_KV_WIDTH), F32)],
        scratch_shapes=[pltpu.VMEM((sb, A_WIDTH), F32)],
        compiler_params=_cparams(("arbitrary",)),
        name="sample_step",
    )(sinks8, state, lft, qat, va, ga, norm_g, q3, kn, vn, ck, cv)


def _merge_kernel(x_ref, oa_ref, ob_ref, sga_ref, sgb_ref, wa_ref, wb_ref, wo_ref, g_ref, b_ref, h_ref):
    merged = (sga_ref[...].astype(F32) * _dot(oa_ref[...], wa_ref[...])
              + sgb_ref[...].astype(F32) * _dot(ob_ref[...], wb_ref[...]))
    mix = _dot(merged.astype(BF16), wo_ref[...])
    _store_tiled(h_ref, _layer_norm(DN_ALPHA * x_ref[...] + mix, g_ref[...], b_ref[...]))


def _merge(x, oa, ob, sga, sgb, wa, wb, wo, g, b, n_rows, x_off_blocks, out_rows, out_off_blocks, tm, h_prev=None):
    nt = n_rows // tm
    row = lambda wd: pl.BlockSpec((tm, wd), lambda i: (i, 0))
    const = lambda a: pl.BlockSpec(a.shape, lambda i: (0, 0))
    args = [x, oa, ob, sga, sgb, wa, wb, wo, g, b]
    in_specs = [pl.BlockSpec((tm, D_MODEL), lambda i: (i + x_off_blocks, 0)),
                row(A_WIDTH), row(B_WIDTH), row(D_MODEL), row(D_MODEL),
                const(wa), const(wb), const(wo), const(g), const(b)]
    kern = _merge_kernel
    aliases = {}
    if h_prev is not None:
        args.append(h_prev)
        in_specs.append(pl.BlockSpec(memory_space=pl.ANY))
        aliases = {len(args) - 1: 0}
        kern = lambda *refs: _merge_kernel(*refs[:10], refs[11])
    return pl.pallas_call(
        kern,
        grid=(nt,),
        in_specs=in_specs,
        out_specs=pl.BlockSpec((tm * SUBLANES, LANES), lambda i: (i + out_off_blocks, 0)),
        out_shape=jax.ShapeDtypeStruct((out_rows * SUBLANES, LANES), F32),
        input_output_aliases=aliases,
        compiler_params=_cparams(("arbitrary",)),
        name="merge",
    )(*args)


def _mixer_layer(l, xp, xs, xs_off, n_prompt, bsz, t, n_sample, prm, h_buf):
    tm = ROW_TILE
    p = _proj(xp, prm["w_in"][l], prm["cos_p"], prm["sin_p"], prm["lower"][l], n_prompt, 0, tm)
    qa, lf, va, ga, qb, kb, vb, sga, sgb = p
    oa, st_p = _hgrn_prompt(qa, lf, va, ga, prm["norm_g"][l], bsz, t)
    ob = _swa_prompt(prm["sinks"][l], qb, kb, vb, bsz, t)
    n_all = n_prompt + n_sample
    h_all = _merge(xp, oa, ob, sga, sgb, prm["wa"][l], prm["wb"][l], prm["wo"][l], prm["ln1_g"][l],
                   prm["ln1_b"][l], n_prompt, 0, n_all, 0, tm, h_prev=h_buf)
    kp = kb.reshape(bsz, t, B_KV_HEADS, B_HEAD_DIM)[:, -WINDOW:]
    vp = vb.reshape(bsz, t, B_KV_HEADS, B_HEAD_DIM)[:, -WINDOW:]

    ts = n_sample
    ps = _proj(xs, prm["w_in"][l], prm["cos_s"], prm["sin_s"], prm["lower"][l], ts, xs_off, ts)
    qa_s, lf_s, va_s, ga_s, qb_s, kb_s, vb_s, sga_s, sgb_s = ps
    sb = SAMPLE_BLOCK
    to_t = lambda a: a.reshape(ts // sb, sb, A_HEADS, A_DK).transpose(2, 0, 3, 1)
    qh = qb_s.astype(F32).reshape(ts, B_Q_HEADS, B_HEAD_DIM)
    z = jnp.zeros_like(qh[:, :B_GROUP])
    q3 = jnp.concatenate([jnp.concatenate([qh[:, :B_GROUP], z], axis=-1),
                          jnp.concatenate([z, qh[:, B_GROUP:]], axis=-1)], axis=1)
    st_s, oa_s, ob3, ck_s, cv_s = _sample_step(
        l, prm["sinks8"][l], prm["state"], to_t(lf_s), to_t(qa_s.astype(F32)), va_s, ga_s, prm["norm_g"][l],
        q3, kb_s, vb_s, prm["cache_k"], prm["cache_v"])
    ob_s = jnp.concatenate([ob3[:, :B_GROUP, :B_HEAD_DIM], ob3[:, B_GROUP:, B_HEAD_DIM:]], axis=1)
    ob_s = ob_s.reshape(ts, B_WIDTH).astype(BF16)
    h_all = _merge(xs, oa_s, ob_s, sga_s, sgb_s, prm["wa"][l], prm["wb"][l], prm["wo"][l], prm["ln1_g"][l],
                   prm["ln1_b"][l], ts, xs_off, n_all, n_prompt // ts, ts, h_prev=h_all)
    return h_all, (kp, vp, st_p, ck_s, cv_s, st_s)


def _router_kernel(h_ref, wr_ref, bias_ref, e_ref, w_ref, mask_ref, cnt_ref):
    tm = h_ref.shape[0] // SUBLANES
    gsz = GROUP_SIZE

    @pl.when(pl.program_id(0) == 0)
    def _():
        cnt_ref[...] = jnp.zeros_like(cnt_ref)

    logits = lax.dot_general(wr_ref[...], _load_tiled(h_ref, tm), (((1,), (1,)), ((), ())),
                             precision=lax.Precision.HIGHEST, preferred_element_type=F32)
    scores = _sigmoid(logits)
    sel = scores + bias_ref[...][:, 0:1]
    rowi = lax.broadcasted_iota(I32, (gsz, tm), 0)
    neg_inf = -jnp.inf
    blocks = [sel[g * gsz:(g + 1) * gsz] for g in range(N_GROUPS)]
    sblocks = [scores[g * gsz:(g + 1) * gsz] for g in range(N_GROUPS)]

    gscore = []
    for blk in blocks:
        m1 = jnp.max(blk, axis=0, keepdims=True)
        i1 = jnp.min(jnp.where(blk == m1, rowi, gsz), axis=0, keepdims=True)
        m2 = jnp.max(jnp.where(rowi == i1, neg_inf, blk), axis=0, keepdims=True)
        gscore.append(m1 + m2)
    work = []
    for g in range(N_GROUPS):
        ahead = jnp.zeros((1, tm), I32)
        for g2 in range(N_GROUPS):
            if g2 != g:
                beats = (gscore[g2] > gscore[g]) | ((gscore[g2] == gscore[g]) & (g2 < g))
                ahead = ahead + beats.astype(I32)
        work.append(jnp.where(ahead < TOPK_GROUPS, blocks[g], NEG_BIG))

    chosen = [jnp.zeros((gsz, tm), F32) for _ in range(N_GROUPS)]
    es, ws = [], []
    for _ in range(TOP_K):
        m = work[0]
        for g in range(1, N_GROUPS):
            m = jnp.maximum(m, work[g])
        m = jnp.max(m, axis=0, keepdims=True)
        cand = jnp.where(work[0] == m, rowi, N_EXPERTS)
        for g in range(1, N_GROUPS):
            cand = jnp.minimum(cand, jnp.where(work[g] == m, rowi + g * gsz, N_EXPERTS))
        idx = jnp.min(cand, axis=0, keepdims=True)
        wj = jnp.zeros((1, tm), F32)
        for g in range(N_GROUPS):
            hit = (rowi + g * gsz) == idx
            wj = wj + jnp.sum(jnp.where(hit, sblocks[g], 0.0), axis=0, keepdims=True)
            chosen[g] = jnp.where(hit, 1.0, chosen[g])
            work[g] = jnp.where(hit, neg_inf, work[g])
        es.append(idx)
        ws.append(wj)
    wsum = ws[0]
    for j in range(1, TOP_K):
        wsum = wsum + ws[j]
    for j in range(TOP_K):
        e_ref[j:j + 1, :] = es[j]
        w_ref[j:j + 1, :] = ws[j] / wsum * ROUTED_SCALE
    for g in range(N_GROUPS):
        rows = slice(g * gsz, (g + 1) * gsz)
        mask_ref[rows, :] = chosen[g]
        part = chosen[g][:, 0:LANES]
        for c in range(1, tm // LANES):
            part = part + chosen[g][:, c * LANES:(c + 1) * LANES]
        cnt_ref[rows, :] = cnt_ref[rows, :] + part


def _router(h_all, wr_t, bias_b, tm):
    n = h_all.shape[0] // SUBLANES
    col = lambda r: pl.BlockSpec((r, tm), lambda i: (0, i))
    return pl.pallas_call(
        _router_kernel,
        grid=(n // tm,),
        in_specs=[pl.BlockSpec((tm * SUBLANES, LANES), lambda i: (i, 0)),
                  pl.BlockSpec((N_EXPERTS, D_MODEL), lambda i: (0, 0)),
                  pl.BlockSpec((N_EXPERTS, LANES), lambda i: (0, 0))],
        out_specs=[col(TOP_K), col(TOP_K), col(N_EXPERTS), pl.BlockSpec((N_EXPERTS, LANES), lambda i: (0, 0))],
        out_shape=[jax.ShapeDtypeStruct((TOP_K, n), I32), jax.ShapeDtypeStruct((TOP_K, n), F32),
                   jax.ShapeDtypeStruct((N_EXPERTS, n), F32), jax.ShapeDtypeStruct((N_EXPERTS, LANES), F32)],
        compiler_params=_cparams(("arbitrary",)),
        name="router",
    )(h_all, wr_t, bias_b)


def _rank_kernel(mask_ref, e_ref, offs_ref, triu_ref, dest_ref, carry):
    tm = mask_ref.shape[1]

    @pl.when(pl.program_id(0) == 0)
    def _():
        carry[...] = jnp.zeros_like(carry)

    mk = mask_ref[...]
    rank = _dot(mk.astype(BF16), triu_ref[...])
    dest_full = rank + (offs_ref[...][:, 0:1] + carry[...][:, 0:1])
    rowi = lax.broadcasted_iota(I32, (N_EXPERTS, tm), 0)
    for j in range(TOP_K):
        d = jnp.sum(jnp.where(rowi == e_ref[j:j + 1, :], dest_full, 0.0), axis=0, keepdims=True)
        dest_ref[j:j + 1, :] = d.astype(I32)
    carry[...] = carry[...] + jnp.sum(mk, axis=1, keepdims=True)


def _rank(mask_t, e_t, offs_b, tm):
    n = mask_t.shape[1]
    r = np.arange(tm)
    triu = jnp.asarray((r[:, None] < r[None, :]).astype(np.float32), BF16)
    return pl.pallas_call(
        _rank_kernel,
        grid=(n // tm,),
        in_specs=[pl.BlockSpec((N_EXPERTS, tm), lambda i: (0, i)),
                  pl.BlockSpec((TOP_K, tm), lambda i: (0, i)),
                  pl.BlockSpec((N_EXPERTS, LANES), lambda i: (0, 0)),
                  pl.BlockSpec((tm, tm), lambda i: (0, 0))],
        out_specs=pl.BlockSpec((TOP_K, tm), lambda i: (0, i)),
        out_shape=jax.ShapeDtypeStruct((TOP_K, n), I32),
        scratch_shapes=[pltpu.VMEM((N_EXPERTS, LANES), F32)],
        compiler_params=_cparams(("arbitrary",)),
        name="rank",
    )(mask_t, e_t, offs_b, triu)


INV_COLS = 512
TOKEN_RADIX = 128


def _inverse_kernel(dest_ref, inv_ref):
    tm = dest_ref.shape[1]
    nq = inv_ref.shape[0]

    @pl.when(pl.program_id(0) == 0)
    def _():
        inv_ref[...] = jnp.zeros_like(inv_ref)

    tok = pl.program_id(0) * tm + lax.broadcasted_iota(I32, (1, tm), 1)
    t_hi = jnp.right_shift(tok, TOKEN_RADIX.bit_length() - 1).astype(F32)
    t_lo = jnp.bitwise_and(tok, TOKEN_RADIX - 1).astype(F32)
    qi = lax.broadcasted_iota(I32, (nq, tm), 0)
    si = lax.broadcasted_iota(I32, (INV_COLS, tm), 0)
    acc_hi = jnp.zeros(inv_ref.shape, F32)
    acc_lo = jnp.zeros(inv_ref.shape, F32)
    for j in range(TOP_K):
        d = dest_ref[j:j + 1, :]
        at_q = qi == jnp.right_shift(d, INV_COLS.bit_length() - 1)
        col = jnp.where(si == jnp.bitwise_and(d, INV_COLS - 1), 1.0, 0.0).astype(BF16)
        acc_hi = acc_hi + _dot_nt(jnp.where(at_q, t_hi, 0.0).astype(BF16), col)
        acc_lo = acc_lo + _dot_nt(jnp.where(at_q, t_lo, 0.0).astype(BF16), col)
    inv_ref[...] = inv_ref[...] + (acc_hi * float(TOKEN_RADIX) + acc_lo)


def _inverse_map(dest_t, tm):
    n = dest_t.shape[1]
    n_rows = n * TOP_K
    nq = -(-(n_rows // INV_COLS) // 8) * 8
    inv = pl.pallas_call(
        _inverse_kernel,
        grid=(n // tm,),
        in_specs=[pl.BlockSpec((TOP_K, tm), lambda i: (0, i))],
        out_specs=pl.BlockSpec((nq, INV_COLS), lambda i: (0, 0)),
        out_shape=jax.ShapeDtypeStruct((nq, INV_COLS), F32),
        compiler_params=_cparams(("arbitrary",)),
        name="inverse_map",
    )(dest_t)
    return inv.reshape(-1)[:n_rows].astype(I32).reshape(n_rows // EXPERT_ROW_TILE, 1, EXPERT_ROW_TILE)


DMA_THREADS = 2


def _token_copy(src_ref, token, dst_ref, dst_row, sem):
    return pltpu.make_async_copy(src_ref.at[pl.ds(pl.multiple_of(token * SUBLANES, SUBLANES), SUBLANES)],
                                 dst_ref.at[pl.ds(dst_row * SUBLANES, SUBLANES)], sem)


GATHER_AHEAD = 2
GATHER_SLOTS = GATHER_AHEAD + 1


def _expert_kernel(n_tiles, tile_ref, exp_ref, valid_ref, first_ref, newexp_ref, lo_ref, hi_ref,
                   inv0_ref, inv1_ref, inva_ref, h_ref, wg_ref, wu_ref, wd_ref, y_ref, xbuf, wgb, wub, wdb, sem):
    i = pl.program_id(0)
    rows = xbuf.shape[1] // SUBLANES
    tile = tile_ref[i]
    slot = tile % GATHER_SLOTS
    ahead_slot = (tile + GATHER_AHEAD) % GATHER_SLOTS
    valid = valid_ref[i] == 1
    first = first_ref[i] == 1

    def wait_rows(s):
        pltpu.make_async_copy(h_ref.at[pl.ds(0, rows * SUBLANES)], xbuf.at[s], sem.at[s]).wait()

    def gather_loop(idx_ref, s):
        def issue(k, c):
            for u in range(SUBLANES):
                r = k * SUBLANES + u
                pltpu.make_async_copy(
                    h_ref.at[pl.ds(pl.multiple_of(idx_ref[0, 0, r] * SUBLANES, SUBLANES), SUBLANES)],
                    xbuf.at[s, pl.ds(pl.multiple_of(r * SUBLANES, SUBLANES), SUBLANES)], sem.at[s]).start()
            return c

        lax.fori_loop(0, rows // SUBLANES, issue, 0)

    @pl.when(valid & first)
    def _():
        @pl.when(i == 0)
        def _():
            gather_loop(inv0_ref, 0)
            gather_loop(inv1_ref, 1)

        wait_rows(slot)

    @pl.when(valid & (newexp_ref[i] == 1))
    def _():
        wgb[...] = wg_ref[0].astype(BF16)
        wub[...] = wu_ref[0].astype(BF16)
        wdb[...] = wd_ref[0].astype(BF16)

    def compute(first_visit):
        xb = _load_tiled(xbuf.at[slot], rows).astype(BF16)
        if first_visit:
            for r in range(rows):
                _token_copy(h_ref, inva_ref[0, 0, r], xbuf.at[ahead_slot], r, sem.at[ahead_slot]).start(
                    priority=r % DMA_THREADS)
        gate = _dot(xb, wgb[...])
        up = _dot(xb, wub[...])
        y = _dot((_silu(gate) * up).astype(BF16), wdb[...])
        rowi = lax.broadcasted_iota(I32, y.shape, 0)
        mine = (rowi >= lo_ref[i]) & (rowi < hi_ref[i])
        _store_tiled(y_ref, jnp.where(mine, y, 0.0 if first_visit else _load_tiled(y_ref, rows)))

    @pl.when(valid & first)
    def _():
        compute(True)

    @pl.when(valid & jnp.logical_not(first))
    def _():
        compute(False)

    @pl.when(valid & first & (tile >= n_tiles - GATHER_AHEAD))
    def _():
        wait_rows(ahead_slot)


def _group_metadata(counts, n_rows):
    tmo = EXPERT_ROW_TILE
    n_tiles = n_rows // tmo
    ends = jnp.cumsum(counts)
    offs = ends - counts
    first_tile = offs // tmo
    n_t = jnp.where(counts > 0, (ends - 1) // tmo - first_tile + 1, 0)
    cum = jnp.cumsum(n_t)
    base = cum - n_t
    n_items = n_tiles + N_EXPERTS
    idx = jnp.arange(n_items, dtype=I32)
    valid = (idx < cum[-1]).astype(I32)
    idc = jnp.minimum(idx, cum[-1] - 1)
    e = jnp.minimum(jnp.sum((cum[None, :] <= idc[:, None]).astype(I32), axis=1), N_EXPERTS - 1)
    tile = (first_tile[e] + idc - base[e]).astype(I32)
    one = jnp.ones((1,), I32)
    first = jnp.concatenate([one, (tile[1:] != tile[:-1]).astype(I32)])
    new_expert = jnp.concatenate([one, (e[1:] != e[:-1]).astype(I32)])
    lo = jnp.clip(offs[e] - tile * tmo, 0, tmo).astype(I32)
    hi = jnp.clip(ends[e] - tile * tmo, 0, tmo).astype(I32)
    return tile, e, valid, first, new_expert, lo, hi


def _experts(l, meta, inv3, h_all, wg, wu, wd):
    tmo = EXPERT_ROW_TILE
    n_rows = inv3.shape[0] * tmo
    n_items = meta[0].shape[0]
    wspec = lambda shp: pl.BlockSpec((None, 1) + shp, lambda i, tl, ex, *_: (l, ex[i], 0, 0))
    n_tiles = inv3.shape[0]
    assert n_tiles > GATHER_AHEAD
    idx_spec = lambda nxt: pl.BlockSpec(
        (1, 1, tmo), lambda i, tl, *_: (jnp.minimum(tl[i] + nxt, n_tiles - 1), 0, 0), memory_space=pltpu.SMEM)
    return pl.pallas_call(
        functools.partial(_expert_kernel, n_tiles),
        grid_spec=pltpu.PrefetchScalarGridSpec(
            num_scalar_prefetch=7,
            grid=(n_items,),
            in_specs=[idx_spec(0), idx_spec(1), idx_spec(GATHER_AHEAD), pl.BlockSpec(memory_space=pl.ANY),
                      wspec((D_MODEL, D_EXPERT)), wspec((D_MODEL, D_EXPERT)), wspec((D_EXPERT, D_MODEL))],
            out_specs=pl.BlockSpec((tmo * SUBLANES, LANES), lambda i, tl, *_: (tl[i], 0)),
            scratch_shapes=[pltpu.VMEM((GATHER_SLOTS, tmo * SUBLANES, LANES), F32),
                            pltpu.VMEM((D_MODEL, D_EXPERT), BF16), pltpu.VMEM((D_MODEL, D_EXPERT), BF16),
                            pltpu.VMEM((D_EXPERT, D_MODEL), BF16),
                            pltpu.SemaphoreType.DMA((GATHER_SLOTS,))]),
        out_shape=jax.ShapeDtypeStruct((n_rows * SUBLANES, LANES), F32),
        compiler_params=_cparams(("arbitrary",)),
        name="experts",
    )(*meta, inv3, inv3, inv3, h_all, wg, wu, wd)


def _post_kernel(n_steps, n_prompt_steps, dest0_ref, dest1_ref, desta_ref, h_ref, wt_ref, y_ref, wsg_ref, wsu_ref,
                 wsd_ref, g_ref, b_ref, outp_ref, outs_ref, *scratch):
    bufs, sem = scratch[:GATHER_SLOTS], scratch[GATHER_SLOTS]
    i = pl.program_id(0)
    tm = outp_ref.shape[0]

    def wait_rows(s):
        for j in range(TOP_K):
            pltpu.make_async_copy(y_ref.at[pl.ds(0, tm * SUBLANES)], bufs[s].at[j], sem.at[s]).wait()

    def gather_loop(idx_ref, s):
        def issue(k, c):
            for u in range(SUBLANES):
                t = k * SUBLANES + u
                for j in range(TOP_K):
                    pltpu.make_async_copy(
                        y_ref.at[pl.ds(pl.multiple_of(idx_ref[j, t] * SUBLANES, SUBLANES), SUBLANES)],
                        bufs[s].at[j, pl.ds(pl.multiple_of(t * SUBLANES, SUBLANES), SUBLANES)], sem.at[s]).start()
            return c

        lax.fori_loop(0, tm // SUBLANES, issue, 0)

    @pl.when(i == 0)
    def _():
        gather_loop(dest0_ref, 0)
        gather_loop(dest1_ref, 1)

    def step(s):
        a = (s + GATHER_AHEAD) % GATHER_SLOTS
        wait_rows(s)
        h = _load_tiled(h_ref, tm)
        hb = h.astype(BF16)
        for t in range(tm):
            for j in range(TOP_K):
                _token_copy(y_ref, desta_ref[j, t], bufs[a].at[j], t, sem.at[a]).start(priority=j % DMA_THREADS)
        shared = _dot((_silu(_dot(hb, wsg_ref[...])) * _dot(hb, wsu_ref[...])).astype(BF16), wsd_ref[...])
        wt = wt_ref[...]
        routed = _load_tiled(bufs[s].at[0], tm) * wt[:, 0:1]
        for j in range(1, TOP_K):
            routed = routed + _load_tiled(bufs[s].at[j], tm) * wt[:, j:j + 1]
        res = _layer_norm(DN_ALPHA * h + (routed + shared), g_ref[...], b_ref[...])

        @pl.when(i < n_prompt_steps)
        def _():
            outp_ref[...] = res

        @pl.when(i >= n_prompt_steps)
        def _():
            outs_ref[...] = res

        @pl.when(i >= n_steps - GATHER_AHEAD)
        def _():
            wait_rows(a)

    for s in range(GATHER_SLOTS):
        pl.when(i % GATHER_SLOTS == s)(functools.partial(step, s))


def _post(dest_t, h_all, w_tok, y, wsg, wsu, wsd, g, b, tm, n_prompt):
    n = h_all.shape[0] // SUBLANES
    n_steps = n // tm
    n_prompt_steps = n_prompt // tm
    assert n_steps > GATHER_AHEAD and n_prompt % tm == 0 and 0 < n_prompt_steps < n_steps
    const = lambda a: pl.BlockSpec(a.shape, lambda i: (0, 0))
    idx_spec = lambda nxt: pl.BlockSpec((TOP_K, tm), lambda i: (0, jnp.minimum(i + nxt, n_steps - 1)),
                                        memory_space=pltpu.SMEM)
    return pl.pallas_call(
        functools.partial(_post_kernel, n_steps, n_prompt_steps),
        grid=(n_steps,),
        in_specs=[idx_spec(0), idx_spec(1), idx_spec(GATHER_AHEAD),
                  pl.BlockSpec((tm * SUBLANES, LANES), lambda i: (i, 0)),
                  pl.BlockSpec((tm, TOP_K), lambda i: (i, 0)),
                  pl.BlockSpec(memory_space=pl.ANY),
                  const(wsg), const(wsu), const(wsd), const(g), const(b)],
        out_specs=[pl.BlockSpec((tm, D_MODEL), lambda i: (jnp.minimum(i, n_prompt_steps - 1), 0)),
                   pl.BlockSpec((tm, D_MODEL), lambda i: (jnp.maximum(i - n_prompt_steps, 0), 0))],
        out_shape=[jax.ShapeDtypeStruct((n_prompt, D_MODEL), F32),
                   jax.ShapeDtypeStruct((n - n_prompt, D_MODEL), F32)],
        scratch_shapes=[pltpu.VMEM((TOP_K, tm * SUBLANES, LANES), F32) for _ in range(GATHER_SLOTS)]
        + [pltpu.SemaphoreType.DMA((GATHER_SLOTS,))],
        compiler_params=_cparams(("arbitrary",)),
        name="moe_post",
    )(dest_t, dest_t, dest_t, h_all, w_tok, y, wsg, wsu, wsd, g, b)


def _moe_layer(l, h_all, n_prompt, prm):
    tm = MOE_ROW_TILE
    e_t, w_t, mask_t, cnt = _router(h_all, prm["wr_t"][l], prm["rbias"][l], tm)
    counts = jnp.sum(cnt, axis=1).astype(I32)
    offs = jnp.cumsum(counts) - counts
    offs_b = jnp.broadcast_to(offs.astype(F32)[:, None], (N_EXPERTS, LANES))
    dest_t = _rank(mask_t, e_t, offs_b, tm)
    inv3 = _inverse_map(dest_t, tm)
    meta = _group_metadata(counts, h_all.shape[0] // SUBLANES * TOP_K)
    y = _experts(l, meta, inv3, h_all, prm["w_exp_gate"], prm["w_exp_up"], prm["w_exp_down"])
    return _post(dest_t, h_all, w_t.T, y, prm["wsg"][l], prm["wsu"][l], prm["wsd"][l],
                 prm["ln2_g"][l], prm["ln2_b"][l], POST_ROW_TILE, n_prompt)


def kernel(x_prompt, x_sample, cache_k, cache_v, state_hgrn, w_in, hgrn_lower_bounds, hgrn_norm_g, attn_sinks, w_branch_a, w_branch_b, w_out, ln1_g, ln1_b, w_router, router_bias, w_exp_gate, w_exp_up, w_exp_down, w_sh_gate, w_sh_up, w_sh_down, ln2_g, ln2_b):
    bsz, t, d = x_prompt.shape
    n_sample = x_sample.shape[0] * x_sample.shape[1]
    n_prompt = bsz * t
    depth = w_in.shape[0]
    assert d == D_MODEL and x_sample.shape[1] == 1 and n_prompt % ROW_TILE == 0 and t % ROW_TILE == 0
    assert n_prompt % n_sample == 0 and (n_prompt + n_sample) % MOE_ROW_TILE == 0
    assert n_prompt % POST_ROW_TILE == 0 and n_sample % POST_ROW_TILE == 0
    assert ((n_prompt + n_sample) * TOP_K) % EXPERT_ROW_TILE == 0 and n_sample % SAMPLE_BLOCK == 0

    lb_prob = jax.nn.softmax(hgrn_lower_bounds.astype(F32), axis=0)
    lower = (jnp.cumsum(lb_prob, axis=0) - lb_prob[0])[:, None, :]
    cos_p, sin_p = _rope_tables(np.arange(t))
    cos_s, sin_s = _rope_tables(np.full((n_sample,), PAST_LEN))
    row = lambda a: a[:, None, :]
    prm = dict(
        w_in=w_in.astype(BF16), lower=lower, cos_p=cos_p, sin_p=sin_p, cos_s=cos_s, sin_s=sin_s,
        norm_g=row(hgrn_norm_g), sinks=attn_sinks,
        sinks8=jnp.broadcast_to(attn_sinks[:, :, None], (depth, B_Q_HEADS, LANES)),
        wa=w_branch_a.astype(BF16), wb=w_branch_b.astype(BF16), wo=w_out.astype(BF16),
        ln1_g=row(ln1_g), ln1_b=row(ln1_b), ln2_g=row(ln2_g), ln2_b=row(ln2_b),
        wr_t=jnp.swapaxes(w_router, 1, 2),
        rbias=jnp.broadcast_to(router_bias[:, :, None], (depth, N_EXPERTS, LANES)),
        w_exp_gate=w_exp_gate, w_exp_up=w_exp_up, w_exp_down=w_exp_down,
        wsg=w_sh_gate.astype(BF16), wsu=w_sh_up.astype(BF16), wsd=w_sh_down.astype(BF16),
        state=state_hgrn,
        cache_k=cache_k.reshape(depth, n_sample, WINDOW, B_KV_WIDTH),
        cache_v=cache_v.reshape(depth, n_sample, WINDOW, B_KV_WIDTH),
    )

    xp, xs, xs_off = x_prompt.reshape(n_prompt, d), x_sample.reshape(n_sample, d), 0
    per_layer = []
    h_all = jnp.zeros(((n_prompt + n_sample) * SUBLANES, LANES), F32)
    for l in range(depth):
        h_all, outs = _mixer_layer(l, xp, xs, xs_off, n_prompt, bsz, t, n_sample, prm, h_all)
        xp, xs = _moe_layer(l, h_all, n_prompt, prm)
        per_layer.append(outs)

    kv_shape = (n_sample, WINDOW, B_KV_HEADS, B_HEAD_DIM)
    stack = lambda k, f=lambda a: a: jnp.stack([f(o[k]) for o in per_layer])
    return (xp.reshape(bsz, t, d), xs.reshape(n_sample, 1, d),
            stack(0), stack(1), stack(2),
            stack(3, lambda a: a.reshape(kv_shape)), stack(4, lambda a: a.reshape(kv_shape)), stack(5))
```

```python
import functools

import numpy as np
import jax
import jax.numpy as jnp
from jax import lax
from jax.experimental import pallas as pl
from jax.experimental.pallas import tpu as pltpu

F32 = jnp.float32
BF16 = jnp.bfloat16
I32 = jnp.int32

D_MODEL = 1024
DEPTH = 2
PAST_LEN = 16384
A_HEADS = 4
A_DK = 128
A_DV = 128
A_KEY = A_HEADS * A_DK
A_WIDTH = A_HEADS * A_DV
B_Q_HEADS = 8
B_KV_HEADS = 2
B_HEAD_DIM = 64
B_GROUP = B_Q_HEADS // B_KV_HEADS
B_WIDTH = B_Q_HEADS * B_HEAD_DIM
B_KV_WIDTH = B_KV_HEADS * B_HEAD_DIM
WINDOW = 128
ROPE_THETA = 10000.0
ATTN_SCALE = B_HEAD_DIM ** -0.5
N_EXPERTS = 64
TOP_K = 8
N_GROUPS = 8
GROUP_SIZE = N_EXPERTS // N_GROUPS
TOPK_GROUPS = 4
D_EXPERT = D_MODEL // 4
D_SHARED = D_EXPERT
ROUTED_SCALE = 2.5
DN_ALPHA = (2 * DEPTH) ** 0.25
LN_EPS = 1e-5
RMS_EPS = 1e-6
NEG_BIG = -1e30
TINY = 1.1754944e-38
OFF_AF = A_KEY
OFF_AI = 2 * A_KEY
OFF_AG = OFF_AI + A_WIDTH
OFF_BQ = OFF_AG + A_WIDTH
OFF_BK = OFF_BQ + B_WIDTH
OFF_BV = OFF_BK + B_KV_WIDTH
OFF_GA = OFF_BV + B_KV_WIDTH
OFF_GB = OFF_GA + D_MODEL
IN_COLS = OFF_GB + D_MODEL

LANES = 128
HGRN_CHUNK = 128
HGRN_LEVELS = 7
ROW_TILE = 512
MOE_ROW_TILE = 384
EXPERT_ROW_TILE = 512
POST_ROW_TILE = 128
VMEM_LIMIT = 56 * 1024 * 1024


def _cparams(sem, vmem=VMEM_LIMIT):
    return pltpu.CompilerParams(dimension_semantics=sem, vmem_limit_bytes=vmem)


def _dot(a, b):
    return jnp.dot(a, b, preferred_element_type=F32)


def _dot_nt(a, b):
    return lax.dot_general(a, b, (((1,), (1,)), ((), ())), preferred_element_type=F32)


def _dot_tn(a, b):
    return lax.dot_general(a, b, (((0,), (0,)), ((), ())), preferred_element_type=F32)


def _sigmoid(x):
    return 0.5 * jnp.tanh(0.5 * x) + 0.5


def _silu(x):
    return x * _sigmoid(x)


def _split3(x):
    hi = x.astype(BF16)
    r1 = x - hi.astype(F32)
    mid = r1.astype(BF16)
    lo = (r1 - mid.astype(F32)).astype(BF16)
    return hi, mid, lo


def _layer_norm(y, g, b):
    mu = jnp.mean(y, axis=-1, keepdims=True)
    d = y - mu
    var = jnp.mean(d * d, axis=-1, keepdims=True)
    return d * lax.rsqrt(var + LN_EPS) * g + b


SUBLANES = 8
assert D_MODEL == SUBLANES * LANES


def _store_tiled(ref, val):
    rows = val.shape[0]
    for c in range(SUBLANES):
        ref[pl.ds(c, rows, stride=SUBLANES), :] = val[:, c * LANES:(c + 1) * LANES]


def _load_tiled(ref, rows):
    return jnp.concatenate([ref[pl.ds(c, rows, stride=SUBLANES), :] for c in range(SUBLANES)], axis=1)


def _log_forget(af, lower):
    ls = jnp.minimum(af, 0.0) - jnp.log1p(jnp.exp(-jnp.abs(af)))
    a = jnp.log(jnp.maximum(lower, TINY))
    b = jnp.log1p(-lower) + ls
    mixed = jnp.maximum(a, b) + jnp.log1p(jnp.exp(-jnp.abs(a - b)))
    return jnp.where(lower > 0.0, mixed, ls)


def _proj_kernel(x_ref, w_ref, cos_ref, sin_ref, low_ref,
                 qa_ref, lf_ref, va_ref, ga_ref, qb_ref, kb_ref, vb_ref, sga_ref, sgb_ref):
    xb = x_ref[...].astype(BF16)

    def mm(c0, n):
        return _dot(xb, w_ref[:, c0:c0 + n])

    qa_ref[...] = _silu(mm(0, A_KEY)).astype(BF16)
    lf_ref[...] = _log_forget(mm(OFF_AF, A_KEY), low_ref[...])
    va_ref[...] = mm(OFF_AI, A_WIDTH).astype(BF16)
    ga_ref[...] = _silu(mm(OFF_AG, A_WIDTH)).astype(BF16)

    cos = cos_ref[...]
    sin = sin_ref[...]
    lane = lax.broadcasted_iota(I32, cos.shape, 1)
    first_half = (lane & (B_HEAD_DIM // 2)) == 0

    def rope(blk):
        partner = jnp.where(first_half, pltpu.roll(blk, LANES - B_HEAD_DIM // 2, 1),
                            pltpu.roll(blk, B_HEAD_DIM // 2, 1))
        return blk * cos + partner * sin

    bq = mm(OFF_BQ, B_WIDTH)
    for j in range(B_WIDTH // LANES):
        sl = slice(j * LANES, (j + 1) * LANES)
        qb_ref[:, sl] = (rope(bq[:, sl]) * ATTN_SCALE).astype(BF16)
    kb_ref[...] = rope(mm(OFF_BK, B_KV_WIDTH))
    vb_ref[...] = mm(OFF_BV, B_KV_WIDTH)
    sga_ref[...] = _sigmoid(mm(OFF_GA, D_MODEL)).astype(BF16)
    sgb_ref[...] = _sigmoid(mm(OFF_GB, D_MODEL)).astype(BF16)


def _proj(x, w_bf, cos_t, sin_t, lower, n_rows, row_off_blocks, tm):
    nt = n_rows // tm
    tab_blocks = cos_t.shape[0] // tm
    row = lambda w: pl.BlockSpec((tm, w), lambda i: (i, 0))
    outs = [(A_KEY, BF16), (A_KEY, F32), (A_WIDTH, BF16), (A_WIDTH, BF16), (B_WIDTH, BF16),
            (B_KV_WIDTH, F32), (B_KV_WIDTH, F32), (D_MODEL, BF16), (D_MODEL, BF16)]
    return pl.pallas_call(
        _proj_kernel,
        grid=(nt,),
        in_specs=[pl.BlockSpec((tm, D_MODEL), lambda i: (i + row_off_blocks, 0)),
                  pl.BlockSpec((D_MODEL, IN_COLS), lambda i: (0, 0)),
                  pl.BlockSpec((tm, LANES), lambda i: (i % tab_blocks, 0)),
                  pl.BlockSpec((tm, LANES), lambda i: (i % tab_blocks, 0)),
                  pl.BlockSpec((1, A_KEY), lambda i: (0, 0))],
        out_specs=[row(w) for w, _ in outs],
        out_shape=[jax.ShapeDtypeStruct((n_rows, w), dt) for w, dt in outs],
        compiler_params=_cparams(("arbitrary",)),
        name="proj",
    )(x, w_bf, cos_t, sin_t, lower)


def _rope_tables(pos):
    half = B_HEAD_DIM // 2
    inv = ROPE_THETA ** (-jnp.arange(half, dtype=F32) / half)
    ang = pos.astype(F32)[:, None] * inv[None, :]
    cos = jnp.cos(ang)
    sin = jnp.sin(ang)
    reps = LANES // B_HEAD_DIM
    cos_t = jnp.tile(jnp.concatenate([cos, cos], axis=1), (1, reps))
    sin_t = jnp.tile(jnp.concatenate([-sin, sin], axis=1), (1, reps))
    return cos_t, sin_t


def _hgrn_constants():
    c = HGRN_CHUNK
    r = np.arange(c)
    tri = (r[None, :] <= r[:, None]).astype(np.float32)
    upper = np.zeros((HGRN_LEVELS, c, A_KEY), np.float32)
    pair = np.zeros((HGRN_LEVELS + 1, c, c), np.float32)
    for l in range(HGRN_LEVELS):
        b = c >> (l + 1)
        up = (r % (2 * b)) >= b
        upper[l] = up[:, None]
        same = (r[:, None] // (2 * b)) == (r[None, :] // (2 * b))
        pair[l] = (up[:, None] & ~up[None, :] & same)
    pair[HGRN_LEVELS] = np.eye(c)
    return jnp.asarray(tri, BF16), jnp.asarray(upper), jnp.asarray(pair)


def _hgrn_kernel(qa_ref, lf_ref, va_ref, ga_ref, g_ref, tri_ref, up_ref, pair_ref,
                 oa_ref, st_ref, s_scr):
    c = HGRN_CHUNK
    step = pl.program_id(1)

    @pl.when(step == 0)
    def _():
        s_scr[...] = jnp.zeros_like(s_scr)

    lf = lf_ref[...]
    tri = tri_ref[...]
    hi, mid, lo = _split3(lf)
    gcum = _dot(tri, hi) + _dot(tri, mid) + _dot(tri, lo)
    qb = qa_ref[...]
    qf = qb.astype(F32)
    kf = 1.0 - jnp.exp(lf)
    kb = kf.astype(BF16)
    vb = va_ref[...]
    ws = []
    g8 = gcum.reshape(c // SUBLANES, SUBLANES, A_KEY)
    sub = lax.broadcasted_iota(I32, g8.shape, 1)
    for l in range(HGRN_LEVELS):
        half = c >> (l + 1)
        if 2 * half >= SUBLANES:
            g3 = gcum.reshape(c // (2 * half), 2 * half, A_KEY)
            gref_l = jnp.broadcast_to(g3[:, half - 1:half, :], g3.shape).reshape(c, A_KEY)
        else:
            gref_g = jnp.broadcast_to(g8[:, half - 1:half, :], g8.shape)
            for first in range(2 * half, SUBLANES, 2 * half):
                pick = jnp.broadcast_to(g8[:, first + half - 1:first + half, :], g8.shape)
                gref_g = jnp.where(sub >= first, pick, gref_g)
            gref_l = gref_g.reshape(c, A_KEY)
        e = jnp.exp(-jnp.abs(gcum - gref_l))
        ws.append((jnp.where(up_ref[l] > 0.5, qf, kf) * e).astype(BF16))
    qg = (qf * jnp.exp(gcum)).astype(BF16)
    gend = gcum[c - 1:c, :]
    kend = (kf * jnp.exp(gend - gcum)).astype(BF16)
    decay = jnp.exp(gend)
    gate = g_ref[...]

    for h in range(A_HEADS):
        sl = slice(h * A_DK, (h + 1) * A_DK)
        att = _dot_nt(qb[:, sl], kb[:, sl]) * pair_ref[HGRN_LEVELS]
        for l in range(HGRN_LEVELS):
            w = ws[l][:, sl]
            att = att + _dot_nt(w, w) * pair_ref[l]
        s_t = s_scr[h]
        o = _dot_nt(qg[:, sl], s_t.astype(BF16)) + _dot(att.astype(BF16), vb[:, sl])
        ms = jnp.mean(o * o, axis=-1, keepdims=True)
        on = o * lax.rsqrt(ms + RMS_EPS) * gate * ga_ref[:, sl].astype(F32)
        oa_ref[:, sl] = on.astype(BF16)
        s_scr[h] = s_t * decay[:, sl] + _dot_tn(vb[:, sl], kend[:, sl])

    @pl.when(step == pl.num_programs(1) - 1)
    def _():
        for h in range(A_HEADS):
            st_ref[0, h] = s_scr[h].T


def _hgrn_prompt(qa, lf, va, ga, norm_g, bsz, t):
    c = HGRN_CHUNK
    nc = t // c
    tri, upper, pair = _hgrn_constants()
    blk = lambda: pl.BlockSpec((c, A_KEY), lambda b, i: (b * nc + i, 0))
    const = lambda a: pl.BlockSpec(a.shape, lambda b, i: (0,) * a.ndim)
    return pl.pallas_call(
        _hgrn_kernel,
        grid=(bsz, nc),
        in_specs=[blk(), blk(), blk(), blk(), pl.BlockSpec((1, A_DV), lambda b, i: (0, 0)),
                  const(tri), const(upper), const(pair)],
        out_specs=[blk(), pl.BlockSpec((1, A_HEADS, A_DK, A_DV), lambda b, i: (b, 0, 0, 0))],
        out_shape=[jax.ShapeDtypeStruct((bsz * t, A_WIDTH), BF16),
                   jax.ShapeDtypeStruct((bsz, A_HEADS, A_DK, A_DV), F32)],
        scratch_shapes=[pltpu.VMEM((A_HEADS, A_DV, A_DK), F32)],
        compiler_params=_cparams(("arbitrary", "arbitrary")),
        name="hgrn_prompt",
    )(qa, lf, va, ga, norm_g, tri, upper, pair)


def _swa_kernel(sink_ref, q_ref, k_ref, v_ref, o_ref, kprev, vprev):
    w = WINDOW
    i = pl.program_id(1)

    @pl.when(i == 0)
    def _():
        kprev[...] = jnp.zeros_like(kprev)
        vprev[...] = jnp.zeros_like(vprev)

    kc = k_ref[...]
    vc = v_ref[...]
    kk = jnp.concatenate([kprev[...], kc], axis=0)
    vv = jnp.concatenate([vprev[...], vc], axis=0)
    kr = pltpu.roll(kk, B_HEAD_DIM, 1)
    vr = pltpu.roll(vv, B_HEAD_DIM, 1)
    lo2 = lax.broadcasted_iota(I32, kk.shape, 1) < B_HEAD_DIM
    zero = jnp.zeros_like(kk)
    k_lo = [jnp.where(lo2, kk, zero).astype(BF16), jnp.where(lo2, kr, zero).astype(BF16)]
    k_hi = [jnp.where(lo2, zero, kr).astype(BF16), jnp.where(lo2, zero, kk).astype(BF16)]
    v_dup = [jnp.where(lo2, vv, vr).astype(BF16), jnp.where(lo2, vr, vv).astype(BF16)]

    qi = lax.broadcasted_iota(I32, (w, 2 * w), 0)
    kj = lax.broadcasted_iota(I32, (w, 2 * w), 1)
    valid = (kj >= qi) & (kj <= qi + w) & ((kj >= w) | (i > 0))
    lo1 = lax.broadcasted_iota(I32, (w, LANES), 1) < B_HEAD_DIM

    for j in range(B_WIDTH // LANES):
        g = (2 * j) // B_GROUP
        qblk = q_ref[:, j * LANES:(j + 1) * LANES]
        res = []
        for half, kmat in enumerate((k_lo[g], k_hi[g])):
            sk = sink_ref[2 * j + half]
            s = jnp.where(valid, _dot_nt(qblk, kmat), NEG_BIG)
            m = jnp.maximum(jnp.max(s, axis=-1, keepdims=True), sk)
            p = jnp.exp(s - m)
            denom = jnp.sum(p, axis=-1, keepdims=True) + jnp.exp(sk - m)
            res.append(_dot(p.astype(BF16), v_dup[g]) / denom)
        o_ref[:, j * LANES:(j + 1) * LANES] = jnp.where(lo1, res[0], res[1]).astype(BF16)

    kprev[...] = kc
    vprev[...] = vc


def _swa_prompt(sinks, qb, kb, vb, bsz, t):
    w = WINDOW
    nb = t // w
    return pl.pallas_call(
        _swa_kernel,
        grid_spec=pltpu.PrefetchScalarGridSpec(
            num_scalar_prefetch=1,
            grid=(bsz, nb),
            in_specs=[pl.BlockSpec((w, B_WIDTH), lambda b, i, s: (b * nb + i, 0)),
                      pl.BlockSpec((w, B_KV_WIDTH), lambda b, i, s: (b * nb + i, 0)),
                      pl.BlockSpec((w, B_KV_WIDTH), lambda b, i, s: (b * nb + i, 0))],
            out_specs=pl.BlockSpec((w, B_WIDTH), lambda b, i, s: (b * nb + i, 0)),
            scratch_shapes=[pltpu.VMEM((w, B_KV_WIDTH), F32), pltpu.VMEM((w, B_KV_WIDTH), F32)]),
        out_shape=jax.ShapeDtypeStruct((bsz * t, B_WIDTH), BF16),
        compiler_params=_cparams(("arbitrary", "arbitrary")),
        name="swa_prompt",
    )(sinks, qb, kb, vb)


SAMPLE_BLOCK = 8


def _sample_kernel(sink_ref, st_ref, lft_ref, qat_ref, va_ref, ga_ref, g_ref, q3_ref, kn_ref, vn_ref,
                   ck_ref, cv_ref, st_out, oa_ref, ob_ref, ck_out, cv_out, o_scr):
    w = WINDOW
    row = lax.broadcasted_iota(I32, (w, B_KV_WIDTH), 0)
    for i in range(SAMPLE_BLOCK):
        for h in range(A_HEADS):
            sl = slice(h * A_DV, (h + 1) * A_DV)
            fcol = jnp.exp(lft_ref[h, 0][:, i:i + 1])
            qcol = qat_ref[h, 0][:, i:i + 1]
            vrow = va_ref[i:i + 1, sl].astype(F32)
            s_new = st_ref[i, h] * fcol + (1.0 - fcol) * vrow
            st_out[i, h] = s_new
            o_scr[i:i + 1, sl] = jnp.sum(s_new * qcol, axis=0, keepdims=True)
        kc = ck_ref[i]
        vc = cv_ref[i]
        kn = kn_ref[i:i + 1, :]
        vn = vn_ref[i:i + 1, :]
        q3 = q3_ref[i]
        s = _dot_nt(q3.astype(BF16), kc.astype(BF16))
        s_new_key = jnp.sum(q3 * kn, axis=-1, keepdims=True)
        sk = sink_ref[...][:, 0:1]
        m = jnp.maximum(jnp.maximum(jnp.max(s, axis=-1, keepdims=True), s_new_key), sk)
        p = jnp.exp(s - m)
        pn = jnp.exp(s_new_key - m)
        denom = jnp.sum(p, axis=-1, keepdims=True) + pn + jnp.exp(sk - m)
        ob_ref[i] = (_dot(p.astype(BF16), vc.astype(BF16)) + pn * vn) / denom
        ck_out[i] = jnp.where(row == w - 1, kn, pltpu.roll(kc, w - 1, 0))
        cv_out[i] = jnp.where(row == w - 1, vn, pltpu.roll(vc, w - 1, 0))
    for h in range(A_HEADS):
        sl = slice(h * A_DV, (h + 1) * A_DV)
        o = o_scr[:, sl]
        ms = jnp.mean(o * o, axis=-1, keepdims=True)
        oa_ref[:, sl] = (o * lax.rsqrt(ms + RMS_EPS) * g_ref[...] * ga_ref[:, sl].astype(F32)).astype(BF16)


def _sample_step(l, sinks8, state, lft, qat, va, ga, norm_g, q3, kn, vn, ck, cv):
    nb = state.shape[1]
    sb = SAMPLE_BLOCK
    steps = nb // sb
    w = WINDOW
    b4 = lambda: pl.BlockSpec((sb, A_HEADS, A_DK, A_DV), lambda i: (i, 0, 0, 0))
    b4_in = pl.BlockSpec((None, sb, A_HEADS, A_DK, A_DV), lambda i: (l, i, 0, 0, 0))
    c3_in = lambda: pl.BlockSpec((None, sb, w, B_KV_WIDTH), lambda i: (l, i, 0, 0))
    t4 = lambda: pl.BlockSpec((A_HEADS, 1, A_DK, sb), lambda i: (0, i, 0, 0))
    r2 = lambda wd: pl.BlockSpec((sb, wd), lambda i: (i, 0))
    c3 = lambda: pl.BlockSpec((sb, w, B_KV_WIDTH), lambda i: (i, 0, 0))
    return pl.pallas_call(
        _sample_kernel,
        grid=(steps,),
        in_specs=[pl.BlockSpec((B_Q_HEADS, LANES), lambda i: (0, 0)),
                  b4_in, t4(), t4(), r2(A_WIDTH), r2(A_WIDTH), pl.BlockSpec((1, A_DV), lambda i: (0, 0)),
                  pl.BlockSpec((sb, B_Q_HEADS, LANES), lambda i: (i, 0, 0)), r2(B_KV_WIDTH), r2(B_KV_WIDTH),
                  c3_in(), c3_in()],
        out_specs=[b4(), r2(A_WIDTH), pl.BlockSpec((sb, B_Q_HEADS, LANES), lambda i: (i, 0, 0)), c3(), c3()],
        out_shape=[jax.ShapeDtypeStruct(state.shape[1:], F32),
                   jax.ShapeDtypeStruct((nb, A_WIDTH), BF16),
                   jax.ShapeDtypeStruct((nb, B_Q_HEADS, LANES), F32),
                   jax.ShapeDtypeStruct((nb, w, B_KV_WIDTH), F32),
                   jax.ShapeDtypeStruct((nb, w, B_KV_WIDTH), F32)],
        scratch_shapes=[pltpu.VMEM((sb, A_WIDTH), F32)],
        compiler_params=_cparams(("arbitrary",)),
        name="sample_step",
    )(sinks8, state, lft, qat, va, ga, norm_g, q3, kn, vn, ck, cv)


def _merge_kernel(x_ref, oa_ref, ob_ref, sga_ref, sgb_ref, wa_ref, wb_ref, wo_ref, g_ref, b_ref, h_ref):
    merged = (sga_ref[...].astype(F32) * _dot(oa_ref[...], wa_ref[...])
              + sgb_ref[...].astype(F32) * _dot(ob_ref[...], wb_ref[...]))
    mix = _dot(merged.astype(BF16), wo_ref[...])
    _store_tiled(h_ref, _layer_norm(DN_ALPHA * x_ref[...] + mix, g_ref[...], b_ref[...]))


def _merge(x, oa, ob, sga, sgb, wa, wb, wo, g, b, n_rows, x_off_blocks, out_rows, out_off_blocks, tm, h_prev=None):
    nt = n_rows // tm
    row = lambda wd: pl.BlockSpec((tm, wd), lambda i: (i, 0))
    const = lambda a: pl.BlockSpec(a.shape, lambda i: (0, 0))
    args = [x, oa, ob, sga, sgb, wa, wb, wo, g, b]
    in_specs = [pl.BlockSpec((tm, D_MODEL), lambda i: (i + x_off_blocks, 0)),
                row(A_WIDTH), row(B_WIDTH), row(D_MODEL), row(D_MODEL),
                const(wa), const(wb), const(wo), const(g), const(b)]
    kern = _merge_kernel
    aliases = {}
    if h_prev is not None:
        args.append(h_prev)
        in_specs.append(pl.BlockSpec(memory_space=pl.ANY))
        aliases = {len(args) - 1: 0}
        kern = lambda *refs: _merge_kernel(*refs[:10], refs[11])
    return pl.pallas_call(
        kern,
        grid=(nt,),
        in_specs=in_specs,
        out_specs=pl.BlockSpec((tm * SUBLANES, LANES), lambda i: (i + out_off_blocks, 0)),
        out_shape=jax.ShapeDtypeStruct((out_rows * SUBLANES, LANES), F32),
        input_output_aliases=aliases,
        compiler_params=_cparams(("arbitrary",)),
        name="merge",
    )(*args)


def _mixer_layer(l, xp, xs, xs_off, n_prompt, bsz, t, n_sample, prm, h_buf):
    tm = ROW_TILE
    p = _proj(xp, prm["w_in"][l], prm["cos_p"], prm["sin_p"], prm["lower"][l], n_prompt, 0, tm)
    qa, lf, va, ga, qb, kb, vb, sga, sgb = p
    oa, st_p = _hgrn_prompt(qa, lf, va, ga, prm["norm_g"][l], bsz, t)
    ob = _swa_prompt(prm["sinks"][l], qb, kb, vb, bsz, t)
    n_all = n_prompt + n_sample
    h_all = _merge(xp, oa, ob, sga, sgb, prm["wa"][l], prm["wb"][l], prm["wo"][l], prm["ln1_g"][l],
                   prm["ln1_b"][l], n_prompt, 0, n_all, 0, tm, h_prev=h_buf)
    kp = kb.reshape(bsz, t, B_KV_HEADS, B_HEAD_DIM)[:, -WINDOW:]
    vp = vb.reshape(bsz, t, B_KV_HEADS, B_HEAD_DIM)[:, -WINDOW:]

    ts = n_sample
    ps = _proj(xs, prm["w_in"][l], prm["cos_s"], prm["sin_s"], prm["lower"][l], ts, xs_off, ts)
    qa_s, lf_s, va_s, ga_s, qb_s, kb_s, vb_s, sga_s, sgb_s = ps
    sb = SAMPLE_BLOCK
    to_t = lambda a: a.reshape(ts // sb, sb, A_HEADS, A_DK).transpose(2, 0, 3, 1)
    qh = qb_s.astype(F32).reshape(ts, B_Q_HEADS, B_HEAD_DIM)
    z = jnp.zeros_like(qh[:, :B_GROUP])
    q3 = jnp.concatenate([jnp.concatenate([qh[:, :B_GROUP], z], axis=-1),
                          jnp.concatenate([z, qh[:, B_GROUP:]], axis=-1)], axis=1)
    st_s, oa_s, ob3, ck_s, cv_s = _sample_step(
        l, prm["sinks8"][l], prm["state"], to_t(lf_s), to_t(qa_s.astype(F32)), va_s, ga_s, prm["norm_g"][l],
        q3, kb_s, vb_s, prm["cache_k"], prm["cache_v"])
    ob_s = jnp.concatenate([ob3[:, :B_GROUP, :B_HEAD_DIM], ob3[:, B_GROUP:, B_HEAD_DIM:]], axis=1)
    ob_s = ob_s.reshape(ts, B_WIDTH).astype(BF16)
    h_all = _merge(xs, oa_s, ob_s, sga_s, sgb_s, prm["wa"][l], prm["wb"][l], prm["wo"][l], prm["ln1_g"][l],
                   prm["ln1_b"][l], ts, xs_off, n_all, n_prompt // ts, ts, h_prev=h_all)
    return h_all, (kp, vp, st_p, ck_s, cv_s, st_s)


def _router_kernel(h_ref, wr_ref, bias_ref, e_ref, w_ref, mask_ref, cnt_ref):
    tm = h_ref.shape[0] // SUBLANES
    gsz = GROUP_SIZE

    @pl.when(pl.program_id(0) == 0)
    def _():
        cnt_ref[...] = jnp.zeros_like(cnt_ref)

    logits = lax.dot_general(wr_ref[...], _load_tiled(h_ref, tm), (((1,), (1,)), ((), ())),
                             precision=lax.Precision.HIGHEST, preferred_element_type=F32)
    scores = _sigmoid(logits)
    sel = scores + bias_ref[...][:, 0:1]
    rowi = lax.broadcasted_iota(I32, (gsz, tm), 0)
    neg_inf = -jnp.inf
    blocks = [sel[g * gsz:(g + 1) * gsz] for g in range(N_GROUPS)]
    sblocks = [scores[g * gsz:(g + 1) * gsz] for g in range(N_GROUPS)]

    gscore = []
    for blk in blocks:
        m1 = jnp.max(blk, axis=0, keepdims=True)
        i1 = jnp.min(jnp.where(blk == m1, rowi, gsz), axis=0, keepdims=True)
        m2 = jnp.max(jnp.where(rowi == i1, neg_inf, blk), axis=0, keepdims=True)
        gscore.append(m1 + m2)
    work = []
    for g in range(N_GROUPS):
        ahead = jnp.zeros((1, tm), I32)
        for g2 in range(N_GROUPS):
            if g2 != g:
                beats = (gscore[g2] > gscore[g]) | ((gscore[g2] == gscore[g]) & (g2 < g))
                ahead = ahead + beats.astype(I32)
        work.append(jnp.where(ahead < TOPK_GROUPS, blocks[g], NEG_BIG))

    chosen = [jnp.zeros((gsz, tm), F32) for _ in range(N_GROUPS)]
    es, ws = [], []
    for _ in range(TOP_K):
        m = work[0]
        for g in range(1, N_GROUPS):
            m = jnp.maximum(m, work[g])
        m = jnp.max(m, axis=0, keepdims=True)
        cand = jnp.where(work[0] == m, rowi, N_EXPERTS)
        for g in range(1, N_GROUPS):
            cand = jnp.minimum(cand, jnp.where(work[g] == m, rowi + g * gsz, N_EXPERTS))
        idx = jnp.min(cand, axis=0, keepdims=True)
        wj = jnp.zeros((1, tm), F32)
        for g in range(N_GROUPS):
            hit = (rowi + g * gsz) == idx
            wj = wj + jnp.sum(jnp.where(hit, sblocks[g], 0.0), axis=0, keepdims=True)
            chosen[g] = jnp.where(hit, 1.0, chosen[g])
            work[g] = jnp.where(hit, neg_inf, work[g])
        es.append(idx)
        ws.append(wj)
    wsum = ws[0]
    for j in range(1, TOP_K):
        wsum = wsum + ws[j]
    for j in range(TOP_K):
        e_ref[j:j + 1, :] = es[j]
        w_ref[j:j + 1, :] = ws[j] / wsum * ROUTED_SCALE
    for g in range(N_GROUPS):
        rows = slice(g * gsz, (g + 1) * gsz)
        mask_ref[rows, :] = chosen[g]
        part = chosen[g][:, 0:LANES]
        for c in range(1, tm // LANES):
            part = part + chosen[g][:, c * LANES:(c + 1) * LANES]
        cnt_ref[rows, :] = cnt_ref[rows, :] + part


def _router(h_all, wr_t, bias_b, tm):
    n = h_all.shape[0] // SUBLANES
    col = lambda r: pl.BlockSpec((r, tm), lambda i: (0, i))
    return pl.pallas_call(
        _router_kernel,
        grid=(n // tm,),
        in_specs=[pl.BlockSpec((tm * SUBLANES, LANES), lambda i: (i, 0)),
                  pl.BlockSpec((N_EXPERTS, D_MODEL), lambda i: (0, 0)),
                  pl.BlockSpec((N_EXPERTS, LANES), lambda i: (0, 0))],
        out_specs=[col(TOP_K), col(TOP_K), col(N_EXPERTS), pl.BlockSpec((N_EXPERTS, LANES), lambda i: (0, 0))],
        out_shape=[jax.ShapeDtypeStruct((TOP_K, n), I32), jax.ShapeDtypeStruct((TOP_K, n), F32),
                   jax.ShapeDtypeStruct((N_EXPERTS, n), F32), jax.ShapeDtypeStruct((N_EXPERTS, LANES), F32)],
        compiler_params=_cparams(("arbitrary",)),
        name="router",
    )(h_all, wr_t, bias_b)


def _rank_kernel(mask_ref, e_ref, offs_ref, triu_ref, dest_ref, carry):
    tm = mask_ref.shape[1]

    @pl.when(pl.program_id(0) == 0)
    def _():
        carry[...] = jnp.zeros_like(carry)

    mk = mask_ref[...]
    rank = _dot(mk.astype(BF16), triu_ref[...])
    dest_full = rank + (offs_ref[...][:, 0:1] + carry[...][:, 0:1])
    rowi = lax.broadcasted_iota(I32, (N_EXPERTS, tm), 0)
    for j in range(TOP_K):
        d = jnp.sum(jnp.where(rowi == e_ref[j:j + 1, :], dest_full, 0.0), axis=0, keepdims=True)
        dest_ref[j:j + 1, :] = d.astype(I32)
    carry[...] = carry[...] + jnp.sum(mk, axis=1, keepdims=True)


def _rank(mask_t, e_t, offs_b, tm):
    n = mask_t.shape[1]
    r = np.arange(tm)
    triu = jnp.asarray((r[:, None] < r[None, :]).astype(np.float32), BF16)
    return pl.pallas_call(
        _rank_kernel,
        grid=(n // tm,),
        in_specs=[pl.BlockSpec((N_EXPERTS, tm), lambda i: (0, i)),
                  pl.BlockSpec((TOP_K, tm), lambda i: (0, i)),
                  pl.BlockSpec((N_EXPERTS, LANES), lambda i: (0, 0)),
                  pl.BlockSpec((tm, tm), lambda i: (0, 0))],
        out_specs=pl.BlockSpec((TOP_K, tm), lambda i: (0, i)),
        out_shape=jax.ShapeDtypeStruct((TOP_K, n), I32),
        scratch_shapes=[pltpu.VMEM((N_EXPERTS, LANES), F32)],
        compiler_params=_cparams(("arbitrary",)),
        name="rank",
    )(mask_t, e_t, offs_b, triu)


INV_COLS = 512
TOKEN_RADIX = 128


def _inverse_kernel(dest_ref, inv_ref):
    tm = dest_ref.shape[1]
    nq = inv_ref.shape[0]

    @pl.when(pl.program_id(0) == 0)
    def _():
        inv_ref[...] = jnp.zeros_like(inv_ref)

    tok = pl.program_id(0) * tm + lax.broadcasted_iota(I32, (1, tm), 1)
    t_hi = jnp.right_shift(tok, TOKEN_RADIX.bit_length() - 1).astype(F32)
    t_lo = jnp.bitwise_and(tok, TOKEN_RADIX - 1).astype(F32)
    qi = lax.broadcasted_iota(I32, (nq, tm), 0)
    si = lax.broadcasted_iota(I32, (INV_COLS, tm), 0)
    acc_hi = jnp.zeros(inv_ref.shape, F32)
    acc_lo = jnp.zeros(inv_ref.shape, F32)
    for j in range(TOP_K):
        d = dest_ref[j:j + 1, :]
        at_q = qi == jnp.right_shift(d, INV_COLS.bit_length() - 1)
        col = jnp.where(si == jnp.bitwise_and(d, INV_COLS - 1), 1.0, 0.0).astype(BF16)
        acc_hi = acc_hi + _dot_nt(jnp.where(at_q, t_hi, 0.0).astype(BF16), col)
        acc_lo = acc_lo + _dot_nt(jnp.where(at_q, t_lo, 0.0).astype(BF16), col)
    inv_ref[...] = inv_ref[...] + (acc_hi * float(TOKEN_RADIX) + acc_lo)


def _inverse_map(dest_t, tm):
    n = dest_t.shape[1]
    n_rows = n * TOP_K
    nq = -(-(n_rows // INV_COLS) // 8) * 8
    inv = pl.pallas_call(
        _inverse_kernel,
        grid=(n // tm,),
        in_specs=[pl.BlockSpec((TOP_K, tm), lambda i: (0, i))],
        out_specs=pl.BlockSpec((nq, INV_COLS), lambda i: (0, 0)),
        out_shape=jax.ShapeDtypeStruct((nq, INV_COLS), F32),
        compiler_params=_cparams(("arbitrary",)),
        name="inverse_map",
    )(dest_t)
    return inv.reshape(-1)[:n_rows].astype(I32).reshape(n_rows // EXPERT_ROW_TILE, 1, EXPERT_ROW_TILE)


DMA_THREADS = 2


def _token_copy(src_ref, token, dst_ref, dst_row, sem):
    return pltpu.make_async_copy(src_ref.at[pl.ds(pl.multiple_of(token * SUBLANES, SUBLANES), SUBLANES)],
                                 dst_ref.at[pl.ds(dst_row * SUBLANES, SUBLANES)], sem)


GATHER_AHEAD = 2
GATHER_SLOTS = GATHER_AHEAD + 1


def _expert_kernel(n_tiles, tile_ref, exp_ref, valid_ref, first_ref, newexp_ref, lo_ref, hi_ref,
                   inv0_ref, inv1_ref, inva_ref, h_ref, wg_ref, wu_ref, wd_ref, y_ref, xbuf, wgb, wub, wdb, sem):
    i = pl.program_id(0)
    rows = xbuf.shape[1] // SUBLANES
    tile = tile_ref[i]
    slot = tile % GATHER_SLOTS
    ahead_slot = (tile + GATHER_AHEAD) % GATHER_SLOTS
    valid = valid_ref[i] == 1
    first = first_ref[i] == 1

    def wait_rows(s):
        pltpu.make_async_copy(h_ref.at[pl.ds(0, rows * SUBLANES)], xbuf.at[s], sem.at[s]).wait()

    def gather_loop(idx_ref, s):
        def issue(k, c):
            for u in range(SUBLANES):
                r = k * SUBLANES + u
                pltpu.make_async_copy(
                    h_ref.at[pl.ds(pl.multiple_of(idx_ref[0, 0, r] * SUBLANES, SUBLANES), SUBLANES)],
                    xbuf.at[s, pl.ds(pl.multiple_of(r * SUBLANES, SUBLANES), SUBLANES)], sem.at[s]).start()
            return c

        lax.fori_loop(0, rows // SUBLANES, issue, 0)

    @pl.when(valid & first)
    def _():
        @pl.when(i == 0)
        def _():
            gather_loop(inv0_ref, 0)
            gather_loop(inv1_ref, 1)

        wait_rows(slot)

    @pl.when(valid & (newexp_ref[i] == 1))
    def _():
        wgb[...] = wg_ref[0].astype(BF16)
        wub[...] = wu_ref[0].astype(BF16)
        wdb[...] = wd_ref[0].astype(BF16)

    def compute(first_visit):
        xb = _load_tiled(xbuf.at[slot], rows).astype(BF16)
        if first_visit:
            for r in range(rows):
                _token_copy(h_ref, inva_ref[0, 0, r], xbuf.at[ahead_slot], r, sem.at[ahead_slot]).start(
                    priority=r % DMA_THREADS)
        gate = _dot(xb, wgb[...])
        up = _dot(xb, wub[...])
        y = _dot((_silu(gate) * up).astype(BF16), wdb[...])
        rowi = lax.broadcasted_iota(I32, y.shape, 0)
        mine = (rowi >= lo_ref[i]) & (rowi < hi_ref[i])
        _store_tiled(y_ref, jnp.where(mine, y, 0.0 if first_visit else _load_tiled(y_ref, rows)))

    @pl.when(valid & first)
    def _():
        compute(True)

    @pl.when(valid & jnp.logical_not(first))
    def _():
        compute(False)

    @pl.when(valid & first & (tile >= n_tiles - GATHER_AHEAD))
    def _():
        wait_rows(ahead_slot)


def _group_metadata(counts, n_rows):
    tmo = EXPERT_ROW_TILE
    n_tiles = n_rows // tmo
    ends = jnp.cumsum(counts)
    offs = ends - counts
    first_tile = offs // tmo
    n_t = jnp.where(counts > 0, (ends - 1) // tmo - first_tile + 1, 0)
    cum = jnp.cumsum(n_t)
    base = cum - n_t
    n_items = n_tiles + N_EXPERTS
    idx = jnp.arange(n_items, dtype=I32)
    valid = (idx < cum[-1]).astype(I32)
    idc = jnp.minimum(idx, cum[-1] - 1)
    e = jnp.minimum(jnp.sum((cum[None, :] <= idc[:, None]).astype(I32), axis=1), N_EXPERTS - 1)
    onehot = (e[:, None] == jnp.arange(N_EXPERTS, dtype=I32)[None, :]).astype(I32)
    pick = lambda table: jnp.sum(onehot * table[None, :].astype(I32), axis=1)
    tile = (pick(first_tile) + idc - pick(base)).astype(I32)
    one = jnp.ones((1,), I32)
    first = jnp.concatenate([one, (tile[1:] != tile[:-1]).astype(I32)])
    new_expert = jnp.concatenate([one, (e[1:] != e[:-1]).astype(I32)])
    lo = jnp.clip(pick(offs) - tile * tmo, 0, tmo).astype(I32)
    hi = jnp.clip(pick(ends) - tile * tmo, 0, tmo).astype(I32)
    return tile, e, valid, first, new_expert, lo, hi


def _experts(l, meta, inv3, h_all, wg, wu, wd):
    tmo = EXPERT_ROW_TILE
    n_rows = inv3.shape[0] * tmo
    n_items = meta[0].shape[0]
    wspec = lambda shp: pl.BlockSpec((None, 1) + shp, lambda i, tl, ex, *_: (l, ex[i], 0, 0))
    n_tiles = inv3.shape[0]
    assert n_tiles > GATHER_AHEAD
    idx_spec = lambda nxt: pl.BlockSpec(
        (1, 1, tmo), lambda i, tl, *_: (jnp.minimum(tl[i] + nxt, n_tiles - 1), 0, 0), memory_space=pltpu.SMEM)
    return pl.pallas_call(
        functools.partial(_expert_kernel, n_tiles),
        grid_spec=pltpu.PrefetchScalarGridSpec(
            num_scalar_prefetch=7,
            grid=(n_items,),
            in_specs=[idx_spec(0), idx_spec(1), idx_spec(GATHER_AHEAD), pl.BlockSpec(memory_space=pl.ANY),
                      wspec((D_MODEL, D_EXPERT)), wspec((D_MODEL, D_EXPERT)), wspec((D_EXPERT, D_MODEL))],
            out_specs=pl.BlockSpec((tmo * SUBLANES, LANES), lambda i, tl, *_: (tl[i], 0)),
            scratch_shapes=[pltpu.VMEM((GATHER_SLOTS, tmo * SUBLANES, LANES), F32),
                            pltpu.VMEM((D_MODEL, D_EXPERT), BF16), pltpu.VMEM((D_MODEL, D_EXPERT), BF16),
                            pltpu.VMEM((D_EXPERT, D_MODEL), BF16),
                            pltpu.SemaphoreType.DMA((GATHER_SLOTS,))]),
        out_shape=jax.ShapeDtypeStruct((n_rows * SUBLANES, LANES), F32),
        compiler_params=_cparams(("arbitrary",)),
        name="experts",
    )(*meta, inv3, inv3, inv3, h_all, wg, wu, wd)


def _post_kernel(n_steps, n_prompt_steps, dest0_ref, dest1_ref, desta_ref, h_ref, wt_ref, y_ref, wsg_ref, wsu_ref,
                 wsd_ref, g_ref, b_ref, outp_ref, outs_ref, *scratch):
    bufs, sem = scratch[:GATHER_SLOTS], scratch[GATHER_SLOTS]
    i = pl.program_id(0)
    tm = outp_ref.shape[0]

    def wait_rows(s):
        for j in range(TOP_K):
            pltpu.make_async_copy(y_ref.at[pl.ds(0, tm * SUBLANES)], bufs[s].at[j], sem.at[s]).wait()

    def gather_loop(idx_ref, s):
        def issue(k, c):
            for u in range(SUBLANES):
                t = k * SUBLANES + u
                for j in range(TOP_K):
                    pltpu.make_async_copy(
                        y_ref.at[pl.ds(pl.multiple_of(idx_ref[j, t] * SUBLANES, SUBLANES), SUBLANES)],
                        bufs[s].at[j, pl.ds(pl.multiple_of(t * SUBLANES, SUBLANES), SUBLANES)], sem.at[s]).start()
            return c

        lax.fori_loop(0, tm // SUBLANES, issue, 0)

    @pl.when(i == 0)
    def _():
        gather_loop(dest0_ref, 0)
        gather_loop(dest1_ref, 1)

    def step(s):
        a = (s + GATHER_AHEAD) % GATHER_SLOTS
        wait_rows(s)
        h = _load_tiled(h_ref, tm)
        hb = h.astype(BF16)
        for t in range(tm):
            for j in range(TOP_K):
                _token_copy(y_ref, desta_ref[j, t], bufs[a].at[j], t, sem.at[a]).start(priority=j % DMA_THREADS)
        shared = _dot((_silu(_dot(hb, wsg_ref[...])) * _dot(hb, wsu_ref[...])).astype(BF16), wsd_ref[...])
        wt = wt_ref[...]
        routed = _load_tiled(bufs[s].at[0], tm) * wt[:, 0:1]
        for j in range(1, TOP_K):
            routed = routed + _load_tiled(bufs[s].at[j], tm) * wt[:, j:j + 1]
        res = _layer_norm(DN_ALPHA * h + (routed + shared), g_ref[...], b_ref[...])

        @pl.when(i < n_prompt_steps)
        def _():
            outp_ref[...] = res

        @pl.when(i >= n_prompt_steps)
        def _():
            outs_ref[...] = res

        @pl.when(i >= n_steps - GATHER_AHEAD)
        def _():
            wait_rows(a)

    for s in range(GATHER_SLOTS):
        pl.when(i % GATHER_SLOTS == s)(functools.partial(step, s))


def _post(dest_t, h_all, w_tok, y, wsg, wsu, wsd, g, b, tm, n_prompt):
    n = h_all.shape[0] // SUBLANES
    n_steps = n // tm
    n_prompt_steps = n_prompt // tm
    assert n_steps > GATHER_AHEAD and n_prompt % tm == 0 and 0 < n_prompt_steps < n_steps
    const = lambda a: pl.BlockSpec(a.shape, lambda i: (0, 0))
    idx_spec = lambda nxt: pl.BlockSpec((TOP_K, tm), lambda i: (0, jnp.minimum(i + nxt, n_steps - 1)),
                                        memory_space=pltpu.SMEM)
    return pl.pallas_call(
        functools.partial(_post_kernel, n_steps, n_prompt_steps),
        grid=(n_steps,),
        in_specs=[idx_spec(0), idx_spec(1), idx_spec(GATHER_AHEAD),
                  pl.BlockSpec((tm * SUBLANES, LANES), lambda i: (i, 0)),
                  pl.BlockSpec((tm, TOP_K), lambda i: (i, 0)),
                  pl.BlockSpec(memory_space=pl.ANY),
                  const(wsg), const(wsu), const(wsd), const(g), const(b)],
        out_specs=[pl.BlockSpec((tm, D_MODEL), lambda i: (jnp.minimum(i, n_prompt_steps - 1), 0)),
                   pl.BlockSpec((tm, D_MODEL), lambda i: (jnp.maximum(i - n_prompt_steps, 0), 0))],
        out_shape=[jax.ShapeDtypeStruct((n_prompt, D_MODEL), F32),
                   jax.ShapeDtypeStruct((n - n_prompt, D_MODEL), F32)],
        scratch_shapes=[pltpu.VMEM((TOP_K, tm * SUBLANES, LANES), F32) for _ in range(GATHER_SLOTS)]
        + [pltpu.SemaphoreType.DMA((GATHER_SLOTS,))],
        compiler_params=_cparams(("arbitrary",)),
        name="moe_post",
    )(dest_t, dest_t, dest_t, h_all, w_tok, y, wsg, wsu, wsd, g, b)


def _moe_layer(l, h_all, n_prompt, prm):
    tm = MOE_ROW_TILE
    e_t, w_t, mask_t, cnt = _router(h_all, prm["wr_t"][l], prm["rbias"][l], tm)
    counts = jnp.sum(cnt, axis=1).astype(I32)
    offs = jnp.cumsum(counts) - counts
    offs_b = jnp.broadcast_to(offs.astype(F32)[:, None], (N_EXPERTS, LANES))
    dest_t = _rank(mask_t, e_t, offs_b, tm)
    inv3 = _inverse_map(dest_t, tm)
    meta = _group_metadata(counts, h_all.shape[0] // SUBLANES * TOP_K)
    y = _experts(l, meta, inv3, h_all, prm["w_exp_gate"], prm["w_exp_up"], prm["w_exp_down"])
    return _post(dest_t, h_all, w_t.T, y, prm["wsg"][l], prm["wsu"][l], prm["wsd"][l],
                 prm["ln2_g"][l], prm["ln2_b"][l], POST_ROW_TILE, n_prompt)


def kernel(x_prompt, x_sample, cache_k, cache_v, state_hgrn, w_in, hgrn_lower_bounds, hgrn_norm_g, attn_sinks, w_branch_a, w_branch_b, w_out, ln1_g, ln1_b, w_router, router_bias, w_exp_gate, w_exp_up, w_exp_down, w_sh_gate, w_sh_up, w_sh_down, ln2_g, ln2_b):
    bsz, t, d = x_prompt.shape
    n_sample = x_sample.shape[0] * x_sample.shape[1]
    n_prompt = bsz * t
    depth = w_in.shape[0]
    assert d == D_MODEL and x_sample.shape[1] == 1 and n_prompt % ROW_TILE == 0 and t % ROW_TILE == 0
    assert n_prompt % n_sample == 0 and (n_prompt + n_sample) % MOE_ROW_TILE == 0
    assert n_prompt % POST_ROW_TILE == 0 and n_sample % POST_ROW_TILE == 0
    assert ((n_prompt + n_sample) * TOP_K) % EXPERT_ROW_TILE == 0 and n_sample % SAMPLE_BLOCK == 0

    lb_prob = jax.nn.softmax(hgrn_lower_bounds.astype(F32), axis=0)
    lower = (jnp.cumsum(lb_prob, axis=0) - lb_prob[0])[:, None, :]
    cos_p, sin_p = _rope_tables(jnp.arange(t))
    cos_s, sin_s = _rope_tables(jnp.full((n_sample,), PAST_LEN))
    row = lambda a: a[:, None, :]
    prm = dict(
        w_in=w_in.astype(BF16), lower=lower, cos_p=cos_p, sin_p=sin_p, cos_s=cos_s, sin_s=sin_s,
        norm_g=row(hgrn_norm_g), sinks=attn_sinks,
        sinks8=jnp.broadcast_to(attn_sinks[:, :, None], (depth, B_Q_HEADS, LANES)),
        wa=w_branch_a.astype(BF16), wb=w_branch_b.astype(BF16), wo=w_out.astype(BF16),
        ln1_g=row(ln1_g), ln1_b=row(ln1_b), ln2_g=row(ln2_g), ln2_b=row(ln2_b),
        wr_t=jnp.swapaxes(w_router, 1, 2),
        rbias=jnp.broadcast_to(router_bias[:, :, None], (depth, N_EXPERTS, LANES)),
        w_exp_gate=w_exp_gate, w_exp_up=w_exp_up, w_exp_down=w_exp_down,
        wsg=w_sh_gate.astype(BF16), wsu=w_sh_up.astype(BF16), wsd=w_sh_down.astype(BF16),
        state=state_hgrn,
        cache_k=cache_k.reshape(depth, n_sample, WINDOW, B_KV_WIDTH),
        cache_v=cache_v.reshape(depth, n_sample, WINDOW, B_KV_WIDTH),
    )

    xp, xs, xs_off = x_prompt.reshape(n_prompt, d), x_sample.reshape(n_sample, d), 0
    per_layer = []
    h_all = jnp.zeros(((n_prompt + n_sample) * SUBLANES, LANES), F32)
    for l in range(depth):
        h_all, outs = _mixer_layer(l, xp, xs, xs_off, n_prompt, bsz, t, n_sample, prm, h_all)
        xp, xs = _moe_layer(l, h_all, n_prompt, prm)
        per_layer.append(outs)

    kv_shape = (n_sample, WINDOW, B_KV_HEADS, B_HEAD_DIM)
    stack = lambda k, f=lambda a: a: jnp.stack([f(o[k]) for o in per_layer])
    return (xp.reshape(bsz, t, d), xs.reshape(n_sample, 1, d),
            stack(0), stack(1), stack(2),
            stack(3, lambda a: a.reshape(kv_shape)), stack(4, lambda a: a.reshape(kv_shape)), stack(5))
```

```python
import functools

import numpy as np
import jax
import jax.numpy as jnp
from jax import lax
from jax.experimental import pallas as pl
from jax.experimental.pallas import tpu as pltpu

F32 = jnp.float32
BF16 = jnp.bfloat16
I32 = jnp.int32

D_MODEL = 1024
DEPTH = 2
PAST_LEN = 16384
A_HEADS = 4
A_DK = 128
A_DV = 128
A_KEY = A_HEADS * A_DK
A_WIDTH = A_HEADS * A_DV
B_Q_HEADS = 8
B_KV_HEADS = 2
B_HEAD_DIM = 64
B_GROUP = B_Q_HEADS // B_KV_HEADS
B_WIDTH = B_Q_HEADS * B_HEAD_DIM
B_KV_WIDTH = B_KV_HEADS * B_HEAD_DIM
WINDOW = 128
ROPE_THETA = 10000.0
ATTN_SCALE = B_HEAD_DIM ** -0.5
N_EXPERTS = 64
TOP_K = 8
N_GROUPS = 8
GROUP_SIZE = N_EXPERTS // N_GROUPS
TOPK_GROUPS = 4
D_EXPERT = D_MODEL // 4
D_SHARED = D_EXPERT
ROUTED_SCALE = 2.5
DN_ALPHA = (2 * DEPTH) ** 0.25
LN_EPS = 1e-5
RMS_EPS = 1e-6
NEG_BIG = -1e30
TINY = 1.1754944e-38
OFF_AF = A_KEY
OFF_AI = 2 * A_KEY
OFF_AG = OFF_AI + A_WIDTH
OFF_BQ = OFF_AG + A_WIDTH
OFF_BK = OFF_BQ + B_WIDTH
OFF_BV = OFF_BK + B_KV_WIDTH
OFF_GA = OFF_BV + B_KV_WIDTH
OFF_GB = OFF_GA + D_MODEL
IN_COLS = OFF_GB + D_MODEL

LANES = 128
HGRN_CHUNK = 128
HGRN_LEVELS = 7
ROW_TILE = 1024
MOE_ROW_TILE = 384
EXPERT_ROW_TILE = 512
POST_ROW_TILE = 128
VMEM_LIMIT = 56 * 1024 * 1024


def _cparams(sem, vmem=VMEM_LIMIT):
    return pltpu.CompilerParams(dimension_semantics=sem, vmem_limit_bytes=vmem)


def _dot(a, b):
    return jnp.dot(a, b, preferred_element_type=F32)


def _dot_nt(a, b):
    return lax.dot_general(a, b, (((1,), (1,)), ((), ())), preferred_element_type=F32)


def _dot_tn(a, b):
    return lax.dot_general(a, b, (((0,), (0,)), ((), ())), preferred_element_type=F32)


def _sigmoid(x):
    return 0.5 * jnp.tanh(0.5 * x) + 0.5


def _silu(x):
    return x * _sigmoid(x)


def _split3(x):
    hi = x.astype(BF16)
    r1 = x - hi.astype(F32)
    mid = r1.astype(BF16)
    lo = (r1 - mid.astype(F32)).astype(BF16)
    return hi, mid, lo


def _layer_norm(y, g, b):
    mu = jnp.mean(y, axis=-1, keepdims=True)
    d = y - mu
    var = jnp.mean(d * d, axis=-1, keepdims=True)
    return d * lax.rsqrt(var + LN_EPS) * g + b


SUBLANES = 8
assert D_MODEL == SUBLANES * LANES


def _store_tiled(ref, val):
    rows = val.shape[0]
    for c in range(SUBLANES):
        ref[pl.ds(c, rows, stride=SUBLANES), :] = val[:, c * LANES:(c + 1) * LANES]


def _load_tiled(ref, rows):
    return jnp.concatenate([ref[pl.ds(c, rows, stride=SUBLANES), :] for c in range(SUBLANES)], axis=1)


def _log_forget(af, lower):
    ls = jnp.minimum(af, 0.0) - jnp.log1p(jnp.exp(-jnp.abs(af)))
    a = jnp.log(jnp.maximum(lower, TINY))
    b = jnp.log1p(-lower) + ls
    mixed = jnp.maximum(a, b) + jnp.log1p(jnp.exp(-jnp.abs(a - b)))
    return jnp.where(lower > 0.0, mixed, ls)


def _proj_kernel(x_ref, w_ref, cos_ref, sin_ref, low_ref,
                 qa_ref, lf_ref, va_ref, ga_ref, qb_ref, kb_ref, vb_ref, sga_ref, sgb_ref):
    xb = x_ref[...].astype(BF16)

    def mm(c0, n):
        return _dot(xb, w_ref[:, c0:c0 + n])

    qa_ref[...] = _silu(mm(0, A_KEY)).astype(BF16)
    lf_ref[...] = _log_forget(mm(OFF_AF, A_KEY), low_ref[...])
    va_ref[...] = mm(OFF_AI, A_WIDTH).astype(BF16)
    ga_ref[...] = _silu(mm(OFF_AG, A_WIDTH)).astype(BF16)

    cos = cos_ref[...]
    sin = sin_ref[...]
    lane = lax.broadcasted_iota(I32, cos.shape, 1)
    first_half = (lane & (B_HEAD_DIM // 2)) == 0

    def rope(blk):
        partner = jnp.where(first_half, pltpu.roll(blk, LANES - B_HEAD_DIM // 2, 1),
                            pltpu.roll(blk, B_HEAD_DIM // 2, 1))
        return blk * cos + partner * sin

    bq = mm(OFF_BQ, B_WIDTH)
    for j in range(B_WIDTH // LANES):
        sl = slice(j * LANES, (j + 1) * LANES)
        qb_ref[:, sl] = (rope(bq[:, sl]) * ATTN_SCALE).astype(BF16)
    kb_ref[...] = rope(mm(OFF_BK, B_KV_WIDTH))
    vb_ref[...] = mm(OFF_BV, B_KV_WIDTH)
    sga_ref[...] = _sigmoid(mm(OFF_GA, D_MODEL)).astype(BF16)
    sgb_ref[...] = _sigmoid(mm(OFF_GB, D_MODEL)).astype(BF16)


def _proj(x, w_bf, cos_t, sin_t, lower, n_rows, row_off_blocks, tm):
    nt = n_rows // tm
    tab_blocks = cos_t.shape[0] // tm
    row = lambda w: pl.BlockSpec((tm, w), lambda i: (i, 0))
    outs = [(A_KEY, BF16), (A_KEY, F32), (A_WIDTH, BF16), (A_WIDTH, BF16), (B_WIDTH, BF16),
            (B_KV_WIDTH, F32), (B_KV_WIDTH, F32), (D_MODEL, BF16), (D_MODEL, BF16)]
    return pl.pallas_call(
        _proj_kernel,
        grid=(nt,),
        in_specs=[pl.BlockSpec((tm, D_MODEL), lambda i: (i + row_off_blocks, 0)),
                  pl.BlockSpec((D_MODEL, IN_COLS), lambda i: (0, 0), pipeline_mode=pl.Buffered(1)),
                  pl.BlockSpec((tm, LANES), lambda i: (i % tab_blocks, 0)),
                  pl.BlockSpec((tm, LANES), lambda i: (i % tab_blocks, 0)),
                  pl.BlockSpec((1, A_KEY), lambda i: (0, 0))],
        out_specs=[row(w) for w, _ in outs],
        out_shape=[jax.ShapeDtypeStruct((n_rows, w), dt) for w, dt in outs],
        compiler_params=_cparams(("arbitrary",)),
        name="proj",
    )(x, w_bf, cos_t, sin_t, lower)


def _rope_tables(pos):
    half = B_HEAD_DIM // 2
    inv = ROPE_THETA ** (-jnp.arange(half, dtype=F32) / half)
    ang = pos.astype(F32)[:, None] * inv[None, :]
    cos = jnp.cos(ang)
    sin = jnp.sin(ang)
    reps = LANES // B_HEAD_DIM
    cos_t = jnp.tile(jnp.concatenate([cos, cos], axis=1), (1, reps))
    sin_t = jnp.tile(jnp.concatenate([-sin, sin], axis=1), (1, reps))
    return cos_t, sin_t


def _hgrn_constants():
    c = HGRN_CHUNK
    r = np.arange(c)
    tri = (r[None, :] <= r[:, None]).astype(np.float32)
    upper = np.zeros((HGRN_LEVELS, c, A_KEY), np.float32)
    pair = np.zeros((HGRN_LEVELS + 1, c, c), np.float32)
    for l in range(HGRN_LEVELS):
        b = c >> (l + 1)
        up = (r % (2 * b)) >= b
        upper[l] = up[:, None]
        same = (r[:, None] // (2 * b)) == (r[None, :] // (2 * b))
        pair[l] = (up[:, None] & ~up[None, :] & same)
    pair[HGRN_LEVELS] = np.eye(c)
    return jnp.asarray(tri, BF16), jnp.asarray(upper), jnp.asarray(pair)


def _hgrn_kernel(qa_ref, lf_ref, va_ref, ga_ref, g_ref, tri_ref, up_ref, pair_ref,
                 oa_ref, st_ref, s_scr):
    c = HGRN_CHUNK
    step = pl.program_id(1)

    @pl.when(step == 0)
    def _():
        s_scr[...] = jnp.zeros_like(s_scr)

    lf = lf_ref[...]
    tri = tri_ref[...]
    hi, mid, lo = _split3(lf)
    gcum = _dot(tri, hi) + _dot(tri, mid) + _dot(tri, lo)
    qb = qa_ref[...]
    qf = qb.astype(F32)
    kf = 1.0 - jnp.exp(lf)
    kb = kf.astype(BF16)
    vb = va_ref[...]
    ws = []
    g8 = gcum.reshape(c // SUBLANES, SUBLANES, A_KEY)
    sub = lax.broadcasted_iota(I32, g8.shape, 1)
    for l in range(HGRN_LEVELS):
        half = c >> (l + 1)
        if 2 * half >= SUBLANES:
            g3 = gcum.reshape(c // (2 * half), 2 * half, A_KEY)
            gref_l = jnp.broadcast_to(g3[:, half - 1:half, :], g3.shape).reshape(c, A_KEY)
        else:
            gref_g = jnp.broadcast_to(g8[:, half - 1:half, :], g8.shape)
            for first in range(2 * half, SUBLANES, 2 * half):
                pick = jnp.broadcast_to(g8[:, first + half - 1:first + half, :], g8.shape)
                gref_g = jnp.where(sub >= first, pick, gref_g)
            gref_l = gref_g.reshape(c, A_KEY)
        e = jnp.exp(-jnp.abs(gcum - gref_l))
        ws.append((jnp.where(up_ref[l] > 0.5, qf, kf) * e).astype(BF16))
    qg = (qf * jnp.exp(gcum)).astype(BF16)
    gend = gcum[c - 1:c, :]
    kend = (kf * jnp.exp(gend - gcum)).astype(BF16)
    decay = jnp.exp(gend)
    gate = g_ref[...]

    for h in range(A_HEADS):
        sl = slice(h * A_DK, (h + 1) * A_DK)
        att = _dot_nt(qb[:, sl], kb[:, sl]) * pair_ref[HGRN_LEVELS]
        for l in range(HGRN_LEVELS):
            w = ws[l][:, sl]
            att = att + _dot_nt(w, w) * pair_ref[l]
        s_t = s_scr[h]
        o = _dot_nt(qg[:, sl], s_t.astype(BF16)) + _dot(att.astype(BF16), vb[:, sl])
        ms = jnp.mean(o * o, axis=-1, keepdims=True)
        on = o * lax.rsqrt(ms + RMS_EPS) * gate * ga_ref[:, sl].astype(F32)
        oa_ref[:, sl] = on.astype(BF16)
        s_scr[h] = s_t * decay[:, sl] + _dot_tn(vb[:, sl], kend[:, sl])

    @pl.when(step == pl.num_programs(1) - 1)
    def _():
        for h in range(A_HEADS):
            st_ref[0, h] = s_scr[h].T


def _hgrn_prompt(qa, lf, va, ga, norm_g, bsz, t):
    c = HGRN_CHUNK
    nc = t // c
    tri, upper, pair = _hgrn_constants()
    blk = lambda: pl.BlockSpec((c, A_KEY), lambda b, i: (b * nc + i, 0))
    const = lambda a: pl.BlockSpec(a.shape, lambda b, i: (0,) * a.ndim)
    return pl.pallas_call(
        _hgrn_kernel,
        grid=(bsz, nc),
        in_specs=[blk(), blk(), blk(), blk(), pl.BlockSpec((1, A_DV), lambda b, i: (0, 0)),
                  const(tri), const(upper), const(pair)],
        out_specs=[blk(), pl.BlockSpec((1, A_HEADS, A_DK, A_DV), lambda b, i: (b, 0, 0, 0))],
        out_shape=[jax.ShapeDtypeStruct((bsz * t, A_WIDTH), BF16),
                   jax.ShapeDtypeStruct((bsz, A_HEADS, A_DK, A_DV), F32)],
        scratch_shapes=[pltpu.VMEM((A_HEADS, A_DV, A_DK), F32)],
        compiler_params=_cparams(("arbitrary", "arbitrary")),
        name="hgrn_prompt",
    )(qa, lf, va, ga, norm_g, tri, upper, pair)


def _swa_kernel(sink_ref, q_ref, k_ref, v_ref, o_ref, kprev, vprev):
    w = WINDOW
    i = pl.program_id(1)

    @pl.when(i == 0)
    def _():
        kprev[...] = jnp.zeros_like(kprev)
        vprev[...] = jnp.zeros_like(vprev)

    kc = k_ref[...]
    vc = v_ref[...]
    kk = jnp.concatenate([kprev[...], kc], axis=0)
    vv = jnp.concatenate([vprev[...], vc], axis=0)
    kr = pltpu.roll(kk, B_HEAD_DIM, 1)
    vr = pltpu.roll(vv, B_HEAD_DIM, 1)
    lo2 = lax.broadcasted_iota(I32, kk.shape, 1) < B_HEAD_DIM
    zero = jnp.zeros_like(kk)
    k_lo = [jnp.where(lo2, kk, zero).astype(BF16), jnp.where(lo2, kr, zero).astype(BF16)]
    k_hi = [jnp.where(lo2, zero, kr).astype(BF16), jnp.where(lo2, zero, kk).astype(BF16)]
    v_dup = [jnp.where(lo2, vv, vr).astype(BF16), jnp.where(lo2, vr, vv).astype(BF16)]

    qi = lax.broadcasted_iota(I32, (w, 2 * w), 0)
    kj = lax.broadcasted_iota(I32, (w, 2 * w), 1)
    valid = (kj >= qi) & (kj <= qi + w) & ((kj >= w) | (i > 0))
    lo1 = lax.broadcasted_iota(I32, (w, LANES), 1) < B_HEAD_DIM

    for j in range(B_WIDTH // LANES):
        g = (2 * j) // B_GROUP
        qblk = q_ref[:, j * LANES:(j + 1) * LANES]
        res = []
        for half, kmat in enumerate((k_lo[g], k_hi[g])):
            sk = sink_ref[2 * j + half]
            s = jnp.where(valid, _dot_nt(qblk, kmat), NEG_BIG)
            m = jnp.maximum(jnp.max(s, axis=-1, keepdims=True), sk)
            p = jnp.exp(s - m)
            denom = jnp.sum(p, axis=-1, keepdims=True) + jnp.exp(sk - m)
            res.append(_dot(p.astype(BF16), v_dup[g]) / denom)
        o_ref[:, j * LANES:(j + 1) * LANES] = jnp.where(lo1, res[0], res[1]).astype(BF16)

    kprev[...] = kc
    vprev[...] = vc


def _swa_prompt(sinks, qb, kb, vb, bsz, t):
    w = WINDOW
    nb = t // w
    return pl.pallas_call(
        _swa_kernel,
        grid_spec=pltpu.PrefetchScalarGridSpec(
            num_scalar_prefetch=1,
            grid=(bsz, nb),
            in_specs=[pl.BlockSpec((w, B_WIDTH), lambda b, i, s: (b * nb + i, 0)),
                      pl.BlockSpec((w, B_KV_WIDTH), lambda b, i, s: (b * nb + i, 0)),
                      pl.BlockSpec((w, B_KV_WIDTH), lambda b, i, s: (b * nb + i, 0))],
            out_specs=pl.BlockSpec((w, B_WIDTH), lambda b, i, s: (b * nb + i, 0)),
            scratch_shapes=[pltpu.VMEM((w, B_KV_WIDTH), F32), pltpu.VMEM((w, B_KV_WIDTH), F32)]),
        out_shape=jax.ShapeDtypeStruct((bsz * t, B_WIDTH), BF16),
        compiler_params=_cparams(("arbitrary", "arbitrary")),
        name="swa_prompt",
    )(sinks, qb, kb, vb)


SAMPLE_BLOCK = 8


def _sample_kernel(sink_ref, st_ref, lft_ref, qat_ref, va_ref, ga_ref, g_ref, q3_ref, kn_ref, vn_ref,
                   ck_ref, cv_ref, st_out, oa_ref, ob_ref, ck_out, cv_out, o_scr):
    w = WINDOW
    row = lax.broadcasted_iota(I32, (w, B_KV_WIDTH), 0)
    for i in range(SAMPLE_BLOCK):
        for h in range(A_HEADS):
            sl = slice(h * A_DV, (h + 1) * A_DV)
            fcol = jnp.exp(lft_ref[h, 0][:, i:i + 1])
            qcol = qat_ref[h, 0][:, i:i + 1]
            vrow = va_ref[i:i + 1, sl].astype(F32)
            s_new = st_ref[i, h] * fcol + (1.0 - fcol) * vrow
            st_out[i, h] = s_new
            o_scr[i:i + 1, sl] = jnp.sum(s_new * qcol, axis=0, keepdims=True)
        kc = ck_ref[i]
        vc = cv_ref[i]
        kn = kn_ref[i:i + 1, :]
        vn = vn_ref[i:i + 1, :]
        q3 = q3_ref[i]
        s = _dot_nt(q3.astype(BF16), kc.astype(BF16))
        s_new_key = jnp.sum(q3 * kn, axis=-1, keepdims=True)
        sk = sink_ref[...][:, 0:1]
        m = jnp.maximum(jnp.maximum(jnp.max(s, axis=-1, keepdims=True), s_new_key), sk)
        p = jnp.exp(s - m)
        pn = jnp.exp(s_new_key - m)
        denom = jnp.sum(p, axis=-1, keepdims=True) + pn + jnp.exp(sk - m)
        ob_ref[i] = (_dot(p.astype(BF16), vc.astype(BF16)) + pn * vn) / denom
        ck_out[i] = jnp.where(row == w - 1, kn, pltpu.roll(kc, w - 1, 0))
        cv_out[i] = jnp.where(row == w - 1, vn, pltpu.roll(vc, w - 1, 0))
    for h in range(A_HEADS):
        sl = slice(h * A_DV, (h + 1) * A_DV)
        o = o_scr[:, sl]
        ms = jnp.mean(o * o, axis=-1, keepdims=True)
        oa_ref[:, sl] = (o * lax.rsqrt(ms + RMS_EPS) * g_ref[...] * ga_ref[:, sl].astype(F32)).astype(BF16)


def _sample_step(l, sinks8, state, lft, qat, va, ga, norm_g, q3, kn, vn, ck, cv):
    nb = state.shape[1]
    sb = SAMPLE_BLOCK
    steps = nb // sb
    w = WINDOW
    b4 = lambda: pl.BlockSpec((sb, A_HEADS, A_DK, A_DV), lambda i: (i, 0, 0, 0))
    b4_in = pl.BlockSpec((None, sb, A_HEADS, A_DK, A_DV), lambda i: (l, i, 0, 0, 0))
    c3_in = lambda: pl.BlockSpec((None, sb, w, B_KV_WIDTH), lambda i: (l, i, 0, 0))
    t4 = lambda: pl.BlockSpec((A_HEADS, 1, A_DK, sb), lambda i: (0, i, 0, 0))
    r2 = lambda wd: pl.BlockSpec((sb, wd), lambda i: (i, 0))
    c3 = lambda: pl.BlockSpec((sb, w, B_KV_WIDTH), lambda i: (i, 0, 0))
    return pl.pallas_call(
        _sample_kernel,
        grid=(steps,),
        in_specs=[pl.BlockSpec((B_Q_HEADS, LANES), lambda i: (0, 0)),
                  b4_in, t4(), t4(), r2(A_WIDTH), r2(A_WIDTH), pl.BlockSpec((1, A_DV), lambda i: (0, 0)),
                  pl.BlockSpec((sb, B_Q_HEADS, LANES), lambda i: (i, 0, 0)), r2(B_KV_WIDTH), r2(B_KV_WIDTH),
                  c3_in(), c3_in()],
        out_specs=[b4(), r2(A_WIDTH), pl.BlockSpec((sb, B_Q_HEADS, LANES), lambda i: (i, 0, 0)), c3(), c3()],
        out_shape=[jax.ShapeDtypeStruct(state.shape[1:], F32),
                   jax.ShapeDtypeStruct((nb, A_WIDTH), BF16),
                   jax.ShapeDtypeStruct((nb, B_Q_HEADS, LANES), F32),
                   jax.ShapeDtypeStruct((nb, w, B_KV_WIDTH), F32),
                   jax.ShapeDtypeStruct((nb, w, B_KV_WIDTH), F32)],
        scratch_shapes=[pltpu.VMEM((sb, A_WIDTH), F32)],
        compiler_params=_cparams(("arbitrary",)),
        name="sample_step",
    )(sinks8, state, lft, qat, va, ga, norm_g, q3, kn, vn, ck, cv)


def _merge_kernel(x_ref, oa_ref, ob_ref, sga_ref, sgb_ref, wa_ref, wb_ref, wo_ref, g_ref, b_ref, h_ref):
    merged = (sga_ref[...].astype(F32) * _dot(oa_ref[...], wa_ref[...])
              + sgb_ref[...].astype(F32) * _dot(ob_ref[...], wb_ref[...]))
    mix = _dot(merged.astype(BF16), wo_ref[...])
    _store_tiled(h_ref, _layer_norm(DN_ALPHA * x_ref[...] + mix, g_ref[...], b_ref[...]))


def _merge(x, oa, ob, sga, sgb, wa, wb, wo, g, b, n_rows, x_off_blocks, out_rows, out_off_blocks, tm, h_prev=None):
    nt = n_rows // tm
    row = lambda wd: pl.BlockSpec((tm, wd), lambda i: (i, 0))
    const = lambda a: pl.BlockSpec(a.shape, lambda i: (0, 0))
    args = [x, oa, ob, sga, sgb, wa, wb, wo, g, b]
    in_specs = [pl.BlockSpec((tm, D_MODEL), lambda i: (i + x_off_blocks, 0)),
                row(A_WIDTH), row(B_WIDTH), row(D_MODEL), row(D_MODEL),
                const(wa), const(wb), const(wo), const(g), const(b)]
    kern = _merge_kernel
    aliases = {}
    if h_prev is not None:
        args.append(h_prev)
        in_specs.append(pl.BlockSpec(memory_space=pl.ANY))
        aliases = {len(args) - 1: 0}
        kern = lambda *refs: _merge_kernel(*refs[:10], refs[11])
    return pl.pallas_call(
        kern,
        grid=(nt,),
        in_specs=in_specs,
        out_specs=pl.BlockSpec((tm * SUBLANES, LANES), lambda i: (i + out_off_blocks, 0)),
        out_shape=jax.ShapeDtypeStruct((out_rows * SUBLANES, LANES), F32),
        input_output_aliases=aliases,
        compiler_params=_cparams(("arbitrary",)),
        name="merge",
    )(*args)


def _mixer_layer(l, xp, xs, xs_off, n_prompt, bsz, t, n_sample, prm, h_buf):
    tm = ROW_TILE
    p = _proj(xp, prm["w_in"][l], prm["cos_p"], prm["sin_p"], prm["lower"][l], n_prompt, 0, tm)
    qa, lf, va, ga, qb, kb, vb, sga, sgb = p
    oa, st_p = _hgrn_prompt(qa, lf, va, ga, prm["norm_g"][l], bsz, t)
    ob = _swa_prompt(prm["sinks"][l], qb, kb, vb, bsz, t)
    n_all = n_prompt + n_sample
    h_all = _merge(xp, oa, ob, sga, sgb, prm["wa"][l], prm["wb"][l], prm["wo"][l], prm["ln1_g"][l],
                   prm["ln1_b"][l], n_prompt, 0, n_all, 0, tm, h_prev=h_buf)
    kp = kb.reshape(bsz, t, B_KV_HEADS, B_HEAD_DIM)[:, -WINDOW:]
    vp = vb.reshape(bsz, t, B_KV_HEADS, B_HEAD_DIM)[:, -WINDOW:]

    ts = n_sample
    ps = _proj(xs, prm["w_in"][l], prm["cos_s"], prm["sin_s"], prm["lower"][l], ts, xs_off, ts)
    qa_s, lf_s, va_s, ga_s, qb_s, kb_s, vb_s, sga_s, sgb_s = ps
    sb = SAMPLE_BLOCK
    to_t = lambda a: a.reshape(ts // sb, sb, A_HEADS, A_DK).transpose(2, 0, 3, 1)
    qh = qb_s.astype(F32).reshape(ts, B_Q_HEADS, B_HEAD_DIM)
    z = jnp.zeros_like(qh[:, :B_GROUP])
    q3 = jnp.concatenate([jnp.concatenate([qh[:, :B_GROUP], z], axis=-1),
                          jnp.concatenate([z, qh[:, B_GROUP:]], axis=-1)], axis=1)
    st_s, oa_s, ob3, ck_s, cv_s = _sample_step(
        l, prm["sinks8"][l], prm["state"], to_t(lf_s), to_t(qa_s.astype(F32)), va_s, ga_s, prm["norm_g"][l],
        q3, kb_s, vb_s, prm["cache_k"], prm["cache_v"])
    ob_s = jnp.concatenate([ob3[:, :B_GROUP, :B_HEAD_DIM], ob3[:, B_GROUP:, B_HEAD_DIM:]], axis=1)
    ob_s = ob_s.reshape(ts, B_WIDTH).astype(BF16)
    h_all = _merge(xs, oa_s, ob_s, sga_s, sgb_s, prm["wa"][l], prm["wb"][l], prm["wo"][l], prm["ln1_g"][l],
                   prm["ln1_b"][l], ts, xs_off, n_all, n_prompt // ts, ts, h_prev=h_all)
    return h_all, (kp, vp, st_p, ck_s, cv_s, st_s)


def _router_kernel(h_ref, wr_ref, bias_ref, e_ref, w_ref, mask_ref, cnt_ref):
    tm = h_ref.shape[0] // SUBLANES
    gsz = GROUP_SIZE

    @pl.when(pl.program_id(0) == 0)
    def _():
        cnt_ref[...] = jnp.zeros_like(cnt_ref)

    logits = lax.dot_general(wr_ref[...], _load_tiled(h_ref, tm), (((1,), (1,)), ((), ())),
                             precision=lax.Precision.HIGHEST, preferred_element_type=F32)
    scores = _sigmoid(logits)
    sel = scores + bias_ref[...][:, 0:1]
    rowi = lax.broadcasted_iota(I32, (gsz, tm), 0)
    neg_inf = -jnp.inf
    blocks = [sel[g * gsz:(g + 1) * gsz] for g in range(N_GROUPS)]
    sblocks = [scores[g * gsz:(g + 1) * gsz] for g in range(N_GROUPS)]

    gscore = []
    for blk in blocks:
        m1 = jnp.max(blk, axis=0, keepdims=True)
        i1 = jnp.min(jnp.where(blk == m1, rowi, gsz), axis=0, keepdims=True)
        m2 = jnp.max(jnp.where(rowi == i1, neg_inf, blk), axis=0, keepdims=True)
        gscore.append(m1 + m2)
    work = []
    for g in range(N_GROUPS):
        ahead = jnp.zeros((1, tm), I32)
        for g2 in range(N_GROUPS):
            if g2 != g:
                beats = (gscore[g2] > gscore[g]) | ((gscore[g2] == gscore[g]) & (g2 < g))
                ahead = ahead + beats.astype(I32)
        work.append(jnp.where(ahead < TOPK_GROUPS, blocks[g], NEG_BIG))

    chosen = [jnp.zeros((gsz, tm), F32) for _ in range(N_GROUPS)]
    es, ws = [], []
    for _ in range(TOP_K):
        m = work[0]
        for g in range(1, N_GROUPS):
            m = jnp.maximum(m, work[g])
        m = jnp.max(m, axis=0, keepdims=True)
        cand = jnp.where(work[0] == m, rowi, N_EXPERTS)
        for g in range(1, N_GROUPS):
            cand = jnp.minimum(cand, jnp.where(work[g] == m, rowi + g * gsz, N_EXPERTS))
        idx = jnp.min(cand, axis=0, keepdims=True)
        wj = jnp.zeros((1, tm), F32)
        for g in range(N_GROUPS):
            hit = (rowi + g * gsz) == idx
            wj = wj + jnp.sum(jnp.where(hit, sblocks[g], 0.0), axis=0, keepdims=True)
            chosen[g] = jnp.where(hit, 1.0, chosen[g])
            work[g] = jnp.where(hit, neg_inf, work[g])
        es.append(idx)
        ws.append(wj)
    wsum = ws[0]
    for j in range(1, TOP_K):
        wsum = wsum + ws[j]
    for j in range(TOP_K):
        e_ref[j:j + 1, :] = es[j]
        w_ref[j:j + 1, :] = ws[j] / wsum * ROUTED_SCALE
    for g in range(N_GROUPS):
        rows = slice(g * gsz, (g + 1) * gsz)
        mask_ref[rows, :] = chosen[g]
        part = chosen[g][:, 0:LANES]
        for c in range(1, tm // LANES):
            part = part + chosen[g][:, c * LANES:(c + 1) * LANES]
        cnt_ref[rows, :] = cnt_ref[rows, :] + part


def _router(h_all, wr_t, bias_b, tm):
    n = h_all.shape[0] // SUBLANES
    col = lambda r: pl.BlockSpec((r, tm), lambda i: (0, i))
    return pl.pallas_call(
        _router_kernel,
        grid=(n // tm,),
        in_specs=[pl.BlockSpec((tm * SUBLANES, LANES), lambda i: (i, 0)),
                  pl.BlockSpec((N_EXPERTS, D_MODEL), lambda i: (0, 0)),
                  pl.BlockSpec((N_EXPERTS, LANES), lambda i: (0, 0))],
        out_specs=[col(TOP_K), col(TOP_K), col(N_EXPERTS), pl.BlockSpec((N_EXPERTS, LANES), lambda i: (0, 0))],
        out_shape=[jax.ShapeDtypeStruct((TOP_K, n), I32), jax.ShapeDtypeStruct((TOP_K, n), F32),
                   jax.ShapeDtypeStruct((N_EXPERTS, n), F32), jax.ShapeDtypeStruct((N_EXPERTS, LANES), F32)],
        compiler_params=_cparams(("arbitrary",)),
        name="router",
    )(h_all, wr_t, bias_b)


def _rank_kernel(mask_ref, e_ref, offs_ref, triu_ref, dest_ref, carry):
    tm = mask_ref.shape[1]

    @pl.when(pl.program_id(0) == 0)
    def _():
        carry[...] = jnp.zeros_like(carry)

    mk = mask_ref[...]
    rank = _dot(mk.astype(BF16), triu_ref[...])
    dest_full = rank + (offs_ref[...][:, 0:1] + carry[...][:, 0:1])
    rowi = lax.broadcasted_iota(I32, (N_EXPERTS, tm), 0)
    for j in range(TOP_K):
        d = jnp.sum(jnp.where(rowi == e_ref[j:j + 1, :], dest_full, 0.0), axis=0, keepdims=True)
        dest_ref[j:j + 1, :] = d.astype(I32)
    carry[...] = carry[...] + jnp.sum(mk, axis=1, keepdims=True)


def _rank(mask_t, e_t, offs_b, tm):
    n = mask_t.shape[1]
    r = np.arange(tm)
    triu = jnp.asarray((r[:, None] < r[None, :]).astype(np.float32), BF16)
    return pl.pallas_call(
        _rank_kernel,
        grid=(n // tm,),
        in_specs=[pl.BlockSpec((N_EXPERTS, tm), lambda i: (0, i)),
                  pl.BlockSpec((TOP_K, tm), lambda i: (0, i)),
                  pl.BlockSpec((N_EXPERTS, LANES), lambda i: (0, 0)),
                  pl.BlockSpec((tm, tm), lambda i: (0, 0))],
        out_specs=pl.BlockSpec((TOP_K, tm), lambda i: (0, i)),
        out_shape=jax.ShapeDtypeStruct((TOP_K, n), I32),
        scratch_shapes=[pltpu.VMEM((N_EXPERTS, LANES), F32)],
        compiler_params=_cparams(("arbitrary",)),
        name="rank",
    )(mask_t, e_t, offs_b, triu)


INV_COLS = 512
TOKEN_RADIX = 128


def _inverse_kernel(dest_ref, inv_ref):
    tm = dest_ref.shape[1]
    nq = inv_ref.shape[0]

    @pl.when(pl.program_id(0) == 0)
    def _():
        inv_ref[...] = jnp.zeros_like(inv_ref)

    tok = pl.program_id(0) * tm + lax.broadcasted_iota(I32, (1, tm), 1)
    t_hi = jnp.right_shift(tok, TOKEN_RADIX.bit_length() - 1).astype(F32)
    t_lo = jnp.bitwise_and(tok, TOKEN_RADIX - 1).astype(F32)
    qi = lax.broadcasted_iota(I32, (nq, tm), 0)
    si = lax.broadcasted_iota(I32, (INV_COLS, tm), 0)
    acc_hi = jnp.zeros(inv_ref.shape, F32)
    acc_lo = jnp.zeros(inv_ref.shape, F32)
    for j in range(TOP_K):
        d = dest_ref[j:j + 1, :]
        at_q = qi == jnp.right_shift(d, INV_COLS.bit_length() - 1)
        col = jnp.where(si == jnp.bitwise_and(d, INV_COLS - 1), 1.0, 0.0).astype(BF16)
        acc_hi = acc_hi + _dot_nt(jnp.where(at_q, t_hi, 0.0).astype(BF16), col)
        acc_lo = acc_lo + _dot_nt(jnp.where(at_q, t_lo, 0.0).astype(BF16), col)
    inv_ref[...] = inv_ref[...] + (acc_hi * float(TOKEN_RADIX) + acc_lo)


def _inverse_map(dest_t, tm):
    n = dest_t.shape[1]
    n_rows = n * TOP_K
    nq = -(-(n_rows // INV_COLS) // 8) * 8
    inv = pl.pallas_call(
        _inverse_kernel,
        grid=(n // tm,),
        in_specs=[pl.BlockSpec((TOP_K, tm), lambda i: (0, i))],
        out_specs=pl.BlockSpec((nq, INV_COLS), lambda i: (0, 0)),
        out_shape=jax.ShapeDtypeStruct((nq, INV_COLS), F32),
        compiler_params=_cparams(("arbitrary",)),
        name="inverse_map",
    )(dest_t)
    return inv.reshape(-1)[:n_rows].astype(I32).reshape(n_rows // EXPERT_ROW_TILE, 1, EXPERT_ROW_TILE)


DMA_THREADS = 2


def _token_copy(src_ref, token, dst_ref, dst_row, sem):
    return pltpu.make_async_copy(src_ref.at[pl.ds(pl.multiple_of(token * SUBLANES, SUBLANES), SUBLANES)],
                                 dst_ref.at[pl.ds(dst_row * SUBLANES, SUBLANES)], sem)


GATHER_AHEAD = 2
GATHER_SLOTS = GATHER_AHEAD + 1


def _expert_kernel(n_tiles, tile_ref, exp_ref, valid_ref, first_ref, newexp_ref, lo_ref, hi_ref,
                   inv0_ref, inv1_ref, inva_ref, h_ref, wg_ref, wu_ref, wd_ref, y_ref, xbuf, wgb, wub, wdb, sem):
    i = pl.program_id(0)
    rows = xbuf.shape[1] // SUBLANES
    tile = tile_ref[i]
    slot = tile % GATHER_SLOTS
    ahead_slot = (tile + GATHER_AHEAD) % GATHER_SLOTS
    valid = valid_ref[i] == 1
    first = first_ref[i] == 1

    def wait_rows(s):
        pltpu.make_async_copy(h_ref.at[pl.ds(0, rows * SUBLANES)], xbuf.at[s], sem.at[s]).wait()

    def gather_loop(idx_ref, s):
        def issue(k, c):
            for u in range(SUBLANES):
                r = k * SUBLANES + u
                pltpu.make_async_copy(
                    h_ref.at[pl.ds(pl.multiple_of(idx_ref[0, 0, r] * SUBLANES, SUBLANES), SUBLANES)],
                    xbuf.at[s, pl.ds(pl.multiple_of(r * SUBLANES, SUBLANES), SUBLANES)], sem.at[s]).start()
            return c

        lax.fori_loop(0, rows // SUBLANES, issue, 0)

    @pl.when(valid & first)
    def _():
        @pl.when(i == 0)
        def _():
            gather_loop(inv0_ref, 0)
            gather_loop(inv1_ref, 1)

        wait_rows(slot)

    @pl.when(valid & (newexp_ref[i] == 1))
    def _():
        wgb[...] = wg_ref[0].astype(BF16)
        wub[...] = wu_ref[0].astype(BF16)
        wdb[...] = wd_ref[0].astype(BF16)

    def compute(first_visit):
        xb = _load_tiled(xbuf.at[slot], rows).astype(BF16)
        if first_visit:
            for r in range(rows):
                _token_copy(h_ref, inva_ref[0, 0, r], xbuf.at[ahead_slot], r, sem.at[ahead_slot]).start(
                    priority=r % DMA_THREADS)
        gate = _dot(xb, wgb[...])
        up = _dot(xb, wub[...])
        y = _dot((_silu(gate) * up).astype(BF16), wdb[...])
        rowi = lax.broadcasted_iota(I32, y.shape, 0)
        mine = (rowi >= lo_ref[i]) & (rowi < hi_ref[i])
        _store_tiled(y_ref, jnp.where(mine, y, 0.0 if first_visit else _load_tiled(y_ref, rows)))

    @pl.when(valid & first)
    def _():
        compute(True)

    @pl.when(valid & jnp.logical_not(first))
    def _():
        compute(False)

    @pl.when(valid & first & (tile >= n_tiles - GATHER_AHEAD))
    def _():
        wait_rows(ahead_slot)


def _group_metadata(counts, n_rows):
    tmo = EXPERT_ROW_TILE
    n_tiles = n_rows // tmo
    ends = jnp.cumsum(counts)
    offs = ends - counts
    first_tile = offs // tmo
    n_t = jnp.where(counts > 0, (ends - 1) // tmo - first_tile + 1, 0)
    cum = jnp.cumsum(n_t)
    base = cum - n_t
    n_items = n_tiles + N_EXPERTS
    idx = jnp.arange(n_items, dtype=I32)
    valid = (idx < cum[-1]).astype(I32)
    idc = jnp.minimum(idx, cum[-1] - 1)
    e = jnp.minimum(jnp.sum((cum[None, :] <= idc[:, None]).astype(I32), axis=1), N_EXPERTS - 1)
    onehot = (e[:, None] == jnp.arange(N_EXPERTS, dtype=I32)[None, :]).astype(I32)
    pick = lambda table: jnp.sum(onehot * table[None, :].astype(I32), axis=1)
    tile = (pick(first_tile) + idc - pick(base)).astype(I32)
    one = jnp.ones((1,), I32)
    first = jnp.concatenate([one, (tile[1:] != tile[:-1]).astype(I32)])
    new_expert = jnp.concatenate([one, (e[1:] != e[:-1]).astype(I32)])
    lo = jnp.clip(pick(offs) - tile * tmo, 0, tmo).astype(I32)
    hi = jnp.clip(pick(ends) - tile * tmo, 0, tmo).astype(I32)
    return tile, e, valid, first, new_expert, lo, hi


def _experts(l, meta, inv3, h_all, wg, wu, wd):
    tmo = EXPERT_ROW_TILE
    n_rows = inv3.shape[0] * tmo
    n_items = meta[0].shape[0]
    wspec = lambda shp: pl.BlockSpec((None, 1) + shp, lambda i, tl, ex, *_: (l, ex[i], 0, 0))
    n_tiles = inv3.shape[0]
    assert n_tiles > GATHER_AHEAD
    idx_spec = lambda nxt: pl.BlockSpec(
        (1, 1, tmo), lambda i, tl, *_: (jnp.minimum(tl[i] + nxt, n_tiles - 1), 0, 0), memory_space=pltpu.SMEM)
    return pl.pallas_call(
        functools.partial(_expert_kernel, n_tiles),
        grid_spec=pltpu.PrefetchScalarGridSpec(
            num_scalar_prefetch=7,
            grid=(n_items,),
            in_specs=[idx_spec(0), idx_spec(1), idx_spec(GATHER_AHEAD), pl.BlockSpec(memory_space=pl.ANY),
                      wspec((D_MODEL, D_EXPERT)), wspec((D_MODEL, D_EXPERT)), wspec((D_EXPERT, D_MODEL))],
            out_specs=pl.BlockSpec((tmo * SUBLANES, LANES), lambda i, tl, *_: (tl[i], 0)),
            scratch_shapes=[pltpu.VMEM((GATHER_SLOTS, tmo * SUBLANES, LANES), F32),
                            pltpu.VMEM((D_MODEL, D_EXPERT), BF16), pltpu.VMEM((D_MODEL, D_EXPERT), BF16),
                            pltpu.VMEM((D_EXPERT, D_MODEL), BF16),
                            pltpu.SemaphoreType.DMA((GATHER_SLOTS,))]),
        out_shape=jax.ShapeDtypeStruct((n_rows * SUBLANES, LANES), F32),
        compiler_params=_cparams(("arbitrary",)),
        name="experts",
    )(*meta, inv3, inv3, inv3, h_all, wg, wu, wd)


def _post_kernel(n_steps, n_prompt_steps, dest0_ref, dest1_ref, desta_ref, h_ref, wt_ref, y_ref, wsg_ref, wsu_ref,
                 wsd_ref, g_ref, b_ref, outp_ref, outs_ref, *scratch):
    bufs, sem = scratch[:GATHER_SLOTS], scratch[GATHER_SLOTS]
    i = pl.program_id(0)
    tm = outp_ref.shape[0]

    def wait_rows(s):
        for j in range(TOP_K):
            pltpu.make_async_copy(y_ref.at[pl.ds(0, tm * SUBLANES)], bufs[s].at[j], sem.at[s]).wait()

    def gather_loop(idx_ref, s):
        def issue(k, c):
            for u in range(SUBLANES):
                t = k * SUBLANES + u
                for j in range(TOP_K):
                    pltpu.make_async_copy(
                        y_ref.at[pl.ds(pl.multiple_of(idx_ref[j, t] * SUBLANES, SUBLANES), SUBLANES)],
                        bufs[s].at[j, pl.ds(pl.multiple_of(t * SUBLANES, SUBLANES), SUBLANES)], sem.at[s]).start()
            return c

        lax.fori_loop(0, tm // SUBLANES, issue, 0)

    @pl.when(i == 0)
    def _():
        gather_loop(dest0_ref, 0)
        gather_loop(dest1_ref, 1)

    def step(s):
        a = (s + GATHER_AHEAD) % GATHER_SLOTS
        wait_rows(s)
        h = _load_tiled(h_ref, tm)
        hb = h.astype(BF16)
        for t in range(tm):
            for j in range(TOP_K):
                _token_copy(y_ref, desta_ref[j, t], bufs[a].at[j], t, sem.at[a]).start(priority=j % DMA_THREADS)
        shared = _dot((_silu(_dot(hb, wsg_ref[...])) * _dot(hb, wsu_ref[...])).astype(BF16), wsd_ref[...])
        wt = wt_ref[...]
        routed = _load_tiled(bufs[s].at[0], tm) * wt[:, 0:1]
        for j in range(1, TOP_K):
            routed = routed + _load_tiled(bufs[s].at[j], tm) * wt[:, j:j + 1]
        res = _layer_norm(DN_ALPHA * h + (routed + shared), g_ref[...], b_ref[...])

        @pl.when(i < n_prompt_steps)
        def _():
            outp_ref[...] = res

        @pl.when(i >= n_prompt_steps)
        def _():
            outs_ref[...] = res

        @pl.when(i >= n_steps - GATHER_AHEAD)
        def _():
            wait_rows(a)

    for s in range(GATHER_SLOTS):
        pl.when(i % GATHER_SLOTS == s)(functools.partial(step, s))


def _post(dest_t, h_all, w_tok, y, wsg, wsu, wsd, g, b, tm, n_prompt):
    n = h_all.shape[0] // SUBLANES
    n_steps = n // tm
    n_prompt_steps = n_prompt // tm
    assert n_steps > GATHER_AHEAD and n_prompt % tm == 0 and 0 < n_prompt_steps < n_steps
    const = lambda a: pl.BlockSpec(a.shape, lambda i: (0, 0))
    idx_spec = lambda nxt: pl.BlockSpec((TOP_K, tm), lambda i: (0, jnp.minimum(i + nxt, n_steps - 1)),
                                        memory_space=pltpu.SMEM)
    return pl.pallas_call(
        functools.partial(_post_kernel, n_steps, n_prompt_steps),
        grid=(n_steps,),
        in_specs=[idx_spec(0), idx_spec(1), idx_spec(GATHER_AHEAD),
                  pl.BlockSpec((tm * SUBLANES, LANES), lambda i: (i, 0)),
                  pl.BlockSpec((tm, TOP_K), lambda i: (i, 0)),
                  pl.BlockSpec(memory_space=pl.ANY),
                  const(wsg), const(wsu), const(wsd), const(g), const(b)],
        out_specs=[pl.BlockSpec((tm, D_MODEL), lambda i: (jnp.minimum(i, n_prompt_steps - 1), 0)),
                   pl.BlockSpec((tm, D_MODEL), lambda i: (jnp.maximum(i - n_prompt_steps, 0), 0))],
        out_shape=[jax.ShapeDtypeStruct((n_prompt, D_MODEL), F32),
                   jax.ShapeDtypeStruct((n - n_prompt, D_MODEL), F32)],
        scratch_shapes=[pltpu.VMEM((TOP_K, tm * SUBLANES, LANES), F32) for _ in range(GATHER_SLOTS)]
        + [pltpu.SemaphoreType.DMA((GATHER_SLOTS,))],
        compiler_params=_cparams(("arbitrary",)),
        name="moe_post",
    )(dest_t, dest_t, dest_t, h_all, w_tok, y, wsg, wsu, wsd, g, b)


def _moe_layer(l, h_all, n_prompt, prm):
    tm = MOE_ROW_TILE
    e_t, w_t, mask_t, cnt = _router(h_all, prm["wr_t"][l], prm["rbias"][l], tm)
    counts = jnp.sum(cnt, axis=1).astype(I32)
    offs = jnp.cumsum(counts) - counts
    offs_b = jnp.broadcast_to(offs.astype(F32)[:, None], (N_EXPERTS, LANES))
    dest_t = _rank(mask_t, e_t, offs_b, tm)
    inv3 = _inverse_map(dest_t, tm)
    meta = _group_metadata(counts, h_all.shape[0] // SUBLANES * TOP_K)
    y = _experts(l, meta, inv3, h_all, prm["w_exp_gate"], prm["w_exp_up"], prm["w_exp_down"])
    return _post(dest_t, h_all, w_t.T, y, prm["wsg"][l], prm["wsu"][l], prm["wsd"][l],
                 prm["ln2_g"][l], prm["ln2_b"][l], POST_ROW_TILE, n_prompt)


def kernel(x_prompt, x_sample, cache_k, cache_v, state_hgrn, w_in, hgrn_lower_bounds, hgrn_norm_g, attn_sinks, w_branch_a, w_branch_b, w_out, ln1_g, ln1_b, w_router, router_bias, w_exp_gate, w_exp_up, w_exp_down, w_sh_gate, w_sh_up, w_sh_down, ln2_g, ln2_b):
    bsz, t, d = x_prompt.shape
    n_sample = x_sample.shape[0] * x_sample.shape[1]
    n_prompt = bsz * t
    depth = w_in.shape[0]
    assert d == D_MODEL and x_sample.shape[1] == 1 and n_prompt % ROW_TILE == 0 and t % ROW_TILE == 0
    assert n_prompt % n_sample == 0 and (n_prompt + n_sample) % MOE_ROW_TILE == 0
    assert n_prompt % POST_ROW_TILE == 0 and n_sample % POST_ROW_TILE == 0
    assert ((n_prompt + n_sample) * TOP_K) % EXPERT_ROW_TILE == 0 and n_sample % SAMPLE_BLOCK == 0

    lb_prob = jax.nn.softmax(hgrn_lower_bounds.astype(F32), axis=0)
    lower = (jnp.cumsum(lb_prob, axis=0) - lb_prob[0])[:, None, :]
    cos_p, sin_p = _rope_tables(jnp.arange(t))
    cos_s, sin_s = _rope_tables(jnp.full((n_sample,), PAST_LEN))
    row = lambda a: a[:, None, :]
    prm = dict(
        w_in=w_in.astype(BF16), lower=lower, cos_p=cos_p, sin_p=sin_p, cos_s=cos_s, sin_s=sin_s,
        norm_g=row(hgrn_norm_g), sinks=attn_sinks,
        sinks8=jnp.broadcast_to(attn_sinks[:, :, None], (depth, B_Q_HEADS, LANES)),
        wa=w_branch_a.astype(BF16), wb=w_branch_b.astype(BF16), wo=w_out.astype(BF16),
        ln1_g=row(ln1_g), ln1_b=row(ln1_b), ln2_g=row(ln2_g), ln2_b=row(ln2_b),
        wr_t=jnp.swapaxes(w_router, 1, 2),
        rbias=jnp.broadcast_to(router_bias[:, :, None], (depth, N_EXPERTS, LANES)),
        w_exp_gate=w_exp_gate, w_exp_up=w_exp_up, w_exp_down=w_exp_down,
        wsg=w_sh_gate.astype(BF16), wsu=w_sh_up.astype(BF16), wsd=w_sh_down.astype(BF16),
        state=state_hgrn,
        cache_k=cache_k.reshape(depth, n_sample, WINDOW, B_KV_WIDTH),
        cache_v=cache_v.reshape(depth, n_sample, WINDOW, B_KV_WIDTH),
    )

    xp, xs, xs_off = x_prompt.reshape(n_prompt, d), x_sample.reshape(n_sample, d), 0
    per_layer = []
    h_all = jnp.zeros(((n_prompt + n_sample) * SUBLANES, LANES), F32)
    for l in range(depth):
        h_all, outs = _mixer_layer(l, xp, xs, xs_off, n_prompt, bsz, t, n_sample, prm, h_all)
        xp, xs = _moe_layer(l, h_all, n_prompt, prm)
        per_layer.append(outs)

    kv_shape = (n_sample, WINDOW, B_KV_HEADS, B_HEAD_DIM)
    stack = lambda k, f=lambda a: a: jnp.stack([f(o[k]) for o in per_layer])
    return (xp.reshape(bsz, t, d), xs.reshape(n_sample, 1, d),
            stack(0), stack(1), stack(2),
            stack(3, lambda a: a.reshape(kv_shape)), stack(4, lambda a: a.reshape(kv_shape)), stack(5))
```
